```python
import math
import jax
import jax.numpy as jnp
from jax import lax
import numpy as np

D_MODEL = 1024
BATCH = 8
SEQ = 4096
DEPTH = 1

HEAD_DIM = 64
NSA_HEADS = 8
NSA_KV_GROUPS = 2
NSA_GROUP = NSA_HEADS // NSA_KV_GROUPS
NSA_WIDTH = NSA_HEADS * HEAD_DIM
NSA_KV_WIDTH = NSA_KV_GROUPS * HEAD_DIM
CMP_BLOCK = 32
CMP_STRIDE = 16
CMP_SPAN = CMP_BLOCK // CMP_STRIDE
CMP_HIDDEN = 256
SLC_BLOCK = 64
SLC_RATIO = SLC_BLOCK // CMP_STRIDE
SLC_TOPK = 16
OVERLAP_W = (1, 2, 2, 2, 1)
WINDOW = 512
WIN_QBLOCK = 128
REL_BUCKETS = 32
REL_MAX_DIST = 128
RWKV_HEADS = 8
RWKV_WIDTH = RWKV_HEADS * HEAD_DIM
LORA_W = 64
LORA_A = 64
LORA_G = 128
GN_EPS = 64e-5
D_FF = 2816
CONV_WIDTH = 3
RMS_EPS = 1e-6
NEG_INF = -1e30
FORCE = 1e9

RWKV_SIZES = (RWKV_WIDTH, RWKV_WIDTH, RWKV_WIDTH, LORA_W, LORA_A, LORA_G)
RWKV_IN_WIDTH = sum(RWKV_SIZES)
RWKV_SPLITS = tuple(np.cumsum(RWKV_SIZES)[:-1].tolist())
IN_SIZES = (NSA_WIDTH, NSA_KV_WIDTH, NSA_KV_WIDTH, NSA_KV_WIDTH, NSA_KV_WIDTH, NSA_KV_WIDTH, NSA_KV_WIDTH,
            NSA_HEADS * 3, RWKV_IN_WIDTH, D_MODEL, D_MODEL)
IN_WIDTH = sum(IN_SIZES)
IN_SPLITS = tuple(np.cumsum(IN_SIZES)[:-1].tolist())

kernel_name = 'hybrid_nsa_rwkv7_convffn'


def rmsnorm(x, g):
    xf = x.astype(jnp.float32)
    y = xf * lax.rsqrt(jnp.mean(xf * xf, axis=-1, keepdims=True) + RMS_EPS)
    return (y * g.astype(jnp.float32)).astype(x.dtype)


def t5_bucket(dist):
    n = jnp.maximum(dist, 0)
    max_exact = REL_BUCKETS // 2
    ratio = jnp.log(jnp.maximum(n, 1).astype(jnp.float32) / max_exact) / math.log(REL_MAX_DIST / max_exact)
    large = jnp.minimum(max_exact + (ratio * (REL_BUCKETS - max_exact)).astype(jnp.int32), REL_BUCKETS - 1)
    return jnp.where(n < max_exact, n, large)


def masked_softmax(s, mask, axis):
    s = jnp.where(mask, s.astype(jnp.float32), NEG_INF)
    e = jnp.exp(s - jnp.max(s, axis=axis, keepdims=True)) * mask
    return e / jnp.maximum(jnp.sum(e, axis=axis, keepdims=True), 1e-30)


def compress_blocks(kv, pe, w1, w2):
    b, s, g, dh = kv.shape
    chunks = kv.reshape(b, s // CMP_STRIDE, CMP_STRIDE, g, dh)
    n_cmp = s // CMP_STRIDE - CMP_SPAN + 1
    blocks = jnp.concatenate([chunks[:, i:i + n_cmp] for i in range(CMP_SPAN)], axis=2)
    blocks = blocks + pe[None, None, :, None, :]
    flat = blocks.transpose(0, 1, 3, 2, 4).reshape(b, n_cmp, g, CMP_BLOCK * dh)
    return jax.nn.gelu(flat @ w1) @ w2


def nsa_mixer(q, k_c, v_c, k_s, v_s, k_w, v_w, gate, rel_bias, q_norm_g, k_norm_g,
              cmp_pe_k, cmp_w1_k, cmp_w2_k, cmp_pe_v, cmp_w1_v, cmp_w2_v):
    b, s, _ = q.shape
    G, R, dh = NSA_KV_GROUPS, NSA_GROUP, HEAD_DIM
    q = rmsnorm(q.reshape(b, s, G, R, dh), q_norm_g) * (dh ** -0.5)
    k_c, v_c, k_s, v_s, k_w, v_w = [t.reshape(b, s, G, dh) for t in (k_c, v_c, k_s, v_s, k_w, v_w)]
    t_pos = jnp.arange(s)
    bias_tab = rel_bias.reshape(REL_BUCKETS, G, R)

    kc = rmsnorm(compress_blocks(k_c, cmp_pe_k, cmp_w1_k, cmp_w2_k), k_norm_g)
    vc = compress_blocks(v_c, cmp_pe_v, cmp_w1_v, cmp_w2_v)
    n_cmp = kc.shape[1]
    blk_end = jnp.arange(n_cmp) * CMP_STRIDE + CMP_BLOCK - 1
    dist_c = t_pos[:, None] - blk_end[None, :]
    bias_c = bias_tab[t5_bucket(dist_c)].transpose(2, 3, 0, 1)
    s_c = jnp.einsum('bsgrd,bcgd->bgrsc', q, kc).astype(jnp.float32) + bias_c
    p_c = masked_softmax(s_c, dist_c >= 0, axis=-1)
    o_cmp = jnp.einsum('bgrsc,bcgd->bsgrd', p_c, vc)

    n_slc = s // SLC_BLOCK
    imp_c = jnp.sum(p_c, axis=2)
    imp_pad = jnp.pad(imp_c, ((0, 0), (0, 0), (0, 0), (CMP_SPAN - 1, CMP_SPAN)))
    imp = OVERLAP_W[0] * imp_pad[..., 0:SLC_RATIO * n_slc:SLC_RATIO]
    for o in range(1, SLC_RATIO + CMP_SPAN - 1):
        imp = imp + OVERLAP_W[o] * imp_pad[..., o:o + SLC_RATIO * n_slc:SLC_RATIO]
    blk = jnp.arange(n_slc)[None, :]
    cur = (t_pos // SLC_BLOCK)[:, None]
    forced = (blk == 0) | (blk == cur) | (blk == cur - 1)
    score = jnp.where(forced, FORCE, jnp.where(blk <= cur, imp, -FORCE))
    n_sel = min(SLC_TOPK, n_slc)
    _, idx = lax.top_k(score, n_sel)
    sel_valid = idx <= cur

    nq = s // SLC_BLOCK
    k_blocks = rmsnorm(k_s, k_norm_g).reshape(b, n_slc, SLC_BLOCK, G, dh).transpose(0, 3, 1, 2, 4)
    v_blocks = v_s.reshape(b, n_slc, SLC_BLOCK, G, dh).transpose(0, 3, 1, 2, 4)
    gather = jax.vmap(jax.vmap(lambda blocks, ix: blocks[ix]))
    gi = jnp.arange(G).reshape(1, G, 1, 1, 1, 1)
    ri = jnp.arange(R).reshape(1, 1, R, 1, 1, 1)

    def slc_block(args):
        q_b, idx_b, val_b, t_b = args
        k_sel = gather(k_blocks, idx_b)
        v_sel = gather(v_blocks, idx_b)
        kpos = idx_b[..., None] * SLC_BLOCK + jnp.arange(SLC_BLOCK)
        dist = t_b[:, None, None] - kpos
        mask = (val_b[..., None] & (dist >= 0))[:, :, None]
        bias = bias_tab[t5_bucket(dist)[:, :, None], gi, ri]
        sc = jnp.einsum('bcgrd,bgcnld->bgrcnl', q_b, k_sel).astype(jnp.float32) + bias
        p = masked_softmax(sc, mask, axis=(-2, -1))
        return jnp.einsum('bgrcnl,bgcnld->bcgrd', p, v_sel)

    q_ch = jnp.moveaxis(q.reshape(b, nq, SLC_BLOCK, G, R, dh), 1, 0)
    idx_ch = jnp.moveaxis(idx.reshape(b, G, nq, SLC_BLOCK, n_sel), 2, 0)
    val_ch = jnp.moveaxis(sel_valid.reshape(b, G, nq, SLC_BLOCK, n_sel), 2, 0)
    t_ch = t_pos.reshape(nq, SLC_BLOCK)
    o_slc = jnp.moveaxis(lax.map(slc_block, (q_ch, idx_ch, val_ch, t_ch)), 0, 1).reshape(b, s, G, R, dh)

    nw = s // WIN_QBLOCK
    k_pad = jnp.pad(rmsnorm(k_w, k_norm_g), ((0, 0), (WINDOW, 0), (0, 0), (0, 0)))
    v_pad = jnp.pad(v_w, ((0, 0), (WINDOW, 0), (0, 0), (0, 0)))

    def win_block(args):
        q_b, i = args
        start = i * WIN_QBLOCK
        kw = lax.dynamic_slice_in_dim(k_pad, start, WIN_QBLOCK + WINDOW, axis=1)
        vw = lax.dynamic_slice_in_dim(v_pad, start, WIN_QBLOCK + WINDOW, axis=1)
        t_b = start + jnp.arange(WIN_QBLOCK)
        s_b = start - WINDOW + jnp.arange(WIN_QBLOCK + WINDOW)
        dist = t_b[:, None] - s_b[None, :]
        mask = (dist >= 0) & (dist < WINDOW) & (s_b[None, :] >= 0)
        bias = bias_tab[t5_bucket(dist)].transpose(2, 3, 0, 1)
        sc = jnp.einsum('bqgrd,bkgd->bgrqk', q_b, kw).astype(jnp.float32) + bias
        p = masked_softmax(sc, mask, axis=-1)
        return jnp.einsum('bgrqk,bkgd->bqgrd', p, vw)

    q_wch = jnp.moveaxis(q.reshape(b, nw, WIN_QBLOCK, G, R, dh), 1, 0)
    o_win = jnp.moveaxis(lax.map(win_block, (q_wch, jnp.arange(nw))), 0, 1).reshape(b, s, G, R, dh)

    gt = jax.nn.sigmoid(gate.reshape(b, s, G, R, 3).astype(jnp.float32))
    o = gt[..., 0:1] * o_cmp + gt[..., 1:2] * o_slc + gt[..., 2:3] * o_win
    return o.reshape(b, s, NSA_WIDTH)


def rwkv7_mixer(p_in, mu, w0, w2, a0, a2, g2, k_k, k_a, r_k, ln_g, ln_b):
    b, s, _ = p_in.shape
    H, N = RWKV_HEADS, HEAD_DIM
    shifted = jnp.pad(p_in, ((0, 0), (1, 0), (0, 0)))[:, :-1]
    xl = p_in + (shifted - p_in) * mu
    r, k, v, xw, xa, xg = jnp.split(xl, RWKV_SPLITS, axis=-1)
    w = -jax.nn.softplus(-(w0 + jnp.tanh(xw) @ w2)) - 0.5
    decay = jnp.exp(-jnp.exp(w.astype(jnp.float32)))
    a = jax.nn.sigmoid(a0 + xa @ a2)
    g = jax.nn.sigmoid(xg) @ g2
    heads = lambda t: t.reshape(b, s, H, N).astype(jnp.float32)
    r, k, v, a, decay = heads(r), heads(k), heads(v), heads(a), heads(decay)
    kk = k * k_k.reshape(H, N).astype(jnp.float32)
    kk = kk / jnp.maximum(jnp.sqrt(jnp.sum(kk * kk, axis=-1, keepdims=True)), 1e-12)
    k = k * (1.0 + (a - 1.0) * k_a.reshape(H, N).astype(jnp.float32))
    xs = tuple(jnp.moveaxis(t, 1, 0) for t in (r, decay, k, v, -kk, kk * a))

    def step(state, inp):
        r_t, w_t, k_t, v_t, a_t, b_t = inp
        sa = jnp.einsum('bhij,bhj->bhi', state, a_t)
        state = state * w_t[:, :, None, :] + sa[..., None] * b_t[:, :, None, :] + v_t[..., None] * k_t[:, :, None, :]
        return state, jnp.einsum('bhij,bhj->bhi', state, r_t)

    state0 = jnp.zeros((b, H, N, N), jnp.float32)
    _, y = lax.scan(step, state0, xs)
    y = jnp.moveaxis(y, 0, 1)
    mean = jnp.mean(y, axis=-1, keepdims=True)
    var = jnp.mean(jnp.square(y - mean), axis=-1, keepdims=True)
    y = (y - mean) * lax.rsqrt(var + GN_EPS) * ln_g.reshape(H, N) + ln_b.reshape(H, N)
    y = y + jnp.sum(r * k * r_k, axis=-1, keepdims=True) * v
    return y.reshape(b, s, RWKV_WIDTH) * g


def conv_ffn(h, w_up, conv_w, conv_b, w_down):
    u = h @ w_up
    s = u.shape[1]
    up = jnp.pad(u, ((0, 0), (CONV_WIDTH - 1, 0), (0, 0)))
    c = conv_b + conv_w[0] * up[:, 0:s]
    for j in range(1, CONV_WIDTH):
        c = c + conv_w[j] * up[:, j:j + s]
    val, gate = jnp.split(c, 2, axis=-1)
    return (jax.nn.silu(gate) * val) @ w_down


def setup_inputs(seed: int = 0) -> dict:
    key = jax.random.key(seed)
    ks = iter(jax.random.split(key, 40))
    L, D = DEPTH, D_MODEL

    def nrm(shape, scale):
        return scale * jax.random.normal(next(ks), shape, jnp.float32)

    def unif(shape, lo, hi):
        return jax.random.uniform(next(ks), shape, jnp.float32, lo, hi)

    return {
        'x': nrm((BATCH, SEQ, D), 1.0),
        'attn_norm_g': 1.0 + nrm((L, D), 0.1),
        'w_in': nrm((L, D, IN_WIDTH), D ** -0.5),
        'rel_bias': nrm((REL_BUCKETS, NSA_HEADS), 0.5),
        'q_norm_g': 1.0 + nrm((L, HEAD_DIM), 0.1),
        'k_norm_g': 1.0 + nrm((L, HEAD_DIM), 0.1),
        'cmp_pe_k': nrm((L, CMP_BLOCK, HEAD_DIM), 0.1),
        'cmp_w1_k': nrm((L, CMP_BLOCK * HEAD_DIM, CMP_HIDDEN), (CMP_BLOCK * HEAD_DIM) ** -0.5),
        'cmp_w2_k': nrm((L, CMP_HIDDEN, HEAD_DIM), CMP_HIDDEN ** -0.5),
        'cmp_pe_v': nrm((L, CMP_BLOCK, HEAD_DIM), 0.1),
        'cmp_w1_v': nrm((L, CMP_BLOCK * HEAD_DIM, CMP_HIDDEN), (CMP_BLOCK * HEAD_DIM) ** -0.5),
        'cmp_w2_v': nrm((L, CMP_HIDDEN, HEAD_DIM), CMP_HIDDEN ** -0.5),
        'rwkv_mu': unif((L, RWKV_IN_WIDTH), 0.0, 1.0),
        'rwkv_w0': unif((L, RWKV_WIDTH), -6.0, -1.0),
        'rwkv_w2': nrm((L, LORA_W, RWKV_WIDTH), 0.5 * LORA_W ** -0.5),
        'rwkv_a0': nrm((L, RWKV_WIDTH), 0.1),
        'rwkv_a2': nrm((L, LORA_A, RWKV_WIDTH), 0.5 * LORA_A ** -0.5),
        'rwkv_g2': nrm((L, LORA_G, RWKV_WIDTH), LORA_G ** -0.5),
        'rwkv_k_k': 0.85 + nrm((L, RWKV_WIDTH), 0.05),
        'rwkv_k_a': 1.0 + nrm((L, RWKV_WIDTH), 0.05),
        'rwkv_r_k': nrm((L, RWKV_HEADS, HEAD_DIM), 0.1),
        'rwkv_ln_g': 1.0 + nrm((L, RWKV_WIDTH), 0.1),
        'rwkv_ln_b': nrm((L, RWKV_WIDTH), 0.01),
        'w_proj_a': nrm((L, NSA_WIDTH, D), NSA_WIDTH ** -0.5),
        'w_proj_b': nrm((L, RWKV_WIDTH, D), RWKV_WIDTH ** -0.5),
        'w_out': nrm((L, D, D), D ** -0.5),
        'ffn_norm_g': 1.0 + nrm((L, D), 0.1),
        'w_up': nrm((L, D, 2 * D_FF), D ** -0.5),
        'conv_w': nrm((L, CONV_WIDTH, 2 * D_FF), CONV_WIDTH ** -0.5),
        'conv_b': nrm((L, 2 * D_FF), 0.01),
        'w_down': nrm((L, D_FF, D), D_FF ** -0.5),
    }


def reference(x, attn_norm_g, w_in, rel_bias, q_norm_g, k_norm_g, cmp_pe_k, cmp_w1_k, cmp_w2_k,
              cmp_pe_v, cmp_w1_v, cmp_w2_v, rwkv_mu, rwkv_w0, rwkv_w2, rwkv_a0, rwkv_a2, rwkv_g2,
              rwkv_k_k, rwkv_k_a, rwkv_r_k, rwkv_ln_g, rwkv_ln_b, w_proj_a, w_proj_b, w_out,
              ffn_norm_g, w_up, conv_w, conv_b, w_down):
    for l in range(DEPTH):
        h = rmsnorm(x, attn_norm_g[l])
        proj = h @ w_in[l]
        q, k_c, v_c, k_s, v_s, k_w, v_w, nsa_gate, rwkv_in, gate_a, gate_b = jnp.split(proj, IN_SPLITS, axis=-1)
        o_a = nsa_mixer(q, k_c, v_c, k_s, v_s, k_w, v_w, nsa_gate, rel_bias, q_norm_g[l], k_norm_g[l],
                        cmp_pe_k[l], cmp_w1_k[l], cmp_w2_k[l], cmp_pe_v[l], cmp_w1_v[l], cmp_w2_v[l])
        o_b = rwkv7_mixer(rwkv_in, rwkv_mu[l], rwkv_w0[l], rwkv_w2[l], rwkv_a0[l], rwkv_a2[l], rwkv_g2[l],
                          rwkv_k_k[l], rwkv_k_a[l], rwkv_r_k[l], rwkv_ln_g[l], rwkv_ln_b[l])
        merged = jax.nn.sigmoid(gate_a) * (o_a @ w_proj_a[l]) + jax.nn.sigmoid(gate_b) * (o_b @ w_proj_b[l])
        x = x + (merged @ w_out[l]).astype(x.dtype)
        h2 = rmsnorm(x, ffn_norm_g[l])
        x = x + conv_ffn(h2, w_up[l], conv_w[l], conv_b[l], w_down[l]).astype(x.dtype)
    return x
```

```python
import functools
import math

import numpy as np
import jax
import jax.numpy as jnp
from jax import lax
from jax.experimental import pallas as pl
from jax.experimental.pallas import tpu as pltpu

F32 = jnp.float32
BF16 = jnp.bfloat16

HEAD_DIM = 64
NSA_HEADS = 8
NSA_KV_GROUPS = 2
NSA_GROUP = NSA_HEADS // NSA_KV_GROUPS
NSA_WIDTH = NSA_HEADS * HEAD_DIM
NSA_KV_WIDTH = NSA_KV_GROUPS * HEAD_DIM
CMP_BLOCK = 32
CMP_STRIDE = 16
CMP_HIDDEN = 256
SLC_BLOCK = 64
SLC_TOPK = 16
OVERLAP_W = (1, 2, 2, 2, 1)
WINDOW = 512
REL_BUCKETS = 32
REL_MAX_DIST = 128
RWKV_HEADS = 8
RWKV_WIDTH = RWKV_HEADS * HEAD_DIM
LORA_W = 64
LORA_A = 64
LORA_G = 128
RWKV_IN_WIDTH = 3 * RWKV_WIDTH + LORA_W + LORA_A + LORA_G
GN_EPS = 64e-5
CONV_WIDTH = 3
RMS_EPS = 1e-6
NEG_INF = -1e30
FORCE = 1e9
M_FLOOR = -1e20

LANES = 128
VMEM_LIMIT = 56 * 1024 * 1024

Q_TILE = 256
CHUNK = 128
HALO = 16

_NT = (((1,), (1,)), ((), ()))


def _params(*sem):
    return pltpu.CompilerParams(dimension_semantics=sem, vmem_limit_bytes=VMEM_LIMIT)


def _mm(a, b):
    return jnp.dot(a.astype(BF16), b.astype(BF16), preferred_element_type=F32)


def _mm_nt(a, b):
    return lax.dot_general(a.astype(BF16), b.astype(BF16), _NT, preferred_element_type=F32)


def _split(a):
    hi = a.astype(BF16)
    lo = (a - hi.astype(F32)).astype(BF16)
    return hi, lo


def _mm3(a, b):
    ah, al = _split(a)
    bh, bl = _split(b)
    d = lambda x, y: jnp.dot(x, y, preferred_element_type=F32)
    return d(ah, bh) + (d(ah, bl) + d(al, bh))


def _mm3_nt(a, b):
    ah, al = _split(a)
    bh, bl = _split(b)
    d = lambda x, y: lax.dot_general(x, y, _NT, preferred_element_type=F32)
    return d(ah, bh) + (d(ah, bl) + d(al, bh))


def _mm_f32(a, b):
    return jnp.dot(a, b, preferred_element_type=F32, precision=lax.Precision.HIGHEST)


def _rms(x, g):
    return x * lax.rsqrt(jnp.mean(x * x, axis=-1, keepdims=True) + RMS_EPS) * g


def _inproj_kernel(x_ref, g_ref, w_ref, *out_refs):
    h = _rms(x_ref[...], g_ref[...]).astype(BF16)
    off = 0
    for ref in out_refs:
        n = ref.shape[-1]
        ref[...] = jnp.dot(h, w_ref[:, off:off + n], preferred_element_type=F32).astype(ref.dtype)
        off += n


def _inproj(x2, g, w, widths, tm):
    n, d = x2.shape
    return pl.pallas_call(
        _inproj_kernel,
        grid=(n // tm,),
        in_specs=[pl.BlockSpec((tm, d), lambda i: (i, 0)),
                  pl.BlockSpec((1, d), lambda i: (0, 0)),
                  pl.BlockSpec(w.shape, lambda i: (0, 0))],
        out_specs=[pl.BlockSpec((tm, wd), lambda i: (i, 0)) for wd in widths],
        out_shape=[jax.ShapeDtypeStruct((n, wd), F32) for wd in widths],
        compiler_params=_params("parallel"),
        name="inproj",
    )(x2, g, w)


def _kvprep_kernel(seq, kv_ref, gk_ref, ksx_ref, kwn_ref, vs_ref, vw_ref):
    tm = kv_ref.shape[0]
    kv = kv_ref[...]
    lane = lax.broadcasted_iota(jnp.int32, (tm, LANES), 1)
    lo = lane < HEAD_DIM

    def norm2(x):
        x2 = x * x
        s_lo = jnp.sum(jnp.where(lo, x2, 0.0), axis=-1, keepdims=True)
        s_hi = jnp.sum(jnp.where(lo, 0.0, x2), axis=-1, keepdims=True)
        ms = jnp.where(lo, s_lo, s_hi) * (1.0 / HEAD_DIM)
        return x * lax.rsqrt(ms + RMS_EPS) * gk_ref[...]

    ks = norm2(kv[:, 2 * LANES:3 * LANES])
    kw = norm2(kv[:, 4 * LANES:5 * LANES])
    row = lax.rem(pl.program_id(0) * tm, seq) + lax.broadcasted_iota(jnp.int32, (tm, LANES), 0)
    onehot = jnp.where(lane - HEAD_DIM == row // SLC_BLOCK, 1.0, 0.0)
    ksx_ref[0] = jnp.where(lo, ks, onehot).astype(BF16)
    ksx_ref[1] = jnp.where(lo, pltpu.roll(ks, HEAD_DIM, axis=1), onehot).astype(BF16)
    kwn_ref[0] = kw[:, :HEAD_DIM].astype(BF16)
    kwn_ref[1] = kw[:, HEAD_DIM:].astype(BF16)
    vs = kv[:, 3 * LANES:4 * LANES]
    vs_ref[0] = vs[:, :HEAD_DIM].astype(BF16)
    vs_ref[1] = vs[:, HEAD_DIM:].astype(BF16)
    vw = kv[:, 5 * LANES:6 * LANES]
    vw_ref[0] = vw[:, :HEAD_DIM].astype(BF16)
    vw_ref[1] = vw[:, HEAD_DIM:].astype(BF16)


def _kvprep(kv, gk2, seq, tm):
    n = kv.shape[0]
    half = lambda: pl.BlockSpec((2, tm, HEAD_DIM), lambda i: (0, i, 0))
    return pl.pallas_call(
        functools.partial(_kvprep_kernel, seq),
        grid=(n // tm,),
        in_specs=[pl.BlockSpec((tm, kv.shape[1]), lambda i: (i, 0)),
                  pl.BlockSpec((1, LANES), lambda i: (0, 0))],
        out_specs=[pl.BlockSpec((2, tm, LANES), lambda i: (0, i, 0)), half(), half(), half()],
        out_shape=[jax.ShapeDtypeStruct((2, n, LANES), BF16)] + [jax.ShapeDtypeStruct((2, n, HEAD_DIM), BF16)] * 3,
        compiler_params=_params("parallel"),
        name="kvprep",
    )(kv, gk2)


def _compress_kernel(c_ref, pe_ref, w1_ref, w2_ref, gk_ref, o_ref):
    half = CMP_STRIDE * HEAD_DIM
    c = c_ref[0, 0, 0]
    pe = pe_ref[0]
    a = _mm(c + pe[:, :half], w1_ref[0, :half, :])
    b = _mm(c + pe[:, half:], w1_ref[0, half:, :])
    nblk = c.shape[0]
    hid = a + pltpu.roll(b, nblk - 1, axis=0)
    out = _mm(jax.nn.gelu(hid), w2_ref[0])
    is_key = pl.program_id(0) == 0
    o_ref[0, 0, 0] = jnp.where(is_key, _rms(out, gk_ref[...]), out)


def _compress(cflat, pe, w1, w2, gk):
    _, b, g, nblk, width = cflat.shape
    return pl.pallas_call(
        _compress_kernel,
        grid=(2, b, g),
        in_specs=[pl.BlockSpec((1, 1, 1, nblk, width), lambda s, i, j: (s, i, j, 0, 0)),
                  pl.BlockSpec((1, 1, 2 * width), lambda s, i, j: (s, 0, 0)),
                  pl.BlockSpec((1, 2 * width, CMP_HIDDEN), lambda s, i, j: (s, 0, 0)),
                  pl.BlockSpec((1, CMP_HIDDEN, HEAD_DIM), lambda s, i, j: (s, 0, 0)),
                  pl.BlockSpec((1, HEAD_DIM), lambda s, i, j: (0, 0))],
        out_specs=pl.BlockSpec((1, 1, 1, nblk, HEAD_DIM), lambda s, i, j: (s, i, j, 0, 0)),
        out_shape=jax.ShapeDtypeStruct((2, b, g, nblk, HEAD_DIM), F32),
        compiler_params=_params("parallel", "parallel", "parallel"),
        name="compress",
    )(cflat, pe, w1, w2, gk)


def _flash_step(state, qx, k, v, bias):
    m, l, acc = state
    s = lax.dot_general(qx, k, _NT, preferred_element_type=F32)
    if bias is not None:
        s = (s.reshape(-1, bias.shape[-2], s.shape[-1]) + bias).reshape(s.shape)
    m_new = jnp.maximum(m, jnp.max(s, axis=-1, keepdims=True))
    alpha = jnp.exp(m - m_new)
    p = jnp.exp(s - m_new)
    l = alpha * l + jnp.sum(p, axis=-1, keepdims=True)
    acc = alpha * acc + jnp.dot(p.astype(BF16), v, preferred_element_type=F32)
    return m_new, l, acc


def _rank_select(score_t, cur_t):
    nb = score_t.shape[0]
    jj = lax.broadcasted_iota(jnp.int32, score_t.shape, 0)
    rank = jnp.zeros(score_t.shape, F32)
    for j in range(nb):
        row = score_t[j:j + 1, :]
        gt = jnp.where(row > score_t, 1.0, 0.0)
        ge = jnp.where(row >= score_t, 1.0, 0.0)
        rank = rank + jnp.where(jj > j, ge, gt)
    keep = jnp.where(rank < float(min(SLC_TOPK, nb)), 1.0, 0.0) * jnp.where(jj <= cur_t, 1.0, 0.0)
    return jnp.where(keep > 0.5, 0.0, NEG_INF)


def _attn_kernel(q_ref, gn_ref, kcv_ref, ksx_ref, vs_ref, kwn_ref, vw_ref, bc_ref, t0_ref, t1_ref, t2_ref,
                 gq_ref, wov_ref, o_ref):
    tq = q_ref.shape[0]
    nblk = ksx_ref.shape[1] // SLC_BLOCK
    qt = pl.program_id(1)
    q = q_ref[...]
    gates = jax.nn.sigmoid(gn_ref[...])
    R = NSA_GROUP
    kprev = jnp.maximum(qt - 1, 0)
    kprev2 = jnp.maximum(qt - 2, 0)
    neg1 = jnp.where(qt < 1, NEG_INF, 0.0)
    neg2 = jnp.where(qt < 2, NEG_INF, 0.0)
    t2 = t2_ref[...]
    outs = []
    for g in range(NSA_KV_GROUPS):
        qall = jnp.concatenate(
            [(_rms(q[:, h * HEAD_DIM:(h + 1) * HEAD_DIM], gq_ref[...]) * HEAD_DIM ** -0.5).astype(BF16)
             for h in range(g * R, (g + 1) * R)], axis=0)

        kc = kcv_ref[0, 0, g].astype(BF16)
        vc = kcv_ref[1, 0, g].astype(BF16)
        s = lax.dot_general(qall, kc, _NT, preferred_element_type=F32)
        s = (s.reshape(R, tq, -1) + bc_ref[g * R:(g + 1) * R]).reshape(s.shape)
        m = jnp.maximum(jnp.max(s, axis=-1, keepdims=True), M_FLOOR)
        e = jnp.exp(s - m)
        p = e / jnp.maximum(jnp.sum(e, axis=-1, keepdims=True), 1e-30)
        o_cmp = jnp.dot(p.astype(BF16), vc, preferred_element_type=F32)
        psum = p[0:tq]
        for r in range(1, R):
            psum = psum + p[r * tq:(r + 1) * tq]
        imp = _mm_f32(psum, wov_ref[...])

        blk = lax.broadcasted_iota(jnp.int32, imp.shape, 1)
        cur = (qt * tq + lax.broadcasted_iota(jnp.int32, imp.shape, 0)) // SLC_BLOCK
        forced = (blk == 0) | (blk == cur) | (blk == cur - 1)
        score = jnp.where(forced, FORCE, jnp.where(blk <= cur, imp, -FORCE))
        cur_t = (qt * tq + lax.broadcasted_iota(jnp.int32, (nblk, tq), 1)) // SLC_BLOCK
        neg_t = _rank_select(score.T[:nblk], cur_t)
        neg_t = jnp.concatenate([neg_t, jnp.full((LANES - nblk, tq), NEG_INF, F32)], axis=0)
        neg = neg_t.T[:, :LANES - HEAD_DIM].astype(BF16)

        qx = jnp.concatenate([qall, jnp.concatenate([neg] * R, axis=0)], axis=1)
        vs_g = vs_ref.at[g]
        ksx_g = ksx_ref.at[g]
        init = (jnp.full((R * tq, 1), M_FLOOR, F32), jnp.zeros((R * tq, 1), F32), jnp.zeros((R * tq, HEAD_DIM), F32))

        def far_step(kt, st):
            sl = pl.ds(pl.multiple_of(kt * tq, tq), tq)
            return _flash_step(st, qx, ksx_g[sl, :], vs_g[sl, :], None)

        st = lax.fori_loop(0, kprev, far_step, init)
        b0 = t0_ref[g * R:(g + 1) * R]
        b1 = t1_ref[g * R:(g + 1) * R]
        sl1 = pl.ds(pl.multiple_of(kprev * tq, tq), tq)
        sl0 = pl.ds(pl.multiple_of(qt * tq, tq), tq)
        st = _flash_step(st, qx, ksx_g[sl1, :], vs_g[sl1, :], b1 + neg1)
        _, l, acc = _flash_step(st, qx, ksx_g[sl0, :], vs_g[sl0, :], b0)
        o_slc = acc / l

        kw_g = kwn_ref.at[g]
        vw_g = vw_ref.at[g]
        sl2 = pl.ds(pl.multiple_of(kprev2 * tq, tq), tq)
        st = _flash_step(init, qall, kw_g[sl2, :], vw_g[sl2, :], (t2 + neg2)[None])
        st = _flash_step(st, qall, kw_g[sl1, :], vw_g[sl1, :], b1 + neg1)
        _, l, acc = _flash_step(st, qall, kw_g[sl0, :], vw_g[sl0, :], b0)
        o_win = acc / l

        for r in range(R):
            h = g * R + r
            rows = slice(r * tq, (r + 1) * tq)
            outs.append(gates[:, 3 * h:3 * h + 1] * o_cmp[rows]
                        + gates[:, 3 * h + 1:3 * h + 2] * o_slc[rows]
                        + gates[:, 3 * h + 2:3 * h + 3] * o_win[rows])
    o_ref[...] = jnp.concatenate(outs, axis=1).astype(o_ref.dtype)


def _attention(q, gn, kcv, ksx, vs, kwn, vw, bias_c, t0, t1, t2, gq, wov, batch, seq):
    n = q.shape[0]
    tq = Q_TILE
    nq = seq // tq
    ncmp = kcv.shape[3]
    row = lambda b, i: (b * nq + i, 0)
    whole = lambda b, i: (0, b, 0)
    const3 = lambda b, i: (0, 0, 0)
    return pl.pallas_call(
        _attn_kernel,
        grid=(batch, nq),
        in_specs=[pl.BlockSpec((tq, NSA_WIDTH), row),
                  pl.BlockSpec((tq, LANES), row),
                  pl.BlockSpec((2, 1, NSA_KV_GROUPS, ncmp, HEAD_DIM), lambda b, i: (0, b, 0, 0, 0)),
                  pl.BlockSpec((2, seq, LANES), whole),
                  pl.BlockSpec((2, seq, HEAD_DIM), whole),
                  pl.BlockSpec((2, seq, HEAD_DIM), whole),
                  pl.BlockSpec((2, seq, HEAD_DIM), whole),
                  pl.BlockSpec((NSA_HEADS, tq, ncmp), lambda b, i: (0, i, 0)),
                  pl.BlockSpec((NSA_HEADS, tq, tq), const3),
                  pl.BlockSpec((NSA_HEADS, tq, tq), const3),
                  pl.BlockSpec((tq, tq), lambda b, i: (0, 0)),
                  pl.BlockSpec((1, HEAD_DIM), lambda b, i: (0, 0)),
                  pl.BlockSpec(wov.shape, lambda b, i: (0, 0))],
        out_specs=pl.BlockSpec((tq, NSA_WIDTH), row),
        out_shape=jax.ShapeDtypeStruct((n, NSA_WIDTH), BF16),
        compiler_params=_params("parallel", "parallel"),
        name="nsa_attention",
    )(q, gn, kcv, ksx, vs, kwn, vw, bias_c, t0, t1, t2, gq, wov)


def _softplus(z):
    return jnp.maximum(z, 0.0) + jnp.log(1.0 + jnp.exp(-jnp.abs(z)))


def _rwkv_kernel(x_ref, xp_ref, mu_ref, w0_ref, w2_ref, a0_ref, a2_ref, g2_ref, kk_ref, ka_ref, rk_ref,
                 lng_ref, lnb_ref, o_ref, st_ref):
    L = x_ref.shape[0]
    W = RWKV_WIDTH
    N = HEAD_DIM
    c = pl.program_id(1)

    @pl.when(c == 0)
    def _():
        st_ref[...] = jnp.zeros(st_ref.shape, F32)

    x = x_ref[...]
    prev = jnp.where(c > 0, xp_ref[xp_ref.shape[0] - 1:, :], 0.0)
    row_id = lax.broadcasted_iota(jnp.int32, x.shape, 0)
    shifted = jnp.where(row_id == 0, prev, pltpu.roll(x, 1, axis=0))
    xl = x + (shifted - x) * mu_ref[...]
    r = xl[:, 0:W]
    k = xl[:, W:2 * W]
    v = xl[:, 2 * W:3 * W]
    xw = xl[:, 3 * W:3 * W + LORA_W]
    xa = xl[:, 3 * W + LORA_W:3 * W + LORA_W + LORA_A]
    xg = xl[:, 3 * W + LORA_W + LORA_A:]
    w = -_softplus(-(w0_ref[...] + _mm(jnp.tanh(xw), w2_ref[...]))) - 0.5
    ld = -jnp.exp(w)
    a = jax.nn.sigmoid(a0_ref[...] + _mm(xa, a2_ref[...]))
    gate = _mm(jax.nn.sigmoid(xg), g2_ref[...])
    kkv = k * kk_ref[...]
    k2 = k * (1.0 + (a - 1.0) * ka_ref[...])

    ti = lax.broadcasted_iota(jnp.int32, (L, L), 0)
    si = lax.broadcasted_iota(jnp.int32, (L, L), 1)
    incl = si <= ti
    strict = si < ti
    cl = _mm_f32(jnp.where(incl, 1.0, 0.0), ld)
    cl_end = cl[L - 1:L, :]
    e_pos = jnp.exp(cl)
    e_neg = jnp.exp(-cl)
    e_prev = jnp.exp(cl - ld)
    e_end = jnp.exp(cl_end - cl)
    r_t = r * e_pos
    k_t = k2 * e_neg
    k_e = k2 * e_end
    eye = jnp.where(ti == si, 1.0, 0.0)
    pl_cols = [e_pos[:, m * LANES:(m + 1) * LANES].T[:, L - 1:L] for m in range(W // LANES)]

    outs = []
    for h in range(RWKV_HEADS):
        hs = slice(h * N, (h + 1) * N)
        kk_h = kkv[:, hs]
        kk_h = kk_h / jnp.maximum(jnp.sqrt(jnp.sum(kk_h * kk_h, axis=-1, keepdims=True)), 1e-12)
        a_t = -kk_h * e_prev[:, hs]
        bv = kk_h * a[:, hs]
        b_t = bv * e_neg[:, hs]
        b_e = bv * e_end[:, hs]
        v_h = v[:, hs]
        rt_h = r_t[:, hs]

        aa = _mm3_nt(jnp.concatenate([a_t, rt_h], axis=0), jnp.concatenate([k_t[:, hs], b_t], axis=0))
        a_ak = jnp.where(strict, aa[:L, :L], 0.0)
        a_ab = jnp.where(strict, aa[:L, L:], 0.0)
        a_rk = jnp.where(incl, aa[L:, :L], 0.0)
        a_rb = jnp.where(incl, aa[L:, L:], 0.0)

        tinv = eye + a_ab
        pw = _mm3(a_ab, a_ab)
        span = 2
        while span < L:
            both = _mm3(jnp.concatenate([tinv, pw], axis=0), pw)
            tinv = tinv + both[:L]
            pw = both[L:]
            span *= 2

        av = _mm3(jnp.concatenate([a_ak, a_rk], axis=0), v_h)
        tw = _mm3(tinv, jnp.concatenate([a_t, av[:L]], axis=1))
        kb_t = jnp.concatenate([k_e[:, hs], b_e], axis=1).T
        kv_loc = _mm3(kb_t[:N], v_h)

        st = st_ref[h]
        ws = _mm3(jnp.concatenate([tw[:, :N], rt_h], axis=0), st)
        u = ws[:L] + tw[:, N:]
        y = ws[L:] + av[L:] + _mm3(a_rb, u)
        pcol = pl_cols[(h * N) // LANES][(h * N) % LANES:(h * N) % LANES + N, :]
        st_ref[h] = pcol * st + kv_loc + _mm3(kb_t[N:], u)

        mean = jnp.mean(y, axis=-1, keepdims=True)
        var = jnp.mean(jnp.square(y - mean), axis=-1, keepdims=True)
        y = (y - mean) * lax.rsqrt(var + GN_EPS) * lng_ref[:, hs] + lnb_ref[:, hs]
        y = y + jnp.sum(r[:, hs] * k2[:, hs] * rk_ref[:, hs], axis=-1, keepdims=True) * v_h
        outs.append(y)
    o_ref[...] = (jnp.concatenate(outs, axis=1) * gate).astype(o_ref.dtype)


def _rwkv(rw, vecs, w2, a2, g2, batch, seq):
    n, width = rw.shape
    L = min(CHUNK, seq)
    nc = seq // L
    sub = 8
    vec = lambda wd: pl.BlockSpec((1, wd), lambda b, c: (0, 0))
    mat = lambda m: pl.BlockSpec(m.shape, lambda b, c: (0, 0))
    mu, w0, a0, kk, ka, rk, lng, lnb = vecs
    return pl.pallas_call(
        _rwkv_kernel,
        grid=(batch, nc),
        in_specs=[pl.BlockSpec((L, width), lambda b, c: (b * nc + c, 0)),
                  pl.BlockSpec((sub, width), lambda b, c: (jnp.maximum((b * nc + c) * (L // sub) - 1, 0), 0)),
                  vec(width), vec(RWKV_WIDTH), mat(w2), vec(RWKV_WIDTH), mat(a2), mat(g2),
                  vec(RWKV_WIDTH), vec(RWKV_WIDTH), vec(RWKV_WIDTH), vec(RWKV_WIDTH), vec(RWKV_WIDTH)],
        out_specs=pl.BlockSpec((L, RWKV_WIDTH), lambda b, c: (b * nc + c, 0)),
        out_shape=jax.ShapeDtypeStruct((n, RWKV_WIDTH), BF16),
        scratch_shapes=[pltpu.VMEM((RWKV_HEADS, HEAD_DIM, HEAD_DIM), F32)],
        compiler_params=_params("parallel", "arbitrary"),
        name="rwkv7",
    )(rw, rw, mu, w0, w2, a0, a2, g2, kk, ka, rk, lng, lnb)


def _merge_kernel(x_ref, oa_ref, ob_ref, gab_ref, wpa_ref, wpb_ref, wo_ref, o_ref):
    d = x_ref.shape[1]
    pa = jnp.dot(oa_ref[...], wpa_ref[...], preferred_element_type=F32)
    pb = jnp.dot(ob_ref[...], wpb_ref[...], preferred_element_type=F32)
    gab = gab_ref[...]
    merged = jax.nn.sigmoid(gab[:, :d]) * pa + jax.nn.sigmoid(gab[:, d:]) * pb
    o_ref[...] = x_ref[...] + _mm(merged, wo_ref[...])


def _merge(x2, oa, ob, gab, wpa, wpb, wo, tm):
    n, d = x2.shape
    row = lambda wd: pl.BlockSpec((tm, wd), lambda i: (i, 0))
    mat = lambda m: pl.BlockSpec(m.shape, lambda i: (0, 0))
    return pl.pallas_call(
        _merge_kernel,
        grid=(n // tm,),
        in_specs=[row(d), row(oa.shape[1]), row(ob.shape[1]), row(gab.shape[1]), mat(wpa), mat(wpb), mat(wo)],
        out_specs=row(d),
        out_shape=jax.ShapeDtypeStruct((n, d), F32),
        compiler_params=_params("parallel"),
        name="merge",
    )(x2, oa, ob, gab, wpa, wpb, wo)


def _ffn_kernel(seq, x_ref, xh_ref, g_ref, wv_ref, wg_ref, cwv_ref, cwg_ref, cbv_ref, cbg_ref, wd_ref, o_ref,
                h_ref, uv_ref, ug_ref):
    tm = x_ref.shape[0]
    i = pl.program_id(0)
    f = pl.program_id(1)

    @pl.when(f == 0)
    def _():
        first = lax.rem(i * tm, seq) == 0
        halo = jnp.where(first, 0.0, _rms(xh_ref[...], g_ref[...]))
        h_ref[0:HALO, :] = halo.astype(BF16)
        h_ref[HALO:, :] = _rms(x_ref[...], g_ref[...]).astype(BF16)

    h = h_ref[...]
    uv_ref[...] = jnp.dot(h, wv_ref[...], preferred_element_type=F32)
    ug_ref[...] = jnp.dot(h, wg_ref[...], preferred_element_type=F32)

    def conv(u_ref, cw_ref, cb_ref):
        acc = cb_ref[...] + cw_ref[0:1, :] * u_ref[pl.ds(HALO - 2, tm), :]
        acc = acc + cw_ref[1:2, :] * u_ref[pl.ds(HALO - 1, tm), :]
        return acc + cw_ref[2:3, :] * u_ref[pl.ds(HALO, tm), :]

    val = conv(uv_ref, cwv_ref, cbv_ref)
    gt = conv(ug_ref, cwg_ref, cbg_ref)
    y = _mm(gt * jax.nn.sigmoid(gt) * val, wd_ref[...])

    @pl.when(f == 0)
    def _():
        o_ref[...] = x_ref[...] + y

    @pl.when(f > 0)
    def _():
        o_ref[...] = o_ref[...] + y


def _ffn(x1, g, w_up, conv_w, conv_b, w_down, seq, tm, tf):
    n, d = x1.shape
    dff = w_down.shape[0]
    nf = dff // tf
    return pl.pallas_call(
        functools.partial(_ffn_kernel, seq),
        grid=(n // tm, nf),
        in_specs=[pl.BlockSpec((tm, d), lambda i, f: (i, 0)),
                  pl.BlockSpec((HALO, d), lambda i, f: (jnp.maximum(i * (tm // HALO) - 1, 0), 0)),
                  pl.BlockSpec((1, d), lambda i, f: (0, 0)),
                  pl.BlockSpec((d, tf), lambda i, f: (0, f)),
                  pl.BlockSpec((d, tf), lambda i, f: (0, nf + f)),
                  pl.BlockSpec((CONV_WIDTH, tf), lambda i, f: (0, f)),
                  pl.BlockSpec((CONV_WIDTH, tf), lambda i, f: (0, nf + f)),
                  pl.BlockSpec((1, tf), lambda i, f: (0, f)),
                  pl.BlockSpec((1, tf), lambda i, f: (0, nf + f)),
                  pl.BlockSpec((tf, d), lambda i, f: (f, 0))],
        out_specs=pl.BlockSpec((tm, d), lambda i, f: (i, 0)),
        out_shape=jax.ShapeDtypeStruct((n, d), F32),
        scratch_shapes=[pltpu.VMEM((tm + HALO, d), BF16),
                        pltpu.VMEM((tm + HALO, tf), F32),
                        pltpu.VMEM((tm + HALO, tf), F32)],
        compiler_params=_params("parallel", "arbitrary"),
        name="convffn",
    )(x1, x1, g, w_up, w_up, conv_w, conv_w, conv_b, conv_b, w_down)


def _t5_bucket(dist):
    n = jnp.maximum(dist, 0)
    max_exact = REL_BUCKETS // 2
    ratio = jnp.log(jnp.maximum(n, 1).astype(F32) / max_exact) / math.log(REL_MAX_DIST / max_exact)
    large = jnp.minimum(max_exact + (ratio * (REL_BUCKETS - max_exact)).astype(jnp.int32), REL_BUCKETS - 1)
    return jnp.where(n < max_exact, n, large)


def _bias_tables(rel_bias, seq, ncmp_pad):
    tq = Q_TILE
    tab = rel_bias.astype(F32)
    look = lambda dist: jnp.moveaxis(tab[_t5_bucket(dist)], -1, 0)
    t = jnp.arange(seq)[:, None]
    c = jnp.arange(ncmp_pad)[None, :]
    dist_c = t - (c * CMP_STRIDE + CMP_BLOCK - 1)
    n_cmp = seq // CMP_STRIDE - CMP_BLOCK // CMP_STRIDE + 1
    bias_c = jnp.where((dist_c >= 0) & (c < n_cmp), look(dist_c), NEG_INF)
    far = tab[REL_BUCKETS - 1][:, None, None]
    d0 = jnp.arange(tq)[:, None] - jnp.arange(tq)[None, :]
    t0 = jnp.where(d0 >= 0, look(d0) - far, NEG_INF)
    t1 = look(d0 + tq) - far
    t2 = jnp.where(d0 + 2 * tq < WINDOW, 0.0, NEG_INF).astype(F32)
    return bias_c, t0, t1, t2


def _overlap_matrix(ncmp_pad, n_slc, n_cmp):
    m = np.zeros((ncmp_pad, LANES), np.float32)
    ratio = SLC_BLOCK // CMP_STRIDE
    for j in range(n_slc):
        for o, wgt in enumerate(OVERLAP_W):
            cidx = ratio * j + o - (CMP_BLOCK // CMP_STRIDE - 1)
            if 0 <= cidx < n_cmp:
                m[cidx, j] += wgt
    return jnp.asarray(m)


def _layer(x, attn_norm_g, w_in, rel_bias, q_norm_g, k_norm_g, cmp_pe_k, cmp_w1_k, cmp_w2_k,
           cmp_pe_v, cmp_w1_v, cmp_w2_v, rwkv_mu, rwkv_w0, rwkv_w2, rwkv_a0, rwkv_a2, rwkv_g2,
           rwkv_k_k, rwkv_k_a, rwkv_r_k, rwkv_ln_g, rwkv_ln_b, w_proj_a, w_proj_b, w_out,
           ffn_norm_g, w_up, conv_w, conv_b, w_down):
    batch, seq, d = x.shape
    n = batch * seq
    assert seq % Q_TILE == 0 and WINDOW == 2 * Q_TILE and seq % CMP_STRIDE == 0
    assert seq // SLC_BLOCK <= LANES - HEAD_DIM
    x2 = x.reshape(n, d)
    row = lambda a: a.reshape(1, -1).astype(F32)

    kvw = 6 * NSA_KV_WIDTH
    o_q, o_kv, o_gn = 0, NSA_WIDTH, NSA_WIDTH + kvw
    o_rw = o_gn + 3 * NSA_HEADS
    o_gab = o_rw + RWKV_IN_WIDTH
    gn_pad = jnp.zeros((d, LANES - 3 * NSA_HEADS), w_in.dtype)
    w_cat = jnp.concatenate([w_in[:, o_rw:o_gab], w_in[:, o_q:o_kv], w_in[:, o_kv:o_gn], w_in[:, o_gab:],
                             w_in[:, o_gn:o_rw], gn_pad], axis=1).astype(BF16)
    widths = (RWKV_IN_WIDTH, NSA_WIDTH, kvw, 2 * d, LANES)
    rw, q, kv, gab, gn = _inproj(x2, row(attn_norm_g), w_cat, widths, tm=256)

    gk = row(k_norm_g)
    ksx, kwn, vs, vw = _kvprep(kv, jnp.concatenate([gk, gk], axis=1), seq, tm=512)

    nchunk = seq // CMP_STRIDE
    n_cmp = nchunk - CMP_BLOCK // CMP_STRIDE + 1
    cflat = kv[:, :2 * NSA_KV_WIDTH].reshape(batch, nchunk, CMP_STRIDE, 2, NSA_KV_GROUPS, HEAD_DIM)
    cflat = cflat.transpose(3, 0, 4, 1, 2, 5).reshape(2, batch, NSA_KV_GROUPS, nchunk, CMP_STRIDE * HEAD_DIM)
    pe = jnp.stack([cmp_pe_k.reshape(1, -1), cmp_pe_v.reshape(1, -1)]).astype(F32)
    w1 = jnp.stack([cmp_w1_k, cmp_w1_v]).astype(BF16)
    w2 = jnp.stack([cmp_w2_k, cmp_w2_v]).astype(BF16)
    kcv = _compress(cflat, pe, w1, w2, gk)

    bias_c, t0, t1, t2 = _bias_tables(rel_bias, seq, nchunk)
    wov = _overlap_matrix(nchunk, seq // SLC_BLOCK, n_cmp)
    o_a = _attention(q, gn, kcv, ksx, vs, kwn, vw, bias_c, t0, t1, t2, row(q_norm_g), wov, batch, seq)

    vecs = tuple(row(a) for a in (rwkv_mu, rwkv_w0, rwkv_a0, rwkv_k_k, rwkv_k_a, rwkv_r_k, rwkv_ln_g, rwkv_ln_b))
    o_b = _rwkv(rw, vecs, rwkv_w2.astype(BF16), rwkv_a2.astype(BF16), rwkv_g2.astype(BF16), batch, seq)

    x1 = _merge(x2, o_a, o_b, gab, w_proj_a.astype(BF16), w_proj_b.astype(BF16), w_out.astype(BF16), tm=512)
    dff = w_down.shape[0]
    out = _ffn(x1, row(ffn_norm_g), w_up.astype(BF16), conv_w.astype(F32), row(conv_b), w_down.astype(BF16),
               seq, tm=512, tf=dff // 2)
    return out.reshape(batch, seq, d)


def kernel(x, attn_norm_g, w_in, rel_bias, q_norm_g, k_norm_g, cmp_pe_k, cmp_w1_k, cmp_w2_k, cmp_pe_v, cmp_w1_v,
           cmp_w2_v, rwkv_mu, rwkv_w0, rwkv_w2, rwkv_a0, rwkv_a2, rwkv_g2, rwkv_k_k, rwkv_k_a, rwkv_r_k,
           rwkv_ln_g, rwkv_ln_b, w_proj_a, w_proj_b, w_out, ffn_norm_g, w_up, conv_w, conv_b, w_down):
    per_layer = (attn_norm_g, w_in, None, q_norm_g, k_norm_g, cmp_pe_k, cmp_w1_k, cmp_w2_k, cmp_pe_v, cmp_w1_v,
                 cmp_w2_v, rwkv_mu, rwkv_w0, rwkv_w2, rwkv_a0, rwkv_a2, rwkv_g2, rwkv_k_k, rwkv_k_a, rwkv_r_k,
                 rwkv_ln_g, rwkv_ln_b, w_proj_a, w_proj_b, w_out, ffn_norm_g, w_up, conv_w, conv_b, w_down)
    for l in range(attn_norm_g.shape[0]):
        args = [rel_bias if p is None else p[l] for p in per_layer]
        x = _layer(x, *args)
    return x
```

```python
import functools
import math

import numpy as np
import jax
import jax.numpy as jnp
from jax import lax
from jax.experimental import pallas as pl
from jax.experimental.pallas import tpu as pltpu

F32 = jnp.float32
BF16 = jnp.bfloat16

HEAD_DIM = 64
NSA_HEADS = 8
NSA_KV_GROUPS = 2
NSA_GROUP = NSA_HEADS // NSA_KV_GROUPS
NSA_WIDTH = NSA_HEADS * HEAD_DIM
NSA_KV_WIDTH = NSA_KV_GROUPS * HEAD_DIM
CMP_BLOCK = 32
CMP_STRIDE = 16
CMP_HIDDEN = 256
SLC_BLOCK = 64
SLC_TOPK = 16
OVERLAP_W = (1, 2, 2, 2, 1)
WINDOW = 512
REL_BUCKETS = 32
REL_MAX_DIST = 128
RWKV_HEADS = 8
RWKV_WIDTH = RWKV_HEADS * HEAD_DIM
LORA_W = 64
LORA_A = 64
LORA_G = 128
RWKV_IN_WIDTH = 3 * RWKV_WIDTH + LORA_W + LORA_A + LORA_G
GN_EPS = 64e-5
CONV_WIDTH = 3
RMS_EPS = 1e-6
NEG_INF = -1e30
FORCE = 1e9
M_FLOOR = -1e20

LANES = 128
VMEM_LIMIT = 56 * 1024 * 1024

Q_TILE = 256
CHUNK = 128
HALO = 16

_NT = (((1,), (1,)), ((), ()))


def _params(*sem):
    return pltpu.CompilerParams(dimension_semantics=sem, vmem_limit_bytes=VMEM_LIMIT)


def _mm(a, b):
    return jnp.dot(a.astype(BF16), b.astype(BF16), preferred_element_type=F32)


def _mm_nt(a, b):
    return lax.dot_general(a.astype(BF16), b.astype(BF16), _NT, preferred_element_type=F32)


def _mm_f32(a, b):
    return jnp.dot(a, b, preferred_element_type=F32, precision=lax.Precision.HIGHEST)


def _rms(x, g):
    return x * lax.rsqrt(jnp.mean(x * x, axis=-1, keepdims=True) + RMS_EPS) * g


def _inproj_kernel(x_ref, g_ref, w_ref, *out_refs):
    h = _rms(x_ref[...], g_ref[...]).astype(BF16)
    off = 0
    for ref in out_refs:
        n = ref.shape[-1]
        ref[...] = jnp.dot(h, w_ref[:, off:off + n], preferred_element_type=F32).astype(ref.dtype)
        off += n


def _inproj(x2, g, w, widths, tm):
    n, d = x2.shape
    return pl.pallas_call(
        _inproj_kernel,
        grid=(n // tm,),
        in_specs=[pl.BlockSpec((tm, d), lambda i: (i, 0)),
                  pl.BlockSpec((1, d), lambda i: (0, 0)),
                  pl.BlockSpec(w.shape, lambda i: (0, 0))],
        out_specs=[pl.BlockSpec((tm, wd), lambda i: (i, 0)) for wd in widths],
        out_shape=[jax.ShapeDtypeStruct((n, wd), F32) for wd in widths],
        compiler_params=_params("parallel"),
        name="inproj",
    )(x2, g, w)


def _kvprep_kernel(seq, kv_ref, gk_ref, ksx_ref, kwn_ref, vs_ref, vw_ref):
    tm = kv_ref.shape[0]
    kv = kv_ref[...]
    lane = lax.broadcasted_iota(jnp.int32, (tm, LANES), 1)
    lo = lane < HEAD_DIM

    def norm2(x):
        x2 = x * x
        s_lo = jnp.sum(jnp.where(lo, x2, 0.0), axis=-1, keepdims=True)
        s_hi = jnp.sum(jnp.where(lo, 0.0, x2), axis=-1, keepdims=True)
        ms = jnp.where(lo, s_lo, s_hi) * (1.0 / HEAD_DIM)
        return x * lax.rsqrt(ms + RMS_EPS) * gk_ref[...]

    ks = norm2(kv[:, 2 * LANES:3 * LANES])
    kw = norm2(kv[:, 4 * LANES:5 * LANES])
    row = lax.rem(pl.program_id(0) * tm, seq) + lax.broadcasted_iota(jnp.int32, (tm, LANES), 0)
    onehot = jnp.where(lane - HEAD_DIM == row // SLC_BLOCK, 1.0, 0.0)
    ksx_ref[0] = jnp.where(lo, ks, onehot).astype(BF16)
    ksx_ref[1] = jnp.where(lo, pltpu.roll(ks, HEAD_DIM, axis=1), onehot).astype(BF16)
    kwn_ref[0] = kw[:, :HEAD_DIM].astype(BF16)
    kwn_ref[1] = kw[:, HEAD_DIM:].astype(BF16)
    vs = kv[:, 3 * LANES:4 * LANES]
    vs_ref[0] = vs[:, :HEAD_DIM].astype(BF16)
    vs_ref[1] = vs[:, HEAD_DIM:].astype(BF16)
    vw = kv[:, 5 * LANES:6 * LANES]
    vw_ref[0] = vw[:, :HEAD_DIM].astype(BF16)
    vw_ref[1] = vw[:, HEAD_DIM:].astype(BF16)


def _kvprep(kv, gk2, seq, tm):
    n = kv.shape[0]
    half = lambda: pl.BlockSpec((2, tm, HEAD_DIM), lambda i: (0, i, 0))
    return pl.pallas_call(
        functools.partial(_kvprep_kernel, seq),
        grid=(n // tm,),
        in_specs=[pl.BlockSpec((tm, kv.shape[1]), lambda i: (i, 0)),
                  pl.BlockSpec((1, LANES), lambda i: (0, 0))],
        out_specs=[pl.BlockSpec((2, tm, LANES), lambda i: (0, i, 0)), half(), half(), half()],
        out_shape=[jax.ShapeDtypeStruct((2, n, LANES), BF16)] + [jax.ShapeDtypeStruct((2, n, HEAD_DIM), BF16)] * 3,
        compiler_params=_params("parallel"),
        name="kvprep",
    )(kv, gk2)


def _compress_kernel(c_ref, pe_ref, w1_ref, w2_ref, gk_ref, o_ref):
    half = CMP_STRIDE * HEAD_DIM
    c = c_ref[0, 0, 0]
    pe = pe_ref[0]
    a = _mm(c + pe[:, :half], w1_ref[0, :half, :])
    b = _mm(c + pe[:, half:], w1_ref[0, half:, :])
    nblk = c.shape[0]
    hid = a + pltpu.roll(b, nblk - 1, axis=0)
    out = _mm(jax.nn.gelu(hid), w2_ref[0])
    is_key = pl.program_id(0) == 0
    o_ref[0, 0, 0] = jnp.where(is_key, _rms(out, gk_ref[...]), out)


def _compress(cflat, pe, w1, w2, gk):
    _, b, g, nblk, width = cflat.shape
    return pl.pallas_call(
        _compress_kernel,
        grid=(2, b, g),
        in_specs=[pl.BlockSpec((1, 1, 1, nblk, width), lambda s, i, j: (s, i, j, 0, 0)),
                  pl.BlockSpec((1, 1, 2 * width), lambda s, i, j: (s, 0, 0)),
                  pl.BlockSpec((1, 2 * width, CMP_HIDDEN), lambda s, i, j: (s, 0, 0)),
                  pl.BlockSpec((1, CMP_HIDDEN, HEAD_DIM), lambda s, i, j: (s, 0, 0)),
                  pl.BlockSpec((1, HEAD_DIM), lambda s, i, j: (0, 0))],
        out_specs=pl.BlockSpec((1, 1, 1, nblk, HEAD_DIM), lambda s, i, j: (s, i, j, 0, 0)),
        out_shape=jax.ShapeDtypeStruct((2, b, g, nblk, HEAD_DIM), F32),
        compiler_params=_params("parallel", "parallel", "parallel"),
        name="compress",
    )(cflat, pe, w1, w2, gk)


def _flash_step(state, qx, k, v, bias):
    m, l, acc = state
    s = lax.dot_general(qx, k, _NT, preferred_element_type=F32)
    if bias is not None:
        s = (s.reshape(-1, bias.shape[-2], s.shape[-1]) + bias).reshape(s.shape)
    m_new = jnp.maximum(m, jnp.max(s, axis=-1, keepdims=True))
    alpha = jnp.exp(m - m_new)
    p = jnp.exp(s - m_new)
    l = alpha * l + jnp.sum(p, axis=-1, keepdims=True)
    acc = alpha * acc + jnp.dot(p.astype(BF16), v, preferred_element_type=F32)
    return m_new, l, acc


def _rank_select(score_t, cur_t):
    nb = score_t.shape[0]
    jj = lax.broadcasted_iota(jnp.int32, score_t.shape, 0)
    rank = jnp.zeros(score_t.shape, F32)
    for j in range(nb):
        row = score_t[j:j + 1, :]
        gt = jnp.where(row > score_t, 1.0, 0.0)
        ge = jnp.where(row >= score_t, 1.0, 0.0)
        rank = rank + jnp.where(jj > j, ge, gt)
    keep = jnp.where(rank < float(min(SLC_TOPK, nb)), 1.0, 0.0) * jnp.where(jj <= cur_t, 1.0, 0.0)
    return jnp.where(keep > 0.5, 0.0, NEG_INF)


def _attn_kernel(q_ref, gn_ref, kcv_ref, ksx_ref, vs_ref, kwn_ref, vw_ref, bc_ref, t0_ref, t1_ref, t2_ref,
                 gq_ref, wov_ref, o_ref):
    tq = q_ref.shape[0]
    nblk = ksx_ref.shape[1] // SLC_BLOCK
    qt = pl.program_id(1)
    q = q_ref[...]
    gates = jax.nn.sigmoid(gn_ref[...])
    R = NSA_GROUP
    kprev = jnp.maximum(qt - 1, 0)
    kprev2 = jnp.maximum(qt - 2, 0)
    neg1 = jnp.where(qt < 1, NEG_INF, 0.0)
    neg2 = jnp.where(qt < 2, NEG_INF, 0.0)
    t2 = t2_ref[...]
    outs = []
    for g in range(NSA_KV_GROUPS):
        qall = jnp.concatenate(
            [(_rms(q[:, h * HEAD_DIM:(h + 1) * HEAD_DIM], gq_ref[...]) * HEAD_DIM ** -0.5).astype(BF16)
             for h in range(g * R, (g + 1) * R)], axis=0)

        kc = kcv_ref[0, 0, g].astype(BF16)
        vc = kcv_ref[1, 0, g].astype(BF16)
        s = lax.dot_general(qall, kc, _NT, preferred_element_type=F32)
        s = (s.reshape(R, tq, -1) + bc_ref[g * R:(g + 1) * R]).reshape(s.shape)
        m = jnp.maximum(jnp.max(s, axis=-1, keepdims=True), M_FLOOR)
        e = jnp.exp(s - m)
        p = e / jnp.maximum(jnp.sum(e, axis=-1, keepdims=True), 1e-30)
        o_cmp = jnp.dot(p.astype(BF16), vc, preferred_element_type=F32)
        psum = p[0:tq]
        for r in range(1, R):
            psum = psum + p[r * tq:(r + 1) * tq]
        imp = _mm_f32(psum, wov_ref[...])

        blk = lax.broadcasted_iota(jnp.int32, imp.shape, 1)
        cur = (qt * tq + lax.broadcasted_iota(jnp.int32, imp.shape, 0)) // SLC_BLOCK
        forced = (blk == 0) | (blk == cur) | (blk == cur - 1)
        score = jnp.where(forced, FORCE, jnp.where(blk <= cur, imp, -FORCE))
        cur_t = (qt * tq + lax.broadcasted_iota(jnp.int32, (nblk, tq), 1)) // SLC_BLOCK
        neg_t = _rank_select(score.T[:nblk], cur_t)
        neg_t = jnp.concatenate([neg_t, jnp.full((LANES - nblk, tq), NEG_INF, F32)], axis=0)
        neg = neg_t.T[:, :LANES - HEAD_DIM].astype(BF16)

        qx = jnp.concatenate([qall, jnp.concatenate([neg] * R, axis=0)], axis=1)
        vs_g = vs_ref.at[g]
        ksx_g = ksx_ref.at[g]
        init = (jnp.full((R * tq, 1), M_FLOOR, F32), jnp.zeros((R * tq, 1), F32), jnp.zeros((R * tq, HEAD_DIM), F32))

        def far_step(kt, st):
            sl = pl.ds(pl.multiple_of(kt * tq, tq), tq)
            return _flash_step(st, qx, ksx_g[sl, :], vs_g[sl, :], None)

        st = lax.fori_loop(0, kprev, far_step, init)
        b0 = t0_ref[g * R:(g + 1) * R]
        b1 = t1_ref[g * R:(g + 1) * R]
        sl1 = pl.ds(pl.multiple_of(kprev * tq, tq), tq)
        sl0 = pl.ds(pl.multiple_of(qt * tq, tq), tq)
        st = _flash_step(st, qx, ksx_g[sl1, :], vs_g[sl1, :], b1 + neg1)
        _, l, acc = _flash_step(st, qx, ksx_g[sl0, :], vs_g[sl0, :], b0)
        o_slc = acc / l

        kw_g = kwn_ref.at[g]
        vw_g = vw_ref.at[g]
        sl2 = pl.ds(pl.multiple_of(kprev2 * tq, tq), tq)
        st = _flash_step(init, qall, kw_g[sl2, :], vw_g[sl2, :], (t2 + neg2)[None])
        st = _flash_step(st, qall, kw_g[sl1, :], vw_g[sl1, :], b1 + neg1)
        _, l, acc = _flash_step(st, qall, kw_g[sl0, :], vw_g[sl0, :], b0)
        o_win = acc / l

        for r in range(R):
            h = g * R + r
            rows = slice(r * tq, (r + 1) * tq)
            outs.append(gates[:, 3 * h:3 * h + 1] * o_cmp[rows]
                        + gates[:, 3 * h + 1:3 * h + 2] * o_slc[rows]
                        + gates[:, 3 * h + 2:3 * h + 3] * o_win[rows])
    o_ref[...] = jnp.concatenate(outs, axis=1).astype(o_ref.dtype)


def _attention(q, gn, kcv, ksx, vs, kwn, vw, bias_c, t0, t1, t2, gq, wov, batch, seq):
    n = q.shape[0]
    tq = Q_TILE
    nq = seq // tq
    ncmp = kcv.shape[3]
    row = lambda b, i: (b * nq + i, 0)
    whole = lambda b, i: (0, b, 0)
    const3 = lambda b, i: (0, 0, 0)
    return pl.pallas_call(
        _attn_kernel,
        grid=(batch, nq),
        in_specs=[pl.BlockSpec((tq, NSA_WIDTH), row),
                  pl.BlockSpec((tq, LANES), row),
                  pl.BlockSpec((2, 1, NSA_KV_GROUPS, ncmp, HEAD_DIM), lambda b, i: (0, b, 0, 0, 0)),
                  pl.BlockSpec((2, seq, LANES), whole),
                  pl.BlockSpec((2, seq, HEAD_DIM), whole),
                  pl.BlockSpec((2, seq, HEAD_DIM), whole),
                  pl.BlockSpec((2, seq, HEAD_DIM), whole),
                  pl.BlockSpec((NSA_HEADS, tq, ncmp), lambda b, i: (0, i, 0)),
                  pl.BlockSpec((NSA_HEADS, tq, tq), const3),
                  pl.BlockSpec((NSA_HEADS, tq, tq), const3),
                  pl.BlockSpec((tq, tq), lambda b, i: (0, 0)),
                  pl.BlockSpec((1, HEAD_DIM), lambda b, i: (0, 0)),
                  pl.BlockSpec(wov.shape, lambda b, i: (0, 0))],
        out_specs=pl.BlockSpec((tq, NSA_WIDTH), row),
        out_shape=jax.ShapeDtypeStruct((n, NSA_WIDTH), BF16),
        compiler_params=_params("parallel", "parallel"),
        name="nsa_attention",
    )(q, gn, kcv, ksx, vs, kwn, vw, bias_c, t0, t1, t2, gq, wov)


def _softplus(z):
    return jnp.maximum(z, 0.0) + jnp.log(1.0 + jnp.exp(-jnp.abs(z)))


def _rwkv_kernel(x_ref, xp_ref, mu_ref, w0_ref, w2_ref, a0_ref, a2_ref, g2_ref, kk_ref, ka_ref, rk_ref,
                 lng_ref, lnb_ref, o_ref, st_ref):
    L = x_ref.shape[0]
    W = RWKV_WIDTH
    N = HEAD_DIM
    c = pl.program_id(1)

    @pl.when(c == 0)
    def _():
        st_ref[...] = jnp.zeros(st_ref.shape, F32)

    x = x_ref[...]
    prev = jnp.where(c > 0, xp_ref[xp_ref.shape[0] - 1:, :], 0.0)
    row_id = lax.broadcasted_iota(jnp.int32, x.shape, 0)
    shifted = jnp.where(row_id == 0, prev, pltpu.roll(x, 1, axis=0))
    xl = x + (shifted - x) * mu_ref[...]
    r = xl[:, 0:W]
    k = xl[:, W:2 * W]
    v = xl[:, 2 * W:3 * W]
    xw = xl[:, 3 * W:3 * W + LORA_W]
    xa = xl[:, 3 * W + LORA_W:3 * W + LORA_W + LORA_A]
    xg = xl[:, 3 * W + LORA_W + LORA_A:]
    w = -_softplus(-(w0_ref[...] + _mm(jnp.tanh(xw), w2_ref[...]))) - 0.5
    ld = -jnp.exp(w)
    a = jax.nn.sigmoid(a0_ref[...] + _mm(xa, a2_ref[...]))
    gate = _mm(jax.nn.sigmoid(xg), g2_ref[...])
    kkv = k * kk_ref[...]
    k2 = k * (1.0 + (a - 1.0) * ka_ref[...])

    ti = lax.broadcasted_iota(jnp.int32, (L, L), 0)
    si = lax.broadcasted_iota(jnp.int32, (L, L), 1)
    incl = si <= ti
    strict = si < ti
    cl = _mm_f32(jnp.where(incl, 1.0, 0.0), ld)
    cl_end = cl[L - 1:L, :]
    e_pos = jnp.exp(cl)
    e_neg = jnp.exp(-cl)
    e_prev = jnp.exp(cl - ld)
    e_end = jnp.exp(cl_end - cl)
    r_t = r * e_pos
    k_t = k2 * e_neg
    k_e = k2 * e_end
    eye = jnp.where(ti == si, 1.0, 0.0)
    pl_cols = [e_pos[:, m * LANES:(m + 1) * LANES].T[:, L - 1:L] for m in range(W // LANES)]

    H = range(RWKV_HEADS)
    hs = [slice(h * N, (h + 1) * N) for h in H]
    kk_n = [kkv[:, s] for s in hs]
    kk_n = [x / jnp.maximum(jnp.sqrt(jnp.sum(x * x, axis=-1, keepdims=True)), 1e-12) for x in kk_n]
    a_t = [-kk_n[h] * e_prev[:, hs[h]] for h in H]
    bv = [kk_n[h] * a[:, hs[h]] for h in H]
    b_t = [bv[h] * e_neg[:, hs[h]] for h in H]
    b_e = [bv[h] * e_end[:, hs[h]] for h in H]
    v_h = [v[:, s] for s in hs]
    rt_h = [r_t[:, s] for s in hs]

    aa = [_mm_nt(jnp.concatenate([a_t[h], rt_h[h]], axis=0), jnp.concatenate([k_t[:, hs[h]], b_t[h]], axis=0))
          for h in H]
    a_ak = [jnp.where(strict, x[:L, :L], 0.0) for x in aa]
    a_ab = [jnp.where(strict, x[:L, L:], 0.0) for x in aa]
    a_rk = [jnp.where(incl, x[L:, :L], 0.0) for x in aa]
    a_rb = [jnp.where(incl, x[L:, L:], 0.0) for x in aa]

    tinv = [eye + x for x in a_ab]
    pw = [_mm(x, x) for x in a_ab]
    span = 2
    while span < L:
        both = [_mm(jnp.concatenate([tinv[h], pw[h]], axis=0), pw[h]) for h in H]
        tinv = [tinv[h] + both[h][:L] for h in H]
        pw = [x[L:] for x in both]
        span *= 2

    av = [_mm(jnp.concatenate([a_ak[h], a_rk[h]], axis=0), v_h[h]) for h in H]
    tw = [_mm(tinv[h], jnp.concatenate([a_t[h], av[h][:L]], axis=1)) for h in H]
    kb_t = [jnp.concatenate([k_e[:, hs[h]], b_e[h]], axis=1).T for h in H]
    kv_loc = [_mm(kb_t[h][:N], v_h[h]) for h in H]

    st = [st_ref[h] for h in H]
    ws = [_mm(jnp.concatenate([tw[h][:, :N], rt_h[h]], axis=0), st[h]) for h in H]
    u = [ws[h][:L] + tw[h][:, N:] for h in H]
    y = [ws[h][L:] + av[h][L:] + _mm(a_rb[h], u[h]) for h in H]
    pcol = [pl_cols[(h * N) // LANES][(h * N) % LANES:(h * N) % LANES + N, :] for h in H]
    st_ref[...] = jnp.stack([pcol[h] * st[h] + kv_loc[h] + _mm(kb_t[h][N:], u[h]) for h in H])

    outs = []
    for h in H:
        mean = jnp.mean(y[h], axis=-1, keepdims=True)
        var = jnp.mean(jnp.square(y[h] - mean), axis=-1, keepdims=True)
        yn = (y[h] - mean) * lax.rsqrt(var + GN_EPS) * lng_ref[:, hs[h]] + lnb_ref[:, hs[h]]
        bonus = jnp.sum(r[:, hs[h]] * k2[:, hs[h]] * rk_ref[:, hs[h]], axis=-1, keepdims=True)
        outs.append(yn + bonus * v_h[h])
    o_ref[...] = (jnp.concatenate(outs, axis=1) * gate).astype(o_ref.dtype)


def _rwkv(rw, vecs, w2, a2, g2, batch, seq):
    n, width = rw.shape
    L = min(CHUNK, seq)
    nc = seq // L
    sub = 8
    vec = lambda wd: pl.BlockSpec((1, wd), lambda b, c: (0, 0))
    mat = lambda m: pl.BlockSpec(m.shape, lambda b, c: (0, 0))
    mu, w0, a0, kk, ka, rk, lng, lnb = vecs
    return pl.pallas_call(
        _rwkv_kernel,
        grid=(batch, nc),
        in_specs=[pl.BlockSpec((L, width), lambda b, c: (b * nc + c, 0)),
                  pl.BlockSpec((sub, width), lambda b, c: (jnp.maximum((b * nc + c) * (L // sub) - 1, 0), 0)),
                  vec(width), vec(RWKV_WIDTH), mat(w2), vec(RWKV_WIDTH), mat(a2), mat(g2),
                  vec(RWKV_WIDTH), vec(RWKV_WIDTH), vec(RWKV_WIDTH), vec(RWKV_WIDTH), vec(RWKV_WIDTH)],
        out_specs=pl.BlockSpec((L, RWKV_WIDTH), lambda b, c: (b * nc + c, 0)),
        out_shape=jax.ShapeDtypeStruct((n, RWKV_WIDTH), BF16),
        scratch_shapes=[pltpu.VMEM((RWKV_HEADS, HEAD_DIM, HEAD_DIM), F32)],
        compiler_params=_params("parallel", "arbitrary"),
        name="rwkv7",
    )(rw, rw, mu, w0, w2, a0, a2, g2, kk, ka, rk, lng, lnb)


def _merge_kernel(x_ref, oa_ref, ob_ref, gab_ref, wpa_ref, wpb_ref, wo_ref, o_ref):
    d = x_ref.shape[1]
    pa = jnp.dot(oa_ref[...], wpa_ref[...], preferred_element_type=F32)
    pb = jnp.dot(ob_ref[...], wpb_ref[...], preferred_element_type=F32)
    gab = gab_ref[...]
    merged = jax.nn.sigmoid(gab[:, :d]) * pa + jax.nn.sigmoid(gab[:, d:]) * pb
    o_ref[...] = x_ref[...] + _mm(merged, wo_ref[...])


def _merge(x2, oa, ob, gab, wpa, wpb, wo, tm):
    n, d = x2.shape
    row = lambda wd: pl.BlockSpec((tm, wd), lambda i: (i, 0))
    mat = lambda m: pl.BlockSpec(m.shape, lambda i: (0, 0))
    return pl.pallas_call(
        _merge_kernel,
        grid=(n // tm,),
        in_specs=[row(d), row(oa.shape[1]), row(ob.shape[1]), row(gab.shape[1]), mat(wpa), mat(wpb), mat(wo)],
        out_specs=row(d),
        out_shape=jax.ShapeDtypeStruct((n, d), F32),
        compiler_params=_params("parallel"),
        name="merge",
    )(x2, oa, ob, gab, wpa, wpb, wo)


def _ffn_kernel(seq, x_ref, xh_ref, g_ref, wv_ref, wg_ref, cwv_ref, cwg_ref, cbv_ref, cbg_ref, wd_ref, o_ref,
                h_ref, uv_ref, ug_ref):
    tm = x_ref.shape[0]
    i = pl.program_id(0)
    f = pl.program_id(1)

    @pl.when(f == 0)
    def _():
        first = lax.rem(i * tm, seq) == 0
        halo = jnp.where(first, 0.0, _rms(xh_ref[...], g_ref[...]))
        h_ref[0:HALO, :] = halo.astype(BF16)
        h_ref[HALO:, :] = _rms(x_ref[...], g_ref[...]).astype(BF16)

    h = h_ref[...]
    uv_ref[...] = jnp.dot(h, wv_ref[...], preferred_element_type=F32)
    ug_ref[...] = jnp.dot(h, wg_ref[...], preferred_element_type=F32)

    def conv(u_ref, cw_ref, cb_ref):
        acc = cb_ref[...] + cw_ref[0:1, :] * u_ref[pl.ds(HALO - 2, tm), :]
        acc = acc + cw_ref[1:2, :] * u_ref[pl.ds(HALO - 1, tm), :]
        return acc + cw_ref[2:3, :] * u_ref[pl.ds(HALO, tm), :]

    val = conv(uv_ref, cwv_ref, cbv_ref)
    gt = conv(ug_ref, cwg_ref, cbg_ref)
    y = _mm(gt * jax.nn.sigmoid(gt) * val, wd_ref[...])

    @pl.when(f == 0)
    def _():
        o_ref[...] = x_ref[...] + y

    @pl.when(f > 0)
    def _():
        o_ref[...] = o_ref[...] + y


def _ffn(x1, g, w_up, conv_w, conv_b, w_down, seq, tm, tf):
    n, d = x1.shape
    dff = w_down.shape[0]
    nf = dff // tf
    return pl.pallas_call(
        functools.partial(_ffn_kernel, seq),
        grid=(n // tm, nf),
        in_specs=[pl.BlockSpec((tm, d), lambda i, f: (i, 0)),
                  pl.BlockSpec((HALO, d), lambda i, f: (jnp.maximum(i * (tm // HALO) - 1, 0), 0)),
                  pl.BlockSpec((1, d), lambda i, f: (0, 0)),
                  pl.BlockSpec((d, tf), lambda i, f: (0, f)),
                  pl.BlockSpec((d, tf), lambda i, f: (0, nf + f)),
                  pl.BlockSpec((CONV_WIDTH, tf), lambda i, f: (0, f)),
                  pl.BlockSpec((CONV_WIDTH, tf), lambda i, f: (0, nf + f)),
                  pl.BlockSpec((1, tf), lambda i, f: (0, f)),
                  pl.BlockSpec((1, tf), lambda i, f: (0, nf + f)),
                  pl.BlockSpec((tf, d), lambda i, f: (f, 0))],
        out_specs=pl.BlockSpec((tm, d), lambda i, f: (i, 0)),
        out_shape=jax.ShapeDtypeStruct((n, d), F32),
        scratch_shapes=[pltpu.VMEM((tm + HALO, d), BF16),
                        pltpu.VMEM((tm + HALO, tf), F32),
                        pltpu.VMEM((tm + HALO, tf), F32)],
        compiler_params=_params("parallel", "arbitrary"),
        name="convffn",
    )(x1, x1, g, w_up, w_up, conv_w, conv_w, conv_b, conv_b, w_down)


def _t5_bucket(dist):
    n = jnp.maximum(dist, 0)
    max_exact = REL_BUCKETS // 2
    ratio = jnp.log(jnp.maximum(n, 1).astype(F32) / max_exact) / math.log(REL_MAX_DIST / max_exact)
    large = jnp.minimum(max_exact + (ratio * (REL_BUCKETS - max_exact)).astype(jnp.int32), REL_BUCKETS - 1)
    return jnp.where(n < max_exact, n, large)


def _bias_tables(rel_bias, seq, ncmp_pad):
    tq = Q_TILE
    tab = rel_bias.astype(F32)

    def look(dist):
        bucket = _t5_bucket(dist)[None]
        out = jnp.zeros((tab.shape[1],) + dist.shape, F32)
        for i in range(REL_BUCKETS):
            out = jnp.where(bucket == i, tab[i].reshape((-1,) + (1,) * dist.ndim), out)
        return out

    t = jnp.arange(seq)[:, None]
    c = jnp.arange(ncmp_pad)[None, :]
    dist_c = t - (c * CMP_STRIDE + CMP_BLOCK - 1)
    n_cmp = seq // CMP_STRIDE - CMP_BLOCK // CMP_STRIDE + 1
    bias_c = jnp.where((dist_c >= 0) & (c < n_cmp), look(dist_c), NEG_INF)
    far = tab[REL_BUCKETS - 1][:, None, None]
    d0 = jnp.arange(tq)[:, None] - jnp.arange(tq)[None, :]
    t0 = jnp.where(d0 >= 0, look(d0) - far, NEG_INF)
    t1 = look(d0 + tq) - far
    t2 = jnp.where(d0 + 2 * tq < WINDOW, 0.0, NEG_INF).astype(F32)
    return bias_c, t0, t1, t2


def _overlap_matrix(ncmp_pad, n_slc, n_cmp):
    m = np.zeros((ncmp_pad, LANES), np.float32)
    ratio = SLC_BLOCK // CMP_STRIDE
    for j in range(n_slc):
        for o, wgt in enumerate(OVERLAP_W):
            cidx = ratio * j + o - (CMP_BLOCK // CMP_STRIDE - 1)
            if 0 <= cidx < n_cmp:
                m[cidx, j] += wgt
    return jnp.asarray(m)


def _layer(x, attn_norm_g, w_in, rel_bias, q_norm_g, k_norm_g, cmp_pe_k, cmp_w1_k, cmp_w2_k,
           cmp_pe_v, cmp_w1_v, cmp_w2_v, rwkv_mu, rwkv_w0, rwkv_w2, rwkv_a0, rwkv_a2, rwkv_g2,
           rwkv_k_k, rwkv_k_a, rwkv_r_k, rwkv_ln_g, rwkv_ln_b, w_proj_a, w_proj_b, w_out,
           ffn_norm_g, w_up, conv_w, conv_b, w_down):
    batch, seq, d = x.shape
    n = batch * seq
    assert seq % Q_TILE == 0 and WINDOW == 2 * Q_TILE and seq % CMP_STRIDE == 0
    assert seq // SLC_BLOCK <= LANES - HEAD_DIM
    x2 = x.reshape(n, d)
    row = lambda a: a.reshape(1, -1).astype(F32)

    kvw = 6 * NSA_KV_WIDTH
    o_q, o_kv, o_gn = 0, NSA_WIDTH, NSA_WIDTH + kvw
    o_rw = o_gn + 3 * NSA_HEADS
    o_gab = o_rw + RWKV_IN_WIDTH
    gn_pad = jnp.zeros((d, LANES - 3 * NSA_HEADS), w_in.dtype)
    w_cat = jnp.concatenate([w_in[:, o_rw:o_gab], w_in[:, o_q:o_kv], w_in[:, o_kv:o_gn], w_in[:, o_gab:],
                             w_in[:, o_gn:o_rw], gn_pad], axis=1).astype(BF16)
    widths = (RWKV_IN_WIDTH, NSA_WIDTH, kvw, 2 * d, LANES)
    rw, q, kv, gab, gn = _inproj(x2, row(attn_norm_g), w_cat, widths, tm=256)

    gk = row(k_norm_g)
    ksx, kwn, vs, vw = _kvprep(kv, jnp.concatenate([gk, gk], axis=1), seq, tm=512)

    nchunk = seq // CMP_STRIDE
    n_cmp = nchunk - CMP_BLOCK // CMP_STRIDE + 1
    cflat = kv[:, :2 * NSA_KV_WIDTH].reshape(batch, nchunk, CMP_STRIDE, 2, NSA_KV_GROUPS, HEAD_DIM)
    cflat = cflat.transpose(3, 0, 4, 1, 2, 5).reshape(2, batch, NSA_KV_GROUPS, nchunk, CMP_STRIDE * HEAD_DIM)
    pe = jnp.stack([cmp_pe_k.reshape(1, -1), cmp_pe_v.reshape(1, -1)]).astype(F32)
    w1 = jnp.stack([cmp_w1_k, cmp_w1_v]).astype(BF16)
    w2 = jnp.stack([cmp_w2_k, cmp_w2_v]).astype(BF16)
    kcv = _compress(cflat, pe, w1, w2, gk)

    bias_c, t0, t1, t2 = _bias_tables(rel_bias, seq, nchunk)
    wov = _overlap_matrix(nchunk, seq // SLC_BLOCK, n_cmp)
    o_a = _attention(q, gn, kcv, ksx, vs, kwn, vw, bias_c, t0, t1, t2, row(q_norm_g), wov, batch, seq)

    vecs = tuple(row(a) for a in (rwkv_mu, rwkv_w0, rwkv_a0, rwkv_k_k, rwkv_k_a, rwkv_r_k, rwkv_ln_g, rwkv_ln_b))
    o_b = _rwkv(rw, vecs, rwkv_w2.astype(BF16), rwkv_a2.astype(BF16), rwkv_g2.astype(BF16), batch, seq)

    x1 = _merge(x2, o_a, o_b, gab, w_proj_a.astype(BF16), w_proj_b.astype(BF16), w_out.astype(BF16), tm=512)
    dff = w_down.shape[0]
    out = _ffn(x1, row(ffn_norm_g), w_up.astype(BF16), conv_w.astype(F32), row(conv_b), w_down.astype(BF16),
               seq, tm=512, tf=dff // 2)
    return out.reshape(batch, seq, d)


def kernel(x, attn_norm_g, w_in, rel_bias, q_norm_g, k_norm_g, cmp_pe_k, cmp_w1_k, cmp_w2_k, cmp_pe_v, cmp_w1_v,
           cmp_w2_v, rwkv_mu, rwkv_w0, rwkv_w2, rwkv_a0, rwkv_a2, rwkv_g2, rwkv_k_k, rwkv_k_a, rwkv_r_k,
           rwkv_ln_g, rwkv_ln_b, w_proj_a, w_proj_b, w_out, ffn_norm_g, w_up, conv_w, conv_b, w_down):
    per_layer = (attn_norm_g, w_in, None, q_norm_g, k_norm_g, cmp_pe_k, cmp_w1_k, cmp_w2_k, cmp_pe_v, cmp_w1_v,
                 cmp_w2_v, rwkv_mu, rwkv_w0, rwkv_w2, rwkv_a0, rwkv_a2, rwkv_g2, rwkv_k_k, rwkv_k_a, rwkv_r_k,
                 rwkv_ln_g, rwkv_ln_b, w_proj_a, w_proj_b, w_out, ffn_norm_g, w_up, conv_w, conv_b, w_down)
    for l in range(attn_norm_g.shape[0]):
        args = [rel_bias if p is None else p[l] for p in per_layer]
        x = _layer(x, *args)
    return x
```

```python
import functools
import math

import numpy as np
import jax
import jax.numpy as jnp
from jax import lax
from jax.experimental import pallas as pl
from jax.experimental.pallas import tpu as pltpu

F32 = jnp.float32
BF16 = jnp.bfloat16

HEAD_DIM = 64
NSA_HEADS = 8
NSA_KV_GROUPS = 2
NSA_GROUP = NSA_HEADS // NSA_KV_GROUPS
NSA_WIDTH = NSA_HEADS * HEAD_DIM
NSA_KV_WIDTH = NSA_KV_GROUPS * HEAD_DIM
CMP_BLOCK = 32
CMP_STRIDE = 16
CMP_HIDDEN = 256
SLC_BLOCK = 64
SLC_TOPK = 16
OVERLAP_W = (1, 2, 2, 2, 1)
WINDOW = 512
REL_BUCKETS = 32
REL_MAX_DIST = 128
RWKV_HEADS = 8
RWKV_WIDTH = RWKV_HEADS * HEAD_DIM
LORA_W = 64
LORA_A = 64
LORA_G = 128
RWKV_IN_WIDTH = 3 * RWKV_WIDTH + LORA_W + LORA_A + LORA_G
GN_EPS = 64e-5
CONV_WIDTH = 3
RMS_EPS = 1e-6
NEG_INF = -1e30
FORCE = 1e9
M_FLOOR = -1e20

LANES = 128
VMEM_LIMIT = 56 * 1024 * 1024

Q_TILE = 256
K_TILE = 256
VT_ROWS = 80
CHUNK = 128
HALO = 16

_NT = (((1,), (1,)), ((), ()))


def _params(*sem):
    return pltpu.CompilerParams(dimension_semantics=sem, vmem_limit_bytes=VMEM_LIMIT)


def _mm(a, b):
    return jnp.dot(a.astype(BF16), b.astype(BF16), preferred_element_type=F32)


def _mm_nt(a, b):
    return lax.dot_general(a.astype(BF16), b.astype(BF16), _NT, preferred_element_type=F32)


def _mm_f32(a, b):
    return jnp.dot(a, b, preferred_element_type=F32, precision=lax.Precision.HIGHEST)


def _rms(x, g):
    return x * lax.rsqrt(jnp.mean(x * x, axis=-1, keepdims=True) + RMS_EPS) * g


def _inproj_kernel(x_ref, g_ref, w_ref, *out_refs):
    h = _rms(x_ref[...], g_ref[...]).astype(BF16)
    off = 0
    for ref in out_refs:
        n = ref.shape[-1]
        ref[...] = jnp.dot(h, w_ref[:, off:off + n], preferred_element_type=F32).astype(ref.dtype)
        off += n


def _inproj(x2, g, w, widths, tm):
    n, d = x2.shape
    return pl.pallas_call(
        _inproj_kernel,
        grid=(n // tm,),
        in_specs=[pl.BlockSpec((tm, d), lambda i: (i, 0)),
                  pl.BlockSpec((1, d), lambda i: (0, 0)),
                  pl.BlockSpec(w.shape, lambda i: (0, 0))],
        out_specs=[pl.BlockSpec((tm, wd), lambda i: (i, 0)) for wd in widths],
        out_shape=[jax.ShapeDtypeStruct((n, wd), F32) for wd in widths],
        compiler_params=_params("parallel"),
        name="inproj",
    )(x2, g, w)


def _kvprep_kernel(seq, kv_ref, gk_ref, ksx_ref, kwn_ref, vs_ref, vw_ref):
    tm = kv_ref.shape[0]
    kv = kv_ref[...]
    lane = lax.broadcasted_iota(jnp.int32, (tm, LANES), 1)
    lo = lane < HEAD_DIM

    def norm2(x):
        x2 = x * x
        s_lo = jnp.sum(jnp.where(lo, x2, 0.0), axis=-1, keepdims=True)
        s_hi = jnp.sum(jnp.where(lo, 0.0, x2), axis=-1, keepdims=True)
        ms = jnp.where(lo, s_lo, s_hi) * (1.0 / HEAD_DIM)
        return x * lax.rsqrt(ms + RMS_EPS) * gk_ref[...]

    ks = norm2(kv[:, 2 * LANES:3 * LANES])
    kw = norm2(kv[:, 4 * LANES:5 * LANES])
    row = lax.rem(pl.program_id(0) * tm, seq) + lax.broadcasted_iota(jnp.int32, (tm, LANES), 0)
    onehot = jnp.where(lane - HEAD_DIM == row // SLC_BLOCK, 1.0, 0.0)
    ksx_ref[0] = jnp.where(lo, ks, onehot).astype(BF16)
    ksx_ref[1] = jnp.where(lo, pltpu.roll(ks, HEAD_DIM, axis=1), onehot).astype(BF16)
    kwn_ref[0] = kw[:, :HEAD_DIM].astype(BF16)
    kwn_ref[1] = kw[:, HEAD_DIM:].astype(BF16)
    tail = jnp.where(lax.broadcasted_iota(jnp.int32, (VT_ROWS - HEAD_DIM, K_TILE), 0) == 0, 1.0, 0.0)
    for col, ref in ((3, vs_ref), (5, vw_ref)):
        for t in range(tm // K_TILE):
            vt = kv[t * K_TILE:(t + 1) * K_TILE, col * LANES:(col + 1) * LANES].T
            for g in range(NSA_KV_GROUPS):
                ref[g, t] = jnp.concatenate([vt[g * HEAD_DIM:(g + 1) * HEAD_DIM], tail], axis=0).astype(BF16)


def _kvprep(kv, gk2, seq, tm):
    n = kv.shape[0]
    vt_spec = lambda: pl.BlockSpec((2, tm // K_TILE, VT_ROWS, K_TILE), lambda i: (0, i, 0, 0))
    vt_shape = jax.ShapeDtypeStruct((2, n // K_TILE, VT_ROWS, K_TILE), BF16)
    return pl.pallas_call(
        functools.partial(_kvprep_kernel, seq),
        grid=(n // tm,),
        in_specs=[pl.BlockSpec((tm, kv.shape[1]), lambda i: (i, 0)),
                  pl.BlockSpec((1, LANES), lambda i: (0, 0))],
        out_specs=[pl.BlockSpec((2, tm, LANES), lambda i: (0, i, 0)),
                   pl.BlockSpec((2, tm, HEAD_DIM), lambda i: (0, i, 0)), vt_spec(), vt_spec()],
        out_shape=[jax.ShapeDtypeStruct((2, n, LANES), BF16), jax.ShapeDtypeStruct((2, n, HEAD_DIM), BF16),
                   vt_shape, vt_shape],
        compiler_params=_params("parallel"),
        name="kvprep",
    )(kv, gk2)


def _compress_kernel(c_ref, pe_ref, w1_ref, w2_ref, gk_ref, o_ref):
    half = CMP_STRIDE * HEAD_DIM
    c = c_ref[0, 0, 0]
    pe = pe_ref[0]
    a = _mm(c + pe[:, :half], w1_ref[0, :half, :])
    b = _mm(c + pe[:, half:], w1_ref[0, half:, :])
    nblk = c.shape[0]
    hid = a + pltpu.roll(b, nblk - 1, axis=0)
    out = _mm(jax.nn.gelu(hid), w2_ref[0])
    is_key = pl.program_id(0) == 0
    o_ref[0, 0, 0] = jnp.where(is_key, _rms(out, gk_ref[...]), out)


def _compress(cflat, pe, w1, w2, gk):
    _, b, g, nblk, width = cflat.shape
    return pl.pallas_call(
        _compress_kernel,
        grid=(2, b, g),
        in_specs=[pl.BlockSpec((1, 1, 1, nblk, width), lambda s, i, j: (s, i, j, 0, 0)),
                  pl.BlockSpec((1, 1, 2 * width), lambda s, i, j: (s, 0, 0)),
                  pl.BlockSpec((1, 2 * width, CMP_HIDDEN), lambda s, i, j: (s, 0, 0)),
                  pl.BlockSpec((1, CMP_HIDDEN, HEAD_DIM), lambda s, i, j: (s, 0, 0)),
                  pl.BlockSpec((1, HEAD_DIM), lambda s, i, j: (0, 0))],
        out_specs=pl.BlockSpec((1, 1, 1, nblk, HEAD_DIM), lambda s, i, j: (s, i, j, 0, 0)),
        out_shape=jax.ShapeDtypeStruct((2, b, g, nblk, HEAD_DIM), F32),
        compiler_params=_params("parallel", "parallel", "parallel"),
        name="compress",
    )(cflat, pe, w1, w2, gk)


def _rank_select(score_t, cur_t):
    nb = score_t.shape[0]
    jj = lax.broadcasted_iota(jnp.int32, score_t.shape, 0)
    rank = jnp.zeros(score_t.shape, F32)
    for j in range(nb):
        row = score_t[j:j + 1, :]
        gt = jnp.where(row > score_t, 1.0, 0.0)
        ge = jnp.where(row >= score_t, 1.0, 0.0)
        rank = rank + jnp.where(jj > j, ge, gt)
    keep = jnp.where(rank < float(min(SLC_TOPK, nb)), 1.0, 0.0) * jnp.where(jj <= cur_t, 1.0, 0.0)
    return jnp.where(keep > 0.5, 0.0, NEG_INF)


def _win_masked(tq, tk):
    return [o for o in range(-(WINDOW // tk), 0) if tq - 1 - o * tk >= WINDOW]


def _scores(ks, ws):
    return [jnp.dot(k, w, preferred_element_type=F32) for k, w in zip(ks, ws)]


def _softmax_update(m_ref, acc_ref, slots, s, vts):
    m_old = [m_ref[i] for i in slots]
    m_new = [jnp.maximum(mo, jnp.max(x, axis=0, keepdims=True)) for mo, x in zip(m_old, s)]
    pr = [jnp.exp(x - mn).astype(BF16) for x, mn in zip(s, m_new)]
    pv = [jnp.dot(vt, y, preferred_element_type=F32) for vt, y in zip(vts, pr)]
    acc = [jnp.exp(mo - mn) * acc_ref[i] + z for mo, mn, i, z in zip(m_old, m_new, slots, pv)]
    for i, mn, ac in zip(slots, m_new, acc):
        m_ref[i] = mn
        acc_ref[i] = ac


def _softmax_steps(m_ref, acc_ref, probs):
    s = _scores([p[1] for p in probs], [p[3] for p in probs])
    s = [x if p[4] is None else x + p[4][...] for x, p in zip(s, probs)]
    _softmax_update(m_ref, acc_ref, [p[0] for p in probs], s, [p[2] for p in probs])


def _attn_kernel(q_ref, gn_ref, kc_ref, vct_ref, ksx_ref, vst_ref, kwn_ref, vwt_ref, bc_ref, bn_ref, wm_ref,
                 gq_ref, wov_ref, o_ref, m_ref, acc_ref, sa_ref, sb_ref):
    tq = q_ref.shape[0]
    tk = K_TILE
    nblk = wov_ref.shape[0]
    G, R = NSA_KV_GROUPS, NSA_GROUP
    qt = pl.program_id(1)
    n_q = tq // tk
    n_b = -(-REL_MAX_DIST // tk)
    j0 = qt * n_q
    qT = q_ref[...].T
    gates = jax.nn.sigmoid(gn_ref[...].T)
    gq = gq_ref[...]

    qn, qx, o_cmp = [], [], []
    for g in range(G):
        xs = []
        for h in range(g * R, (g + 1) * R):
            x = qT[h * HEAD_DIM:(h + 1) * HEAD_DIM]
            x = x * lax.rsqrt(jnp.mean(x * x, axis=0, keepdims=True) + RMS_EPS) * gq * HEAD_DIM ** -0.5
            xs.append(x.astype(BF16))
        qn.append(jnp.concatenate(xs, axis=1))

    for g in range(G):
        s = jnp.dot(kc_ref[0, g].astype(BF16), qn[g], preferred_element_type=F32) + bc_ref[g, 0]
        m = jnp.maximum(jnp.max(s, axis=0, keepdims=True), M_FLOOR)
        e = jnp.exp(s - m)
        p = e / jnp.maximum(jnp.sum(e, axis=0, keepdims=True), 1e-30)
        o_cmp.append(jnp.dot(vct_ref[0, g].astype(BF16), p.astype(BF16), preferred_element_type=F32))
        psum = p[:, 0:tq]
        for r in range(1, R):
            psum = psum + p[:, r * tq:(r + 1) * tq]
        imp = _mm_f32(wov_ref[...], psum)

        blk = lax.broadcasted_iota(jnp.int32, imp.shape, 0)
        cur = (qt * tq + lax.broadcasted_iota(jnp.int32, imp.shape, 1)) // SLC_BLOCK
        forced = (blk == 0) | (blk == cur) | (blk == cur - 1)
        score = jnp.where(forced, FORCE, jnp.where(blk <= cur, imp, -FORCE))
        neg = _rank_select(score, cur)
        if nblk < LANES - HEAD_DIM:
            neg = jnp.concatenate([neg, jnp.zeros((LANES - HEAD_DIM - nblk, tq), F32)], axis=0)
        neg = neg.astype(BF16)
        qx.append(jnp.concatenate([qn[g], jnp.concatenate([neg] * R, axis=1)], axis=0))

    m_ref[...] = jnp.full(m_ref.shape, M_FLOOR, F32)
    acc_ref[...] = jnp.zeros(acc_ref.shape, F32)

    def slc_probs(j, bias):
        sl = pl.ds(pl.multiple_of(j * tk, tk), tk)
        return [(g, ksx_ref[g, sl, :], vst_ref[g, j], qx[g], None if bias is None else bias.at[g])
                for g in range(G)]

    def win_probs(j, bias, per_group):
        sl = pl.ds(pl.multiple_of(j * tk, tk), tk)
        pick = lambda g: None if bias is None else (bias.at[g] if per_group else bias)
        return [(G + g, kwn_ref[g, sl, :], vwt_ref[g, j], qn[g], pick(g)) for g in range(G)]

    n_far = jnp.maximum(j0 - n_b, 0)
    groups = list(range(G))

    def far_scores(j, buf):
        sl = pl.ds(pl.multiple_of(j * tk, tk), tk)
        for g, x in zip(groups, _scores([ksx_ref[g, sl, :] for g in groups], qx)):
            buf[g] = x

    def far_update(j, buf):
        _softmax_update(m_ref, acc_ref, groups, [buf[g] for g in groups], [vst_ref[g, j] for g in groups])

    far_scores(0, sa_ref)

    def far_body(i, carry):
        far_scores(2 * i + 1, sb_ref)
        far_update(2 * i, sa_ref)
        far_scores(2 * i + 2, sa_ref)
        far_update(2 * i + 1, sb_ref)
        return carry

    lax.fori_loop(0, n_far // 2, far_body, 0)

    @pl.when(n_far % 2 == 1)
    def _():
        far_update(n_far - 1, sa_ref)

    masked = _win_masked(tq, tk)
    by_start = {}
    for o in range(-(WINDOW // tk), n_q):
        probs = []
        if o >= -n_b:
            probs += [functools.partial(slc_probs, j0 + o, bn_ref.at[o + n_b])]
            probs += [functools.partial(win_probs, j0 + o, bn_ref.at[o + n_b], True)]
        elif o in masked:
            probs += [functools.partial(win_probs, j0 + o, wm_ref.at[masked.index(o)], False)]
        else:
            probs += [functools.partial(win_probs, j0 + o, None, False)]
        by_start.setdefault(max(-(o // n_q), 0), []).append(probs)

    for first_qt in sorted(by_start, reverse=True):
        def run(tiles=by_start[first_qt]):
            for probs in tiles:
                _softmax_steps(m_ref, acc_ref, [p for f in probs for p in f()])
        if first_qt == 0:
            run()
        else:
            pl.when(qt >= first_qt)(run)

    outs = []
    for g in range(G):
        acc = acc_ref[g]
        o_slc = acc[:HEAD_DIM] / acc[HEAD_DIM:HEAD_DIM + 1]
        acc = acc_ref[G + g]
        o_win = acc[:HEAD_DIM] / acc[HEAD_DIM:HEAD_DIM + 1]
        for r in range(R):
            h = g * R + r
            cols = slice(r * tq, (r + 1) * tq)
            outs.append(gates[3 * h:3 * h + 1] * o_cmp[g][:, cols] + gates[3 * h + 1:3 * h + 2] * o_slc[:, cols]
                        + gates[3 * h + 2:3 * h + 3] * o_win[:, cols])
    o_ref[...] = jnp.concatenate(outs, axis=0).T.astype(o_ref.dtype)


def _attention(q, gn, kc, vct, ksx, vst, kwn, vwt, bias_c, bias_near, win_mask, gq, wov, batch, seq):
    n = q.shape[0]
    tq = Q_TILE
    nq = seq // tq
    ncmp = kc.shape[2]
    row = lambda b, i: (b * nq + i, 0)
    whole = lambda b, i: (0, b, 0)
    tiles = lambda b, i: (0, b, 0, 0)
    const = lambda a: pl.BlockSpec(a.shape, lambda b, i: (0,) * a.ndim)
    return pl.pallas_call(
        _attn_kernel,
        grid=(batch, nq),
        in_specs=[pl.BlockSpec((tq, NSA_WIDTH), row),
                  pl.BlockSpec((tq, LANES), row),
                  pl.BlockSpec((1, NSA_KV_GROUPS, ncmp, HEAD_DIM), lambda b, i: (b, 0, 0, 0)),
                  pl.BlockSpec((1, NSA_KV_GROUPS, HEAD_DIM, ncmp), lambda b, i: (b, 0, 0, 0)),
                  pl.BlockSpec((2, seq, LANES), whole),
                  pl.BlockSpec((2, seq // K_TILE, VT_ROWS, K_TILE), tiles),
                  pl.BlockSpec((2, seq, HEAD_DIM), whole),
                  pl.BlockSpec((2, seq // K_TILE, VT_ROWS, K_TILE), tiles),
                  pl.BlockSpec((NSA_KV_GROUPS, 1, ncmp, NSA_GROUP * tq), lambda b, i: (0, i, 0, 0)),
                  const(bias_near), const(win_mask), const(gq), const(wov)],
        out_specs=pl.BlockSpec((tq, NSA_WIDTH), row),
        out_shape=jax.ShapeDtypeStruct((n, NSA_WIDTH), BF16),
        scratch_shapes=[pltpu.VMEM((2 * NSA_KV_GROUPS, 1, NSA_GROUP * tq), F32),
                        pltpu.VMEM((2 * NSA_KV_GROUPS, VT_ROWS, NSA_GROUP * tq), F32),
                        pltpu.VMEM((NSA_KV_GROUPS, K_TILE, NSA_GROUP * tq), F32),
                        pltpu.VMEM((NSA_KV_GROUPS, K_TILE, NSA_GROUP * tq), F32)],
        compiler_params=_params("parallel", "parallel"),
        name="nsa_attention",
    )(q, gn, kc, vct, ksx, vst, kwn, vwt, bias_c, bias_near, win_mask, gq, wov)


def _softplus(z):
    return jnp.maximum(z, 0.0) + jnp.log(1.0 + jnp.exp(-jnp.abs(z)))


def _rwkv_kernel(x_ref, xp_ref, mu_ref, w0_ref, w2_ref, a0_ref, a2_ref, g2_ref, kk_ref, ka_ref, rk_ref,
                 lng_ref, lnb_ref, o_ref, st_ref):
    L = x_ref.shape[0]
    W = RWKV_WIDTH
    N = HEAD_DIM
    c = pl.program_id(1)

    @pl.when(c == 0)
    def _():
        st_ref[...] = jnp.zeros(st_ref.shape, F32)

    x = x_ref[...]
    prev = jnp.where(c > 0, xp_ref[xp_ref.shape[0] - 1:, :], 0.0)
    row_id = lax.broadcasted_iota(jnp.int32, x.shape, 0)
    shifted = jnp.where(row_id == 0, prev, pltpu.roll(x, 1, axis=0))
    xl = x + (shifted - x) * mu_ref[...]
    r = xl[:, 0:W]
    k = xl[:, W:2 * W]
    v = xl[:, 2 * W:3 * W]
    xw = xl[:, 3 * W:3 * W + LORA_W]
    xa = xl[:, 3 * W + LORA_W:3 * W + LORA_W + LORA_A]
    xg = xl[:, 3 * W + LORA_W + LORA_A:]
    w = -_softplus(-(w0_ref[...] + _mm(jnp.tanh(xw), w2_ref[...]))) - 0.5
    ld = -jnp.exp(w)
    a = jax.nn.sigmoid(a0_ref[...] + _mm(xa, a2_ref[...]))
    gate = _mm(jax.nn.sigmoid(xg), g2_ref[...])
    kkv = k * kk_ref[...]
    k2 = k * (1.0 + (a - 1.0) * ka_ref[...])

    ti = lax.broadcasted_iota(jnp.int32, (L, L), 0)
    si = lax.broadcasted_iota(jnp.int32, (L, L), 1)
    incl = si <= ti
    strict = si < ti
    cl = _mm_f32(jnp.where(incl, 1.0, 0.0), ld)
    cl_end = cl[L - 1:L, :]
    e_pos = jnp.exp(cl)
    e_neg = jnp.exp(-cl)
    e_prev = jnp.exp(cl - ld)
    e_end = jnp.exp(cl_end - cl)
    r_t = r * e_pos
    k_t = k2 * e_neg
    k_e = k2 * e_end
    eye = jnp.where(ti == si, 1.0, 0.0)
    pl_cols = [e_pos[:, m * LANES:(m + 1) * LANES].T[:, L - 1:L] for m in range(W // LANES)]

    H = range(RWKV_HEADS)
    hs = [slice(h * N, (h + 1) * N) for h in H]
    kk_n = [kkv[:, s] for s in hs]
    kk_n = [x / jnp.maximum(jnp.sqrt(jnp.sum(x * x, axis=-1, keepdims=True)), 1e-12) for x in kk_n]
    a_t = [-kk_n[h] * e_prev[:, hs[h]] for h in H]
    bv = [kk_n[h] * a[:, hs[h]] for h in H]
    b_t = [bv[h] * e_neg[:, hs[h]] for h in H]
    b_e = [bv[h] * e_end[:, hs[h]] for h in H]
    v_h = [v[:, s] for s in hs]
    rt_h = [r_t[:, s] for s in hs]

    aa = [_mm_nt(jnp.concatenate([a_t[h], rt_h[h]], axis=0), jnp.concatenate([k_t[:, hs[h]], b_t[h]], axis=0))
          for h in H]
    a_ak = [jnp.where(strict, x[:L, :L], 0.0) for x in aa]
    a_ab = [jnp.where(strict, x[:L, L:], 0.0) for x in aa]
    a_rk = [jnp.where(incl, x[L:, :L], 0.0) for x in aa]
    a_rb = [jnp.where(incl, x[L:, L:], 0.0) for x in aa]

    tinv = [eye + x for x in a_ab]
    pw = [_mm(x, x) for x in a_ab]
    span = 2
    while span < L:
        both = [_mm(jnp.concatenate([tinv[h], pw[h]], axis=0), pw[h]) for h in H]
        tinv = [tinv[h] + both[h][:L] for h in H]
        pw = [x[L:] for x in both]
        span *= 2

    av = [_mm(jnp.concatenate([a_ak[h], a_rk[h]], axis=0), v_h[h]) for h in H]
    tw = [_mm(tinv[h], jnp.concatenate([a_t[h], av[h][:L]], axis=1)) for h in H]
    kb_t = [jnp.concatenate([k_e[:, hs[h]], b_e[h]], axis=1).T for h in H]
    kv_loc = [_mm(kb_t[h][:N], v_h[h]) for h in H]

    st = [st_ref[h] for h in H]
    ws = [_mm(jnp.concatenate([tw[h][:, :N], rt_h[h]], axis=0), st[h]) for h in H]
    u = [ws[h][:L] + tw[h][:, N:] for h in H]
    y = [ws[h][L:] + av[h][L:] + _mm(a_rb[h], u[h]) for h in H]
    pcol = [pl_cols[(h * N) // LANES][(h * N) % LANES:(h * N) % LANES + N, :] for h in H]
    st_ref[...] = jnp.stack([pcol[h] * st[h] + kv_loc[h] + _mm(kb_t[h][N:], u[h]) for h in H])

    outs = []
    for h in H:
        mean = jnp.mean(y[h], axis=-1, keepdims=True)
        var = jnp.mean(jnp.square(y[h] - mean), axis=-1, keepdims=True)
        yn = (y[h] - mean) * lax.rsqrt(var + GN_EPS) * lng_ref[:, hs[h]] + lnb_ref[:, hs[h]]
        bonus = jnp.sum(r[:, hs[h]] * k2[:, hs[h]] * rk_ref[:, hs[h]], axis=-1, keepdims=True)
        outs.append(yn + bonus * v_h[h])
    o_ref[...] = (jnp.concatenate(outs, axis=1) * gate).astype(o_ref.dtype)


def _rwkv(rw, vecs, w2, a2, g2, batch, seq):
    n, width = rw.shape
    L = min(CHUNK, seq)
    nc = seq // L
    sub = 8
    vec = lambda wd: pl.BlockSpec((1, wd), lambda b, c: (0, 0))
    mat = lambda m: pl.BlockSpec(m.shape, lambda b, c: (0, 0))
    mu, w0, a0, kk, ka, rk, lng, lnb = vecs
    return pl.pallas_call(
        _rwkv_kernel,
        grid=(batch, nc),
        in_specs=[pl.BlockSpec((L, width), lambda b, c: (b * nc + c, 0)),
                  pl.BlockSpec((sub, width), lambda b, c: (jnp.maximum((b * nc + c) * (L // sub) - 1, 0), 0)),
                  vec(width), vec(RWKV_WIDTH), mat(w2), vec(RWKV_WIDTH), mat(a2), mat(g2),
                  vec(RWKV_WIDTH), vec(RWKV_WIDTH), vec(RWKV_WIDTH), vec(RWKV_WIDTH), vec(RWKV_WIDTH)],
        out_specs=pl.BlockSpec((L, RWKV_WIDTH), lambda b, c: (b * nc + c, 0)),
        out_shape=jax.ShapeDtypeStruct((n, RWKV_WIDTH), BF16),
        scratch_shapes=[pltpu.VMEM((RWKV_HEADS, HEAD_DIM, HEAD_DIM), F32)],
        compiler_params=_params("parallel", "arbitrary"),
        name="rwkv7",
    )(rw, rw, mu, w0, w2, a0, a2, g2, kk, ka, rk, lng, lnb)


def _merge_kernel(x_ref, oa_ref, ob_ref, gab_ref, wpa_ref, wpb_ref, wo_ref, o_ref):
    d = x_ref.shape[1]
    pa = jnp.dot(oa_ref[...], wpa_ref[...], preferred_element_type=F32)
    pb = jnp.dot(ob_ref[...], wpb_ref[...], preferred_element_type=F32)
    gab = gab_ref[...]
    merged = jax.nn.sigmoid(gab[:, :d]) * pa + jax.nn.sigmoid(gab[:, d:]) * pb
    o_ref[...] = x_ref[...] + _mm(merged, wo_ref[...])


def _merge(x2, oa, ob, gab, wpa, wpb, wo, tm):
    n, d = x2.shape
    row = lambda wd: pl.BlockSpec((tm, wd), lambda i: (i, 0))
    mat = lambda m: pl.BlockSpec(m.shape, lambda i: (0, 0))
    return pl.pallas_call(
        _merge_kernel,
        grid=(n // tm,),
        in_specs=[row(d), row(oa.shape[1]), row(ob.shape[1]), row(gab.shape[1]), mat(wpa), mat(wpb), mat(wo)],
        out_specs=row(d),
        out_shape=jax.ShapeDtypeStruct((n, d), F32),
        compiler_params=_params("parallel"),
        name="merge",
    )(x2, oa, ob, gab, wpa, wpb, wo)


def _ffn_kernel(seq, x_ref, xh_ref, g_ref, wv_ref, wg_ref, cwv_ref, cwg_ref, cbv_ref, cbg_ref, wd_ref, o_ref,
                h_ref, uv_ref, ug_ref):
    tm = x_ref.shape[0]
    i = pl.program_id(0)
    f = pl.program_id(1)

    @pl.when(f == 0)
    def _():
        first = lax.rem(i * tm, seq) == 0
        halo = jnp.where(first, 0.0, _rms(xh_ref[...], g_ref[...]))
        h_ref[0:HALO, :] = halo.astype(BF16)
        h_ref[HALO:, :] = _rms(x_ref[...], g_ref[...]).astype(BF16)

    h = h_ref[...]
    uv_ref[...] = jnp.dot(h, wv_ref[...], preferred_element_type=F32)
    ug_ref[...] = jnp.dot(h, wg_ref[...], preferred_element_type=F32)

    def conv(u_ref, cw_ref, cb_ref):
        acc = cb_ref[...] + cw_ref[0:1, :] * u_ref[pl.ds(HALO - 2, tm), :]
        acc = acc + cw_ref[1:2, :] * u_ref[pl.ds(HALO - 1, tm), :]
        return acc + cw_ref[2:3, :] * u_ref[pl.ds(HALO, tm), :]

    val = conv(uv_ref, cwv_ref, cbv_ref)
    gt = conv(ug_ref, cwg_ref, cbg_ref)
    y = _mm(gt * jax.nn.sigmoid(gt) * val, wd_ref[...])

    @pl.when(f == 0)
    def _():
        o_ref[...] = x_ref[...] + y

    @pl.when(f > 0)
    def _():
        o_ref[...] = o_ref[...] + y


def _ffn(x1, g, w_up, conv_w, conv_b, w_down, seq, tm, tf):
    n, d = x1.shape
    dff = w_down.shape[0]
    nf = dff // tf
    return pl.pallas_call(
        functools.partial(_ffn_kernel, seq),
        grid=(n // tm, nf),
        in_specs=[pl.BlockSpec((tm, d), lambda i, f: (i, 0)),
                  pl.BlockSpec((HALO, d), lambda i, f: (jnp.maximum(i * (tm // HALO) - 1, 0), 0)),
                  pl.BlockSpec((1, d), lambda i, f: (0, 0)),
                  pl.BlockSpec((d, tf), lambda i, f: (0, f)),
                  pl.BlockSpec((d, tf), lambda i, f: (0, nf + f)),
                  pl.BlockSpec((CONV_WIDTH, tf), lambda i, f: (0, f)),
                  pl.BlockSpec((CONV_WIDTH, tf), lambda i, f: (0, nf + f)),
                  pl.BlockSpec((1, tf), lambda i, f: (0, f)),
                  pl.BlockSpec((1, tf), lambda i, f: (0, nf + f)),
                  pl.BlockSpec((tf, d), lambda i, f: (f, 0))],
        out_specs=pl.BlockSpec((tm, d), lambda i, f: (i, 0)),
        out_shape=jax.ShapeDtypeStruct((n, d), F32),
        scratch_shapes=[pltpu.VMEM((tm + HALO, d), BF16),
                        pltpu.VMEM((tm + HALO, tf), F32),
                        pltpu.VMEM((tm + HALO, tf), F32)],
        compiler_params=_params("parallel", "arbitrary"),
        name="convffn",
    )(x1, x1, g, w_up, w_up, conv_w, conv_w, conv_b, conv_b, w_down)


def _t5_bucket(dist):
    n = jnp.maximum(dist, 0)
    max_exact = REL_BUCKETS // 2
    ratio = jnp.log(jnp.maximum(n, 1).astype(F32) / max_exact) / math.log(REL_MAX_DIST / max_exact)
    large = jnp.minimum(max_exact + (ratio * (REL_BUCKETS - max_exact)).astype(jnp.int32), REL_BUCKETS - 1)
    return jnp.where(n < max_exact, n, large)


def _bias_tables(rel_bias, seq, ncmp_pad):
    tq, tk = Q_TILE, K_TILE
    tab = rel_bias.astype(F32)

    def look(dist):
        bucket = _t5_bucket(dist)[None]
        out = jnp.zeros((tab.shape[1],) + dist.shape, F32)
        for i in range(REL_BUCKETS):
            out = jnp.where(bucket == i, tab[i].reshape((-1,) + (1,) * dist.ndim), out)
        return out

    t = jnp.arange(seq)[None, :]
    c = jnp.arange(ncmp_pad)[:, None]
    dist_c = t - (c * CMP_STRIDE + CMP_BLOCK - 1)
    n_cmp = seq // CMP_STRIDE - CMP_BLOCK // CMP_STRIDE + 1
    bias_c = jnp.where((dist_c >= 0) & (c < n_cmp), look(dist_c), NEG_INF)
    far = tab[REL_BUCKETS - 1][:, None, None]
    d = jnp.arange(tq)[None, :] - jnp.arange(tk)[:, None]
    n_b = -(-REL_MAX_DIST // tk)
    near = [jnp.where(d - o * tk >= 0, look(d - o * tk) - far, NEG_INF) for o in range(-n_b, tq // tk)]
    win = [jnp.where(d - o * tk < WINDOW, 0.0, NEG_INF).astype(F32) for o in _win_masked(tq, tk)]
    G, R = NSA_KV_GROUPS, NSA_GROUP
    by_group = lambda x: x.reshape(G, R, tk, tq).transpose(0, 2, 1, 3).reshape(G, tk, R * tq)
    bias_near = jnp.stack([by_group(x) for x in near])
    win_mask = jnp.stack([jnp.tile(x, (1, R)) for x in win])
    nq = seq // tq
    bias_c = bias_c.reshape(G, R, ncmp_pad, nq, tq).transpose(0, 3, 2, 1, 4).reshape(G, nq, ncmp_pad, R * tq)
    return bias_c, bias_near, win_mask


def _overlap_matrix(ncmp_pad, n_slc, n_cmp):
    m = np.zeros((ncmp_pad, n_slc), np.float32)
    ratio = SLC_BLOCK // CMP_STRIDE
    for j in range(n_slc):
        for o, wgt in enumerate(OVERLAP_W):
            cidx = ratio * j + o - (CMP_BLOCK // CMP_STRIDE - 1)
            if 0 <= cidx < n_cmp:
                m[cidx, j] += wgt
    return jnp.asarray(m.T)


def _layer(x, attn_norm_g, w_in, rel_bias, q_norm_g, k_norm_g, cmp_pe_k, cmp_w1_k, cmp_w2_k,
           cmp_pe_v, cmp_w1_v, cmp_w2_v, rwkv_mu, rwkv_w0, rwkv_w2, rwkv_a0, rwkv_a2, rwkv_g2,
           rwkv_k_k, rwkv_k_a, rwkv_r_k, rwkv_ln_g, rwkv_ln_b, w_proj_a, w_proj_b, w_out,
           ffn_norm_g, w_up, conv_w, conv_b, w_down):
    batch, seq, d = x.shape
    n = batch * seq
    assert seq % Q_TILE == 0 and Q_TILE % K_TILE == 0 and WINDOW % K_TILE == 0
    assert seq % CMP_STRIDE == 0
    assert seq // SLC_BLOCK <= LANES - HEAD_DIM
    x2 = x.reshape(n, d)
    row = lambda a: a.reshape(1, -1).astype(F32)

    kvw = 6 * NSA_KV_WIDTH
    o_q, o_kv, o_gn = 0, NSA_WIDTH, NSA_WIDTH + kvw
    o_rw = o_gn + 3 * NSA_HEADS
    o_gab = o_rw + RWKV_IN_WIDTH
    gn_pad = jnp.zeros((d, LANES - 3 * NSA_HEADS), w_in.dtype)
    w_cat = jnp.concatenate([w_in[:, o_rw:o_gab], w_in[:, o_q:o_kv], w_in[:, o_kv:o_gn], w_in[:, o_gab:],
                             w_in[:, o_gn:o_rw], gn_pad], axis=1).astype(BF16)
    widths = (RWKV_IN_WIDTH, NSA_WIDTH, kvw, 2 * d, LANES)
    rw, q, kv, gab, gn = _inproj(x2, row(attn_norm_g), w_cat, widths, tm=256)

    gk = row(k_norm_g)
    ksx, kwn, vst, vwt = _kvprep(kv, jnp.concatenate([gk, gk], axis=1), seq, tm=512)

    nchunk = seq // CMP_STRIDE
    n_cmp = nchunk - CMP_BLOCK // CMP_STRIDE + 1
    cflat = kv[:, :2 * NSA_KV_WIDTH].reshape(batch, nchunk, CMP_STRIDE, 2, NSA_KV_GROUPS, HEAD_DIM)
    cflat = cflat.transpose(3, 0, 4, 1, 2, 5).reshape(2, batch, NSA_KV_GROUPS, nchunk, CMP_STRIDE * HEAD_DIM)
    pe = jnp.stack([cmp_pe_k.reshape(1, -1), cmp_pe_v.reshape(1, -1)]).astype(F32)
    w1 = jnp.stack([cmp_w1_k, cmp_w1_v]).astype(BF16)
    w2 = jnp.stack([cmp_w2_k, cmp_w2_v]).astype(BF16)
    kcv = _compress(cflat, pe, w1, w2, gk)

    bias_c, bias_near, win_mask = _bias_tables(rel_bias, seq, nchunk)
    wov = _overlap_matrix(nchunk, seq // SLC_BLOCK, n_cmp)
    o_a = _attention(q, gn, kcv[0], jnp.swapaxes(kcv[1], -1, -2), ksx, vst, kwn, vwt, bias_c, bias_near, win_mask,
                     q_norm_g.reshape(-1, 1).astype(F32), wov, batch, seq)

    vecs = tuple(row(a) for a in (rwkv_mu, rwkv_w0, rwkv_a0, rwkv_k_k, rwkv_k_a, rwkv_r_k, rwkv_ln_g, rwkv_ln_b))
    o_b = _rwkv(rw, vecs, rwkv_w2.astype(BF16), rwkv_a2.astype(BF16), rwkv_g2.astype(BF16), batch, seq)

    x1 = _merge(x2, o_a, o_b, gab, w_proj_a.astype(BF16), w_proj_b.astype(BF16), w_out.astype(BF16), tm=512)
    dff = w_down.shape[0]
    out = _ffn(x1, row(ffn_norm_g), w_up.astype(BF16), conv_w.astype(F32), row(conv_b), w_down.astype(BF16),
               seq, tm=512, tf=dff // 2)
    return out.reshape(batch, seq, d)


def kernel(x, attn_norm_g, w_in, rel_bias, q_norm_g, k_norm_g, cmp_pe_k, cmp_w1_k, cmp_w2_k, cmp_pe_v, cmp_w1_v,
           cmp_w2_v, rwkv_mu, rwkv_w0, rwkv_w2, rwkv_a0, rwkv_a2, rwkv_g2, rwkv_k_k, rwkv_k_a, rwkv_r_k,
           rwkv_ln_g, rwkv_ln_b, w_proj_a, w_proj_b, w_out, ffn_norm_g, w_up, conv_w, conv_b, w_down):
    per_layer = (attn_norm_g, w_in, None, q_norm_g, k_norm_g, cmp_pe_k, cmp_w1_k, cmp_w2_k, cmp_pe_v, cmp_w1_v,
                 cmp_w2_v, rwkv_mu, rwkv_w0, rwkv_w2, rwkv_a0, rwkv_a2, rwkv_g2, rwkv_k_k, rwkv_k_a, rwkv_r_k,
                 rwkv_ln_g, rwkv_ln_b, w_proj_a, w_proj_b, w_out, ffn_norm_g, w_up, conv_w, conv_b, w_down)
    for l in range(attn_norm_g.shape[0]):
        args = [rel_bias if p is None else p[l] for p in per_layer]
        x = _layer(x, *args)
    return x
```

```python
import functools
import math

import numpy as np
import jax
import jax.numpy as jnp
from jax import lax
from jax.experimental import pallas as pl
from jax.experimental.pallas import tpu as pltpu

F32 = jnp.float32
BF16 = jnp.bfloat16

HEAD_DIM = 64
NSA_HEADS = 8
NSA_KV_GROUPS = 2
NSA_GROUP = NSA_HEADS // NSA_KV_GROUPS
NSA_WIDTH = NSA_HEADS * HEAD_DIM
NSA_KV_WIDTH = NSA_KV_GROUPS * HEAD_DIM
CMP_BLOCK = 32
CMP_STRIDE = 16
CMP_HIDDEN = 256
SLC_BLOCK = 64
SLC_TOPK = 16
OVERLAP_W = (1, 2, 2, 2, 1)
WINDOW = 512
REL_BUCKETS = 32
REL_MAX_DIST = 128
RWKV_HEADS = 8
RWKV_WIDTH = RWKV_HEADS * HEAD_DIM
LORA_W = 64
LORA_A = 64
LORA_G = 128
RWKV_IN_WIDTH = 3 * RWKV_WIDTH + LORA_W + LORA_A + LORA_G
GN_EPS = 64e-5
CONV_WIDTH = 3
RMS_EPS = 1e-6
NEG_INF = -1e30
FORCE = 1e9
M_FLOOR = -1e20

LANES = 128
VMEM_LIMIT = 56 * 1024 * 1024

Q_TILE = 256
K_TILE = 256
VT_ROWS = 80
CHUNK = 128
HALO = 16

_NT = (((1,), (1,)), ((), ()))


def _params(*sem):
    return pltpu.CompilerParams(dimension_semantics=sem, vmem_limit_bytes=VMEM_LIMIT)


def _mm(a, b):
    return jnp.dot(a.astype(BF16), b.astype(BF16), preferred_element_type=F32)


def _mm_nt(a, b):
    return lax.dot_general(a.astype(BF16), b.astype(BF16), _NT, preferred_element_type=F32)


def _mm_f32(a, b):
    return jnp.dot(a, b, preferred_element_type=F32, precision=lax.Precision.HIGHEST)


def _rms(x, g):
    return x * lax.rsqrt(jnp.mean(x * x, axis=-1, keepdims=True) + RMS_EPS) * g


def _inproj_kernel(x_ref, g_ref, w_ref, *out_refs):
    h = _rms(x_ref[...], g_ref[...]).astype(BF16)
    off = 0
    for ref in out_refs:
        n = ref.shape[-1]
        ref[...] = jnp.dot(h, w_ref[:, off:off + n], preferred_element_type=F32).astype(ref.dtype)
        off += n


def _inproj(x2, g, w, widths, tm):
    n, d = x2.shape
    return pl.pallas_call(
        _inproj_kernel,
        grid=(n // tm,),
        in_specs=[pl.BlockSpec((tm, d), lambda i: (i, 0)),
                  pl.BlockSpec((1, d), lambda i: (0, 0)),
                  pl.BlockSpec(w.shape, lambda i: (0, 0))],
        out_specs=[pl.BlockSpec((tm, wd), lambda i: (i, 0)) for wd in widths],
        out_shape=[jax.ShapeDtypeStruct((n, wd), F32) for wd in widths],
        compiler_params=_params("parallel"),
        name="inproj",
    )(x2, g, w)


def _kvprep_kernel(seq, kv_ref, gk_ref, ksx_ref, kwn_ref, vs_ref, vw_ref):
    tm = kv_ref.shape[0]
    kv = kv_ref[...]
    lane = lax.broadcasted_iota(jnp.int32, (tm, LANES), 1)
    lo = lane < HEAD_DIM

    def norm2(x):
        x2 = x * x
        s_lo = jnp.sum(jnp.where(lo, x2, 0.0), axis=-1, keepdims=True)
        s_hi = jnp.sum(jnp.where(lo, 0.0, x2), axis=-1, keepdims=True)
        ms = jnp.where(lo, s_lo, s_hi) * (1.0 / HEAD_DIM)
        return x * lax.rsqrt(ms + RMS_EPS) * gk_ref[...]

    ks = norm2(kv[:, 2 * LANES:3 * LANES])
    kw = norm2(kv[:, 4 * LANES:5 * LANES])
    row = lax.rem(pl.program_id(0) * tm, seq) + lax.broadcasted_iota(jnp.int32, (tm, LANES), 0)
    onehot = jnp.where(lane - HEAD_DIM == row // SLC_BLOCK, 1.0, 0.0)
    ksx_ref[0] = jnp.where(lo, ks, onehot).astype(BF16)
    ksx_ref[1] = jnp.where(lo, pltpu.roll(ks, HEAD_DIM, axis=1), onehot).astype(BF16)
    kwn_ref[0] = kw[:, :HEAD_DIM].astype(BF16)
    kwn_ref[1] = kw[:, HEAD_DIM:].astype(BF16)
    tail = jnp.where(lax.broadcasted_iota(jnp.int32, (VT_ROWS - HEAD_DIM, K_TILE), 0) == 0, 1.0, 0.0)
    for col, ref in ((3, vs_ref), (5, vw_ref)):
        for t in range(tm // K_TILE):
            vt = kv[t * K_TILE:(t + 1) * K_TILE, col * LANES:(col + 1) * LANES].T
            for g in range(NSA_KV_GROUPS):
                ref[g, t] = jnp.concatenate([vt[g * HEAD_DIM:(g + 1) * HEAD_DIM], tail], axis=0).astype(BF16)


def _kvprep(kv, gk2, seq, tm):
    n = kv.shape[0]
    vt_spec = lambda: pl.BlockSpec((2, tm // K_TILE, VT_ROWS, K_TILE), lambda i: (0, i, 0, 0))
    vt_shape = jax.ShapeDtypeStruct((2, n // K_TILE, VT_ROWS, K_TILE), BF16)
    return pl.pallas_call(
        functools.partial(_kvprep_kernel, seq),
        grid=(n // tm,),
        in_specs=[pl.BlockSpec((tm, kv.shape[1]), lambda i: (i, 0)),
                  pl.BlockSpec((1, LANES), lambda i: (0, 0))],
        out_specs=[pl.BlockSpec((2, tm, LANES), lambda i: (0, i, 0)),
                   pl.BlockSpec((2, tm, HEAD_DIM), lambda i: (0, i, 0)), vt_spec(), vt_spec()],
        out_shape=[jax.ShapeDtypeStruct((2, n, LANES), BF16), jax.ShapeDtypeStruct((2, n, HEAD_DIM), BF16),
                   vt_shape, vt_shape],
        compiler_params=_params("parallel"),
        name="kvprep",
    )(kv, gk2)


def _compress_kernel(k_ref, v_ref, pe_ref, w1_ref, w2_ref, gk_ref, kc_ref, vc_ref):
    nchunk = k_ref.shape[0] // CMP_STRIDE
    span = CMP_BLOCK // CMP_STRIDE
    G = NSA_KV_GROUPS
    acc = [[[jnp.zeros((nchunk, CMP_HIDDEN), F32) for _ in range(span)] for _ in range(G)] for _ in range(2)]
    for p in range(CMP_STRIDE):
        for s, ref in enumerate((k_ref, v_ref)):
            x = ref[pl.ds(p, nchunk, stride=CMP_STRIDE), :]
            for g in range(G):
                xs = x[:, g * HEAD_DIM:(g + 1) * HEAD_DIM]
                for half in range(span):
                    q = half * CMP_STRIDE + p
                    acc[s][g][half] = acc[s][g][half] + _mm(xs + pe_ref[s, q:q + 1, :], w1_ref[s, q])
    for s, out_ref in ((0, kc_ref), (1, vc_ref)):
        for g in range(G):
            hid = acc[s][g][0]
            for half in range(1, span):
                hid = hid + pltpu.roll(acc[s][g][half], nchunk - half, axis=0)
            out = _mm(jax.nn.gelu(hid), w2_ref[s])
            out_ref[0, g] = _rms(out, gk_ref[...]) if s == 0 else out


def _compress(kv, pe, w1, w2, gk, batch, seq):
    nchunk = seq // CMP_STRIDE
    out_spec = lambda: pl.BlockSpec((1, NSA_KV_GROUPS, nchunk, HEAD_DIM), lambda b: (b, 0, 0, 0))
    out_shape = jax.ShapeDtypeStruct((batch, NSA_KV_GROUPS, nchunk, HEAD_DIM), F32)
    const = lambda a: pl.BlockSpec(a.shape, lambda b: (0,) * a.ndim)
    return pl.pallas_call(
        _compress_kernel,
        grid=(batch,),
        in_specs=[pl.BlockSpec((seq, NSA_KV_WIDTH), lambda b: (b, 0)), pl.BlockSpec((seq, NSA_KV_WIDTH), lambda b: (b, 1)),
                  const(pe), const(w1), const(w2), const(gk)],
        out_specs=[out_spec(), out_spec()],
        out_shape=[out_shape, out_shape],
        compiler_params=_params("parallel"),
        name="compress",
    )(kv, kv, pe, w1, w2, gk)


def _rank_select(score_t, cur_t):
    nb = score_t.shape[0]
    jj = lax.broadcasted_iota(jnp.int32, score_t.shape, 0)
    rank = jnp.zeros(score_t.shape, F32)
    for j in range(nb):
        row = score_t[j:j + 1, :]
        gt = jnp.where(row > score_t, 1.0, 0.0)
        ge = jnp.where(row >= score_t, 1.0, 0.0)
        rank = rank + jnp.where(jj > j, ge, gt)
    keep = jnp.where(rank < float(min(SLC_TOPK, nb)), 1.0, 0.0) * jnp.where(jj <= cur_t, 1.0, 0.0)
    return jnp.where(keep > 0.5, 0.0, NEG_INF)


def _win_masked(tq, tk):
    return [o for o in range(-(WINDOW // tk), 0) if tq - 1 - o * tk >= WINDOW]


def _scores(ks, ws):
    return [jnp.dot(k, w, preferred_element_type=F32) for k, w in zip(ks, ws)]


def _softmax_update(m_ref, acc_ref, slots, s, vts):
    m_old = [m_ref[i] for i in slots]
    m_new = [jnp.maximum(mo, jnp.max(x, axis=0, keepdims=True)) for mo, x in zip(m_old, s)]
    pr = [jnp.exp(x - mn).astype(BF16) for x, mn in zip(s, m_new)]
    pv = [jnp.dot(vt, y, preferred_element_type=F32) for vt, y in zip(vts, pr)]
    acc = [jnp.exp(mo - mn) * acc_ref[i] + z for mo, mn, i, z in zip(m_old, m_new, slots, pv)]
    for i, mn, ac in zip(slots, m_new, acc):
        m_ref[i] = mn
        acc_ref[i] = ac


def _softmax_steps(m_ref, acc_ref, probs):
    s = _scores([p[1] for p in probs], [p[3] for p in probs])
    s = [x if p[4] is None else x + p[4][...] for x, p in zip(s, probs)]
    _softmax_update(m_ref, acc_ref, [p[0] for p in probs], s, [p[2] for p in probs])


def _attn_kernel(q_ref, gn_ref, kc_ref, vct_ref, ksx_ref, vst_ref, kwn_ref, vwt_ref, bc_ref, bn_ref, wm_ref,
                 gq_ref, wov_ref, o_ref, m_ref, acc_ref, sa_ref, sb_ref):
    tq = q_ref.shape[0]
    tk = K_TILE
    nblk = wov_ref.shape[0]
    G, R = NSA_KV_GROUPS, NSA_GROUP
    qt = pl.program_id(1)
    n_q = tq // tk
    n_b = -(-REL_MAX_DIST // tk)
    j0 = qt * n_q
    qT = q_ref[...].T
    gates = jax.nn.sigmoid(gn_ref[...].T)
    gq = gq_ref[...]

    qn, qx, o_cmp = [], [], []
    for g in range(G):
        xs = []
        for h in range(g * R, (g + 1) * R):
            x = qT[h * HEAD_DIM:(h + 1) * HEAD_DIM]
            x = x * lax.rsqrt(jnp.mean(x * x, axis=0, keepdims=True) + RMS_EPS) * gq * HEAD_DIM ** -0.5
            xs.append(x.astype(BF16))
        qn.append(jnp.concatenate(xs, axis=1))

    for g in range(G):
        s = jnp.dot(kc_ref[0, g].astype(BF16), qn[g], preferred_element_type=F32) + bc_ref[0, g]
        m = jnp.maximum(jnp.max(s, axis=0, keepdims=True), M_FLOOR)
        e = jnp.exp(s - m)
        p = e / jnp.maximum(jnp.sum(e, axis=0, keepdims=True), 1e-30)
        o_cmp.append(jnp.dot(vct_ref[0, g].astype(BF16), p.astype(BF16), preferred_element_type=F32))
        psum = p[:, 0:tq]
        for r in range(1, R):
            psum = psum + p[:, r * tq:(r + 1) * tq]
        imp = _mm_f32(wov_ref[...], psum)

        blk = lax.broadcasted_iota(jnp.int32, imp.shape, 0)
        cur = (qt * tq + lax.broadcasted_iota(jnp.int32, imp.shape, 1)) // SLC_BLOCK
        forced = (blk == 0) | (blk == cur) | (blk == cur - 1)
        score = jnp.where(forced, FORCE, jnp.where(blk <= cur, imp, -FORCE))
        neg = _rank_select(score, cur)
        if nblk < LANES - HEAD_DIM:
            neg = jnp.concatenate([neg, jnp.zeros((LANES - HEAD_DIM - nblk, tq), F32)], axis=0)
        neg = neg.astype(BF16)
        qx.append(jnp.concatenate([qn[g], jnp.concatenate([neg] * R, axis=1)], axis=0))

    m_ref[...] = jnp.full(m_ref.shape, M_FLOOR, F32)
    acc_ref[...] = jnp.zeros(acc_ref.shape, F32)

    def slc_probs(j, bias):
        sl = pl.ds(pl.multiple_of(j * tk, tk), tk)
        return [(g, ksx_ref[g, sl, :], vst_ref[g, j], qx[g], None if bias is None else bias.at[g])
                for g in range(G)]

    def win_probs(j, bias, per_group):
        sl = pl.ds(pl.multiple_of(j * tk, tk), tk)
        pick = lambda g: None if bias is None else (bias.at[g] if per_group else bias)
        return [(G + g, kwn_ref[g, sl, :], vwt_ref[g, j], qn[g], pick(g)) for g in range(G)]

    n_far = jnp.maximum(j0 - n_b, 0)
    groups = list(range(G))

    def far_scores(j, buf):
        sl = pl.ds(pl.multiple_of(j * tk, tk), tk)
        for g, x in zip(groups, _scores([ksx_ref[g, sl, :] for g in groups], qx)):
            buf[g] = x

    def far_update(j, buf):
        _softmax_update(m_ref, acc_ref, groups, [buf[g] for g in groups], [vst_ref[g, j] for g in groups])

    far_scores(0, sa_ref)

    def far_body(i, carry):
        far_scores(2 * i + 1, sb_ref)
        far_update(2 * i, sa_ref)
        far_scores(2 * i + 2, sa_ref)
        far_update(2 * i + 1, sb_ref)
        return carry

    lax.fori_loop(0, n_far // 2, far_body, 0)

    @pl.when(n_far % 2 == 1)
    def _():
        far_update(n_far - 1, sa_ref)

    masked = _win_masked(tq, tk)
    by_start = {}
    for o in range(-(WINDOW // tk), n_q):
        probs = []
        if o >= -n_b:
            probs += [functools.partial(slc_probs, j0 + o, bn_ref.at[o + n_b])]
            probs += [functools.partial(win_probs, j0 + o, bn_ref.at[o + n_b], True)]
        elif o in masked:
            probs += [functools.partial(win_probs, j0 + o, wm_ref.at[masked.index(o)], False)]
        else:
            probs += [functools.partial(win_probs, j0 + o, None, False)]
        by_start.setdefault(max(-(o // n_q), 0), []).append(probs)

    for first_qt in sorted(by_start, reverse=True):
        def run(tiles=by_start[first_qt]):
            for probs in tiles:
                _softmax_steps(m_ref, acc_ref, [p for f in probs for p in f()])
        if first_qt == 0:
            run()
        else:
            pl.when(qt >= first_qt)(run)

    outs = []
    for g in range(G):
        acc = acc_ref[g]
        o_slc = acc[:HEAD_DIM] / acc[HEAD_DIM:HEAD_DIM + 1]
        acc = acc_ref[G + g]
        o_win = acc[:HEAD_DIM] / acc[HEAD_DIM:HEAD_DIM + 1]
        for r in range(R):
            h = g * R + r
            cols = slice(r * tq, (r + 1) * tq)
            outs.append(gates[3 * h:3 * h + 1] * o_cmp[g][:, cols] + gates[3 * h + 1:3 * h + 2] * o_slc[:, cols]
                        + gates[3 * h + 2:3 * h + 3] * o_win[:, cols])
    o_ref[...] = jnp.concatenate(outs, axis=0).T.astype(o_ref.dtype)


def _attention(q, gn, kc, vct, ksx, vst, kwn, vwt, bias_c, bias_near, win_mask, gq, wov, batch, seq):
    n = q.shape[0]
    tq = Q_TILE
    nq = seq // tq
    ncmp = kc.shape[2]
    row = lambda b, i: (b * nq + i, 0)
    whole = lambda b, i: (0, b, 0)
    tiles = lambda b, i: (0, b, 0, 0)
    const = lambda a: pl.BlockSpec(a.shape, lambda b, i: (0,) * a.ndim)
    return pl.pallas_call(
        _attn_kernel,
        grid=(batch, nq),
        in_specs=[pl.BlockSpec((tq, NSA_WIDTH), row),
                  pl.BlockSpec((tq, LANES), row),
                  pl.BlockSpec((1, NSA_KV_GROUPS, ncmp, HEAD_DIM), lambda b, i: (b, 0, 0, 0)),
                  pl.BlockSpec((1, NSA_KV_GROUPS, HEAD_DIM, ncmp), lambda b, i: (b, 0, 0, 0)),
                  pl.BlockSpec((2, seq, LANES), whole),
                  pl.BlockSpec((2, seq // K_TILE, VT_ROWS, K_TILE), tiles),
                  pl.BlockSpec((2, seq, HEAD_DIM), whole),
                  pl.BlockSpec((2, seq // K_TILE, VT_ROWS, K_TILE), tiles),
                  pl.BlockSpec((1, NSA_KV_GROUPS, ncmp, NSA_GROUP * tq), lambda b, i: (i, 0, 0, 0)),
                  const(bias_near), const(win_mask), const(gq), const(wov)],
        out_specs=pl.BlockSpec((tq, NSA_WIDTH), row),
        out_shape=jax.ShapeDtypeStruct((n, NSA_WIDTH), BF16),
        scratch_shapes=[pltpu.VMEM((2 * NSA_KV_GROUPS, 1, NSA_GROUP * tq), F32),
                        pltpu.VMEM((2 * NSA_KV_GROUPS, VT_ROWS, NSA_GROUP * tq), F32),
                        pltpu.VMEM((NSA_KV_GROUPS, K_TILE, NSA_GROUP * tq), F32),
                        pltpu.VMEM((NSA_KV_GROUPS, K_TILE, NSA_GROUP * tq), F32)],
        compiler_params=_params("parallel", "parallel"),
        name="nsa_attention",
    )(q, gn, kc, vct, ksx, vst, kwn, vwt, bias_c, bias_near, win_mask, gq, wov)


def _softplus(z):
    return jnp.maximum(z, 0.0) + jnp.log(1.0 + jnp.exp(-jnp.abs(z)))


def _rwkv_kernel(x_ref, xp_ref, mu_ref, w0_ref, w2_ref, a0_ref, a2_ref, g2_ref, kk_ref, ka_ref, rk_ref,
                 lng_ref, lnb_ref, o_ref, st_ref):
    L = x_ref.shape[0]
    W = RWKV_WIDTH
    N = HEAD_DIM
    c = pl.program_id(1)

    @pl.when(c == 0)
    def _():
        st_ref[...] = jnp.zeros(st_ref.shape, F32)

    x = x_ref[...]
    prev = jnp.where(c > 0, xp_ref[xp_ref.shape[0] - 1:, :], 0.0)
    row_id = lax.broadcasted_iota(jnp.int32, x.shape, 0)
    shifted = jnp.where(row_id == 0, prev, pltpu.roll(x, 1, axis=0))
    xl = x + (shifted - x) * mu_ref[...]
    r = xl[:, 0:W]
    k = xl[:, W:2 * W]
    v = xl[:, 2 * W:3 * W]
    xw = xl[:, 3 * W:3 * W + LORA_W]
    xa = xl[:, 3 * W + LORA_W:3 * W + LORA_W + LORA_A]
    xg = xl[:, 3 * W + LORA_W + LORA_A:]
    w = -_softplus(-(w0_ref[...] + _mm(jnp.tanh(xw), w2_ref[...]))) - 0.5
    ld = -jnp.exp(w)
    a = jax.nn.sigmoid(a0_ref[...] + _mm(xa, a2_ref[...]))
    gate = _mm(jax.nn.sigmoid(xg), g2_ref[...])
    kkv = k * kk_ref[...]
    k2 = k * (1.0 + (a - 1.0) * ka_ref[...])

    ti = lax.broadcasted_iota(jnp.int32, (L, L), 0)
    si = lax.broadcasted_iota(jnp.int32, (L, L), 1)
    incl = si <= ti
    strict = si < ti
    cl = _mm_f32(jnp.where(incl, 1.0, 0.0), ld)
    cl_end = cl[L - 1:L, :]
    e_pos = jnp.exp(cl)
    e_neg = jnp.exp(-cl)
    e_prev = jnp.exp(cl - ld)
    e_end = jnp.exp(cl_end - cl)
    r_t = r * e_pos
    k_t = k2 * e_neg
    k_e = k2 * e_end
    eye = jnp.where(ti == si, 1.0, 0.0)
    pl_cols = [e_pos[:, m * LANES:(m + 1) * LANES].T[:, L - 1:L] for m in range(W // LANES)]

    H = range(RWKV_HEADS)
    hs = [slice(h * N, (h + 1) * N) for h in H]
    kk_n = [kkv[:, s] for s in hs]
    kk_n = [x / jnp.maximum(jnp.sqrt(jnp.sum(x * x, axis=-1, keepdims=True)), 1e-12) for x in kk_n]
    a_t = [-kk_n[h] * e_prev[:, hs[h]] for h in H]
    bv = [kk_n[h] * a[:, hs[h]] for h in H]
    b_t = [bv[h] * e_neg[:, hs[h]] for h in H]
    b_e = [bv[h] * e_end[:, hs[h]] for h in H]
    v_h = [v[:, s] for s in hs]
    rt_h = [r_t[:, s] for s in hs]

    aa = [_mm_nt(jnp.concatenate([a_t[h], rt_h[h]], axis=0), jnp.concatenate([k_t[:, hs[h]], b_t[h]], axis=0))
          for h in H]
    a_ak = [jnp.where(strict, x[:L, :L], 0.0) for x in aa]
    a_ab = [jnp.where(strict, x[:L, L:], 0.0) for x in aa]
    a_rk = [jnp.where(incl, x[L:, :L], 0.0) for x in aa]
    a_rb = [jnp.where(incl, x[L:, L:], 0.0) for x in aa]

    tinv = [eye + x for x in a_ab]
    pw = [_mm(x, x) for x in a_ab]
    span = 2
    while span < L:
        both = [_mm(jnp.concatenate([tinv[h], pw[h]], axis=0), pw[h]) for h in H]
        tinv = [tinv[h] + both[h][:L] for h in H]
        pw = [x[L:] for x in both]
        span *= 2

    av = [_mm(jnp.concatenate([a_ak[h], a_rk[h]], axis=0), v_h[h]) for h in H]
    tw = [_mm(tinv[h], jnp.concatenate([a_t[h], av[h][:L]], axis=1)) for h in H]
    kb_t = [jnp.concatenate([k_e[:, hs[h]], b_e[h]], axis=1).T for h in H]
    kv_loc = [_mm(kb_t[h][:N], v_h[h]) for h in H]

    st = [st_ref[h] for h in H]
    ws = [_mm(jnp.concatenate([tw[h][:, :N], rt_h[h]], axis=0), st[h]) for h in H]
    u = [ws[h][:L] + tw[h][:, N:] for h in H]
    y = [ws[h][L:] + av[h][L:] + _mm(a_rb[h], u[h]) for h in H]
    pcol = [pl_cols[(h * N) // LANES][(h * N) % LANES:(h * N) % LANES + N, :] for h in H]
    st_ref[...] = jnp.stack([pcol[h] * st[h] + kv_loc[h] + _mm(kb_t[h][N:], u[h]) for h in H])

    outs = []
    for h in H:
        mean = jnp.mean(y[h], axis=-1, keepdims=True)
        var = jnp.mean(jnp.square(y[h] - mean), axis=-1, keepdims=True)
        yn = (y[h] - mean) * lax.rsqrt(var + GN_EPS) * lng_ref[:, hs[h]] + lnb_ref[:, hs[h]]
        bonus = jnp.sum(r[:, hs[h]] * k2[:, hs[h]] * rk_ref[:, hs[h]], axis=-1, keepdims=True)
        outs.append(yn + bonus * v_h[h])
    o_ref[...] = (jnp.concatenate(outs, axis=1) * gate).astype(o_ref.dtype)


def _rwkv(rw, vecs, w2, a2, g2, batch, seq):
    n, width = rw.shape
    L = min(CHUNK, seq)
    nc = seq // L
    sub = 8
    vec = lambda wd: pl.BlockSpec((1, wd), lambda b, c: (0, 0))
    mat = lambda m: pl.BlockSpec(m.shape, lambda b, c: (0, 0))
    mu, w0, a0, kk, ka, rk, lng, lnb = vecs
    return pl.pallas_call(
        _rwkv_kernel,
        grid=(batch, nc),
        in_specs=[pl.BlockSpec((L, width), lambda b, c: (b * nc + c, 0)),
                  pl.BlockSpec((sub, width), lambda b, c: (jnp.maximum((b * nc + c) * (L // sub) - 1, 0), 0)),
                  vec(width), vec(RWKV_WIDTH), mat(w2), vec(RWKV_WIDTH), mat(a2), mat(g2),
                  vec(RWKV_WIDTH), vec(RWKV_WIDTH), vec(RWKV_WIDTH), vec(RWKV_WIDTH), vec(RWKV_WIDTH)],
        out_specs=pl.BlockSpec((L, RWKV_WIDTH), lambda b, c: (b * nc + c, 0)),
        out_shape=jax.ShapeDtypeStruct((n, RWKV_WIDTH), BF16),
        scratch_shapes=[pltpu.VMEM((RWKV_HEADS, HEAD_DIM, HEAD_DIM), F32)],
        compiler_params=_params("parallel", "arbitrary"),
        name="rwkv7",
    )(rw, rw, mu, w0, w2, a0, a2, g2, kk, ka, rk, lng, lnb)


def _merge_kernel(x_ref, oa_ref, ob_ref, gab_ref, wpa_ref, wpb_ref, wo_ref, o_ref):
    d = x_ref.shape[1]
    pa = jnp.dot(oa_ref[...], wpa_ref[...], preferred_element_type=F32)
    pb = jnp.dot(ob_ref[...], wpb_ref[...], preferred_element_type=F32)
    gab = gab_ref[...]
    merged = jax.nn.sigmoid(gab[:, :d]) * pa + jax.nn.sigmoid(gab[:, d:]) * pb
    o_ref[...] = x_ref[...] + _mm(merged, wo_ref[...])


def _merge(x2, oa, ob, gab, wpa, wpb, wo, tm):
    n, d = x2.shape
    row = lambda wd: pl.BlockSpec((tm, wd), lambda i: (i, 0))
    mat = lambda m: pl.BlockSpec(m.shape, lambda i: (0, 0))
    return pl.pallas_call(
        _merge_kernel,
        grid=(n // tm,),
        in_specs=[row(d), row(oa.shape[1]), row(ob.shape[1]), row(gab.shape[1]), mat(wpa), mat(wpb), mat(wo)],
        out_specs=row(d),
        out_shape=jax.ShapeDtypeStruct((n, d), F32),
        compiler_params=_params("parallel"),
        name="merge",
    )(x2, oa, ob, gab, wpa, wpb, wo)


def _ffn_kernel(seq, x_ref, xh_ref, g_ref, wv_ref, wg_ref, cwv_ref, cwg_ref, cbv_ref, cbg_ref, wd_ref, o_ref,
                h_ref, uv_ref, ug_ref):
    tm = x_ref.shape[0]
    i = pl.program_id(0)
    f = pl.program_id(1)

    @pl.when(f == 0)
    def _():
        first = lax.rem(i * tm, seq) == 0
        halo = jnp.where(first, 0.0, _rms(xh_ref[...], g_ref[...]))
        h_ref[0:HALO, :] = halo.astype(BF16)
        h_ref[HALO:, :] = _rms(x_ref[...], g_ref[...]).astype(BF16)

    h = h_ref[...]
    uv_ref[...] = jnp.dot(h, wv_ref[...], preferred_element_type=F32)
    ug_ref[...] = jnp.dot(h, wg_ref[...], preferred_element_type=F32)

    def conv(u_ref, cw_ref, cb_ref):
        acc = cb_ref[...] + cw_ref[0:1, :] * u_ref[pl.ds(HALO - 2, tm), :]
        acc = acc + cw_ref[1:2, :] * u_ref[pl.ds(HALO - 1, tm), :]
        return acc + cw_ref[2:3, :] * u_ref[pl.ds(HALO, tm), :]

    val = conv(uv_ref, cwv_ref, cbv_ref)
    gt = conv(ug_ref, cwg_ref, cbg_ref)
    y = _mm(gt * jax.nn.sigmoid(gt) * val, wd_ref[...])

    @pl.when(f == 0)
    def _():
        o_ref[...] = x_ref[...] + y

    @pl.when(f > 0)
    def _():
        o_ref[...] = o_ref[...] + y


def _ffn(x1, g, w_up, conv_w, conv_b, w_down, seq, tm, tf):
    n, d = x1.shape
    dff = w_down.shape[0]
    nf = dff // tf
    return pl.pallas_call(
        functools.partial(_ffn_kernel, seq),
        grid=(n // tm, nf),
        in_specs=[pl.BlockSpec((tm, d), lambda i, f: (i, 0)),
                  pl.BlockSpec((HALO, d), lambda i, f: (jnp.maximum(i * (tm // HALO) - 1, 0), 0)),
                  pl.BlockSpec((1, d), lambda i, f: (0, 0)),
                  pl.BlockSpec((d, tf), lambda i, f: (0, f)),
                  pl.BlockSpec((d, tf), lambda i, f: (0, nf + f)),
                  pl.BlockSpec((CONV_WIDTH, tf), lambda i, f: (0, f)),
                  pl.BlockSpec((CONV_WIDTH, tf), lambda i, f: (0, nf + f)),
                  pl.BlockSpec((1, tf), lambda i, f: (0, f)),
                  pl.BlockSpec((1, tf), lambda i, f: (0, nf + f)),
                  pl.BlockSpec((tf, d), lambda i, f: (f, 0))],
        out_specs=pl.BlockSpec((tm, d), lambda i, f: (i, 0)),
        out_shape=jax.ShapeDtypeStruct((n, d), F32),
        scratch_shapes=[pltpu.VMEM((tm + HALO, d), BF16),
                        pltpu.VMEM((tm + HALO, tf), F32),
                        pltpu.VMEM((tm + HALO, tf), F32)],
        compiler_params=_params("parallel", "arbitrary"),
        name="convffn",
    )(x1, x1, g, w_up, w_up, conv_w, conv_w, conv_b, conv_b, w_down)


def _t5_bucket(dist):
    n = np.maximum(dist, 0)
    max_exact = REL_BUCKETS // 2
    ratio = np.log(np.maximum(n, 1).astype(np.float32) / max_exact) / math.log(REL_MAX_DIST / max_exact)
    large = np.minimum(max_exact + (ratio * (REL_BUCKETS - max_exact)).astype(np.int32), REL_BUCKETS - 1)
    return np.where(n < max_exact, n, large).astype(np.int32)


def _bias_tables(rel_bias, seq, ncmp):
    tq, tk = Q_TILE, K_TILE
    G, R = NSA_KV_GROUPS, NSA_GROUP
    nq = seq // tq
    step = tq // CMP_STRIDE
    tab = rel_bias.astype(F32)

    off = (nq - 1) * step
    d_c = np.arange(tq)[None, :] - (np.arange(ncmp + off)[:, None] - off) * CMP_STRIDE - (CMP_BLOCK - 1)
    n_b = -(-REL_MAX_DIST // tk)
    d0 = np.arange(tq)[None, :] - np.arange(tk)[:, None]
    d_n = np.stack([d0 - o * tk for o in range(-n_b, tq // tk)])

    buckets = np.concatenate([_t5_bucket(d_c).reshape(-1), _t5_bucket(d_n).reshape(-1)])
    onehot = (jnp.asarray(buckets)[:, None] == jnp.arange(REL_BUCKETS, dtype=jnp.int32)[None, :]).astype(F32)
    vals = jnp.dot(onehot, tab, precision=lax.Precision.HIGHEST, preferred_element_type=F32).T
    by_group = lambda x: x.reshape(G, R, -1, tq).transpose(0, 2, 1, 3).reshape(G, -1, R * tq)
    tile_r = lambda m: np.tile(m, (1, R))

    base = jnp.where(jnp.asarray(tile_r(d_c >= 0)), by_group(vals[:, :d_c.size]), NEG_INF)
    bias_c = jnp.stack([base[:, off - i * step:off - i * step + ncmp] for i in range(nq)])

    far = jnp.repeat(tab[REL_BUCKETS - 1], tq).reshape(G, 1, R * tq)
    near = by_group(vals[:, d_c.size:]) - far
    near = jnp.where(jnp.asarray(tile_r(d_n.reshape(-1, tq) >= 0)), near, NEG_INF)
    bias_near = near.reshape(G, d_n.shape[0], tk, R * tq).transpose(1, 0, 2, 3)
    win = np.stack([tile_r(np.where(d0 - o * tk < WINDOW, 0.0, NEG_INF)) for o in _win_masked(tq, tk)])
    return bias_c, bias_near, jnp.asarray(win.astype(np.float32))


def _overlap_matrix(ncmp_pad, n_slc, n_cmp):
    m = np.zeros((ncmp_pad, n_slc), np.float32)
    ratio = SLC_BLOCK // CMP_STRIDE
    for j in range(n_slc):
        for o, wgt in enumerate(OVERLAP_W):
            cidx = ratio * j + o - (CMP_BLOCK // CMP_STRIDE - 1)
            if 0 <= cidx < n_cmp:
                m[cidx, j] += wgt
    return jnp.asarray(m.T)


def _layer(x, attn_norm_g, w_in, rel_bias, q_norm_g, k_norm_g, cmp_pe_k, cmp_w1_k, cmp_w2_k,
           cmp_pe_v, cmp_w1_v, cmp_w2_v, rwkv_mu, rwkv_w0, rwkv_w2, rwkv_a0, rwkv_a2, rwkv_g2,
           rwkv_k_k, rwkv_k_a, rwkv_r_k, rwkv_ln_g, rwkv_ln_b, w_proj_a, w_proj_b, w_out,
           ffn_norm_g, w_up, conv_w, conv_b, w_down):
    batch, seq, d = x.shape
    n = batch * seq
    assert seq % Q_TILE == 0 and Q_TILE % K_TILE == 0 and WINDOW % K_TILE == 0
    assert seq % CMP_STRIDE == 0
    assert seq // SLC_BLOCK <= LANES - HEAD_DIM
    x2 = x.reshape(n, d)
    row = lambda a: a.reshape(1, -1).astype(F32)

    kvw = 6 * NSA_KV_WIDTH
    o_q, o_kv, o_gn = 0, NSA_WIDTH, NSA_WIDTH + kvw
    o_rw = o_gn + 3 * NSA_HEADS
    o_gab = o_rw + RWKV_IN_WIDTH
    gn_pad = jnp.zeros((d, LANES - 3 * NSA_HEADS), w_in.dtype)
    w_cat = jnp.concatenate([w_in[:, o_rw:o_gab], w_in[:, o_q:o_kv], w_in[:, o_kv:o_gn], w_in[:, o_gab:],
                             w_in[:, o_gn:o_rw], gn_pad], axis=1).astype(BF16)
    widths = (RWKV_IN_WIDTH, NSA_WIDTH, kvw, 2 * d, LANES)
    rw, q, kv, gab, gn = _inproj(x2, row(attn_norm_g), w_cat, widths, tm=256)

    gk = row(k_norm_g)
    ksx, kwn, vst, vwt = _kvprep(kv, jnp.concatenate([gk, gk], axis=1), seq, tm=512)

    nchunk = seq // CMP_STRIDE
    n_cmp = nchunk - CMP_BLOCK // CMP_STRIDE + 1
    pe = jnp.stack([cmp_pe_k, cmp_pe_v]).astype(F32)
    w1 = jnp.stack([cmp_w1_k, cmp_w1_v]).astype(BF16).reshape(2, CMP_BLOCK, HEAD_DIM, CMP_HIDDEN)
    w2 = jnp.stack([cmp_w2_k, cmp_w2_v]).astype(BF16)
    kc, vc = _compress(kv, pe, w1, w2, gk, batch, seq)

    bias_c, bias_near, win_mask = _bias_tables(rel_bias, seq, nchunk)
    wov = _overlap_matrix(nchunk, seq // SLC_BLOCK, n_cmp)
    o_a = _attention(q, gn, kc, jnp.swapaxes(vc, -1, -2), ksx, vst, kwn, vwt, bias_c, bias_near, win_mask,
                     q_norm_g.reshape(-1, 1).astype(F32), wov, batch, seq)

    vecs = tuple(row(a) for a in (rwkv_mu, rwkv_w0, rwkv_a0, rwkv_k_k, rwkv_k_a, rwkv_r_k, rwkv_ln_g, rwkv_ln_b))
    o_b = _rwkv(rw, vecs, rwkv_w2.astype(BF16), rwkv_a2.astype(BF16), rwkv_g2.astype(BF16), batch, seq)

    x1 = _merge(x2, o_a, o_b, gab, w_proj_a.astype(BF16), w_proj_b.astype(BF16), w_out.astype(BF16), tm=512)
    dff = w_down.shape[0]
    out = _ffn(x1, row(ffn_norm_g), w_up.astype(BF16), conv_w.astype(F32), row(conv_b), w_down.astype(BF16),
               seq, tm=512, tf=dff // 2)
    return out.reshape(batch, seq, d)


def kernel(x, attn_norm_g, w_in, rel_bias, q_norm_g, k_norm_g, cmp_pe_k, cmp_w1_k, cmp_w2_k, cmp_pe_v, cmp_w1_v,
           cmp_w2_v, rwkv_mu, rwkv_w0, rwkv_w2, rwkv_a0, rwkv_a2, rwkv_g2, rwkv_k_k, rwkv_k_a, rwkv_r_k,
           rwkv_ln_g, rwkv_ln_b, w_proj_a, w_proj_b, w_out, ffn_norm_g, w_up, conv_w, conv_b, w_down):
    per_layer = (attn_norm_g, w_in, None, q_norm_g, k_norm_g, cmp_pe_k, cmp_w1_k, cmp_w2_k, cmp_pe_v, cmp_w1_v,
                 cmp_w2_v, rwkv_mu, rwkv_w0, rwkv_w2, rwkv_a0, rwkv_a2, rwkv_g2, rwkv_k_k, rwkv_k_a, rwkv_r_k,
                 rwkv_ln_g, rwkv_ln_b, w_proj_a, w_proj_b, w_out, ffn_norm_g, w_up, conv_w, conv_b, w_down)
    for l in range(attn_norm_g.shape[0]):
        args = [rel_bias if p is None else p[l] for p in per_layer]
        x = _layer(x, *args)
    return x
```

```python
import functools
import math

import numpy as np
import jax
import jax.numpy as jnp
from jax import lax
from jax.experimental import pallas as pl
from jax.experimental.pallas import tpu as pltpu

F32 = jnp.float32
BF16 = jnp.bfloat16

HEAD_DIM = 64
NSA_HEADS = 8
NSA_KV_GROUPS = 2
NSA_GROUP = NSA_HEADS // NSA_KV_GROUPS
NSA_WIDTH = NSA_HEADS * HEAD_DIM
NSA_KV_WIDTH = NSA_KV_GROUPS * HEAD_DIM
CMP_BLOCK = 32
CMP_STRIDE = 16
CMP_HIDDEN = 256
SLC_BLOCK = 64
SLC_TOPK = 16
OVERLAP_W = (1, 2, 2, 2, 1)
WINDOW = 512
REL_BUCKETS = 32
REL_MAX_DIST = 128
RWKV_HEADS = 8
RWKV_WIDTH = RWKV_HEADS * HEAD_DIM
LORA_W = 64
LORA_A = 64
LORA_G = 128
RWKV_IN_WIDTH = 3 * RWKV_WIDTH + LORA_W + LORA_A + LORA_G
GN_EPS = 64e-5
CONV_WIDTH = 3
RMS_EPS = 1e-6
NEG_INF = -1e30
FORCE = 1e9
LOG2E = math.log2(math.e)
M_FLOOR = -1e20

LANES = 128
VMEM_LIMIT = 56 * 1024 * 1024

Q_TILE = 256
K_TILE = 256
VT_ROWS = 80
CHUNK = 128
HALO = 16

_NT = (((1,), (1,)), ((), ()))


def _params(*sem):
    return pltpu.CompilerParams(dimension_semantics=sem, vmem_limit_bytes=VMEM_LIMIT)


def _mm(a, b):
    return jnp.dot(a.astype(BF16), b.astype(BF16), preferred_element_type=F32)


def _mm_nt(a, b):
    return lax.dot_general(a.astype(BF16), b.astype(BF16), _NT, preferred_element_type=F32)


def _mm_f32(a, b):
    return jnp.dot(a, b, preferred_element_type=F32, precision=lax.Precision.HIGHEST)


def _rms(x, g):
    return x * lax.rsqrt(jnp.mean(x * x, axis=-1, keepdims=True) + RMS_EPS) * g


def _inproj_kernel(x_ref, g_ref, w_ref, *out_refs):
    h = _rms(x_ref[...], g_ref[...]).astype(BF16)
    off = 0
    for ref in out_refs:
        n = ref.shape[-1]
        ref[...] = jnp.dot(h, w_ref[:, off:off + n], preferred_element_type=F32).astype(ref.dtype)
        off += n


def _inproj(x2, g, w, widths, dtypes, tm):
    n, d = x2.shape
    return pl.pallas_call(
        _inproj_kernel,
        grid=(n // tm,),
        in_specs=[pl.BlockSpec((tm, d), lambda i: (i, 0)),
                  pl.BlockSpec((1, d), lambda i: (0, 0)),
                  pl.BlockSpec(w.shape, lambda i: (0, 0))],
        out_specs=[pl.BlockSpec((tm, wd), lambda i: (i, 0)) for wd in widths],
        out_shape=[jax.ShapeDtypeStruct((n, wd), dt) for wd, dt in zip(widths, dtypes)],
        compiler_params=_params("parallel"),
        name="inproj",
    )(x2, g, w)


def _kvprep_kernel(seq, kv_ref, gk_ref, ksx_ref, kwn_ref, vs_ref, vw_ref):
    tm = kv_ref.shape[0]
    kv = kv_ref[...].astype(F32)
    lane = lax.broadcasted_iota(jnp.int32, (tm, LANES), 1)
    lo = lane < HEAD_DIM

    def norm2(x):
        x2 = x * x
        s_lo = jnp.sum(jnp.where(lo, x2, 0.0), axis=-1, keepdims=True)
        s_hi = jnp.sum(jnp.where(lo, 0.0, x2), axis=-1, keepdims=True)
        ms = jnp.where(lo, s_lo, s_hi) * (1.0 / HEAD_DIM)
        return x * lax.rsqrt(ms + RMS_EPS) * gk_ref[...]

    ks = norm2(kv[:, 0:LANES])
    kw = norm2(kv[:, 2 * LANES:3 * LANES])
    row = lax.rem(pl.program_id(0) * tm, seq) + lax.broadcasted_iota(jnp.int32, (tm, LANES), 0)
    onehot = jnp.where(lane - HEAD_DIM == row // SLC_BLOCK, 1.0, 0.0)
    ksx_ref[0] = jnp.where(lo, ks, onehot).astype(BF16)
    ksx_ref[1] = jnp.where(lo, pltpu.roll(ks, HEAD_DIM, axis=1), onehot).astype(BF16)
    kwn_ref[0] = kw[:, :HEAD_DIM].astype(BF16)
    kwn_ref[1] = kw[:, HEAD_DIM:].astype(BF16)
    tail = jnp.where(lax.broadcasted_iota(jnp.int32, (VT_ROWS - HEAD_DIM, K_TILE), 0) == 0, 1.0, 0.0)
    for col, ref in ((1, vs_ref), (3, vw_ref)):
        for t in range(tm // K_TILE):
            vt = kv[t * K_TILE:(t + 1) * K_TILE, col * LANES:(col + 1) * LANES].T
            for g in range(NSA_KV_GROUPS):
                ref[g, t] = jnp.concatenate([vt[g * HEAD_DIM:(g + 1) * HEAD_DIM], tail], axis=0).astype(BF16)


def _kvprep(kv, gk2, seq, tm):
    n = kv.shape[0]
    vt_spec = lambda: pl.BlockSpec((2, tm // K_TILE, VT_ROWS, K_TILE), lambda i: (0, i, 0, 0))
    vt_shape = jax.ShapeDtypeStruct((2, n // K_TILE, VT_ROWS, K_TILE), BF16)
    return pl.pallas_call(
        functools.partial(_kvprep_kernel, seq),
        grid=(n // tm,),
        in_specs=[pl.BlockSpec((tm, kv.shape[1]), lambda i: (i, 0)),
                  pl.BlockSpec((1, LANES), lambda i: (0, 0))],
        out_specs=[pl.BlockSpec((2, tm, LANES), lambda i: (0, i, 0)),
                   pl.BlockSpec((2, tm, HEAD_DIM), lambda i: (0, i, 0)), vt_spec(), vt_spec()],
        out_shape=[jax.ShapeDtypeStruct((2, n, LANES), BF16), jax.ShapeDtypeStruct((2, n, HEAD_DIM), BF16),
                   vt_shape, vt_shape],
        compiler_params=_params("parallel"),
        name="kvprep",
    )(kv, gk2)


def _compress_kernel(k_ref, v_ref, pe_ref, w1_ref, w2_ref, gk_ref, kc_ref, vc_ref):
    nchunk = k_ref.shape[0] // CMP_STRIDE
    span = CMP_BLOCK // CMP_STRIDE
    G = NSA_KV_GROUPS
    acc = [[[jnp.zeros((nchunk, CMP_HIDDEN), F32) for _ in range(span)] for _ in range(G)] for _ in range(2)]
    for p in range(CMP_STRIDE):
        for s, ref in enumerate((k_ref, v_ref)):
            x = ref[pl.ds(p, nchunk, stride=CMP_STRIDE), :]
            for g in range(G):
                xs = x[:, g * HEAD_DIM:(g + 1) * HEAD_DIM]
                for half in range(span):
                    q = half * CMP_STRIDE + p
                    acc[s][g][half] = acc[s][g][half] + _mm(xs + pe_ref[s, q:q + 1, :], w1_ref[s, q])
    for s, out_ref in ((0, kc_ref), (1, vc_ref)):
        for g in range(G):
            hid = acc[s][g][0]
            for half in range(1, span):
                hid = hid + pltpu.roll(acc[s][g][half], nchunk - half, axis=0)
            out = _mm(jax.nn.gelu(hid), w2_ref[s])
            out_ref[0, g] = _rms(out, gk_ref[...]) if s == 0 else out


def _compress(kv, pe, w1, w2, gk, batch, seq):
    nchunk = seq // CMP_STRIDE
    out_spec = lambda: pl.BlockSpec((1, NSA_KV_GROUPS, nchunk, HEAD_DIM), lambda b: (b, 0, 0, 0))
    out_shape = jax.ShapeDtypeStruct((batch, NSA_KV_GROUPS, nchunk, HEAD_DIM), F32)
    const = lambda a: pl.BlockSpec(a.shape, lambda b: (0,) * a.ndim)
    return pl.pallas_call(
        _compress_kernel,
        grid=(batch,),
        in_specs=[pl.BlockSpec((seq, NSA_KV_WIDTH), lambda b: (b, 0)), pl.BlockSpec((seq, NSA_KV_WIDTH), lambda b: (b, 1)),
                  const(pe), const(w1), const(w2), const(gk)],
        out_specs=[out_spec(), out_spec()],
        out_shape=[out_shape, out_shape],
        compiler_params=_params("parallel"),
        name="compress",
    )(kv, kv, pe, w1, w2, gk)


def _rank_select(score_t, cur_t):
    nb = score_t.shape[0]
    jj = lax.broadcasted_iota(jnp.int32, score_t.shape, 0)
    rank = jnp.zeros(score_t.shape, F32)
    for j in range(nb):
        row = score_t[j:j + 1, :]
        gt = jnp.where(row > score_t, 1.0, 0.0)
        ge = jnp.where(row >= score_t, 1.0, 0.0)
        rank = rank + jnp.where(jj > j, ge, gt)
    keep = jnp.where(rank < float(min(SLC_TOPK, nb)), 1.0, 0.0) * jnp.where(jj <= cur_t, 1.0, 0.0)
    return jnp.where(keep > 0.5, 0.0, NEG_INF)


def _win_masked(tq, tk):
    return [o for o in range(-(WINDOW // tk), 0) if tq - 1 - o * tk >= WINDOW]


def _scores(ks, ws):
    return [jnp.dot(k, w, preferred_element_type=F32) for k, w in zip(ks, ws)]


def _softmax_update(m_ref, acc_ref, slots, s, vts):
    m_old = [m_ref[i] for i in slots]
    m_new = [jnp.maximum(mo, jnp.max(x, axis=0, keepdims=True)) for mo, x in zip(m_old, s)]
    pr = [jnp.exp2(x - mn).astype(BF16) for x, mn in zip(s, m_new)]
    pv = [jnp.dot(vt, y, preferred_element_type=F32) for vt, y in zip(vts, pr)]
    acc = [jnp.exp2(mo - mn) * acc_ref[i] + z for mo, mn, i, z in zip(m_old, m_new, slots, pv)]
    for i, mn, ac in zip(slots, m_new, acc):
        m_ref[i] = mn
        acc_ref[i] = ac


def _softmax_steps(m_ref, acc_ref, probs):
    s = _scores([p[1] for p in probs], [p[3] for p in probs])
    s = [x if p[4] is None else x + p[4][...] for x, p in zip(s, probs)]
    _softmax_update(m_ref, acc_ref, [p[0] for p in probs], s, [p[2] for p in probs])


def _attn_kernel(q_ref, gn_ref, kc_ref, vct_ref, ksx_ref, vst_ref, kwn_ref, vwt_ref, bc_ref, bn_ref, wm_ref,
                 gq_ref, wov_ref, o_ref, m_ref, acc_ref, sa_ref, sb_ref):
    tq = q_ref.shape[0]
    tk = K_TILE
    nblk = wov_ref.shape[0]
    G, R = NSA_KV_GROUPS, NSA_GROUP
    qt = pl.program_id(1)
    n_q = tq // tk
    n_b = -(-REL_MAX_DIST // tk)
    j0 = qt * n_q
    qT = q_ref[...].astype(F32).T
    gates = jax.nn.sigmoid(gn_ref[...].T)
    gq = gq_ref[...]

    qn, qx, o_cmp = [], [], []
    for g in range(G):
        xs = []
        for h in range(g * R, (g + 1) * R):
            x = qT[h * HEAD_DIM:(h + 1) * HEAD_DIM]
            x = x * lax.rsqrt(jnp.mean(x * x, axis=0, keepdims=True) + RMS_EPS) * gq * (HEAD_DIM ** -0.5 * LOG2E)
            xs.append(x.astype(BF16))
        qn.append(jnp.concatenate(xs, axis=1))

    for g in range(G):
        s = jnp.dot(kc_ref[0, g].astype(BF16), qn[g], preferred_element_type=F32) + bc_ref[0, g]
        m = jnp.maximum(jnp.max(s, axis=0, keepdims=True), M_FLOOR)
        e = jnp.exp2(s - m)
        p = e * (1.0 / jnp.maximum(jnp.sum(e, axis=0, keepdims=True), 1e-30))
        o_cmp.append(jnp.dot(vct_ref[0, g].astype(BF16), p.astype(BF16), preferred_element_type=F32))
        psum = p[:, 0:tq]
        for r in range(1, R):
            psum = psum + p[:, r * tq:(r + 1) * tq]
        imp = _mm_f32(wov_ref[...], psum)

        blk = lax.broadcasted_iota(jnp.int32, imp.shape, 0)
        cur = (qt * tq + lax.broadcasted_iota(jnp.int32, imp.shape, 1)) // SLC_BLOCK
        forced = (blk == 0) | (blk == cur) | (blk == cur - 1)
        score = jnp.where(forced, FORCE, jnp.where(blk <= cur, imp, -FORCE))
        neg = _rank_select(score, cur)
        if nblk < LANES - HEAD_DIM:
            neg = jnp.concatenate([neg, jnp.zeros((LANES - HEAD_DIM - nblk, tq), F32)], axis=0)
        neg = neg.astype(BF16)
        qx.append(jnp.concatenate([qn[g], jnp.concatenate([neg] * R, axis=1)], axis=0))

    m_ref[...] = jnp.full(m_ref.shape, M_FLOOR, F32)
    acc_ref[...] = jnp.zeros(acc_ref.shape, F32)

    def slc_probs(j, bias):
        sl = pl.ds(pl.multiple_of(j * tk, tk), tk)
        return [(g, ksx_ref[g, sl, :], vst_ref[g, j], qx[g], None if bias is None else bias.at[g])
                for g in range(G)]

    def win_probs(j, bias, per_group):
        sl = pl.ds(pl.multiple_of(j * tk, tk), tk)
        pick = lambda g: None if bias is None else (bias.at[g] if per_group else bias)
        return [(G + g, kwn_ref[g, sl, :], vwt_ref[g, j], qn[g], pick(g)) for g in range(G)]

    n_far = jnp.maximum(j0 - n_b, 0)
    groups = list(range(G))

    def far_scores(j, buf):
        sl = pl.ds(pl.multiple_of(j * tk, tk), tk)
        for g, x in zip(groups, _scores([ksx_ref[g, sl, :] for g in groups], qx)):
            buf[g] = x

    def far_update(j, buf):
        _softmax_update(m_ref, acc_ref, groups, [buf[g] for g in groups], [vst_ref[g, j] for g in groups])

    far_scores(0, sa_ref)

    def far_body(i, carry):
        far_scores(2 * i + 1, sb_ref)
        far_update(2 * i, sa_ref)
        far_scores(2 * i + 2, sa_ref)
        far_update(2 * i + 1, sb_ref)
        return carry

    lax.fori_loop(0, n_far // 2, far_body, 0)

    @pl.when(n_far % 2 == 1)
    def _():
        far_update(n_far - 1, sa_ref)

    masked = _win_masked(tq, tk)
    by_start = {}
    for o in range(-(WINDOW // tk), n_q):
        probs = []
        if o >= -n_b:
            probs += [functools.partial(slc_probs, j0 + o, bn_ref.at[o + n_b])]
            probs += [functools.partial(win_probs, j0 + o, bn_ref.at[o + n_b], True)]
        elif o in masked:
            probs += [functools.partial(win_probs, j0 + o, wm_ref.at[masked.index(o)], False)]
        else:
            probs += [functools.partial(win_probs, j0 + o, None, False)]
        by_start.setdefault(max(-(o // n_q), 0), []).append(probs)

    for first_qt in sorted(by_start, reverse=True):
        def run(tiles=by_start[first_qt]):
            for probs in tiles:
                _softmax_steps(m_ref, acc_ref, [p for f in probs for p in f()])
        if first_qt == 0:
            run()
        else:
            pl.when(qt >= first_qt)(run)

    outs = []
    for g in range(G):
        acc = acc_ref[g]
        o_slc = acc[:HEAD_DIM] / acc[HEAD_DIM:HEAD_DIM + 1]
        acc = acc_ref[G + g]
        o_win = acc[:HEAD_DIM] / acc[HEAD_DIM:HEAD_DIM + 1]
        for r in range(R):
            h = g * R + r
            cols = slice(r * tq, (r + 1) * tq)
            outs.append(gates[3 * h:3 * h + 1] * o_cmp[g][:, cols] + gates[3 * h + 1:3 * h + 2] * o_slc[:, cols]
                        + gates[3 * h + 2:3 * h + 3] * o_win[:, cols])
    o_ref[...] = jnp.concatenate(outs, axis=0).T.astype(o_ref.dtype)


def _attention(q, gn, kc, vct, ksx, vst, kwn, vwt, bias_c, bias_near, win_mask, gq, wov, batch, seq):
    n = q.shape[0]
    tq = Q_TILE
    nq = seq // tq
    ncmp = kc.shape[2]
    row = lambda b, i: (b * nq + i, 0)
    whole = lambda b, i: (0, b, 0)
    tiles = lambda b, i: (0, b, 0, 0)
    const = lambda a: pl.BlockSpec(a.shape, lambda b, i: (0,) * a.ndim)
    return pl.pallas_call(
        _attn_kernel,
        grid=(batch, nq),
        in_specs=[pl.BlockSpec((tq, NSA_WIDTH), row),
                  pl.BlockSpec((tq, LANES), row),
                  pl.BlockSpec((1, NSA_KV_GROUPS, ncmp, HEAD_DIM), lambda b, i: (b, 0, 0, 0)),
                  pl.BlockSpec((1, NSA_KV_GROUPS, HEAD_DIM, ncmp), lambda b, i: (b, 0, 0, 0)),
                  pl.BlockSpec((2, seq, LANES), whole),
                  pl.BlockSpec((2, seq // K_TILE, VT_ROWS, K_TILE), tiles),
                  pl.BlockSpec((2, seq, HEAD_DIM), whole),
                  pl.BlockSpec((2, seq // K_TILE, VT_ROWS, K_TILE), tiles),
                  pl.BlockSpec((1, NSA_KV_GROUPS, ncmp, NSA_GROUP * tq), lambda b, i: (i, 0, 0, 0)),
                  const(bias_near), const(win_mask), const(gq), const(wov)],
        out_specs=pl.BlockSpec((tq, NSA_WIDTH), row),
        out_shape=jax.ShapeDtypeStruct((n, NSA_WIDTH), BF16),
        scratch_shapes=[pltpu.VMEM((2 * NSA_KV_GROUPS, 1, NSA_GROUP * tq), F32),
                        pltpu.VMEM((2 * NSA_KV_GROUPS, VT_ROWS, NSA_GROUP * tq), F32),
                        pltpu.VMEM((NSA_KV_GROUPS, K_TILE, NSA_GROUP * tq), F32),
                        pltpu.VMEM((NSA_KV_GROUPS, K_TILE, NSA_GROUP * tq), F32)],
        compiler_params=_params("parallel", "parallel"),
        name="nsa_attention",
    )(q, gn, kc, vct, ksx, vst, kwn, vwt, bias_c, bias_near, win_mask, gq, wov)


def _head_sums(z, ones):
    hi = z.astype(BF16)
    lo = (z - hi.astype(F32)).astype(BF16)
    return jnp.dot(hi, ones, preferred_element_type=F32) + jnp.dot(lo, ones, preferred_element_type=F32)


def _softplus(z):
    return jnp.maximum(z, 0.0) + jnp.log(1.0 + jnp.exp(-jnp.abs(z)))


def _rwkv_kernel(x_ref, xp_ref, mu_ref, w0_ref, w2_ref, a0_ref, a2_ref, g2_ref, kk_ref, ka_ref, rk_ref,
                 lng_ref, lnb_ref, ones_ref, o_ref, st_ref):
    L = x_ref.shape[0]
    W = RWKV_WIDTH
    N = HEAD_DIM
    c = pl.program_id(1)

    @pl.when(c == 0)
    def _():
        st_ref[...] = jnp.zeros(st_ref.shape, F32)

    x = x_ref[...]
    prev = jnp.where(c > 0, xp_ref[xp_ref.shape[0] - 1:, :], 0.0)
    row_id = lax.broadcasted_iota(jnp.int32, x.shape, 0)
    shifted = jnp.where(row_id == 0, prev, pltpu.roll(x, 1, axis=0))
    xl = x + (shifted - x) * mu_ref[...]
    r = xl[:, 0:W]
    k = xl[:, W:2 * W]
    v = xl[:, 2 * W:3 * W]
    xw = xl[:, 3 * W:3 * W + LORA_W]
    xa = xl[:, 3 * W + LORA_W:3 * W + LORA_W + LORA_A]
    xg = xl[:, 3 * W + LORA_W + LORA_A:]
    w = -_softplus(-(w0_ref[...] + _mm(jnp.tanh(xw), w2_ref[...]))) - 0.5
    ld = -jnp.exp(w)
    a = jax.nn.sigmoid(a0_ref[...] + _mm(xa, a2_ref[...]))
    gate = _mm(jax.nn.sigmoid(xg), g2_ref[...])
    kkv = k * kk_ref[...]
    k2 = k * (1.0 + (a - 1.0) * ka_ref[...])

    ti = lax.broadcasted_iota(jnp.int32, (L, L), 0)
    si = lax.broadcasted_iota(jnp.int32, (L, L), 1)
    incl = si <= ti
    strict = si < ti
    cl = _mm_f32(jnp.where(incl, 1.0, 0.0), ld)
    cl_end = cl[L - 1:L, :]
    e_pos = jnp.exp(cl)
    e_neg = jnp.exp(-cl)
    e_prev = jnp.exp(cl - ld)
    e_end = jnp.exp(cl_end - cl)
    eye = jnp.where(ti == si, 1.0, 0.0)

    H = range(RWKV_HEADS)
    hs = [slice(h * N, (h + 1) * N) for h in H]
    hsum = lambda z: _head_sums(z, ones_ref[...])

    kk_n = kkv * lax.rsqrt(jnp.maximum(hsum(kkv * kkv), 1e-24))
    bv = kk_n * a
    a_t = (-kk_n * e_prev).astype(BF16)
    b_t = (bv * e_neg).astype(BF16)
    k_t = (k2 * e_neg).astype(BF16)
    r_t = (r * e_pos).astype(BF16)
    v_b = v.astype(BF16)
    k_e = k2 * e_end
    b_e = bv * e_end
    tr = lambda z: [z[:, m * LANES:(m + 1) * LANES].T for m in range(W // LANES)]
    sub = lambda blocks, h: blocks[(h * N) // LANES][(h * N) % LANES:(h * N) % LANES + N]
    ke_t, be_t, ep_t = tr(k_e), tr(b_e), tr(e_pos)

    aa = [lax.dot_general(jnp.concatenate([a_t[:, s], r_t[:, s]], axis=0),
                          jnp.concatenate([k_t[:, s], b_t[:, s]], axis=0), _NT, preferred_element_type=F32)
          for s in hs]
    a_ak = [jnp.where(strict, x[:L, :L], 0.0) for x in aa]
    a_ab = [jnp.where(strict, x[:L, L:], 0.0) for x in aa]
    a_rk = [jnp.where(incl, x[L:, :L], 0.0) for x in aa]
    a_rb = [jnp.where(incl, x[L:, L:], 0.0) for x in aa]

    tinv = [eye + x for x in a_ab]
    pw = [_mm(x, x) for x in a_ab]
    span = 2
    while 2 * span < L:
        both = [_mm(jnp.concatenate([tinv[h], pw[h]], axis=0), pw[h]) for h in H]
        tinv = [tinv[h] + both[h][:L] for h in H]
        pw = [x[L:] for x in both]
        span *= 2
    tinv = [tinv[h] + _mm(tinv[h], pw[h]) for h in H]

    av = [_mm(jnp.concatenate([a_ak[h], a_rk[h]], axis=0), v_b[:, hs[h]]) for h in H]
    tw = [_mm(tinv[h], jnp.concatenate([a_t[:, hs[h]], av[h][:L].astype(BF16)], axis=1)) for h in H]
    kv_loc = [_mm(sub(ke_t, h), v_b[:, hs[h]]) for h in H]

    st = [st_ref[h] for h in H]
    ws = [_mm(jnp.concatenate([tw[h][:, :N].astype(BF16), r_t[:, hs[h]]], axis=0), st[h]) for h in H]
    u = [ws[h][:L] + tw[h][:, N:] for h in H]
    y = [ws[h][L:] + av[h][L:] + _mm(a_rb[h], u[h]) for h in H]
    st_ref[...] = jnp.stack([sub(ep_t, h)[:, L - 1:L] * st[h] + kv_loc[h] + _mm(sub(be_t, h), u[h]) for h in H])

    y = jnp.concatenate(y, axis=1)
    d = y - hsum(y) * (1.0 / N)
    yn = d * lax.rsqrt(hsum(d * d) * (1.0 / N) + GN_EPS) * lng_ref[...] + lnb_ref[...]
    bonus = hsum(r * k2 * rk_ref[...])
    o_ref[...] = ((yn + bonus * v) * gate).astype(o_ref.dtype)


def _rwkv(rw, vecs, w2, a2, g2, batch, seq):
    n, width = rw.shape
    L = min(CHUNK, seq)
    nc = seq // L
    sub = 8
    vec = lambda wd: pl.BlockSpec((1, wd), lambda b, c: (0, 0))
    mat = lambda m: pl.BlockSpec(m.shape, lambda b, c: (0, 0))
    mu, w0, a0, kk, ka, rk, lng, lnb = vecs
    head = np.arange(RWKV_WIDTH) // HEAD_DIM
    ones = jnp.asarray(head[:, None] == head[None, :], BF16)
    return pl.pallas_call(
        _rwkv_kernel,
        grid=(batch, nc),
        in_specs=[pl.BlockSpec((L, width), lambda b, c: (b * nc + c, 0)),
                  pl.BlockSpec((sub, width), lambda b, c: (jnp.maximum((b * nc + c) * (L // sub) - 1, 0), 0)),
                  vec(width), vec(RWKV_WIDTH), mat(w2), vec(RWKV_WIDTH), mat(a2), mat(g2),
                  vec(RWKV_WIDTH), vec(RWKV_WIDTH), vec(RWKV_WIDTH), vec(RWKV_WIDTH), vec(RWKV_WIDTH), mat(ones)],
        out_specs=pl.BlockSpec((L, RWKV_WIDTH), lambda b, c: (b * nc + c, 0)),
        out_shape=jax.ShapeDtypeStruct((n, RWKV_WIDTH), BF16),
        scratch_shapes=[pltpu.VMEM((RWKV_HEADS, HEAD_DIM, HEAD_DIM), F32)],
        compiler_params=_params("parallel", "arbitrary"),
        name="rwkv7",
    )(rw, rw, mu, w0, w2, a0, a2, g2, kk, ka, rk, lng, lnb, ones)


def _merge_kernel(x_ref, oa_ref, ob_ref, gab_ref, wpa_ref, wpb_ref, wo_ref, o_ref):
    d = x_ref.shape[1]
    pa = jnp.dot(oa_ref[...], wpa_ref[...], preferred_element_type=F32)
    pb = jnp.dot(ob_ref[...], wpb_ref[...], preferred_element_type=F32)
    gab = gab_ref[...].astype(F32)
    merged = jax.nn.sigmoid(gab[:, :d]) * pa + jax.nn.sigmoid(gab[:, d:]) * pb
    o_ref[...] = x_ref[...] + _mm(merged, wo_ref[...])


def _merge(x2, oa, ob, gab, wpa, wpb, wo, tm):
    n, d = x2.shape
    row = lambda wd: pl.BlockSpec((tm, wd), lambda i: (i, 0))
    mat = lambda m: pl.BlockSpec(m.shape, lambda i: (0, 0))
    return pl.pallas_call(
        _merge_kernel,
        grid=(n // tm,),
        in_specs=[row(d), row(oa.shape[1]), row(ob.shape[1]), row(gab.shape[1]), mat(wpa), mat(wpb), mat(wo)],
        out_specs=row(d),
        out_shape=jax.ShapeDtypeStruct((n, d), F32),
        compiler_params=_params("parallel"),
        name="merge",
    )(x2, oa, ob, gab, wpa, wpb, wo)


def _ffn_kernel(seq, x_ref, xh_ref, g_ref, wv_ref, wg_ref, cwv_ref, cwg_ref, cbv_ref, cbg_ref, wd_ref, o_ref,
                h_ref, uv_ref, ug_ref):
    tm = x_ref.shape[0]
    i = pl.program_id(0)
    f = pl.program_id(1)

    @pl.when(f == 0)
    def _():
        first = lax.rem(i * tm, seq) == 0
        halo = jnp.where(first, 0.0, _rms(xh_ref[...], g_ref[...]))
        h_ref[0:HALO, :] = halo.astype(BF16)
        h_ref[HALO:, :] = _rms(x_ref[...], g_ref[...]).astype(BF16)

    h = h_ref[...]
    uv_ref[...] = jnp.dot(h, wv_ref[...], preferred_element_type=F32)
    ug_ref[...] = jnp.dot(h, wg_ref[...], preferred_element_type=F32)

    def conv(u_ref, cw_ref, cb_ref):
        acc = cb_ref[...] + cw_ref[0:1, :] * u_ref[pl.ds(HALO - 2, tm), :]
        acc = acc + cw_ref[1:2, :] * u_ref[pl.ds(HALO - 1, tm), :]
        return acc + cw_ref[2:3, :] * u_ref[pl.ds(HALO, tm), :]

    val = conv(uv_ref, cwv_ref, cbv_ref)
    gt = conv(ug_ref, cwg_ref, cbg_ref)
    y = _mm(gt * jax.nn.sigmoid(gt) * val, wd_ref[...])

    @pl.when(f == 0)
    def _():
        o_ref[...] = x_ref[...] + y

    @pl.when(f > 0)
    def _():
        o_ref[...] = o_ref[...] + y


def _ffn(x1, g, w_up, conv_w, conv_b, w_down, seq, tm, tf):
    n, d = x1.shape
    dff = w_down.shape[0]
    nf = dff // tf
    return pl.pallas_call(
        functools.partial(_ffn_kernel, seq),
        grid=(n // tm, nf),
        in_specs=[pl.BlockSpec((tm, d), lambda i, f: (i, 0)),
                  pl.BlockSpec((HALO, d), lambda i, f: (jnp.maximum(i * (tm // HALO) - 1, 0), 0)),
                  pl.BlockSpec((1, d), lambda i, f: (0, 0)),
                  pl.BlockSpec((d, tf), lambda i, f: (0, f)),
                  pl.BlockSpec((d, tf), lambda i, f: (0, nf + f)),
                  pl.BlockSpec((CONV_WIDTH, tf), lambda i, f: (0, f)),
                  pl.BlockSpec((CONV_WIDTH, tf), lambda i, f: (0, nf + f)),
                  pl.BlockSpec((1, tf), lambda i, f: (0, f)),
                  pl.BlockSpec((1, tf), lambda i, f: (0, nf + f)),
                  pl.BlockSpec((tf, d), lambda i, f: (f, 0))],
        out_specs=pl.BlockSpec((tm, d), lambda i, f: (i, 0)),
        out_shape=jax.ShapeDtypeStruct((n, d), F32),
        scratch_shapes=[pltpu.VMEM((tm + HALO, d), BF16),
                        pltpu.VMEM((tm + HALO, tf), F32),
                        pltpu.VMEM((tm + HALO, tf), F32)],
        compiler_params=_params("parallel", "arbitrary"),
        name="convffn",
    )(x1, x1, g, w_up, w_up, conv_w, conv_w, conv_b, conv_b, w_down)


def _t5_bucket(dist):
    n = np.maximum(dist, 0)
    max_exact = REL_BUCKETS // 2
    ratio = np.log(np.maximum(n, 1).astype(np.float32) / max_exact) / math.log(REL_MAX_DIST / max_exact)
    large = np.minimum(max_exact + (ratio * (REL_BUCKETS - max_exact)).astype(np.int32), REL_BUCKETS - 1)
    return np.where(n < max_exact, n, large).astype(np.int32)


def _bias_tables(rel_bias, seq, ncmp):
    tq, tk = Q_TILE, K_TILE
    G, R = NSA_KV_GROUPS, NSA_GROUP
    nq = seq // tq
    step = tq // CMP_STRIDE
    tab = rel_bias.astype(F32)

    off = (nq - 1) * step
    d_c = np.arange(tq)[None, :] - (np.arange(ncmp + off)[:, None] - off) * CMP_STRIDE - (CMP_BLOCK - 1)
    n_b = -(-REL_MAX_DIST // tk)
    d0 = np.arange(tq)[None, :] - np.arange(tk)[:, None]
    d_n = np.stack([d0 - o * tk for o in range(-n_b, tq // tk)])

    buckets = np.concatenate([_t5_bucket(d_c).reshape(-1), _t5_bucket(d_n).reshape(-1)])
    onehot = (jnp.asarray(buckets)[:, None] == jnp.arange(REL_BUCKETS, dtype=jnp.int32)[None, :]).astype(F32)
    vals = jnp.dot(onehot, tab * LOG2E, precision=lax.Precision.HIGHEST, preferred_element_type=F32).T
    by_group = lambda x: x.reshape(G, R, -1, tq).transpose(0, 2, 1, 3).reshape(G, -1, R * tq)
    tile_r = lambda m: np.tile(m, (1, R))

    base = jnp.where(jnp.asarray(tile_r(d_c >= 0)), by_group(vals[:, :d_c.size]), NEG_INF)
    bias_c = jnp.stack([base[:, off - i * step:off - i * step + ncmp] for i in range(nq)])

    far = jnp.repeat(tab[REL_BUCKETS - 1] * LOG2E, tq).reshape(G, 1, R * tq)
    near = by_group(vals[:, d_c.size:]) - far
    near = jnp.where(jnp.asarray(tile_r(d_n.reshape(-1, tq) >= 0)), near, NEG_INF)
    bias_near = near.reshape(G, d_n.shape[0], tk, R * tq).transpose(1, 0, 2, 3)
    win = np.stack([tile_r(np.where(d0 - o * tk < WINDOW, 0.0, NEG_INF)) for o in _win_masked(tq, tk)])
    return bias_c, bias_near, jnp.asarray(win.astype(np.float32))


def _overlap_matrix(ncmp_pad, n_slc, n_cmp):
    m = np.zeros((ncmp_pad, n_slc), np.float32)
    ratio = SLC_BLOCK // CMP_STRIDE
    for j in range(n_slc):
        for o, wgt in enumerate(OVERLAP_W):
            cidx = ratio * j + o - (CMP_BLOCK // CMP_STRIDE - 1)
            if 0 <= cidx < n_cmp:
                m[cidx, j] += wgt
    return jnp.asarray(m.T)


def _layer(x, attn_norm_g, w_in, rel_bias, q_norm_g, k_norm_g, cmp_pe_k, cmp_w1_k, cmp_w2_k,
           cmp_pe_v, cmp_w1_v, cmp_w2_v, rwkv_mu, rwkv_w0, rwkv_w2, rwkv_a0, rwkv_a2, rwkv_g2,
           rwkv_k_k, rwkv_k_a, rwkv_r_k, rwkv_ln_g, rwkv_ln_b, w_proj_a, w_proj_b, w_out,
           ffn_norm_g, w_up, conv_w, conv_b, w_down):
    batch, seq, d = x.shape
    n = batch * seq
    assert seq % Q_TILE == 0 and Q_TILE % K_TILE == 0 and WINDOW % K_TILE == 0
    assert seq % CMP_STRIDE == 0
    assert seq // SLC_BLOCK <= LANES - HEAD_DIM
    x2 = x.reshape(n, d)
    row = lambda a: a.reshape(1, -1).astype(F32)

    kvw = 6 * NSA_KV_WIDTH
    o_q, o_kv, o_gn = 0, NSA_WIDTH, NSA_WIDTH + kvw
    o_rw = o_gn + 3 * NSA_HEADS
    o_gab = o_rw + RWKV_IN_WIDTH
    gn_pad = jnp.zeros((d, LANES - 3 * NSA_HEADS), w_in.dtype)
    w_cat = jnp.concatenate([w_in[:, o_rw:o_gab], w_in[:, o_q:o_kv], w_in[:, o_kv:o_gn], w_in[:, o_gab:],
                             w_in[:, o_gn:o_rw], gn_pad], axis=1).astype(BF16)
    widths = (RWKV_IN_WIDTH, NSA_WIDTH, 2 * NSA_KV_WIDTH, 4 * NSA_KV_WIDTH, 2 * d, LANES)
    dtypes = (F32, BF16, F32, BF16, BF16, F32)
    rw, q, kv_cmp, kv, gab, gn = _inproj(x2, row(attn_norm_g), w_cat, widths, dtypes, tm=256)

    gk = row(k_norm_g)
    ksx, kwn, vst, vwt = _kvprep(kv, jnp.concatenate([gk, gk], axis=1), seq, tm=512)

    nchunk = seq // CMP_STRIDE
    n_cmp = nchunk - CMP_BLOCK // CMP_STRIDE + 1
    pe = jnp.stack([cmp_pe_k, cmp_pe_v]).astype(F32)
    w1 = jnp.stack([cmp_w1_k, cmp_w1_v]).astype(BF16).reshape(2, CMP_BLOCK, HEAD_DIM, CMP_HIDDEN)
    w2 = jnp.stack([cmp_w2_k, cmp_w2_v]).astype(BF16)
    kc, vc = _compress(kv_cmp, pe, w1, w2, gk, batch, seq)

    bias_c, bias_near, win_mask = _bias_tables(rel_bias, seq, nchunk)
    wov = _overlap_matrix(nchunk, seq // SLC_BLOCK, n_cmp)
    o_a = _attention(q, gn, kc, jnp.swapaxes(vc, -1, -2), ksx, vst, kwn, vwt, bias_c, bias_near, win_mask,
                     q_norm_g.reshape(-1, 1).astype(F32), wov, batch, seq)

    vecs = tuple(row(a) for a in (rwkv_mu, rwkv_w0, rwkv_a0, rwkv_k_k, rwkv_k_a, rwkv_r_k, rwkv_ln_g, rwkv_ln_b))
    o_b = _rwkv(rw, vecs, rwkv_w2.astype(BF16), rwkv_a2.astype(BF16), rwkv_g2.astype(BF16), batch, seq)

    x1 = _merge(x2, o_a, o_b, gab, w_proj_a.astype(BF16), w_proj_b.astype(BF16), w_out.astype(BF16), tm=512)
    dff = w_down.shape[0]
    out = _ffn(x1, row(ffn_norm_g), w_up.astype(BF16), conv_w.astype(F32), row(conv_b), w_down.astype(BF16),
               seq, tm=512, tf=dff // 2)
    return out.reshape(batch, seq, d)


def kernel(x, attn_norm_g, w_in, rel_bias, q_norm_g, k_norm_g, cmp_pe_k, cmp_w1_k, cmp_w2_k, cmp_pe_v, cmp_w1_v,
           cmp_w2_v, rwkv_mu, rwkv_w0, rwkv_w2, rwkv_a0, rwkv_a2, rwkv_g2, rwkv_k_k, rwkv_k_a, rwkv_r_k,
           rwkv_ln_g, rwkv_ln_b, w_proj_a, w_proj_b, w_out, ffn_norm_g, w_up, conv_w, conv_b, w_down):
    per_layer = (attn_norm_g, w_in, None, q_norm_g, k_norm_g, cmp_pe_k, cmp_w1_k, cmp_w2_k, cmp_pe_v, cmp_w1_v,
                 cmp_w2_v, rwkv_mu, rwkv_w0, rwkv_w2, rwkv_a0, rwkv_a2, rwkv_g2, rwkv_k_k, rwkv_k_a, rwkv_r_k,
                 rwkv_ln_g, rwkv_ln_b, w_proj_a, w_proj_b, w_out, ffn_norm_g, w_up, conv_w, conv_b, w_down)
    for l in range(attn_norm_g.shape[0]):
        args = [rel_bias if p is None else p[l] for p in per_layer]
        x = _layer(x, *args)
    return x
```

```python
import functools
import math

import numpy as np
import jax
import jax.numpy as jnp
from jax import lax
from jax.experimental import pallas as pl
from jax.experimental.pallas import tpu as pltpu

F32 = jnp.float32
BF16 = jnp.bfloat16

HEAD_DIM = 64
NSA_HEADS = 8
NSA_KV_GROUPS = 2
NSA_GROUP = NSA_HEADS // NSA_KV_GROUPS
NSA_WIDTH = NSA_HEADS * HEAD_DIM
NSA_KV_WIDTH = NSA_KV_GROUPS * HEAD_DIM
CMP_BLOCK = 32
CMP_STRIDE = 16
CMP_HIDDEN = 256
SLC_BLOCK = 64
SLC_TOPK = 16
OVERLAP_W = (1, 2, 2, 2, 1)
WINDOW = 512
REL_BUCKETS = 32
REL_MAX_DIST = 128
RWKV_HEADS = 8
RWKV_WIDTH = RWKV_HEADS * HEAD_DIM
LORA_W = 64
LORA_A = 64
LORA_G = 128
RWKV_IN_WIDTH = 3 * RWKV_WIDTH + LORA_W + LORA_A + LORA_G
GN_EPS = 64e-5
CONV_WIDTH = 3
RMS_EPS = 1e-6
NEG_INF = -1e30
FORCE = 1e9
LOG2E = math.log2(math.e)
M_FLOOR = -1e20

LANES = 128
VMEM_LIMIT = 56 * 1024 * 1024

Q_TILE = 256
K_TILE = 256
VT_ROWS = 80
CHUNK = 128
RWKV_SEQS = 2
HALO = 16

_NT = (((1,), (1,)), ((), ()))


def _params(*sem):
    return pltpu.CompilerParams(dimension_semantics=sem, vmem_limit_bytes=VMEM_LIMIT)


def _mm(a, b):
    return jnp.dot(a.astype(BF16), b.astype(BF16), preferred_element_type=F32)


def _mm_nt(a, b):
    return lax.dot_general(a.astype(BF16), b.astype(BF16), _NT, preferred_element_type=F32)


def _mm_f32(a, b):
    return jnp.dot(a, b, preferred_element_type=F32, precision=lax.Precision.HIGHEST)


def _rms(x, g):
    return x * lax.rsqrt(jnp.mean(x * x, axis=-1, keepdims=True) + RMS_EPS) * g


def _inproj_kernel(x_ref, g_ref, w_ref, *out_refs):
    h = _rms(x_ref[...], g_ref[...]).astype(BF16)
    off = 0
    for ref in out_refs:
        n = ref.shape[-1]
        ref[...] = jnp.dot(h, w_ref[:, off:off + n], preferred_element_type=F32).astype(ref.dtype)
        off += n


def _inproj(x2, g, w, widths, dtypes, tm):
    n, d = x2.shape
    return pl.pallas_call(
        _inproj_kernel,
        grid=(n // tm,),
        in_specs=[pl.BlockSpec((tm, d), lambda i: (i, 0)),
                  pl.BlockSpec((1, d), lambda i: (0, 0)),
                  pl.BlockSpec(w.shape, lambda i: (0, 0))],
        out_specs=[pl.BlockSpec((tm, wd), lambda i: (i, 0)) for wd in widths],
        out_shape=[jax.ShapeDtypeStruct((n, wd), dt) for wd, dt in zip(widths, dtypes)],
        compiler_params=_params("parallel"),
        name="inproj",
    )(x2, g, w)


def _kvprep_kernel(seq, kv_ref, gk_ref, ksx_ref, kwn_ref, vs_ref, vw_ref):
    tm = kv_ref.shape[0]
    kv = kv_ref[...].astype(F32)
    lane = lax.broadcasted_iota(jnp.int32, (tm, LANES), 1)
    lo = lane < HEAD_DIM

    def norm2(x):
        x2 = x * x
        s_lo = jnp.sum(jnp.where(lo, x2, 0.0), axis=-1, keepdims=True)
        s_hi = jnp.sum(jnp.where(lo, 0.0, x2), axis=-1, keepdims=True)
        ms = jnp.where(lo, s_lo, s_hi) * (1.0 / HEAD_DIM)
        return x * lax.rsqrt(ms + RMS_EPS) * gk_ref[...]

    ks = norm2(kv[:, 0:LANES])
    kw = norm2(kv[:, 2 * LANES:3 * LANES])
    row = lax.rem(pl.program_id(0) * tm, seq) + lax.broadcasted_iota(jnp.int32, (tm, LANES), 0)
    onehot = jnp.where(lane - HEAD_DIM == row // SLC_BLOCK, 1.0, 0.0)
    ksx_ref[0] = jnp.where(lo, ks, onehot).astype(BF16)
    ksx_ref[1] = jnp.where(lo, pltpu.roll(ks, HEAD_DIM, axis=1), onehot).astype(BF16)
    kwn_ref[0] = kw[:, :HEAD_DIM].astype(BF16)
    kwn_ref[1] = kw[:, HEAD_DIM:].astype(BF16)
    tail = jnp.where(lax.broadcasted_iota(jnp.int32, (VT_ROWS - HEAD_DIM, K_TILE), 0) == 0, 1.0, 0.0)
    for col, ref in ((1, vs_ref), (3, vw_ref)):
        for t in range(tm // K_TILE):
            vt = kv[t * K_TILE:(t + 1) * K_TILE, col * LANES:(col + 1) * LANES].T
            for g in range(NSA_KV_GROUPS):
                ref[g, t] = jnp.concatenate([vt[g * HEAD_DIM:(g + 1) * HEAD_DIM], tail], axis=0).astype(BF16)


def _kvprep(kv, gk2, seq, tm):
    n = kv.shape[0]
    vt_spec = lambda: pl.BlockSpec((2, tm // K_TILE, VT_ROWS, K_TILE), lambda i: (0, i, 0, 0))
    vt_shape = jax.ShapeDtypeStruct((2, n // K_TILE, VT_ROWS, K_TILE), BF16)
    return pl.pallas_call(
        functools.partial(_kvprep_kernel, seq),
        grid=(n // tm,),
        in_specs=[pl.BlockSpec((tm, kv.shape[1]), lambda i: (i, 0)),
                  pl.BlockSpec((1, LANES), lambda i: (0, 0))],
        out_specs=[pl.BlockSpec((2, tm, LANES), lambda i: (0, i, 0)),
                   pl.BlockSpec((2, tm, HEAD_DIM), lambda i: (0, i, 0)), vt_spec(), vt_spec()],
        out_shape=[jax.ShapeDtypeStruct((2, n, LANES), BF16), jax.ShapeDtypeStruct((2, n, HEAD_DIM), BF16),
                   vt_shape, vt_shape],
        compiler_params=_params("parallel"),
        name="kvprep",
    )(kv, gk2)


def _compress_kernel(k_ref, v_ref, pe_ref, w1_ref, w2_ref, gk_ref, kc_ref, vc_ref):
    nchunk = k_ref.shape[0] // CMP_STRIDE
    span = CMP_BLOCK // CMP_STRIDE
    G = NSA_KV_GROUPS
    acc = [[[jnp.zeros((nchunk, CMP_HIDDEN), F32) for _ in range(span)] for _ in range(G)] for _ in range(2)]
    for p in range(CMP_STRIDE):
        for s, ref in enumerate((k_ref, v_ref)):
            x = ref[pl.ds(p, nchunk, stride=CMP_STRIDE), :]
            for g in range(G):
                xs = x[:, g * HEAD_DIM:(g + 1) * HEAD_DIM]
                for half in range(span):
                    q = half * CMP_STRIDE + p
                    acc[s][g][half] = acc[s][g][half] + _mm(xs + pe_ref[s, q:q + 1, :], w1_ref[s, q])
    for s, out_ref in ((0, kc_ref), (1, vc_ref)):
        for g in range(G):
            hid = acc[s][g][0]
            for half in range(1, span):
                hid = hid + pltpu.roll(acc[s][g][half], nchunk - half, axis=0)
            out = _mm(jax.nn.gelu(hid), w2_ref[s])
            out_ref[0, g] = _rms(out, gk_ref[...]) if s == 0 else out


def _compress(kv, pe, w1, w2, gk, batch, seq):
    nchunk = seq // CMP_STRIDE
    out_spec = lambda: pl.BlockSpec((1, NSA_KV_GROUPS, nchunk, HEAD_DIM), lambda b: (b, 0, 0, 0))
    out_shape = jax.ShapeDtypeStruct((batch, NSA_KV_GROUPS, nchunk, HEAD_DIM), F32)
    const = lambda a: pl.BlockSpec(a.shape, lambda b: (0,) * a.ndim)
    return pl.pallas_call(
        _compress_kernel,
        grid=(batch,),
        in_specs=[pl.BlockSpec((seq, NSA_KV_WIDTH), lambda b: (b, 0)), pl.BlockSpec((seq, NSA_KV_WIDTH), lambda b: (b, 1)),
                  const(pe), const(w1), const(w2), const(gk)],
        out_specs=[out_spec(), out_spec()],
        out_shape=[out_shape, out_shape],
        compiler_params=_params("parallel"),
        name="compress",
    )(kv, kv, pe, w1, w2, gk)


def _rank_select(score_t, cur_t):
    nb = score_t.shape[0]
    jj = lax.broadcasted_iota(jnp.int32, score_t.shape, 0)
    rank = jnp.zeros(score_t.shape, F32)
    for j in range(nb):
        row = score_t[j:j + 1, :]
        gt = jnp.where(row > score_t, 1.0, 0.0)
        ge = jnp.where(row >= score_t, 1.0, 0.0)
        rank = rank + jnp.where(jj > j, ge, gt)
    keep = jnp.where(rank < float(min(SLC_TOPK, nb)), 1.0, 0.0) * jnp.where(jj <= cur_t, 1.0, 0.0)
    return jnp.where(keep > 0.5, 0.0, NEG_INF)


def _win_masked(tq, tk):
    return [o for o in range(-(WINDOW // tk), 0) if tq - 1 - o * tk >= WINDOW]


def _scores(ks, ws):
    return [jnp.dot(k, w, preferred_element_type=F32) for k, w in zip(ks, ws)]


def _softmax_update(m_ref, acc_ref, slots, s, vts):
    m_old = [m_ref[i] for i in slots]
    m_new = [jnp.maximum(mo, jnp.max(x, axis=0, keepdims=True)) for mo, x in zip(m_old, s)]
    pr = [jnp.exp2(x - mn).astype(BF16) for x, mn in zip(s, m_new)]
    pv = [jnp.dot(vt, y, preferred_element_type=F32) for vt, y in zip(vts, pr)]
    acc = [jnp.exp2(mo - mn) * acc_ref[i] + z for mo, mn, i, z in zip(m_old, m_new, slots, pv)]
    for i, mn, ac in zip(slots, m_new, acc):
        m_ref[i] = mn
        acc_ref[i] = ac


def _softmax_steps(m_ref, acc_ref, probs):
    s = _scores([p[1] for p in probs], [p[3] for p in probs])
    s = [x if p[4] is None else x + p[4][...] for x, p in zip(s, probs)]
    _softmax_update(m_ref, acc_ref, [p[0] for p in probs], s, [p[2] for p in probs])


def _attn_kernel(q_ref, gn_ref, kc_ref, vct_ref, ksx_ref, vst_ref, kwn_ref, vwt_ref, bc_ref, bn_ref, wm_ref,
                 gq_ref, wov_ref, o_ref, m_ref, acc_ref, sa_ref, sb_ref):
    tq = q_ref.shape[0]
    tk = K_TILE
    nblk = wov_ref.shape[0]
    G, R = NSA_KV_GROUPS, NSA_GROUP
    qt = pl.program_id(1)
    n_q = tq // tk
    n_b = -(-REL_MAX_DIST // tk)
    j0 = qt * n_q
    qT = q_ref[...].astype(F32).T
    gates = jax.nn.sigmoid(gn_ref[...].T)
    gq = gq_ref[...]

    qn, qx, o_cmp = [], [], []
    for g in range(G):
        xs = []
        for h in range(g * R, (g + 1) * R):
            x = qT[h * HEAD_DIM:(h + 1) * HEAD_DIM]
            x = x * lax.rsqrt(jnp.mean(x * x, axis=0, keepdims=True) + RMS_EPS) * gq * (HEAD_DIM ** -0.5 * LOG2E)
            xs.append(x.astype(BF16))
        qn.append(jnp.concatenate(xs, axis=1))

    for g in range(G):
        s = jnp.dot(kc_ref[0, g].astype(BF16), qn[g], preferred_element_type=F32) + bc_ref[0, g]
        m = jnp.maximum(jnp.max(s, axis=0, keepdims=True), M_FLOOR)
        e = jnp.exp2(s - m)
        p = e * (1.0 / jnp.maximum(jnp.sum(e, axis=0, keepdims=True), 1e-30))
        o_cmp.append(jnp.dot(vct_ref[0, g].astype(BF16), p.astype(BF16), preferred_element_type=F32))
        psum = p[:, 0:tq]
        for r in range(1, R):
            psum = psum + p[:, r * tq:(r + 1) * tq]
        imp = _mm_f32(wov_ref[...], psum)

        blk = lax.broadcasted_iota(jnp.int32, imp.shape, 0)
        cur = (qt * tq + lax.broadcasted_iota(jnp.int32, imp.shape, 1)) // SLC_BLOCK
        forced = (blk == 0) | (blk == cur) | (blk == cur - 1)
        score = jnp.where(forced, FORCE, jnp.where(blk <= cur, imp, -FORCE))
        neg = _rank_select(score, cur)
        if nblk < LANES - HEAD_DIM:
            neg = jnp.concatenate([neg, jnp.zeros((LANES - HEAD_DIM - nblk, tq), F32)], axis=0)
        neg = neg.astype(BF16)
        qx.append(jnp.concatenate([qn[g], jnp.concatenate([neg] * R, axis=1)], axis=0))

    m_ref[...] = jnp.full(m_ref.shape, M_FLOOR, F32)
    acc_ref[...] = jnp.zeros(acc_ref.shape, F32)

    def slc_probs(j, bias):
        sl = pl.ds(pl.multiple_of(j * tk, tk), tk)
        return [(g, ksx_ref[g, sl, :], vst_ref[g, j], qx[g], None if bias is None else bias.at[g])
                for g in range(G)]

    def win_probs(j, bias, per_group):
        sl = pl.ds(pl.multiple_of(j * tk, tk), tk)
        pick = lambda g: None if bias is None else (bias.at[g] if per_group else bias)
        return [(G + g, kwn_ref[g, sl, :], vwt_ref[g, j], qn[g], pick(g)) for g in range(G)]

    n_far = jnp.maximum(j0 - n_b, 0)
    groups = list(range(G))

    def far_scores(j, buf):
        sl = pl.ds(pl.multiple_of(j * tk, tk), tk)
        for g, x in zip(groups, _scores([ksx_ref[g, sl, :] for g in groups], qx)):
            buf[g] = x

    def far_update(j, buf):
        _softmax_update(m_ref, acc_ref, groups, [buf[g] for g in groups], [vst_ref[g, j] for g in groups])

    far_scores(0, sa_ref)

    def far_body(i, carry):
        far_scores(2 * i + 1, sb_ref)
        far_update(2 * i, sa_ref)
        far_scores(2 * i + 2, sa_ref)
        far_update(2 * i + 1, sb_ref)
        return carry

    lax.fori_loop(0, n_far // 2, far_body, 0)

    @pl.when(n_far % 2 == 1)
    def _():
        far_update(n_far - 1, sa_ref)

    masked = _win_masked(tq, tk)
    by_start = {}
    for o in range(-(WINDOW // tk), n_q):
        probs = []
        if o >= -n_b:
            probs += [functools.partial(slc_probs, j0 + o, bn_ref.at[o + n_b])]
            probs += [functools.partial(win_probs, j0 + o, bn_ref.at[o + n_b], True)]
        elif o in masked:
            probs += [functools.partial(win_probs, j0 + o, wm_ref.at[masked.index(o)], False)]
        else:
            probs += [functools.partial(win_probs, j0 + o, None, False)]
        by_start.setdefault(max(-(o // n_q), 0), []).append(probs)

    for first_qt in sorted(by_start, reverse=True):
        def run(tiles=by_start[first_qt]):
            for probs in tiles:
                _softmax_steps(m_ref, acc_ref, [p for f in probs for p in f()])
        if first_qt == 0:
            run()
        else:
            pl.when(qt >= first_qt)(run)

    outs = []
    for g in range(G):
        acc = acc_ref[g]
        o_slc = acc[:HEAD_DIM] / acc[HEAD_DIM:HEAD_DIM + 1]
        acc = acc_ref[G + g]
        o_win = acc[:HEAD_DIM] / acc[HEAD_DIM:HEAD_DIM + 1]
        for r in range(R):
            h = g * R + r
            cols = slice(r * tq, (r + 1) * tq)
            outs.append(gates[3 * h:3 * h + 1] * o_cmp[g][:, cols] + gates[3 * h + 1:3 * h + 2] * o_slc[:, cols]
                        + gates[3 * h + 2:3 * h + 3] * o_win[:, cols])
    o_ref[...] = jnp.concatenate(outs, axis=0).T.astype(o_ref.dtype)


def _attention(q, gn, kc, vct, ksx, vst, kwn, vwt, bias_c, bias_near, win_mask, gq, wov, batch, seq):
    n = q.shape[0]
    tq = Q_TILE
    nq = seq // tq
    ncmp = kc.shape[2]
    row = lambda b, i: (b * nq + i, 0)
    whole = lambda b, i: (0, b, 0)
    tiles = lambda b, i: (0, b, 0, 0)
    const = lambda a: pl.BlockSpec(a.shape, lambda b, i: (0,) * a.ndim, pipeline_mode=pl.Buffered(1))
    return pl.pallas_call(
        _attn_kernel,
        grid=(batch, nq),
        in_specs=[pl.BlockSpec((tq, NSA_WIDTH), row),
                  pl.BlockSpec((tq, LANES), row),
                  pl.BlockSpec((1, NSA_KV_GROUPS, ncmp, HEAD_DIM), lambda b, i: (b, 0, 0, 0)),
                  pl.BlockSpec((1, NSA_KV_GROUPS, HEAD_DIM, ncmp), lambda b, i: (b, 0, 0, 0)),
                  pl.BlockSpec((2, seq, LANES), whole),
                  pl.BlockSpec((2, seq // K_TILE, VT_ROWS, K_TILE), tiles),
                  pl.BlockSpec((2, seq, HEAD_DIM), whole),
                  pl.BlockSpec((2, seq // K_TILE, VT_ROWS, K_TILE), tiles),
                  pl.BlockSpec((1, NSA_KV_GROUPS, ncmp, NSA_GROUP * tq), lambda b, i: (i, 0, 0, 0)),
                  const(bias_near), const(win_mask), const(gq), const(wov)],
        out_specs=pl.BlockSpec((tq, NSA_WIDTH), row),
        out_shape=jax.ShapeDtypeStruct((n, NSA_WIDTH), BF16),
        scratch_shapes=[pltpu.VMEM((2 * NSA_KV_GROUPS, 1, NSA_GROUP * tq), F32),
                        pltpu.VMEM((2 * NSA_KV_GROUPS, VT_ROWS, NSA_GROUP * tq), F32),
                        pltpu.VMEM((NSA_KV_GROUPS, K_TILE, NSA_GROUP * tq), F32),
                        pltpu.VMEM((NSA_KV_GROUPS, K_TILE, NSA_GROUP * tq), F32)],
        compiler_params=_params("parallel", "parallel"),
        name="nsa_attention",
    )(q, gn, kc, vct, ksx, vst, kwn, vwt, bias_c, bias_near, win_mask, gq, wov)


def _split_bf16(z, parts):
    out = []
    for _ in range(parts - 1):
        hi = z.astype(BF16)
        out.append(hi)
        z = z - hi.astype(F32)
    return out + [z.astype(BF16)]


def _head_sums(z, ones):
    nb = z.shape[1] // LANES
    rows = z.shape[0]
    parts = _split_bf16(z, 2)
    stacked = jnp.concatenate([p[:, m * LANES:(m + 1) * LANES] for p in parts for m in range(nb)], axis=0)
    sums = jnp.dot(stacked, ones, preferred_element_type=F32)
    hi, lo = sums[:nb * rows], sums[nb * rows:]
    return jnp.concatenate([hi[m * rows:(m + 1) * rows] + lo[m * rows:(m + 1) * rows] for m in range(nb)], axis=1)


def _softplus(z):
    return jnp.maximum(z, 0.0) + jnp.log(1.0 + jnp.exp(-jnp.abs(z)))


def _rwkv_chunk(x, prev, st, mu_ref, w0_ref, w2_ref, a0_ref, a2_ref, g2_ref, kk_ref, ka_ref, rk_ref,
                lng_ref, lnb_ref, ones_ref):
    L = x.shape[0]
    W = RWKV_WIDTH
    N = HEAD_DIM
    row_id = lax.broadcasted_iota(jnp.int32, x.shape, 0)
    shifted = jnp.where(row_id == 0, prev, pltpu.roll(x, 1, axis=0))
    xl = x + (shifted - x) * mu_ref[...]
    r = xl[:, 0:W]
    k = xl[:, W:2 * W]
    v = xl[:, 2 * W:3 * W]
    xw = xl[:, 3 * W:3 * W + LORA_W]
    xa = xl[:, 3 * W + LORA_W:3 * W + LORA_W + LORA_A]
    xg = xl[:, 3 * W + LORA_W + LORA_A:]
    w = -_softplus(-(w0_ref[...] + _mm(jnp.tanh(xw), w2_ref[...]))) - 0.5
    ld = -jnp.exp(w)
    a = jax.nn.sigmoid(a0_ref[...] + _mm(xa, a2_ref[...]))
    gate = _mm(jax.nn.sigmoid(xg), g2_ref[...])
    kkv = k * kk_ref[...]
    k2 = k * (1.0 + (a - 1.0) * ka_ref[...])

    ti = lax.broadcasted_iota(jnp.int32, (L, L), 0)
    si = lax.broadcasted_iota(jnp.int32, (L, L), 1)
    incl = si <= ti
    strict = si < ti
    tri = jnp.where(incl, 1.0, 0.0).astype(BF16)
    cl3 = jnp.dot(tri, jnp.concatenate(_split_bf16(ld, 3), axis=1), preferred_element_type=F32)
    cl = cl3[:, :W] + (cl3[:, W:2 * W] + cl3[:, 2 * W:])
    cl_end = cl[L - 1:L, :]
    e_pos = jnp.exp(cl)
    e_neg = jnp.exp(-cl)
    e_prev = jnp.exp(cl - ld)
    e_end = jnp.exp(cl_end - cl)
    eye = jnp.where(ti == si, 1.0, 0.0)

    hsum = lambda z: _head_sums(z, ones_ref[...])

    kk_n = kkv * lax.rsqrt(jnp.maximum(hsum(kkv * kkv), 1e-24))
    bv = kk_n * a
    a_t = (-kk_n * e_prev).astype(BF16)
    b_t = (bv * e_neg).astype(BF16)
    k_t = (k2 * e_neg).astype(BF16)
    r_t = (r * e_pos).astype(BF16)
    v_b = v.astype(BF16)
    k_e = k2 * e_end
    b_e = bv * e_end

    yield None

    P = range(W // LANES)
    ps = [slice(m * LANES, (m + 1) * LANES) for m in P]
    first = lax.broadcasted_iota(jnp.int32, (1, LANES), 1) < N
    same_head = (lax.broadcasted_iota(jnp.int32, (LANES, LANES), 0) // N
                 == lax.broadcasted_iota(jnp.int32, (LANES, LANES), 1) // N)

    def diag_rows(x):
        zero = jnp.zeros_like(x)
        return jnp.concatenate([jnp.where(first, x, zero), jnp.where(first, zero, x)], axis=0)

    def diag_blocks(x):
        zero = jnp.zeros_like(x[:, :LANES])
        return jnp.concatenate([jnp.concatenate([x[:, :LANES], zero], axis=1),
                                jnp.concatenate([zero, x[:, LANES:]], axis=1)], axis=0)

    ke_t = [k_e[:, s].T for s in ps]
    be_t = [b_e[:, s].T for s in ps]
    p_end = [e_pos[:, s].T[:, L - 1:L] for s in ps]

    lhs = [jnp.concatenate([a_t[:, s], r_t[:, s]], axis=0) for s in ps]
    rhs = [jnp.concatenate([k_t[:, s], b_t[:, s]], axis=0) for s in ps]
    zero_b = jnp.zeros_like(lhs[0])
    aa = [[lax.dot_general(jnp.where(first, x, zero_b) if j == 0 else jnp.where(first, zero_b, x), y, _NT,
                           preferred_element_type=F32) for j in range(2)] for x, y in zip(lhs, rhs)]
    cat2 = lambda f: [jnp.concatenate([f(pair[0]), f(pair[1])], axis=1) for pair in aa]
    a_ak = cat2(lambda x: jnp.where(strict, x[:L, :L], 0.0))
    a_ab = cat2(lambda x: jnp.where(strict, x[:L, L:], 0.0))
    a_rk = cat2(lambda x: jnp.where(incl, x[L:, :L], 0.0))
    a_rb = cat2(lambda x: jnp.where(incl, x[L:, L:], 0.0))

    eye2 = jnp.concatenate([eye, eye], axis=1)
    tinv = [eye2 + x for x in a_ab]
    pw = [_mm(x, diag_blocks(x)) for x in a_ab]
    span = 2
    while 2 * span < L:
        both = [_mm(jnp.concatenate([t, p], axis=0), diag_blocks(p)) for t, p in zip(tinv, pw)]
        tinv = [t + x[:L] for t, x in zip(tinv, both)]
        pw = [x[L:] for x in both]
        span *= 2
    tinv = [t + _mm(t, diag_blocks(p)) for t, p in zip(tinv, pw)]

    v_d = [diag_rows(v_b[:, s]) for s in ps]
    av = [_mm(jnp.concatenate([a_ak[m], a_rk[m]], axis=0), v_d[m]) for m in P]
    tw = [_mm(tinv[m], jnp.concatenate([diag_rows(a_t[:, ps[m]]), diag_rows(av[m][:L].astype(BF16))], axis=1))
          for m in P]
    kv_loc = [jnp.where(same_head, _mm(ke_t[m], v_b[:, ps[m]]), 0.0) for m in P]

    ws = [_mm(jnp.concatenate([tw[m][:, :LANES].astype(BF16), r_t[:, ps[m]]], axis=0), st[m]) for m in P]
    u = [ws[m][:L] + tw[m][:, LANES:] for m in P]
    y = [ws[m][L:] + av[m][L:] + _mm(a_rb[m], diag_rows(u[m])) for m in P]
    st_new = [p_end[m] * st[m] + kv_loc[m] + jnp.where(same_head, _mm(be_t[m], u[m]), 0.0) for m in P]

    y = jnp.concatenate(y, axis=1)
    d = y - hsum(y) * (1.0 / N)
    yn = d * lax.rsqrt(hsum(d * d) * (1.0 / N) + GN_EPS) * lng_ref[...] + lnb_ref[...]
    bonus = hsum(r * k2 * rk_ref[...])
    yield (yn + bonus * v) * gate, st_new


def _rwkv_kernel(x_ref, xp_ref, *refs):
    *param_refs, o_ref, st_ref = refs
    c = pl.program_id(1)

    @pl.when(c == 0)
    def _():
        st_ref[...] = jnp.zeros(st_ref.shape, F32)

    chunks = []
    for i in range(x_ref.shape[0]):
        prev = jnp.where(c > 0, xp_ref[i, xp_ref.shape[1] - 1:, :], 0.0)
        st = [st_ref[i, m] for m in range(st_ref.shape[1])]
        chunks.append(_rwkv_chunk(x_ref[i], prev, st, *param_refs))
    for chunk in chunks:
        next(chunk)
    outs, states = [], []
    for chunk in chunks:
        out, st_new = next(chunk)
        outs.append(out.astype(o_ref.dtype))
        states.append(jnp.stack(st_new))
    o_ref[...] = jnp.stack(outs)
    st_ref[...] = jnp.stack(states)


def _rwkv(rw, vecs, w2, a2, g2, batch, seq):
    width = rw.shape[1]
    L = min(CHUNK, seq)
    nc = seq // L
    nb = RWKV_SEQS if batch % RWKV_SEQS == 0 else 1
    sub = 8
    rw3 = rw.reshape(batch, seq, width)
    vec = lambda wd: pl.BlockSpec((1, wd), lambda b, c: (0, 0))
    mat = lambda m: pl.BlockSpec(m.shape, lambda b, c: (0, 0))
    mu, w0, a0, kk, ka, rk, lng, lnb = vecs
    head = np.arange(LANES) // HEAD_DIM
    ones = jnp.asarray(head[:, None] == head[None, :], BF16)
    out = pl.pallas_call(
        _rwkv_kernel,
        grid=(batch // nb, nc),
        in_specs=[pl.BlockSpec((nb, L, width), lambda b, c: (b, c, 0)),
                  pl.BlockSpec((nb, sub, width), lambda b, c: (b, jnp.maximum(c * (L // sub) - 1, 0), 0)),
                  vec(width), vec(RWKV_WIDTH), mat(w2), vec(RWKV_WIDTH), mat(a2), mat(g2),
                  vec(RWKV_WIDTH), vec(RWKV_WIDTH), vec(RWKV_WIDTH), vec(RWKV_WIDTH), vec(RWKV_WIDTH), mat(ones)],
        out_specs=pl.BlockSpec((nb, L, RWKV_WIDTH), lambda b, c: (b, c, 0)),
        out_shape=jax.ShapeDtypeStruct((batch, seq, RWKV_WIDTH), BF16),
        scratch_shapes=[pltpu.VMEM((nb, RWKV_WIDTH // LANES, LANES, LANES), F32)],
        compiler_params=_params("parallel", "arbitrary"),
        name="rwkv7",
    )(rw3, rw3, mu, w0, w2, a0, a2, g2, kk, ka, rk, lng, lnb, ones)
    return out.reshape(batch * seq, RWKV_WIDTH)


def _merge_kernel(x_ref, oa_ref, ob_ref, gab_ref, wpa_ref, wpb_ref, wo_ref, o_ref):
    d = x_ref.shape[1]
    pa = jnp.dot(oa_ref[...], wpa_ref[...], preferred_element_type=F32)
    pb = jnp.dot(ob_ref[...], wpb_ref[...], preferred_element_type=F32)
    gab = gab_ref[...].astype(F32)
    merged = jax.nn.sigmoid(gab[:, :d]) * pa + jax.nn.sigmoid(gab[:, d:]) * pb
    o_ref[...] = x_ref[...] + _mm(merged, wo_ref[...])


def _merge(x2, oa, ob, gab, wpa, wpb, wo, tm):
    n, d = x2.shape
    row = lambda wd: pl.BlockSpec((tm, wd), lambda i: (i, 0))
    mat = lambda m: pl.BlockSpec(m.shape, lambda i: (0, 0))
    return pl.pallas_call(
        _merge_kernel,
        grid=(n // tm,),
        in_specs=[row(d), row(oa.shape[1]), row(ob.shape[1]), row(gab.shape[1]), mat(wpa), mat(wpb), mat(wo)],
        out_specs=row(d),
        out_shape=jax.ShapeDtypeStruct((n, d), F32),
        compiler_params=_params("parallel"),
        name="merge",
    )(x2, oa, ob, gab, wpa, wpb, wo)


def _ffn_kernel(seq, x_ref, xh_ref, g_ref, wv_ref, wg_ref, cwv_ref, cwg_ref, cbv_ref, cbg_ref, wd_ref, o_ref,
                h_ref, uv_ref, ug_ref):
    tm = x_ref.shape[0]
    i = pl.program_id(0)
    f = pl.program_id(1)

    @pl.when(f == 0)
    def _():
        first = lax.rem(i * tm, seq) == 0
        halo = jnp.where(first, 0.0, _rms(xh_ref[...], g_ref[...]))
        h_ref[0:HALO, :] = halo.astype(BF16)
        h_ref[HALO:, :] = _rms(x_ref[...], g_ref[...]).astype(BF16)

    h = h_ref[...]
    uv_ref[...] = jnp.dot(h, wv_ref[...], preferred_element_type=F32)
    ug_ref[...] = jnp.dot(h, wg_ref[...], preferred_element_type=F32)

    def conv(u_ref, cw_ref, cb_ref):
        acc = cb_ref[...] + cw_ref[0:1, :] * u_ref[pl.ds(HALO - 2, tm), :]
        acc = acc + cw_ref[1:2, :] * u_ref[pl.ds(HALO - 1, tm), :]
        return acc + cw_ref[2:3, :] * u_ref[pl.ds(HALO, tm), :]

    val = conv(uv_ref, cwv_ref, cbv_ref)
    gt = conv(ug_ref, cwg_ref, cbg_ref)
    y = _mm(gt * jax.nn.sigmoid(gt) * val, wd_ref[...])

    @pl.when(f == 0)
    def _():
        o_ref[...] = x_ref[...] + y

    @pl.when(f > 0)
    def _():
        o_ref[...] = o_ref[...] + y


def _ffn(x1, g, w_up, conv_w, conv_b, w_down, seq, tm, tf):
    n, d = x1.shape
    dff = w_down.shape[0]
    nf = dff // tf
    return pl.pallas_call(
        functools.partial(_ffn_kernel, seq),
        grid=(n // tm, nf),
        in_specs=[pl.BlockSpec((tm, d), lambda i, f: (i, 0)),
                  pl.BlockSpec((HALO, d), lambda i, f: (jnp.maximum(i * (tm // HALO) - 1, 0), 0)),
                  pl.BlockSpec((1, d), lambda i, f: (0, 0)),
                  pl.BlockSpec((d, tf), lambda i, f: (0, f)),
                  pl.BlockSpec((d, tf), lambda i, f: (0, nf + f)),
                  pl.BlockSpec((CONV_WIDTH, tf), lambda i, f: (0, f)),
                  pl.BlockSpec((CONV_WIDTH, tf), lambda i, f: (0, nf + f)),
                  pl.BlockSpec((1, tf), lambda i, f: (0, f)),
                  pl.BlockSpec((1, tf), lambda i, f: (0, nf + f)),
                  pl.BlockSpec((tf, d), lambda i, f: (f, 0))],
        out_specs=pl.BlockSpec((tm, d), lambda i, f: (i, 0)),
        out_shape=jax.ShapeDtypeStruct((n, d), F32),
        scratch_shapes=[pltpu.VMEM((tm + HALO, d), BF16),
                        pltpu.VMEM((tm + HALO, tf), F32),
                        pltpu.VMEM((tm + HALO, tf), F32)],
        compiler_params=_params("parallel", "arbitrary"),
        name="convffn",
    )(x1, x1, g, w_up, w_up, conv_w, conv_w, conv_b, conv_b, w_down)


def _t5_bucket(dist):
    n = np.maximum(dist, 0)
    max_exact = REL_BUCKETS // 2
    ratio = np.log(np.maximum(n, 1).astype(np.float32) / max_exact) / math.log(REL_MAX_DIST / max_exact)
    large = np.minimum(max_exact + (ratio * (REL_BUCKETS - max_exact)).astype(np.int32), REL_BUCKETS - 1)
    return np.where(n < max_exact, n, large).astype(np.int32)


def _bias_tables(rel_bias, seq, ncmp):
    tq, tk = Q_TILE, K_TILE
    G, R = NSA_KV_GROUPS, NSA_GROUP
    nq = seq // tq
    step = tq // CMP_STRIDE
    tab = rel_bias.astype(F32)

    off = (nq - 1) * step
    d_c = np.arange(tq)[None, :] - (np.arange(ncmp + off)[:, None] - off) * CMP_STRIDE - (CMP_BLOCK - 1)
    n_b = -(-REL_MAX_DIST // tk)
    d0 = np.arange(tq)[None, :] - np.arange(tk)[:, None]
    d_n = np.stack([d0 - o * tk for o in range(-n_b, tq // tk)])

    buckets = np.concatenate([_t5_bucket(d_c).reshape(-1), _t5_bucket(d_n).reshape(-1)])
    onehot = (jnp.asarray(buckets)[:, None] == jnp.arange(REL_BUCKETS, dtype=jnp.int32)[None, :]).astype(F32)
    vals = jnp.dot(onehot, tab * LOG2E, precision=lax.Precision.HIGHEST, preferred_element_type=F32).T
    by_group = lambda x: x.reshape(G, R, -1, tq).transpose(0, 2, 1, 3).reshape(G, -1, R * tq)
    tile_r = lambda m: np.tile(m, (1, R))

    base = jnp.where(jnp.asarray(tile_r(d_c >= 0)), by_group(vals[:, :d_c.size]), NEG_INF)
    bias_c = jnp.stack([base[:, off - i * step:off - i * step + ncmp] for i in range(nq)])

    far = jnp.repeat(tab[REL_BUCKETS - 1] * LOG2E, tq).reshape(G, 1, R * tq)
    near = by_group(vals[:, d_c.size:]) - far
    near = jnp.where(jnp.asarray(tile_r(d_n.reshape(-1, tq) >= 0)), near, NEG_INF)
    bias_near = near.reshape(G, d_n.shape[0], tk, R * tq).transpose(1, 0, 2, 3)
    win = np.stack([tile_r(np.where(d0 - o * tk < WINDOW, 0.0, NEG_INF)) for o in _win_masked(tq, tk)])
    return bias_c, bias_near, jnp.asarray(win.astype(np.float32))


def _overlap_matrix(ncmp_pad, n_slc, n_cmp):
    m = np.zeros((ncmp_pad, n_slc), np.float32)
    ratio = SLC_BLOCK // CMP_STRIDE
    for j in range(n_slc):
        for o, wgt in enumerate(OVERLAP_W):
            cidx = ratio * j + o - (CMP_BLOCK // CMP_STRIDE - 1)
            if 0 <= cidx < n_cmp:
                m[cidx, j] += wgt
    return jnp.asarray(m.T)


def _layer(x, attn_norm_g, w_in, rel_bias, q_norm_g, k_norm_g, cmp_pe_k, cmp_w1_k, cmp_w2_k,
           cmp_pe_v, cmp_w1_v, cmp_w2_v, rwkv_mu, rwkv_w0, rwkv_w2, rwkv_a0, rwkv_a2, rwkv_g2,
           rwkv_k_k, rwkv_k_a, rwkv_r_k, rwkv_ln_g, rwkv_ln_b, w_proj_a, w_proj_b, w_out,
           ffn_norm_g, w_up, conv_w, conv_b, w_down):
    batch, seq, d = x.shape
    n = batch * seq
    assert seq % Q_TILE == 0 and Q_TILE % K_TILE == 0 and WINDOW % K_TILE == 0
    assert seq % CMP_STRIDE == 0
    assert seq // SLC_BLOCK <= LANES - HEAD_DIM
    x2 = x.reshape(n, d)
    row = lambda a: a.reshape(1, -1).astype(F32)

    kvw = 6 * NSA_KV_WIDTH
    o_q, o_kv, o_gn = 0, NSA_WIDTH, NSA_WIDTH + kvw
    o_rw = o_gn + 3 * NSA_HEADS
    o_gab = o_rw + RWKV_IN_WIDTH
    gn_pad = jnp.zeros((d, LANES - 3 * NSA_HEADS), w_in.dtype)
    w_cat = jnp.concatenate([w_in[:, o_rw:o_gab], w_in[:, o_q:o_kv], w_in[:, o_kv:o_gn], w_in[:, o_gab:],
                             w_in[:, o_gn:o_rw], gn_pad], axis=1).astype(BF16)
    widths = (RWKV_IN_WIDTH, NSA_WIDTH, 2 * NSA_KV_WIDTH, 4 * NSA_KV_WIDTH, 2 * d, LANES)
    dtypes = (F32, BF16, F32, BF16, BF16, F32)
    rw, q, kv_cmp, kv, gab, gn = _inproj(x2, row(attn_norm_g), w_cat, widths, dtypes, tm=256)

    gk = row(k_norm_g)
    ksx, kwn, vst, vwt = _kvprep(kv, jnp.concatenate([gk, gk], axis=1), seq, tm=512)

    nchunk = seq // CMP_STRIDE
    n_cmp = nchunk - CMP_BLOCK // CMP_STRIDE + 1
    pe = jnp.stack([cmp_pe_k, cmp_pe_v]).astype(F32)
    w1 = jnp.stack([cmp_w1_k, cmp_w1_v]).astype(BF16).reshape(2, CMP_BLOCK, HEAD_DIM, CMP_HIDDEN)
    w2 = jnp.stack([cmp_w2_k, cmp_w2_v]).astype(BF16)
    kc, vc = _compress(kv_cmp, pe, w1, w2, gk, batch, seq)

    bias_c, bias_near, win_mask = _bias_tables(rel_bias, seq, nchunk)
    wov = _overlap_matrix(nchunk, seq // SLC_BLOCK, n_cmp)
    o_a = _attention(q, gn, kc, jnp.swapaxes(vc, -1, -2), ksx, vst, kwn, vwt, bias_c, bias_near, win_mask,
                     q_norm_g.reshape(-1, 1).astype(F32), wov, batch, seq)

    vecs = tuple(row(a) for a in (rwkv_mu, rwkv_w0, rwkv_a0, rwkv_k_k, rwkv_k_a, rwkv_r_k, rwkv_ln_g, rwkv_ln_b))
    o_b = _rwkv(rw, vecs, rwkv_w2.astype(BF16), rwkv_a2.astype(BF16), rwkv_g2.astype(BF16), batch, seq)

    x1 = _merge(x2, o_a, o_b, gab, w_proj_a.astype(BF16), w_proj_b.astype(BF16), w_out.astype(BF16), tm=512)
    dff = w_down.shape[0]
    out = _ffn(x1, row(ffn_norm_g), w_up.astype(BF16), conv_w.astype(F32), row(conv_b), w_down.astype(BF16),
               seq, tm=512, tf=dff // 2)
    return out.reshape(batch, seq, d)


def kernel(x, attn_norm_g, w_in, rel_bias, q_norm_g, k_norm_g, cmp_pe_k, cmp_w1_k, cmp_w2_k, cmp_pe_v, cmp_w1_v,
           cmp_w2_v, rwkv_mu, rwkv_w0, rwkv_w2, rwkv_a0, rwkv_a2, rwkv_g2, rwkv_k_k, rwkv_k_a, rwkv_r_k,
           rwkv_ln_g, rwkv_ln_b, w_proj_a, w_proj_b, w_out, ffn_norm_g, w_up, conv_w, conv_b, w_down):
    per_layer = (attn_norm_g, w_in, None, q_norm_g, k_norm_g, cmp_pe_k, cmp_w1_k, cmp_w2_k, cmp_pe_v, cmp_w1_v,
                 cmp_w2_v, rwkv_mu, rwkv_w0, rwkv_w2, rwkv_a0, rwkv_a2, rwkv_g2, rwkv_k_k, rwkv_k_a, rwkv_r_k,
                 rwkv_ln_g, rwkv_ln_b, w_proj_a, w_proj_b, w_out, ffn_norm_g, w_up, conv_w, conv_b, w_down)
    for l in range(attn_norm_g.shape[0]):
        args = [rel_bias if p is None else p[l] for p in per_layer]
        x = _layer(x, *args)
    return x
```

```python
import functools
import math

import numpy as np
import jax
import jax.numpy as jnp
from jax import lax
from jax.experimental import pallas as pl
from jax.experimental.pallas import tpu as pltpu

F32 = jnp.float32
BF16 = jnp.bfloat16

HEAD_DIM = 64
NSA_HEADS = 8
NSA_KV_GROUPS = 2
NSA_GROUP = NSA_HEADS // NSA_KV_GROUPS
NSA_WIDTH = NSA_HEADS * HEAD_DIM
NSA_KV_WIDTH = NSA_KV_GROUPS * HEAD_DIM
CMP_BLOCK = 32
CMP_STRIDE = 16
CMP_HIDDEN = 256
SLC_BLOCK = 64
SLC_TOPK = 16
OVERLAP_W = (1, 2, 2, 2, 1)
WINDOW = 512
REL_BUCKETS = 32
REL_MAX_DIST = 128
RWKV_HEADS = 8
RWKV_WIDTH = RWKV_HEADS * HEAD_DIM
LORA_W = 64
LORA_A = 64
LORA_G = 128
RWKV_IN_WIDTH = 3 * RWKV_WIDTH + LORA_W + LORA_A + LORA_G
GN_EPS = 64e-5
CONV_WIDTH = 3
RMS_EPS = 1e-6
NEG_INF = -1e30
FORCE = 1e9
LOG2E = math.log2(math.e)
M_FLOOR = -1e20

LANES = 128
VMEM_LIMIT = 56 * 1024 * 1024

Q_TILE = 256
K_TILE = 256
VT_ROWS = 80
CHUNK = 128
RWKV_SEQS = 2
HALO = 16

_NT = (((1,), (1,)), ((), ()))


def _params(*sem):
    return pltpu.CompilerParams(dimension_semantics=sem, vmem_limit_bytes=VMEM_LIMIT)


def _mm(a, b):
    return jnp.dot(a.astype(BF16), b.astype(BF16), preferred_element_type=F32)


def _mm_nt(a, b):
    return lax.dot_general(a.astype(BF16), b.astype(BF16), _NT, preferred_element_type=F32)


def _mm_f32(a, b):
    return jnp.dot(a, b, preferred_element_type=F32, precision=lax.Precision.HIGHEST)


def _rms(x, g):
    return x * lax.rsqrt(jnp.mean(x * x, axis=-1, keepdims=True) + RMS_EPS) * g


def _inproj_kernel(x_ref, g_ref, w_ref, *out_refs):
    h = _rms(x_ref[...], g_ref[...]).astype(BF16)
    off = 0
    for ref in out_refs:
        n = ref.shape[-1]
        ref[...] = jnp.dot(h, w_ref[:, off:off + n], preferred_element_type=F32).astype(ref.dtype)
        off += n


def _inproj(x2, g, w, widths, dtypes, tm):
    n, d = x2.shape
    return pl.pallas_call(
        _inproj_kernel,
        grid=(n // tm,),
        in_specs=[pl.BlockSpec((tm, d), lambda i: (i, 0)),
                  pl.BlockSpec((1, d), lambda i: (0, 0)),
                  pl.BlockSpec(w.shape, lambda i: (0, 0))],
        out_specs=[pl.BlockSpec((tm, wd), lambda i: (i, 0)) for wd in widths],
        out_shape=[jax.ShapeDtypeStruct((n, wd), dt) for wd, dt in zip(widths, dtypes)],
        compiler_params=_params("parallel"),
        name="inproj",
    )(x2, g, w)


def _kvprep_kernel(seq, kv_ref, gk_ref, ksx_ref, kwn_ref, vs_ref, vw_ref):
    tm = kv_ref.shape[0]
    kv = kv_ref[...].astype(F32)
    lane = lax.broadcasted_iota(jnp.int32, (tm, LANES), 1)
    lo = lane < HEAD_DIM

    def norm2(x):
        x2 = x * x
        s_lo = jnp.sum(jnp.where(lo, x2, 0.0), axis=-1, keepdims=True)
        s_hi = jnp.sum(jnp.where(lo, 0.0, x2), axis=-1, keepdims=True)
        ms = jnp.where(lo, s_lo, s_hi) * (1.0 / HEAD_DIM)
        return x * lax.rsqrt(ms + RMS_EPS) * gk_ref[...]

    ks = norm2(kv[:, 0:LANES])
    kw = norm2(kv[:, 2 * LANES:3 * LANES])
    row = lax.rem(pl.program_id(0) * tm, seq) + lax.broadcasted_iota(jnp.int32, (tm, LANES), 0)
    onehot = jnp.where(lane - HEAD_DIM == row // SLC_BLOCK, 1.0, 0.0)
    ksx_ref[0] = jnp.where(lo, ks, onehot).astype(BF16)
    ksx_ref[1] = jnp.where(lo, pltpu.roll(ks, HEAD_DIM, axis=1), onehot).astype(BF16)
    flag_col = jnp.where(lane == HEAD_DIM, 1.0, 0.0)
    kwn_ref[0] = jnp.where(lo, kw, flag_col).astype(BF16)
    kwn_ref[1] = jnp.where(lo, pltpu.roll(kw, HEAD_DIM, axis=1), flag_col).astype(BF16)
    tail = jnp.where(lax.broadcasted_iota(jnp.int32, (VT_ROWS - HEAD_DIM, K_TILE), 0) == 0, 1.0, 0.0)
    for col, ref in ((1, vs_ref), (3, vw_ref)):
        for t in range(tm // K_TILE):
            vt = kv[t * K_TILE:(t + 1) * K_TILE, col * LANES:(col + 1) * LANES].T
            for g in range(NSA_KV_GROUPS):
                ref[g, t] = jnp.concatenate([vt[g * HEAD_DIM:(g + 1) * HEAD_DIM], tail], axis=0).astype(BF16)


def _kvprep(kv, gk2, seq, tm):
    n = kv.shape[0]
    vt_spec = lambda: pl.BlockSpec((2, tm // K_TILE, VT_ROWS, K_TILE), lambda i: (0, i, 0, 0))
    vt_shape = jax.ShapeDtypeStruct((2, n // K_TILE, VT_ROWS, K_TILE), BF16)
    return pl.pallas_call(
        functools.partial(_kvprep_kernel, seq),
        grid=(n // tm,),
        in_specs=[pl.BlockSpec((tm, kv.shape[1]), lambda i: (i, 0)),
                  pl.BlockSpec((1, LANES), lambda i: (0, 0))],
        out_specs=[pl.BlockSpec((2, tm, LANES), lambda i: (0, i, 0)),
                   pl.BlockSpec((2, tm, LANES), lambda i: (0, i, 0)), vt_spec(), vt_spec()],
        out_shape=[jax.ShapeDtypeStruct((2, n, LANES), BF16), jax.ShapeDtypeStruct((2, n, LANES), BF16),
                   vt_shape, vt_shape],
        compiler_params=_params("parallel"),
        name="kvprep",
    )(kv, gk2)


def _compress_kernel(k_ref, v_ref, pe_ref, w1_ref, w2_ref, gk_ref, kc_ref, vc_ref):
    nchunk = k_ref.shape[0] // CMP_STRIDE
    span = CMP_BLOCK // CMP_STRIDE
    G = NSA_KV_GROUPS
    acc = [[[jnp.zeros((nchunk, CMP_HIDDEN), F32) for _ in range(span)] for _ in range(G)] for _ in range(2)]
    for p in range(CMP_STRIDE):
        for s, ref in enumerate((k_ref, v_ref)):
            x = ref[pl.ds(p, nchunk, stride=CMP_STRIDE), :]
            for g in range(G):
                xs = x[:, g * HEAD_DIM:(g + 1) * HEAD_DIM]
                for half in range(span):
                    q = half * CMP_STRIDE + p
                    acc[s][g][half] = acc[s][g][half] + _mm(xs + pe_ref[s, q:q + 1, :], w1_ref[s, q])
    for s, out_ref in ((0, kc_ref), (1, vc_ref)):
        for g in range(G):
            hid = acc[s][g][0]
            for half in range(1, span):
                hid = hid + pltpu.roll(acc[s][g][half], nchunk - half, axis=0)
            out = _mm(jax.nn.gelu(hid), w2_ref[s])
            out_ref[0, g] = _rms(out, gk_ref[...]) if s == 0 else out


def _compress(kv, pe, w1, w2, gk, batch, seq):
    nchunk = seq // CMP_STRIDE
    out_spec = lambda: pl.BlockSpec((1, NSA_KV_GROUPS, nchunk, HEAD_DIM), lambda b: (b, 0, 0, 0))
    out_shape = jax.ShapeDtypeStruct((batch, NSA_KV_GROUPS, nchunk, HEAD_DIM), F32)
    const = lambda a: pl.BlockSpec(a.shape, lambda b: (0,) * a.ndim)
    return pl.pallas_call(
        _compress_kernel,
        grid=(batch,),
        in_specs=[pl.BlockSpec((seq, NSA_KV_WIDTH), lambda b: (b, 0)), pl.BlockSpec((seq, NSA_KV_WIDTH), lambda b: (b, 1)),
                  const(pe), const(w1), const(w2), const(gk)],
        out_specs=[out_spec(), out_spec()],
        out_shape=[out_shape, out_shape],
        compiler_params=_params("parallel"),
        name="compress",
    )(kv, kv, pe, w1, w2, gk)


def _rank_select(score_t, cur_t):
    nb = score_t.shape[0]
    jj = lax.broadcasted_iota(jnp.int32, score_t.shape, 0)
    rank = jnp.zeros(score_t.shape, F32)
    for j in range(nb):
        row = score_t[j:j + 1, :]
        gt = jnp.where(row > score_t, 1.0, 0.0)
        ge = jnp.where(row >= score_t, 1.0, 0.0)
        rank = rank + jnp.where(jj > j, ge, gt)
    keep = jnp.where(rank < float(min(SLC_TOPK, nb)), 1.0, 0.0) * jnp.where(jj <= cur_t, 1.0, 0.0)
    return jnp.where(keep > 0.5, 0.0, NEG_INF)


def _win_masked(tq, tk):
    return [o for o in range(-(WINDOW // tk), 0) if tq - 1 - o * tk >= WINDOW]


def _scores(ks, ws):
    return [jnp.dot(k, w, preferred_element_type=F32) for k, w in zip(ks, ws)]


def _softmax_update(m_ref, acc_ref, slots, s, vts):
    m_old = [m_ref[i] for i in slots]
    m_new = [jnp.maximum(mo, jnp.max(x, axis=0, keepdims=True)) for mo, x in zip(m_old, s)]
    pr = [jnp.exp2(x - mn).astype(BF16) for x, mn in zip(s, m_new)]
    pv = [jnp.dot(vt, y, preferred_element_type=F32) for vt, y in zip(vts, pr)]
    acc = [jnp.exp2(mo - mn) * acc_ref[i] + z for mo, mn, i, z in zip(m_old, m_new, slots, pv)]
    for i, mn, ac in zip(slots, m_new, acc):
        m_ref[i] = mn
        acc_ref[i] = ac


def _attn_kernel(q_ref, gn_ref, kc_ref, vct_ref, ksx_ref, vst_ref, kwn_ref, vwt_ref, bc_ref, bn_ref, wm_ref,
                 gq_ref, wov_ref, o_ref, m_ref, acc_ref, sa_ref, sb_ref):
    tq = q_ref.shape[0]
    tk = K_TILE
    nblk = wov_ref.shape[0]
    G, R = NSA_KV_GROUPS, NSA_GROUP
    qt = pl.program_id(1)
    n_q = tq // tk
    n_b = -(-REL_MAX_DIST // tk)
    j0 = qt * n_q
    qT = q_ref[...].astype(F32).T
    gates = jax.nn.sigmoid(gn_ref[...].T)
    gq = gq_ref[...]

    qn, qx, negs, o_cmp = [], [], [], []
    for g in range(G):
        xs = []
        for h in range(g * R, (g + 1) * R):
            x = qT[h * HEAD_DIM:(h + 1) * HEAD_DIM]
            x = x * lax.rsqrt(jnp.mean(x * x, axis=0, keepdims=True) + RMS_EPS) * gq * (HEAD_DIM ** -0.5 * LOG2E)
            xs.append(x.astype(BF16))
        qn.append(jnp.concatenate(xs, axis=1))

    for g in range(G):
        s = jnp.dot(kc_ref[0, g].astype(BF16), qn[g], preferred_element_type=F32) + bc_ref[0, g]
        m = jnp.maximum(jnp.max(s, axis=0, keepdims=True), M_FLOOR)
        e = jnp.exp2(s - m)
        p = e * (1.0 / jnp.maximum(jnp.sum(e, axis=0, keepdims=True), 1e-30))
        o_cmp.append(jnp.dot(vct_ref[0, g].astype(BF16), p.astype(BF16), preferred_element_type=F32))
        psum = p[:, 0:tq]
        for r in range(1, R):
            psum = psum + p[:, r * tq:(r + 1) * tq]
        imp = _mm_f32(wov_ref[...], psum)

        blk = lax.broadcasted_iota(jnp.int32, imp.shape, 0)
        cur = (qt * tq + lax.broadcasted_iota(jnp.int32, imp.shape, 1)) // SLC_BLOCK
        forced = (blk == 0) | (blk == cur) | (blk == cur - 1)
        score = jnp.where(forced, FORCE, jnp.where(blk <= cur, imp, -FORCE))
        neg = _rank_select(score, cur)
        if nblk < LANES - HEAD_DIM:
            neg = jnp.concatenate([neg, jnp.zeros((LANES - HEAD_DIM - nblk, tq), F32)], axis=0)
        neg = neg.astype(BF16)
        negs.append(neg)
        qx.append(jnp.concatenate([qn[g], jnp.concatenate([neg] * R, axis=1)], axis=0))

    m_ref[...] = jnp.full(m_ref.shape, M_FLOOR, F32)
    acc_ref[...] = jnp.zeros(acc_ref.shape, F32)

    def invalid_before(level):
        return jnp.where(qt >= level, 0.0, NEG_INF)

    def slc_queries(g, level):
        if level == 0:
            return qx[g]
        rows = jnp.minimum(negs[g], invalid_before(level).astype(BF16))
        return jnp.concatenate([qn[g], jnp.concatenate([rows] * R, axis=1)], axis=0)

    def win_queries(g, level):
        flag_row = lax.broadcasted_iota(jnp.int32, (LANES - HEAD_DIM, R * tq), 0) == 0
        extra = jnp.where(flag_row, invalid_before(level), 0.0).astype(BF16)
        return jnp.concatenate([qn[g], extra], axis=0)

    def slc_probs(o, bias):
        j = jnp.maximum(j0 + o, 0)
        sl = pl.ds(pl.multiple_of(j * tk, tk), tk)
        return [(g, ksx_ref[g, sl, :], vst_ref[g, j], slc_queries(g, max(-(o // n_q), 0)), bias.at[g])
                for g in range(G)]

    def win_probs(o, bias, per_group):
        j = jnp.maximum(j0 + o, 0)
        sl = pl.ds(pl.multiple_of(j * tk, tk), tk)
        pick = lambda g: None if bias is None else (bias.at[g] if per_group else bias)
        return [(G + g, kwn_ref[g, sl, :], vwt_ref[g, j], win_queries(g, max(-(o // n_q), 0)), pick(g))
                for g in range(G)]

    n_far = jnp.maximum(j0 - n_b, 0)
    groups = list(range(G))

    def far_scores(j, buf, gs=groups):
        sl = pl.ds(pl.multiple_of(j * tk, tk), tk)
        for g, x in zip(gs, _scores([ksx_ref[g, sl, :] for g in gs], [qx[g] for g in gs])):
            buf[g] = x

    def far_update(j, buf, gs=groups):
        _softmax_update(m_ref, acc_ref, gs, [buf[g] for g in gs], [vst_ref[g, j] for g in gs])

    far_scores(0, sa_ref)

    def far_body(i, carry):
        for g in groups:
            far_scores(2 * i + 1, sb_ref, [g])
            far_update(2 * i, sa_ref, [g])
        for g in groups:
            far_scores(2 * i + 2, sa_ref, [g])
            far_update(2 * i + 1, sb_ref, [g])
        return carry

    lax.fori_loop(0, n_far // 2, far_body, 0)

    @pl.when(n_far % 2 == 1)
    def _():
        far_update(n_far - 1, sa_ref)

    masked = _win_masked(tq, tk)
    rounds = []
    for o in range(-(WINDOW // tk), n_q):
        if o >= -n_b:
            rounds.append(slc_probs(o, bn_ref.at[o + n_b]) + win_probs(o, bn_ref.at[o + n_b], True))
        elif o in masked:
            rounds.append(win_probs(o, wm_ref.at[masked.index(o)], False))
        else:
            rounds.append(win_probs(o, None, False))

    def round_scores(probs):
        sc = _scores([p[1] for p in probs], [p[3] for p in probs])
        return [x if p[4] is None else x + p[4][...] for x, p in zip(sc, probs)]

    sc = round_scores(rounds[0])
    for i, probs in enumerate(rounds):
        sc_next = round_scores(rounds[i + 1]) if i + 1 < len(rounds) else None
        _softmax_update(m_ref, acc_ref, [p[0] for p in probs], sc, [p[2] for p in probs])
        sc = sc_next

    outs = []
    for g in range(G):
        acc = acc_ref[g]
        o_slc = acc[:HEAD_DIM] / acc[HEAD_DIM:HEAD_DIM + 1]
        acc = acc_ref[G + g]
        o_win = acc[:HEAD_DIM] / acc[HEAD_DIM:HEAD_DIM + 1]
        for r in range(R):
            h = g * R + r
            cols = slice(r * tq, (r + 1) * tq)
            outs.append(gates[3 * h:3 * h + 1] * o_cmp[g][:, cols] + gates[3 * h + 1:3 * h + 2] * o_slc[:, cols]
                        + gates[3 * h + 2:3 * h + 3] * o_win[:, cols])
    o_ref[...] = jnp.concatenate(outs, axis=0).T.astype(o_ref.dtype)


def _attention(q, gn, kc, vct, ksx, vst, kwn, vwt, bias_c, bias_near, win_mask, gq, wov, batch, seq):
    n = q.shape[0]
    tq = Q_TILE
    nq = seq // tq
    ncmp = kc.shape[2]
    row = lambda b, i: (b * nq + i, 0)
    whole = lambda b, i: (0, b, 0)
    tiles = lambda b, i: (0, b, 0, 0)
    const = lambda a: pl.BlockSpec(a.shape, lambda b, i: (0,) * a.ndim, pipeline_mode=pl.Buffered(1))
    return pl.pallas_call(
        _attn_kernel,
        grid=(batch, nq),
        in_specs=[pl.BlockSpec((tq, NSA_WIDTH), row),
                  pl.BlockSpec((tq, LANES), row),
                  pl.BlockSpec((1, NSA_KV_GROUPS, ncmp, HEAD_DIM), lambda b, i: (b, 0, 0, 0)),
                  pl.BlockSpec((1, NSA_KV_GROUPS, HEAD_DIM, ncmp), lambda b, i: (b, 0, 0, 0)),
                  pl.BlockSpec((2, seq, LANES), whole),
                  pl.BlockSpec((2, seq // K_TILE, VT_ROWS, K_TILE), tiles),
                  pl.BlockSpec((2, seq, LANES), whole),
                  pl.BlockSpec((2, seq // K_TILE, VT_ROWS, K_TILE), tiles),
                  pl.BlockSpec((1, NSA_KV_GROUPS, ncmp, NSA_GROUP * tq), lambda b, i: (i, 0, 0, 0)),
                  const(bias_near), const(win_mask), const(gq), const(wov)],
        out_specs=pl.BlockSpec((tq, NSA_WIDTH), row),
        out_shape=jax.ShapeDtypeStruct((n, NSA_WIDTH), BF16),
        scratch_shapes=[pltpu.VMEM((2 * NSA_KV_GROUPS, 1, NSA_GROUP * tq), F32),
                        pltpu.VMEM((2 * NSA_KV_GROUPS, VT_ROWS, NSA_GROUP * tq), F32),
                        pltpu.VMEM((NSA_KV_GROUPS, K_TILE, NSA_GROUP * tq), F32),
                        pltpu.VMEM((NSA_KV_GROUPS, K_TILE, NSA_GROUP * tq), F32)],
        compiler_params=_params("parallel", "parallel"),
        name="nsa_attention",
    )(q, gn, kc, vct, ksx, vst, kwn, vwt, bias_c, bias_near, win_mask, gq, wov)


def _split_bf16(z, parts):
    out = []
    for _ in range(parts - 1):
        hi = z.astype(BF16)
        out.append(hi)
        z = z - hi.astype(F32)
    return out + [z.astype(BF16)]


def _head_sums(z, ones):
    nb = z.shape[1] // LANES
    rows = z.shape[0]
    parts = _split_bf16(z, 2)
    stacked = jnp.concatenate([p[:, m * LANES:(m + 1) * LANES] for p in parts for m in range(nb)], axis=0)
    sums = jnp.dot(stacked, ones, preferred_element_type=F32)
    hi, lo = sums[:nb * rows], sums[nb * rows:]
    return jnp.concatenate([hi[m * rows:(m + 1) * rows] + lo[m * rows:(m + 1) * rows] for m in range(nb)], axis=1)


def _softplus(z):
    return jnp.maximum(z, 0.0) + jnp.log(1.0 + jnp.exp(-jnp.abs(z)))


def _rwkv_chunk(x, prev, st, mu_ref, w0_ref, w2_ref, a0_ref, a2_ref, g2_ref, kk_ref, ka_ref, rk_ref,
                lng_ref, lnb_ref, ones_ref):
    L = x.shape[0]
    W = RWKV_WIDTH
    N = HEAD_DIM
    row_id = lax.broadcasted_iota(jnp.int32, x.shape, 0)
    shifted = jnp.where(row_id == 0, prev, pltpu.roll(x, 1, axis=0))
    xl = x + (shifted - x) * mu_ref[...]
    r = xl[:, 0:W]
    k = xl[:, W:2 * W]
    v = xl[:, 2 * W:3 * W]
    xw = xl[:, 3 * W:3 * W + LORA_W]
    xa = xl[:, 3 * W + LORA_W:3 * W + LORA_W + LORA_A]
    xg = xl[:, 3 * W + LORA_W + LORA_A:]
    w = -_softplus(-(w0_ref[...] + _mm(jnp.tanh(xw), w2_ref[...]))) - 0.5
    ld = -jnp.exp(w)
    a = jax.nn.sigmoid(a0_ref[...] + _mm(xa, a2_ref[...]))
    gate = _mm(jax.nn.sigmoid(xg), g2_ref[...])
    kkv = k * kk_ref[...]
    k2 = k * (1.0 + (a - 1.0) * ka_ref[...])

    ti = lax.broadcasted_iota(jnp.int32, (L, L), 0)
    si = lax.broadcasted_iota(jnp.int32, (L, L), 1)
    incl = si <= ti
    strict = si < ti
    tri = jnp.where(incl, 1.0, 0.0).astype(BF16)
    cl3 = jnp.dot(tri, jnp.concatenate(_split_bf16(ld, 3), axis=1), preferred_element_type=F32)
    cl = cl3[:, :W] + (cl3[:, W:2 * W] + cl3[:, 2 * W:])
    cl_end = cl[L - 1:L, :]
    e_pos = jnp.exp(cl)
    e_neg = jnp.exp(-cl)
    e_prev = jnp.exp(cl - ld)
    e_end = jnp.exp(cl_end - cl)
    eye = jnp.where(ti == si, 1.0, 0.0)

    hsum = lambda z: _head_sums(z, ones_ref[...])

    kk_n = kkv * lax.rsqrt(jnp.maximum(hsum(kkv * kkv), 1e-24))
    bv = kk_n * a
    a_t = (-kk_n * e_prev).astype(BF16)
    b_t = (bv * e_neg).astype(BF16)
    k_t = (k2 * e_neg).astype(BF16)
    r_t = (r * e_pos).astype(BF16)
    v_b = v.astype(BF16)
    k_e = k2 * e_end
    b_e = bv * e_end

    yield None

    P = range(W // LANES)
    ps = [slice(m * LANES, (m + 1) * LANES) for m in P]
    first = lax.broadcasted_iota(jnp.int32, (1, LANES), 1) < N
    same_head = (lax.broadcasted_iota(jnp.int32, (LANES, LANES), 0) // N
                 == lax.broadcasted_iota(jnp.int32, (LANES, LANES), 1) // N)

    def diag_rows(x):
        zero = jnp.zeros_like(x)
        return jnp.concatenate([jnp.where(first, x, zero), jnp.where(first, zero, x)], axis=0)

    def diag_blocks(x):
        zero = jnp.zeros_like(x[:, :LANES])
        return jnp.concatenate([jnp.concatenate([x[:, :LANES], zero], axis=1),
                                jnp.concatenate([zero, x[:, LANES:]], axis=1)], axis=0)

    ke_t = [k_e[:, s].T for s in ps]
    be_t = [b_e[:, s].T for s in ps]
    p_end = [e_pos[:, s].T[:, L - 1:L] for s in ps]

    lhs = [jnp.concatenate([a_t[:, s], r_t[:, s]], axis=0) for s in ps]
    rhs = [jnp.concatenate([k_t[:, s], b_t[:, s]], axis=0) for s in ps]
    zero_b = jnp.zeros_like(lhs[0])
    aa = [[lax.dot_general(jnp.where(first, x, zero_b) if j == 0 else jnp.where(first, zero_b, x), y, _NT,
                           preferred_element_type=F32) for j in range(2)] for x, y in zip(lhs, rhs)]
    cat2 = lambda f: [jnp.concatenate([f(pair[0]), f(pair[1])], axis=1) for pair in aa]
    a_ak = cat2(lambda x: jnp.where(strict, x[:L, :L], 0.0))
    a_ab = cat2(lambda x: jnp.where(strict, x[:L, L:], 0.0))
    a_rk = cat2(lambda x: jnp.where(incl, x[L:, :L], 0.0))
    a_rb = cat2(lambda x: jnp.where(incl, x[L:, L:], 0.0))

    eye2 = jnp.concatenate([eye, eye], axis=1)
    tinv = [eye2 + x for x in a_ab]
    pw = [_mm(x, diag_blocks(x)) for x in a_ab]
    span = 2
    while 2 * span < L:
        both = [_mm(jnp.concatenate([t, p], axis=0), diag_blocks(p)) for t, p in zip(tinv, pw)]
        tinv = [t + x[:L] for t, x in zip(tinv, both)]
        pw = [x[L:] for x in both]
        span *= 2
    tinv = [t + _mm(t, diag_blocks(p)) for t, p in zip(tinv, pw)]

    v_d = [diag_rows(v_b[:, s]) for s in ps]
    av = [_mm(jnp.concatenate([a_ak[m], a_rk[m]], axis=0), v_d[m]) for m in P]
    tw = [_mm(tinv[m], jnp.concatenate([diag_rows(a_t[:, ps[m]]), diag_rows(av[m][:L].astype(BF16))], axis=1))
          for m in P]
    kv_loc = [jnp.where(same_head, _mm(ke_t[m], v_b[:, ps[m]]), 0.0) for m in P]

    ws = [_mm(jnp.concatenate([tw[m][:, :LANES].astype(BF16), r_t[:, ps[m]]], axis=0), st[m]) for m in P]
    u = [ws[m][:L] + tw[m][:, LANES:] for m in P]
    y = [ws[m][L:] + av[m][L:] + _mm(a_rb[m], diag_rows(u[m])) for m in P]
    st_new = [p_end[m] * st[m] + kv_loc[m] + jnp.where(same_head, _mm(be_t[m], u[m]), 0.0) for m in P]

    y = jnp.concatenate(y, axis=1)
    d = y - hsum(y) * (1.0 / N)
    yn = d * lax.rsqrt(hsum(d * d) * (1.0 / N) + GN_EPS) * lng_ref[...] + lnb_ref[...]
    bonus = hsum(r * k2 * rk_ref[...])
    yield (yn + bonus * v) * gate, st_new


def _rwkv_kernel(x_ref, xp_ref, *refs):
    *param_refs, o_ref, st_ref = refs
    c = pl.program_id(1)

    @pl.when(c == 0)
    def _():
        st_ref[...] = jnp.zeros(st_ref.shape, F32)

    chunks = []
    for i in range(x_ref.shape[0]):
        prev = jnp.where(c > 0, xp_ref[i, xp_ref.shape[1] - 1:, :], 0.0)
        st = [st_ref[i, m] for m in range(st_ref.shape[1])]
        chunks.append(_rwkv_chunk(x_ref[i], prev, st, *param_refs))
    for chunk in chunks:
        next(chunk)
    outs, states = [], []
    for chunk in chunks:
        out, st_new = next(chunk)
        outs.append(out.astype(o_ref.dtype))
        states.append(jnp.stack(st_new))
    o_ref[...] = jnp.stack(outs)
    st_ref[...] = jnp.stack(states)


def _rwkv(rw, vecs, w2, a2, g2, batch, seq):
    width = rw.shape[1]
    L = min(CHUNK, seq)
    nc = seq // L
    nb = RWKV_SEQS if batch % RWKV_SEQS == 0 else 1
    sub = 8
    rw3 = rw.reshape(batch, seq, width)
    vec = lambda wd: pl.BlockSpec((1, wd), lambda b, c: (0, 0))
    mat = lambda m: pl.BlockSpec(m.shape, lambda b, c: (0, 0))
    mu, w0, a0, kk, ka, rk, lng, lnb = vecs
    head = np.arange(LANES) // HEAD_DIM
    ones = jnp.asarray(head[:, None] == head[None, :], BF16)
    out = pl.pallas_call(
        _rwkv_kernel,
        grid=(batch // nb, nc),
        in_specs=[pl.BlockSpec((nb, L, width), lambda b, c: (b, c, 0)),
                  pl.BlockSpec((nb, sub, width), lambda b, c: (b, jnp.maximum(c * (L // sub) - 1, 0), 0)),
                  vec(width), vec(RWKV_WIDTH), mat(w2), vec(RWKV_WIDTH), mat(a2), mat(g2),
                  vec(RWKV_WIDTH), vec(RWKV_WIDTH), vec(RWKV_WIDTH), vec(RWKV_WIDTH), vec(RWKV_WIDTH), mat(ones)],
        out_specs=pl.BlockSpec((nb, L, RWKV_WIDTH), lambda b, c: (b, c, 0)),
        out_shape=jax.ShapeDtypeStruct((batch, seq, RWKV_WIDTH), BF16),
        scratch_shapes=[pltpu.VMEM((nb, RWKV_WIDTH // LANES, LANES, LANES), F32)],
        compiler_params=_params("parallel", "arbitrary"),
        name="rwkv7",
    )(rw3, rw3, mu, w0, w2, a0, a2, g2, kk, ka, rk, lng, lnb, ones)
    return out.reshape(batch * seq, RWKV_WIDTH)


def _merge_kernel(x_ref, oa_ref, ob_ref, gab_ref, wpa_ref, wpb_ref, wo_ref, o_ref):
    d = x_ref.shape[1]
    pa = jnp.dot(oa_ref[...], wpa_ref[...], preferred_element_type=F32)
    pb = jnp.dot(ob_ref[...], wpb_ref[...], preferred_element_type=F32)
    gab = gab_ref[...].astype(F32)
    merged = jax.nn.sigmoid(gab[:, :d]) * pa + jax.nn.sigmoid(gab[:, d:]) * pb
    o_ref[...] = x_ref[...] + _mm(merged, wo_ref[...])


def _merge(x2, oa, ob, gab, wpa, wpb, wo, tm):
    n, d = x2.shape
    row = lambda wd: pl.BlockSpec((tm, wd), lambda i: (i, 0))
    mat = lambda m: pl.BlockSpec(m.shape, lambda i: (0, 0))
    return pl.pallas_call(
        _merge_kernel,
        grid=(n // tm,),
        in_specs=[row(d), row(oa.shape[1]), row(ob.shape[1]), row(gab.shape[1]), mat(wpa), mat(wpb), mat(wo)],
        out_specs=row(d),
        out_shape=jax.ShapeDtypeStruct((n, d), F32),
        compiler_params=_params("parallel"),
        name="merge",
    )(x2, oa, ob, gab, wpa, wpb, wo)


def _ffn_kernel(seq, x_ref, xh_ref, g_ref, wv_ref, wg_ref, cwv_ref, cwg_ref, cbv_ref, cbg_ref, wd_ref, o_ref,
                h_ref, uv_ref, ug_ref):
    tm = x_ref.shape[0]
    i = pl.program_id(0)
    f = pl.program_id(1)

    @pl.when(f == 0)
    def _():
        first = lax.rem(i * tm, seq) == 0
        halo = jnp.where(first, 0.0, _rms(xh_ref[...], g_ref[...]))
        h_ref[0:HALO, :] = halo.astype(BF16)
        h_ref[HALO:, :] = _rms(x_ref[...], g_ref[...]).astype(BF16)

    h = h_ref[...]
    uv_ref[...] = jnp.dot(h, wv_ref[...], preferred_element_type=F32)
    ug_ref[...] = jnp.dot(h, wg_ref[...], preferred_element_type=F32)

    def conv(u_ref, cw_ref, cb_ref):
        acc = cb_ref[...] + cw_ref[0:1, :] * u_ref[pl.ds(HALO - 2, tm), :]
        acc = acc + cw_ref[1:2, :] * u_ref[pl.ds(HALO - 1, tm), :]
        return acc + cw_ref[2:3, :] * u_ref[pl.ds(HALO, tm), :]

    val = conv(uv_ref, cwv_ref, cbv_ref)
    gt = conv(ug_ref, cwg_ref, cbg_ref)
    y = _mm(gt * jax.nn.sigmoid(gt) * val, wd_ref[...])

    @pl.when(f == 0)
    def _():
        o_ref[...] = x_ref[...] + y

    @pl.when(f > 0)
    def _():
        o_ref[...] = o_ref[...] + y


def _ffn(x1, g, w_up, conv_w, conv_b, w_down, seq, tm, tf):
    n, d = x1.shape
    dff = w_down.shape[0]
    nf = dff // tf
    return pl.pallas_call(
        functools.partial(_ffn_kernel, seq),
        grid=(n // tm, nf),
        in_specs=[pl.BlockSpec((tm, d), lambda i, f: (i, 0)),
                  pl.BlockSpec((HALO, d), lambda i, f: (jnp.maximum(i * (tm // HALO) - 1, 0), 0)),
                  pl.BlockSpec((1, d), lambda i, f: (0, 0)),
                  pl.BlockSpec((d, tf), lambda i, f: (0, f)),
                  pl.BlockSpec((d, tf), lambda i, f: (0, nf + f)),
                  pl.BlockSpec((CONV_WIDTH, tf), lambda i, f: (0, f)),
                  pl.BlockSpec((CONV_WIDTH, tf), lambda i, f: (0, nf + f)),
                  pl.BlockSpec((1, tf), lambda i, f: (0, f)),
                  pl.BlockSpec((1, tf), lambda i, f: (0, nf + f)),
                  pl.BlockSpec((tf, d), lambda i, f: (f, 0))],
        out_specs=pl.BlockSpec((tm, d), lambda i, f: (i, 0)),
        out_shape=jax.ShapeDtypeStruct((n, d), F32),
        scratch_shapes=[pltpu.VMEM((tm + HALO, d), BF16),
                        pltpu.VMEM((tm + HALO, tf), F32),
                        pltpu.VMEM((tm + HALO, tf), F32)],
        compiler_params=_params("parallel", "arbitrary"),
        name="convffn",
    )(x1, x1, g, w_up, w_up, conv_w, conv_w, conv_b, conv_b, w_down)


def _t5_bucket(dist):
    n = np.maximum(dist, 0)
    max_exact = REL_BUCKETS // 2
    ratio = np.log(np.maximum(n, 1).astype(np.float32) / max_exact) / math.log(REL_MAX_DIST / max_exact)
    large = np.minimum(max_exact + (ratio * (REL_BUCKETS - max_exact)).astype(np.int32), REL_BUCKETS - 1)
    return np.where(n < max_exact, n, large).astype(np.int32)


def _bias_tables(rel_bias, seq, ncmp):
    tq, tk = Q_TILE, K_TILE
    G, R = NSA_KV_GROUPS, NSA_GROUP
    nq = seq // tq
    step = tq // CMP_STRIDE
    tab = rel_bias.astype(F32)

    off = (nq - 1) * step
    d_c = np.arange(tq)[None, :] - (np.arange(ncmp + off)[:, None] - off) * CMP_STRIDE - (CMP_BLOCK - 1)
    n_b = -(-REL_MAX_DIST // tk)
    d0 = np.arange(tq)[None, :] - np.arange(tk)[:, None]
    d_n = np.stack([d0 - o * tk for o in range(-n_b, tq // tk)])

    buckets = np.concatenate([_t5_bucket(d_c).reshape(-1), _t5_bucket(d_n).reshape(-1)])
    onehot = (jnp.asarray(buckets)[:, None] == jnp.arange(REL_BUCKETS, dtype=jnp.int32)[None, :]).astype(F32)
    vals = jnp.dot(onehot, tab * LOG2E, precision=lax.Precision.HIGHEST, preferred_element_type=F32).T
    by_group = lambda x: x.reshape(G, R, -1, tq).transpose(0, 2, 1, 3).reshape(G, -1, R * tq)
    tile_r = lambda m: np.tile(m, (1, R))

    base = jnp.where(jnp.asarray(tile_r(d_c >= 0)), by_group(vals[:, :d_c.size]), NEG_INF)
    bias_c = jnp.stack([base[:, off - i * step:off - i * step + ncmp] for i in range(nq)])

    far = jnp.repeat(tab[REL_BUCKETS - 1] * LOG2E, tq).reshape(G, 1, R * tq)
    near = by_group(vals[:, d_c.size:]) - far
    near = jnp.where(jnp.asarray(tile_r(d_n.reshape(-1, tq) >= 0)), near, NEG_INF)
    bias_near = near.reshape(G, d_n.shape[0], tk, R * tq).transpose(1, 0, 2, 3)
    win = np.stack([tile_r(np.where(d0 - o * tk < WINDOW, 0.0, NEG_INF)) for o in _win_masked(tq, tk)])
    return bias_c, bias_near, jnp.asarray(win.astype(np.float32))


def _overlap_matrix(ncmp_pad, n_slc, n_cmp):
    m = np.zeros((ncmp_pad, n_slc), np.float32)
    ratio = SLC_BLOCK // CMP_STRIDE
    for j in range(n_slc):
        for o, wgt in enumerate(OVERLAP_W):
            cidx = ratio * j + o - (CMP_BLOCK // CMP_STRIDE - 1)
            if 0 <= cidx < n_cmp:
                m[cidx, j] += wgt
    return jnp.asarray(m.T)


def _layer(x, attn_norm_g, w_in, rel_bias, q_norm_g, k_norm_g, cmp_pe_k, cmp_w1_k, cmp_w2_k,
           cmp_pe_v, cmp_w1_v, cmp_w2_v, rwkv_mu, rwkv_w0, rwkv_w2, rwkv_a0, rwkv_a2, rwkv_g2,
           rwkv_k_k, rwkv_k_a, rwkv_r_k, rwkv_ln_g, rwkv_ln_b, w_proj_a, w_proj_b, w_out,
           ffn_norm_g, w_up, conv_w, conv_b, w_down):
    batch, seq, d = x.shape
    n = batch * seq
    assert seq % Q_TILE == 0 and Q_TILE % K_TILE == 0 and WINDOW % K_TILE == 0
    assert seq % CMP_STRIDE == 0
    assert seq // SLC_BLOCK <= LANES - HEAD_DIM
    x2 = x.reshape(n, d)
    row = lambda a: a.reshape(1, -1).astype(F32)

    kvw = 6 * NSA_KV_WIDTH
    o_q, o_kv, o_gn = 0, NSA_WIDTH, NSA_WIDTH + kvw
    o_rw = o_gn + 3 * NSA_HEADS
    o_gab = o_rw + RWKV_IN_WIDTH
    gn_pad = jnp.zeros((d, LANES - 3 * NSA_HEADS), w_in.dtype)
    w_cat = jnp.concatenate([w_in[:, o_rw:o_gab], w_in[:, o_q:o_kv], w_in[:, o_kv:o_gn], w_in[:, o_gab:],
                             w_in[:, o_gn:o_rw], gn_pad], axis=1).astype(BF16)
    widths = (RWKV_IN_WIDTH, NSA_WIDTH, 2 * NSA_KV_WIDTH, 4 * NSA_KV_WIDTH, 2 * d, LANES)
    dtypes = (F32, BF16, F32, BF16, BF16, F32)
    rw, q, kv_cmp, kv, gab, gn = _inproj(x2, row(attn_norm_g), w_cat, widths, dtypes, tm=256)

    gk = row(k_norm_g)
    ksx, kwn, vst, vwt = _kvprep(kv, jnp.concatenate([gk, gk], axis=1), seq, tm=512)

    nchunk = seq // CMP_STRIDE
    n_cmp = nchunk - CMP_BLOCK // CMP_STRIDE + 1
    pe = jnp.stack([cmp_pe_k, cmp_pe_v]).astype(F32)
    w1 = jnp.stack([cmp_w1_k, cmp_w1_v]).astype(BF16).reshape(2, CMP_BLOCK, HEAD_DIM, CMP_HIDDEN)
    w2 = jnp.stack([cmp_w2_k, cmp_w2_v]).astype(BF16)
    kc, vc = _compress(kv_cmp, pe, w1, w2, gk, batch, seq)

    bias_c, bias_near, win_mask = _bias_tables(rel_bias, seq, nchunk)
    wov = _overlap_matrix(nchunk, seq // SLC_BLOCK, n_cmp)
    o_a = _attention(q, gn, kc, jnp.swapaxes(vc, -1, -2), ksx, vst, kwn, vwt, bias_c, bias_near, win_mask,
                     q_norm_g.reshape(-1, 1).astype(F32), wov, batch, seq)

    vecs = tuple(row(a) for a in (rwkv_mu, rwkv_w0, rwkv_a0, rwkv_k_k, rwkv_k_a, rwkv_r_k, rwkv_ln_g, rwkv_ln_b))
    o_b = _rwkv(rw, vecs, rwkv_w2.astype(BF16), rwkv_a2.astype(BF16), rwkv_g2.astype(BF16), batch, seq)

    x1 = _merge(x2, o_a, o_b, gab, w_proj_a.astype(BF16), w_proj_b.astype(BF16), w_out.astype(BF16), tm=512)
    dff = w_down.shape[0]
    out = _ffn(x1, row(ffn_norm_g), w_up.astype(BF16), conv_w.astype(F32), row(conv_b), w_down.astype(BF16),
               seq, tm=512, tf=dff // 2)
    return out.reshape(batch, seq, d)


def kernel(x, attn_norm_g, w_in, rel_bias, q_norm_g, k_norm_g, cmp_pe_k, cmp_w1_k, cmp_w2_k, cmp_pe_v, cmp_w1_v,
           cmp_w2_v, rwkv_mu, rwkv_w0, rwkv_w2, rwkv_a0, rwkv_a2, rwkv_g2, rwkv_k_k, rwkv_k_a, rwkv_r_k,
           rwkv_ln_g, rwkv_ln_b, w_proj_a, w_proj_b, w_out, ffn_norm_g, w_up, conv_w, conv_b, w_down):
    per_layer = (attn_norm_g, w_in, None, q_norm_g, k_norm_g, cmp_pe_k, cmp_w1_k, cmp_w2_k, cmp_pe_v, cmp_w1_v,
                 cmp_w2_v, rwkv_mu, rwkv_w0, rwkv_w2, rwkv_a0, rwkv_a2, rwkv_g2, rwkv_k_k, rwkv_k_a, rwkv_r_k,
                 rwkv_ln_g, rwkv_ln_b, w_proj_a, w_proj_b, w_out, ffn_norm_g, w_up, conv_w, conv_b, w_down)
    for l in range(attn_norm_g.shape[0]):
        args = [rel_bias if p is None else p[l] for p in per_layer]
        x = _layer(x, *args)
    return x
```

```python
import functools
import math

import numpy as np
import jax
import jax.numpy as jnp
from jax import lax
from jax.experimental import pallas as pl
from jax.experimental.pallas import tpu as pltpu

F32 = jnp.float32
BF16 = jnp.bfloat16

HEAD_DIM = 64
NSA_HEADS = 8
NSA_KV_GROUPS = 2
NSA_GROUP = NSA_HEADS // NSA_KV_GROUPS
NSA_WIDTH = NSA_HEADS * HEAD_DIM
NSA_KV_WIDTH = NSA_KV_GROUPS * HEAD_DIM
CMP_BLOCK = 32
CMP_STRIDE = 16
CMP_HIDDEN = 256
SLC_BLOCK = 64
SLC_TOPK = 16
OVERLAP_W = (1, 2, 2, 2, 1)
WINDOW = 512
REL_BUCKETS = 32
REL_MAX_DIST = 128
RWKV_HEADS = 8
RWKV_WIDTH = RWKV_HEADS * HEAD_DIM
LORA_W = 64
LORA_A = 64
LORA_G = 128
RWKV_IN_WIDTH = 3 * RWKV_WIDTH + LORA_W + LORA_A + LORA_G
GN_EPS = 64e-5
CONV_WIDTH = 3
RMS_EPS = 1e-6
NEG_INF = -1e30
FORCE = 1e9
LOG2E = math.log2(math.e)
M_FLOOR = -1e20

LANES = 128
VMEM_LIMIT = 56 * 1024 * 1024

Q_TILE = 256
K_TILE = 256
VT_ROWS = 80
CHUNK = 128
RWKV_SEQS = 4
HALO = 16

_NT = (((1,), (1,)), ((), ()))


def _params(*sem):
    return pltpu.CompilerParams(dimension_semantics=sem, vmem_limit_bytes=VMEM_LIMIT)


def _mm(a, b):
    return jnp.dot(a.astype(BF16), b.astype(BF16), preferred_element_type=F32)


def _mm_nt(a, b):
    return lax.dot_general(a.astype(BF16), b.astype(BF16), _NT, preferred_element_type=F32)


def _mm_f32(a, b):
    return jnp.dot(a, b, preferred_element_type=F32, precision=lax.Precision.HIGHEST)


def _rms(x, g):
    return x * lax.rsqrt(jnp.mean(x * x, axis=-1, keepdims=True) + RMS_EPS) * g


def _inproj_kernel(x_ref, g_ref, w_ref, *out_refs):
    h = _rms(x_ref[...], g_ref[...]).astype(BF16)
    off = 0
    for ref in out_refs:
        n = ref.shape[-1]
        ref[...] = jnp.dot(h, w_ref[:, off:off + n], preferred_element_type=F32).astype(ref.dtype)
        off += n


def _inproj(x2, g, w, widths, dtypes, tm):
    n, d = x2.shape
    return pl.pallas_call(
        _inproj_kernel,
        grid=(n // tm,),
        in_specs=[pl.BlockSpec((tm, d), lambda i: (i, 0)),
                  pl.BlockSpec((1, d), lambda i: (0, 0)),
                  pl.BlockSpec(w.shape, lambda i: (0, 0))],
        out_specs=[pl.BlockSpec((tm, wd), lambda i: (i, 0)) for wd in widths],
        out_shape=[jax.ShapeDtypeStruct((n, wd), dt) for wd, dt in zip(widths, dtypes)],
        compiler_params=_params("parallel"),
        name="inproj",
    )(x2, g, w)


def _kvprep_kernel(seq, kv_ref, gk_ref, ksx_ref, kwn_ref, vs_ref, vw_ref):
    tm = kv_ref.shape[0]
    kv = kv_ref[...].astype(F32)
    lane = lax.broadcasted_iota(jnp.int32, (tm, LANES), 1)
    lo = lane < HEAD_DIM

    def norm2(x):
        x2 = x * x
        s_lo = jnp.sum(jnp.where(lo, x2, 0.0), axis=-1, keepdims=True)
        s_hi = jnp.sum(jnp.where(lo, 0.0, x2), axis=-1, keepdims=True)
        ms = jnp.where(lo, s_lo, s_hi) * (1.0 / HEAD_DIM)
        return x * lax.rsqrt(ms + RMS_EPS) * gk_ref[...]

    ks = norm2(kv[:, 0:LANES])
    kw = norm2(kv[:, 2 * LANES:3 * LANES])
    row = lax.rem(pl.program_id(0) * tm, seq) + lax.broadcasted_iota(jnp.int32, (tm, LANES), 0)
    onehot = jnp.where(lane - HEAD_DIM == row // SLC_BLOCK, 1.0, 0.0)
    ksx_ref[0] = jnp.where(lo, ks, onehot).astype(BF16)
    ksx_ref[1] = jnp.where(lo, pltpu.roll(ks, HEAD_DIM, axis=1), onehot).astype(BF16)
    flag_col = jnp.where(lane == HEAD_DIM, 1.0, 0.0)
    kwn_ref[0] = jnp.where(lo, kw, flag_col).astype(BF16)
    kwn_ref[1] = jnp.where(lo, pltpu.roll(kw, HEAD_DIM, axis=1), flag_col).astype(BF16)
    tail = jnp.where(lax.broadcasted_iota(jnp.int32, (VT_ROWS - HEAD_DIM, K_TILE), 0) == 0, 1.0, 0.0)
    for col, ref in ((1, vs_ref), (3, vw_ref)):
        for t in range(tm // K_TILE):
            vt = kv[t * K_TILE:(t + 1) * K_TILE, col * LANES:(col + 1) * LANES].T
            for g in range(NSA_KV_GROUPS):
                ref[g, t] = jnp.concatenate([vt[g * HEAD_DIM:(g + 1) * HEAD_DIM], tail], axis=0).astype(BF16)


def _kvprep(kv, gk2, seq, tm):
    n = kv.shape[0]
    vt_spec = lambda: pl.BlockSpec((2, tm // K_TILE, VT_ROWS, K_TILE), lambda i: (0, i, 0, 0))
    vt_shape = jax.ShapeDtypeStruct((2, n // K_TILE, VT_ROWS, K_TILE), BF16)
    return pl.pallas_call(
        functools.partial(_kvprep_kernel, seq),
        grid=(n // tm,),
        in_specs=[pl.BlockSpec((tm, kv.shape[1]), lambda i: (i, 0)),
                  pl.BlockSpec((1, LANES), lambda i: (0, 0))],
        out_specs=[pl.BlockSpec((2, tm, LANES), lambda i: (0, i, 0)),
                   pl.BlockSpec((2, tm, LANES), lambda i: (0, i, 0)), vt_spec(), vt_spec()],
        out_shape=[jax.ShapeDtypeStruct((2, n, LANES), BF16), jax.ShapeDtypeStruct((2, n, LANES), BF16),
                   vt_shape, vt_shape],
        compiler_params=_params("parallel"),
        name="kvprep",
    )(kv, gk2)


def _compress_kernel(k_ref, v_ref, pe_ref, w1_ref, w2_ref, gk_ref, kc_ref, vc_ref):
    nchunk = k_ref.shape[0] // CMP_STRIDE
    span = CMP_BLOCK // CMP_STRIDE
    G = NSA_KV_GROUPS
    acc = [[[jnp.zeros((nchunk, CMP_HIDDEN), F32) for _ in range(span)] for _ in range(G)] for _ in range(2)]
    for p in range(CMP_STRIDE):
        for s, ref in enumerate((k_ref, v_ref)):
            x = ref[pl.ds(p, nchunk, stride=CMP_STRIDE), :]
            for g in range(G):
                xs = x[:, g * HEAD_DIM:(g + 1) * HEAD_DIM]
                for half in range(span):
                    q = half * CMP_STRIDE + p
                    acc[s][g][half] = acc[s][g][half] + _mm(xs + pe_ref[s, q:q + 1, :], w1_ref[s, q])
    for s, out_ref in ((0, kc_ref), (1, vc_ref)):
        for g in range(G):
            hid = acc[s][g][0]
            for half in range(1, span):
                hid = hid + pltpu.roll(acc[s][g][half], nchunk - half, axis=0)
            out = _mm(jax.nn.gelu(hid), w2_ref[s])
            out_ref[0, g] = _rms(out, gk_ref[...]) if s == 0 else out


def _compress(kv, pe, w1, w2, gk, batch, seq):
    nchunk = seq // CMP_STRIDE
    out_spec = lambda: pl.BlockSpec((1, NSA_KV_GROUPS, nchunk, HEAD_DIM), lambda b: (b, 0, 0, 0))
    out_shape = jax.ShapeDtypeStruct((batch, NSA_KV_GROUPS, nchunk, HEAD_DIM), F32)
    const = lambda a: pl.BlockSpec(a.shape, lambda b: (0,) * a.ndim)
    return pl.pallas_call(
        _compress_kernel,
        grid=(batch,),
        in_specs=[pl.BlockSpec((seq, NSA_KV_WIDTH), lambda b: (b, 0)), pl.BlockSpec((seq, NSA_KV_WIDTH), lambda b: (b, 1)),
                  const(pe), const(w1), const(w2), const(gk)],
        out_specs=[out_spec(), out_spec()],
        out_shape=[out_shape, out_shape],
        compiler_params=_params("parallel"),
        name="compress",
    )(kv, kv, pe, w1, w2, gk)


def _rank_select(score_t, cur_t):
    nb = score_t.shape[0]
    jj = lax.broadcasted_iota(jnp.int32, score_t.shape, 0)
    rank = jnp.zeros(score_t.shape, F32)
    for j in range(nb):
        row = score_t[j:j + 1, :]
        gt = jnp.where(row > score_t, 1.0, 0.0)
        ge = jnp.where(row >= score_t, 1.0, 0.0)
        rank = rank + jnp.where(jj > j, ge, gt)
    keep = jnp.where(rank < float(min(SLC_TOPK, nb)), 1.0, 0.0) * jnp.where(jj <= cur_t, 1.0, 0.0)
    return jnp.where(keep > 0.5, 0.0, NEG_INF)


def _win_masked(tq, tk):
    return [o for o in range(-(WINDOW // tk), 0) if tq - 1 - o * tk >= WINDOW]


def _scores(ks, ws):
    return [jnp.dot(k, w, preferred_element_type=F32) for k, w in zip(ks, ws)]


def _softmax_update(m_ref, acc_ref, slots, s, vts):
    m_old = [m_ref[i] for i in slots]
    ncol = s[0].shape[1] // LANES
    m_cols = [[] for _ in slots]
    p_cols = [[] for _ in slots]
    for c in range(ncol):
        cols = slice(c * LANES, (c + 1) * LANES)
        for n, x in enumerate(s):
            xc = x[:, cols]
            mc = jnp.maximum(m_old[n][:, cols], jnp.max(xc, axis=0, keepdims=True))
            m_cols[n].append(mc)
            p_cols[n].append(jnp.exp2(xc - mc).astype(BF16))
    m_new = [jnp.concatenate(mc, axis=1) for mc in m_cols]
    pv = [jnp.dot(vt, jnp.concatenate(pc, axis=1), preferred_element_type=F32) for vt, pc in zip(vts, p_cols)]
    acc = [jnp.exp2(mo - mn) * acc_ref[i] + z for mo, mn, i, z in zip(m_old, m_new, slots, pv)]
    for i, mn, ac in zip(slots, m_new, acc):
        m_ref[i] = mn
        acc_ref[i] = ac


def _attn_kernel(q_ref, gn_ref, kc_ref, vct_ref, ksx_ref, vst_ref, kwn_ref, vwt_ref, bc_ref, bn_ref, wm_ref,
                 gq_ref, wov_ref, o_ref, m_ref, acc_ref, sa_ref, sb_ref):
    tq = q_ref.shape[0]
    tk = K_TILE
    nblk = wov_ref.shape[0]
    G, R = NSA_KV_GROUPS, NSA_GROUP
    qt = pl.program_id(1)
    n_q = tq // tk
    n_b = -(-REL_MAX_DIST // tk)
    j0 = qt * n_q
    qT = q_ref[...].astype(F32).T
    gates = jax.nn.sigmoid(gn_ref[...].T)
    gq = gq_ref[...]

    qn, qx, negs, o_cmp = [], [], [], []
    for g in range(G):
        xs = []
        for h in range(g * R, (g + 1) * R):
            x = qT[h * HEAD_DIM:(h + 1) * HEAD_DIM]
            x = x * lax.rsqrt(jnp.mean(x * x, axis=0, keepdims=True) + RMS_EPS) * gq * (HEAD_DIM ** -0.5 * LOG2E)
            xs.append(x.astype(BF16))
        qn.append(jnp.concatenate(xs, axis=1))

    for g in range(G):
        s = jnp.dot(kc_ref[0, g].astype(BF16), qn[g], preferred_element_type=F32)
        p_cols = []
        for c in range(R * tq // LANES):
            cols = slice(c * LANES, (c + 1) * LANES)
            sc = s[:, cols] + bc_ref[0, g, :, cols]
            e = jnp.exp2(sc - jnp.maximum(jnp.max(sc, axis=0, keepdims=True), M_FLOOR))
            p_cols.append(e * (1.0 / jnp.maximum(jnp.sum(e, axis=0, keepdims=True), 1e-30)))
        o_cmp.append(jnp.dot(vct_ref[0, g].astype(BF16), jnp.concatenate(p_cols, axis=1).astype(BF16),
                             preferred_element_type=F32))
        per_head = tq // LANES
        psum = jnp.concatenate([sum(p_cols[r * per_head + t] for r in range(R)) for t in range(per_head)], axis=1)
        imp = _mm_f32(wov_ref[...], psum)

        blk = lax.broadcasted_iota(jnp.int32, imp.shape, 0)
        cur = (qt * tq + lax.broadcasted_iota(jnp.int32, imp.shape, 1)) // SLC_BLOCK
        forced = (blk == 0) | (blk == cur) | (blk == cur - 1)
        score = jnp.where(forced, FORCE, jnp.where(blk <= cur, imp, -FORCE))
        neg = _rank_select(score, cur)
        if nblk < LANES - HEAD_DIM:
            neg = jnp.concatenate([neg, jnp.zeros((LANES - HEAD_DIM - nblk, tq), F32)], axis=0)
        neg = neg.astype(BF16)
        negs.append(neg)
        qx.append(jnp.concatenate([qn[g], jnp.concatenate([neg] * R, axis=1)], axis=0))

    m_ref[...] = jnp.full(m_ref.shape, M_FLOOR, F32)
    acc_ref[...] = jnp.zeros(acc_ref.shape, F32)

    def invalid_before(level):
        return jnp.where(qt >= level, 0.0, NEG_INF)

    def slc_queries(g, level):
        if level == 0:
            return qx[g]
        rows = jnp.minimum(negs[g], invalid_before(level).astype(BF16))
        return jnp.concatenate([qn[g], jnp.concatenate([rows] * R, axis=1)], axis=0)

    def win_queries(g, level):
        flag_row = lax.broadcasted_iota(jnp.int32, (LANES - HEAD_DIM, R * tq), 0) == 0
        extra = jnp.where(flag_row, invalid_before(level), 0.0).astype(BF16)
        return jnp.concatenate([qn[g], extra], axis=0)

    def slc_probs(o, bias):
        j = jnp.maximum(j0 + o, 0)
        sl = pl.ds(pl.multiple_of(j * tk, tk), tk)
        return [(g, ksx_ref[g, sl, :], vst_ref[g, j], slc_queries(g, max(-(o // n_q), 0)), bias.at[g])
                for g in range(G)]

    def win_probs(o, bias, per_group):
        j = jnp.maximum(j0 + o, 0)
        sl = pl.ds(pl.multiple_of(j * tk, tk), tk)
        pick = lambda g: None if bias is None else (bias.at[g] if per_group else bias)
        return [(G + g, kwn_ref[g, sl, :], vwt_ref[g, j], win_queries(g, max(-(o // n_q), 0)), pick(g))
                for g in range(G)]

    n_far = jnp.maximum(j0 - n_b, 0)
    groups = list(range(G))

    def far_scores(j, buf, gs=groups):
        sl = pl.ds(pl.multiple_of(j * tk, tk), tk)
        for g, x in zip(gs, _scores([ksx_ref[g, sl, :] for g in gs], [qx[g] for g in gs])):
            buf[g] = x

    def far_update(j, buf, gs=groups):
        _softmax_update(m_ref, acc_ref, gs, [buf[g] for g in gs], [vst_ref[g, j] for g in gs])

    far_scores(0, sa_ref)

    def far_body(i, carry):
        for g in groups:
            far_scores(2 * i + 1, sb_ref, [g])
            far_update(2 * i, sa_ref, [g])
        for g in groups:
            far_scores(2 * i + 2, sa_ref, [g])
            far_update(2 * i + 1, sb_ref, [g])
        return carry

    lax.fori_loop(0, n_far // 2, far_body, 0)

    @pl.when(n_far % 2 == 1)
    def _():
        far_update(n_far - 1, sa_ref)

    masked = _win_masked(tq, tk)
    rounds = []
    for o in range(-(WINDOW // tk), n_q):
        if o >= -n_b:
            rounds.append(slc_probs(o, bn_ref.at[o + n_b]) + win_probs(o, bn_ref.at[o + n_b], True))
        elif o in masked:
            rounds.append(win_probs(o, wm_ref.at[masked.index(o)], False))
        else:
            rounds.append(win_probs(o, None, False))

    def round_scores(probs):
        sc = _scores([p[1] for p in probs], [p[3] for p in probs])
        return [x if p[4] is None else x + p[4][...] for x, p in zip(sc, probs)]

    sc = round_scores(rounds[0])
    for i, probs in enumerate(rounds):
        sc_next = round_scores(rounds[i + 1]) if i + 1 < len(rounds) else None
        _softmax_update(m_ref, acc_ref, [p[0] for p in probs], sc, [p[2] for p in probs])
        sc = sc_next

    outs = []
    for g in range(G):
        acc = acc_ref[g]
        o_slc = acc[:HEAD_DIM] / acc[HEAD_DIM:HEAD_DIM + 1]
        acc = acc_ref[G + g]
        o_win = acc[:HEAD_DIM] / acc[HEAD_DIM:HEAD_DIM + 1]
        for r in range(R):
            h = g * R + r
            cols = slice(r * tq, (r + 1) * tq)
            outs.append(gates[3 * h:3 * h + 1] * o_cmp[g][:, cols] + gates[3 * h + 1:3 * h + 2] * o_slc[:, cols]
                        + gates[3 * h + 2:3 * h + 3] * o_win[:, cols])
    o_ref[...] = jnp.concatenate(outs, axis=0).T.astype(o_ref.dtype)


def _attention(q, gn, kc, vct, ksx, vst, kwn, vwt, bias_c, bias_near, win_mask, gq, wov, batch, seq):
    n = q.shape[0]
    tq = Q_TILE
    nq = seq // tq
    ncmp = kc.shape[2]
    row = lambda b, i: (b * nq + i, 0)
    whole = lambda b, i: (0, b, 0)
    tiles = lambda b, i: (0, b, 0, 0)
    const = lambda a: pl.BlockSpec(a.shape, lambda b, i: (0,) * a.ndim, pipeline_mode=pl.Buffered(1))
    return pl.pallas_call(
        _attn_kernel,
        grid=(batch, nq),
        in_specs=[pl.BlockSpec((tq, NSA_WIDTH), row),
                  pl.BlockSpec((tq, LANES), row),
                  pl.BlockSpec((1, NSA_KV_GROUPS, ncmp, HEAD_DIM), lambda b, i: (b, 0, 0, 0)),
                  pl.BlockSpec((1, NSA_KV_GROUPS, HEAD_DIM, ncmp), lambda b, i: (b, 0, 0, 0)),
                  pl.BlockSpec((2, seq, LANES), whole),
                  pl.BlockSpec((2, seq // K_TILE, VT_ROWS, K_TILE), tiles),
                  pl.BlockSpec((2, seq, LANES), whole),
                  pl.BlockSpec((2, seq // K_TILE, VT_ROWS, K_TILE), tiles),
                  pl.BlockSpec((1, NSA_KV_GROUPS, ncmp, NSA_GROUP * tq), lambda b, i: (i, 0, 0, 0)),
                  const(bias_near), const(win_mask), const(gq), const(wov)],
        out_specs=pl.BlockSpec((tq, NSA_WIDTH), row),
        out_shape=jax.ShapeDtypeStruct((n, NSA_WIDTH), BF16),
        scratch_shapes=[pltpu.VMEM((2 * NSA_KV_GROUPS, 1, NSA_GROUP * tq), F32),
                        pltpu.VMEM((2 * NSA_KV_GROUPS, VT_ROWS, NSA_GROUP * tq), F32),
                        pltpu.VMEM((NSA_KV_GROUPS, K_TILE, NSA_GROUP * tq), F32),
                        pltpu.VMEM((NSA_KV_GROUPS, K_TILE, NSA_GROUP * tq), F32)],
        compiler_params=_params("parallel", "parallel"),
        name="nsa_attention",
    )(q, gn, kc, vct, ksx, vst, kwn, vwt, bias_c, bias_near, win_mask, gq, wov)


def _split_bf16(z, parts):
    out = []
    for _ in range(parts - 1):
        hi = z.astype(BF16)
        out.append(hi)
        z = z - hi.astype(F32)
    return out + [z.astype(BF16)]


def _head_sums(z, ones):
    width = ones.shape[0]
    nb = z.shape[1] // width
    rows = z.shape[0]
    zb = z.astype(BF16)
    stacked = jnp.concatenate([zb[:, m * width:(m + 1) * width] for m in range(nb)], axis=0)
    sums = jnp.dot(stacked, ones, preferred_element_type=F32)
    return jnp.concatenate([sums[m * rows:(m + 1) * rows] for m in range(nb)], axis=1)


def _softplus(z):
    return jnp.maximum(z, 0.0) + jnp.log(1.0 + jnp.exp(-jnp.abs(z)))


def _rwkv_chunk(x, prev, st, mu_ref, w0_ref, w2_ref, a0_ref, a2_ref, g2_ref, kk_ref, ka_ref, rk_ref,
                lng_ref, lnb_ref, ones_ref):
    L = x.shape[0]
    W = RWKV_WIDTH
    N = HEAD_DIM
    row_id = lax.broadcasted_iota(jnp.int32, x.shape, 0)
    shifted = jnp.where(row_id == 0, prev, pltpu.roll(x, 1, axis=0))
    xl = x + (shifted - x) * mu_ref[...]
    yield None
    r = xl[:, 0:W]
    k = xl[:, W:2 * W]
    v = xl[:, 2 * W:3 * W]
    xw = xl[:, 3 * W:3 * W + LORA_W]
    xa = xl[:, 3 * W + LORA_W:3 * W + LORA_W + LORA_A]
    xg = xl[:, 3 * W + LORA_W + LORA_A:]
    w = -_softplus(-(w0_ref[...] + _mm(jnp.tanh(xw), w2_ref[...]))) - 0.5
    ld = -jnp.exp(w)
    a = jax.nn.sigmoid(a0_ref[...] + _mm(xa, a2_ref[...]))
    gate = _mm(jax.nn.sigmoid(xg), g2_ref[...])
    kkv = k * kk_ref[...]
    k2 = k * (1.0 + (a - 1.0) * ka_ref[...])
    yield None

    ti = lax.broadcasted_iota(jnp.int32, (L, L), 0)
    si = lax.broadcasted_iota(jnp.int32, (L, L), 1)
    incl = si <= ti
    strict = si < ti
    tri = jnp.where(incl, 1.0, 0.0).astype(BF16)
    cl3 = jnp.dot(tri, jnp.concatenate(_split_bf16(ld, 3), axis=1), preferred_element_type=F32)
    cl = cl3[:, :W] + (cl3[:, W:2 * W] + cl3[:, 2 * W:])
    cl_end = cl[L - 1:L, :]
    yield None
    e_pos = jnp.exp(cl)
    e_neg = jnp.exp(-cl)
    e_prev = jnp.exp(cl - ld)
    e_end = jnp.exp(cl_end - cl)
    eye = jnp.where(ti == si, 1.0, 0.0)
    yield None

    hsum = lambda z: _head_sums(z, ones_ref[...])

    kk_n = kkv * lax.rsqrt(jnp.maximum(hsum(kkv * kkv), 1e-24))
    bv = kk_n * a
    yield None
    a_t = (-kk_n * e_prev).astype(BF16)
    b_t = (bv * e_neg).astype(BF16)
    k_t = (k2 * e_neg).astype(BF16)
    r_t = (r * e_pos).astype(BF16)
    yield None
    v_b = v.astype(BF16)
    k_e = k2 * e_end
    b_e = bv * e_end

    yield "elementwise done"

    P = range(W // LANES)
    ps = [slice(m * LANES, (m + 1) * LANES) for m in P]
    first = lax.broadcasted_iota(jnp.int32, (1, LANES), 1) < N
    same_head = (lax.broadcasted_iota(jnp.int32, (LANES, LANES), 0) // N
                 == lax.broadcasted_iota(jnp.int32, (LANES, LANES), 1) // N)

    def diag_rows(x):
        zero = jnp.zeros_like(x)
        return jnp.concatenate([jnp.where(first, x, zero), jnp.where(first, zero, x)], axis=0)

    def diag_blocks(x):
        zero = jnp.zeros_like(x[:, :LANES])
        return jnp.concatenate([jnp.concatenate([x[:, :LANES], zero], axis=1),
                                jnp.concatenate([zero, x[:, LANES:]], axis=1)], axis=0)

    ke_t = [k_e[:, s].T for s in ps]
    be_t = [b_e[:, s].T for s in ps]
    p_end = [e_pos[:, s].T[:, L - 1:L] for s in ps]

    lhs = [jnp.concatenate([a_t[:, s], r_t[:, s]], axis=0) for s in ps]
    rhs = [jnp.concatenate([k_t[:, s], b_t[:, s]], axis=0) for s in ps]
    zero_b = jnp.zeros_like(lhs[0])
    aa = [[lax.dot_general(jnp.where(first, x, zero_b) if j == 0 else jnp.where(first, zero_b, x), y, _NT,
                           preferred_element_type=F32) for j in range(2)] for x, y in zip(lhs, rhs)]
    cat2 = lambda f: [jnp.concatenate([f(pair[0]), f(pair[1])], axis=1) for pair in aa]
    a_ak = cat2(lambda x: jnp.where(strict, x[:L, :L], 0.0))
    a_ab = cat2(lambda x: jnp.where(strict, x[:L, L:], 0.0))
    a_rk = cat2(lambda x: jnp.where(incl, x[L:, :L], 0.0))
    a_rb = cat2(lambda x: jnp.where(incl, x[L:, L:], 0.0))
    yield None

    eye2 = jnp.concatenate([eye, eye], axis=1)
    tinv = [eye2 + x for x in a_ab]
    pw = [_mm(x, diag_blocks(x)) for x in a_ab]
    yield None
    span = 2
    while 2 * span < L:
        both = [_mm(jnp.concatenate([t, p], axis=0), diag_blocks(p)) for t, p in zip(tinv, pw)]
        tinv = [t + x[:L] for t, x in zip(tinv, both)]
        pw = [x[L:] for x in both]
        yield None
        span *= 2
    tinv = [t + _mm(t, diag_blocks(p)) for t, p in zip(tinv, pw)]
    yield None

    v_d = [diag_rows(v_b[:, s]) for s in ps]
    av = [_mm(jnp.concatenate([a_ak[m], a_rk[m]], axis=0), v_d[m]) for m in P]
    tw = [_mm(tinv[m], jnp.concatenate([diag_rows(a_t[:, ps[m]]), diag_rows(av[m][:L].astype(BF16))], axis=1))
          for m in P]
    kv_loc = [jnp.where(same_head, _mm(ke_t[m], v_b[:, ps[m]]), 0.0) for m in P]
    yield None

    ws = [_mm(jnp.concatenate([tw[m][:, :LANES].astype(BF16), r_t[:, ps[m]]], axis=0), st[m]) for m in P]
    u = [ws[m][:L] + tw[m][:, LANES:] for m in P]
    yield None
    y = [ws[m][L:] + av[m][L:] + _mm(a_rb[m], diag_rows(u[m])) for m in P]
    st_new = [p_end[m] * st[m] + kv_loc[m] + jnp.where(same_head, _mm(be_t[m], u[m]), 0.0) for m in P]
    yield None

    y = jnp.concatenate(y, axis=1)
    d = y - hsum(y) * (1.0 / N)
    yn = d * lax.rsqrt(hsum(d * d) * (1.0 / N) + GN_EPS) * lng_ref[...] + lnb_ref[...]
    bonus = hsum(r * k2 * rk_ref[...])
    yield (yn + bonus * v) * gate, st_new


def _rwkv_kernel(x_ref, xp_ref, *refs):
    *param_refs, o_ref, st_ref = refs
    c = pl.program_id(1)

    @pl.when(c == 0)
    def _():
        st_ref[...] = jnp.zeros(st_ref.shape, F32)

    chunks = []
    for i in range(x_ref.shape[0]):
        prev = jnp.where(c > 0, xp_ref[i, xp_ref.shape[1] - 1:, :], 0.0)
        st = [st_ref[i, m] for m in range(st_ref.shape[1])]
        chunks.append(_rwkv_chunk(x_ref[i], prev, st, *param_refs))
    def advance(gen, until_result):
        item = next(gen)
        return (isinstance(item, tuple), item) if until_result else (item == "elementwise done", item)

    while not advance(chunks[0], False)[0]:
        pass
    outs, states = [], []
    for i, chunk in enumerate(chunks):
        following = chunks[i + 1] if i + 1 < len(chunks) else None
        result = None
        while result is None or following is not None:
            if result is None:
                done, item = advance(chunk, True)
                result = item if done else None
            if following is not None and advance(following, False)[0]:
                following = None
        outs.append(result[0].astype(o_ref.dtype))
        states.append(jnp.stack(result[1]))
    o_ref[...] = jnp.stack(outs)
    st_ref[...] = jnp.stack(states)


def _rwkv(rw, vecs, w2, a2, g2, batch, seq):
    width = rw.shape[1]
    L = min(CHUNK, seq)
    nc = seq // L
    nb = RWKV_SEQS if batch % RWKV_SEQS == 0 else 1
    sub = 8
    rw3 = rw.reshape(batch, seq, width)
    vec = lambda wd: pl.BlockSpec((1, wd), lambda b, c: (0, 0))
    mat = lambda m: pl.BlockSpec(m.shape, lambda b, c: (0, 0))
    mu, w0, a0, kk, ka, rk, lng, lnb = vecs
    head = np.arange(2 * LANES) // HEAD_DIM
    ones = jnp.asarray(head[:, None] == head[None, :], BF16)
    out = pl.pallas_call(
        _rwkv_kernel,
        grid=(batch // nb, nc),
        in_specs=[pl.BlockSpec((nb, L, width), lambda b, c: (b, c, 0)),
                  pl.BlockSpec((nb, sub, width), lambda b, c: (b, jnp.maximum(c * (L // sub) - 1, 0), 0)),
                  vec(width), vec(RWKV_WIDTH), mat(w2), vec(RWKV_WIDTH), mat(a2), mat(g2),
                  vec(RWKV_WIDTH), vec(RWKV_WIDTH), vec(RWKV_WIDTH), vec(RWKV_WIDTH), vec(RWKV_WIDTH), mat(ones)],
        out_specs=pl.BlockSpec((nb, L, RWKV_WIDTH), lambda b, c: (b, c, 0)),
        out_shape=jax.ShapeDtypeStruct((batch, seq, RWKV_WIDTH), BF16),
        scratch_shapes=[pltpu.VMEM((nb, RWKV_WIDTH // LANES, LANES, LANES), F32)],
        compiler_params=_params("parallel", "arbitrary"),
        name="rwkv7",
    )(rw3, rw3, mu, w0, w2, a0, a2, g2, kk, ka, rk, lng, lnb, ones)
    return out.reshape(batch * seq, RWKV_WIDTH)


def _merge_kernel(x_ref, oa_ref, ob_ref, gab_ref, wpa_ref, wpb_ref, wo_ref, o_ref):
    d = x_ref.shape[1]
    pa = jnp.dot(oa_ref[...], wpa_ref[...], preferred_element_type=F32)
    pb = jnp.dot(ob_ref[...], wpb_ref[...], preferred_element_type=F32)
    gab = gab_ref[...].astype(F32)
    merged = jax.nn.sigmoid(gab[:, :d]) * pa + jax.nn.sigmoid(gab[:, d:]) * pb
    o_ref[...] = x_ref[...] + _mm(merged, wo_ref[...])


def _merge(x2, oa, ob, gab, wpa, wpb, wo, tm):
    n, d = x2.shape
    row = lambda wd: pl.BlockSpec((tm, wd), lambda i: (i, 0))
    mat = lambda m: pl.BlockSpec(m.shape, lambda i: (0, 0))
    return pl.pallas_call(
        _merge_kernel,
        grid=(n // tm,),
        in_specs=[row(d), row(oa.shape[1]), row(ob.shape[1]), row(gab.shape[1]), mat(wpa), mat(wpb), mat(wo)],
        out_specs=row(d),
        out_shape=jax.ShapeDtypeStruct((n, d), F32),
        compiler_params=_params("parallel"),
        name="merge",
    )(x2, oa, ob, gab, wpa, wpb, wo)


def _ffn_kernel(seq, x_ref, xh_ref, g_ref, wv_ref, wg_ref, cwv_ref, cwg_ref, cbv_ref, cbg_ref, wd_ref, o_ref,
                h_ref, uv_ref, ug_ref):
    tm = x_ref.shape[0]
    i = pl.program_id(0)
    f = pl.program_id(1)

    @pl.when(f == 0)
    def _():
        first = lax.rem(i * tm, seq) == 0
        halo = jnp.where(first, 0.0, _rms(xh_ref[...], g_ref[...]))
        h_ref[0:HALO, :] = halo.astype(BF16)
        h_ref[HALO:, :] = _rms(x_ref[...], g_ref[...]).astype(BF16)

    h = h_ref[...]
    uv_ref[...] = jnp.dot(h, wv_ref[...], preferred_element_type=F32)
    ug_ref[...] = jnp.dot(h, wg_ref[...], preferred_element_type=F32)

    def conv(u_ref, cw_ref, cb_ref):
        acc = cb_ref[...] + cw_ref[0:1, :] * u_ref[pl.ds(HALO - 2, tm), :]
        acc = acc + cw_ref[1:2, :] * u_ref[pl.ds(HALO - 1, tm), :]
        return acc + cw_ref[2:3, :] * u_ref[pl.ds(HALO, tm), :]

    val = conv(uv_ref, cwv_ref, cbv_ref)
    gt = conv(ug_ref, cwg_ref, cbg_ref)
    y = _mm(gt * jax.nn.sigmoid(gt) * val, wd_ref[...])

    @pl.when(f == 0)
    def _():
        o_ref[...] = x_ref[...] + y

    @pl.when(f > 0)
    def _():
        o_ref[...] = o_ref[...] + y


def _ffn(x1, g, w_up, conv_w, conv_b, w_down, seq, tm, tf):
    n, d = x1.shape
    dff = w_down.shape[0]
    nf = dff // tf
    return pl.pallas_call(
        functools.partial(_ffn_kernel, seq),
        grid=(n // tm, nf),
        in_specs=[pl.BlockSpec((tm, d), lambda i, f: (i, 0)),
                  pl.BlockSpec((HALO, d), lambda i, f: (jnp.maximum(i * (tm // HALO) - 1, 0), 0)),
                  pl.BlockSpec((1, d), lambda i, f: (0, 0)),
                  pl.BlockSpec((d, tf), lambda i, f: (0, f)),
                  pl.BlockSpec((d, tf), lambda i, f: (0, nf + f)),
                  pl.BlockSpec((CONV_WIDTH, tf), lambda i, f: (0, f)),
                  pl.BlockSpec((CONV_WIDTH, tf), lambda i, f: (0, nf + f)),
                  pl.BlockSpec((1, tf), lambda i, f: (0, f)),
                  pl.BlockSpec((1, tf), lambda i, f: (0, nf + f)),
                  pl.BlockSpec((tf, d), lambda i, f: (f, 0))],
        out_specs=pl.BlockSpec((tm, d), lambda i, f: (i, 0)),
        out_shape=jax.ShapeDtypeStruct((n, d), F32),
        scratch_shapes=[pltpu.VMEM((tm + HALO, d), BF16),
                        pltpu.VMEM((tm + HALO, tf), F32),
                        pltpu.VMEM((tm + HALO, tf), F32)],
        compiler_params=_params("parallel", "arbitrary"),
        name="convffn",
    )(x1, x1, g, w_up, w_up, conv_w, conv_w, conv_b, conv_b, w_down)


def _t5_bucket(dist):
    n = np.maximum(dist, 0)
    max_exact = REL_BUCKETS // 2
    ratio = np.log(np.maximum(n, 1).astype(np.float32) / max_exact) / math.log(REL_MAX_DIST / max_exact)
    large = np.minimum(max_exact + (ratio * (REL_BUCKETS - max_exact)).astype(np.int32), REL_BUCKETS - 1)
    return np.where(n < max_exact, n, large).astype(np.int32)


def _bias_tables(rel_bias, seq, ncmp):
    tq, tk = Q_TILE, K_TILE
    G, R = NSA_KV_GROUPS, NSA_GROUP
    nq = seq // tq
    step = tq // CMP_STRIDE
    tab = rel_bias.astype(F32)

    off = (nq - 1) * step
    d_c = np.arange(tq)[None, :] - (np.arange(ncmp + off)[:, None] - off) * CMP_STRIDE - (CMP_BLOCK - 1)
    n_b = -(-REL_MAX_DIST // tk)
    d0 = np.arange(tq)[None, :] - np.arange(tk)[:, None]
    d_n = np.stack([d0 - o * tk for o in range(-n_b, tq // tk)])

    buckets = np.concatenate([_t5_bucket(d_c).reshape(-1), _t5_bucket(d_n).reshape(-1)])
    onehot = (jnp.asarray(buckets)[:, None] == jnp.arange(REL_BUCKETS, dtype=jnp.int32)[None, :]).astype(F32)
    vals = jnp.dot(onehot, tab * LOG2E, precision=lax.Precision.HIGHEST, preferred_element_type=F32).T
    by_group = lambda x: x.reshape(G, R, -1, tq).transpose(0, 2, 1, 3).reshape(G, -1, R * tq)
    tile_r = lambda m: np.tile(m, (1, R))

    base = jnp.where(jnp.asarray(tile_r(d_c >= 0)), by_group(vals[:, :d_c.size]), NEG_INF)
    bias_c = jnp.stack([base[:, off - i * step:off - i * step + ncmp] for i in range(nq)])

    far = jnp.repeat(tab[REL_BUCKETS - 1] * LOG2E, tq).reshape(G, 1, R * tq)
    near = by_group(vals[:, d_c.size:]) - far
    near = jnp.where(jnp.asarray(tile_r(d_n.reshape(-1, tq) >= 0)), near, NEG_INF)
    bias_near = near.reshape(G, d_n.shape[0], tk, R * tq).transpose(1, 0, 2, 3)
    win = np.stack([tile_r(np.where(d0 - o * tk < WINDOW, 0.0, NEG_INF)) for o in _win_masked(tq, tk)])
    return bias_c, bias_near, jnp.asarray(win.astype(np.float32))


def _overlap_matrix(ncmp_pad, n_slc, n_cmp):
    m = np.zeros((ncmp_pad, n_slc), np.float32)
    ratio = SLC_BLOCK // CMP_STRIDE
    for j in range(n_slc):
        for o, wgt in enumerate(OVERLAP_W):
            cidx = ratio * j + o - (CMP_BLOCK // CMP_STRIDE - 1)
            if 0 <= cidx < n_cmp:
                m[cidx, j] += wgt
    return jnp.asarray(m.T)


def _layer(x, attn_norm_g, w_in, rel_bias, q_norm_g, k_norm_g, cmp_pe_k, cmp_w1_k, cmp_w2_k,
           cmp_pe_v, cmp_w1_v, cmp_w2_v, rwkv_mu, rwkv_w0, rwkv_w2, rwkv_a0, rwkv_a2, rwkv_g2,
           rwkv_k_k, rwkv_k_a, rwkv_r_k, rwkv_ln_g, rwkv_ln_b, w_proj_a, w_proj_b, w_out,
           ffn_norm_g, w_up, conv_w, conv_b, w_down):
    batch, seq, d = x.shape
    n = batch * seq
    assert seq % Q_TILE == 0 and Q_TILE % K_TILE == 0 and WINDOW % K_TILE == 0
    assert seq % CMP_STRIDE == 0
    assert seq // SLC_BLOCK <= LANES - HEAD_DIM
    x2 = x.reshape(n, d)
    row = lambda a: a.reshape(1, -1).astype(F32)

    kvw = 6 * NSA_KV_WIDTH
    o_q, o_kv, o_gn = 0, NSA_WIDTH, NSA_WIDTH + kvw
    o_rw = o_gn + 3 * NSA_HEADS
    o_gab = o_rw + RWKV_IN_WIDTH
    gn_pad = jnp.zeros((d, LANES - 3 * NSA_HEADS), w_in.dtype)
    w_cat = jnp.concatenate([w_in[:, o_rw:o_gab], w_in[:, o_q:o_kv], w_in[:, o_kv:o_gn], w_in[:, o_gab:],
                             w_in[:, o_gn:o_rw], gn_pad], axis=1).astype(BF16)
    widths = (RWKV_IN_WIDTH, NSA_WIDTH, 2 * NSA_KV_WIDTH, 4 * NSA_KV_WIDTH, 2 * d, LANES)
    dtypes = (F32, BF16, F32, BF16, BF16, F32)
    rw, q, kv_cmp, kv, gab, gn = _inproj(x2, row(attn_norm_g), w_cat, widths, dtypes, tm=256)

    gk = row(k_norm_g)
    ksx, kwn, vst, vwt = _kvprep(kv, jnp.concatenate([gk, gk], axis=1), seq, tm=512)

    nchunk = seq // CMP_STRIDE
    n_cmp = nchunk - CMP_BLOCK // CMP_STRIDE + 1
    pe = jnp.stack([cmp_pe_k, cmp_pe_v]).astype(F32)
    w1 = jnp.stack([cmp_w1_k, cmp_w1_v]).astype(BF16).reshape(2, CMP_BLOCK, HEAD_DIM, CMP_HIDDEN)
    w2 = jnp.stack([cmp_w2_k, cmp_w2_v]).astype(BF16)
    kc, vc = _compress(kv_cmp, pe, w1, w2, gk, batch, seq)

    bias_c, bias_near, win_mask = _bias_tables(rel_bias, seq, nchunk)
    wov = _overlap_matrix(nchunk, seq // SLC_BLOCK, n_cmp)
    o_a = _attention(q, gn, kc, jnp.swapaxes(vc, -1, -2), ksx, vst, kwn, vwt, bias_c, bias_near, win_mask,
                     q_norm_g.reshape(-1, 1).astype(F32), wov, batch, seq)

    vecs = tuple(row(a) for a in (rwkv_mu, rwkv_w0, rwkv_a0, rwkv_k_k, rwkv_k_a, rwkv_r_k, rwkv_ln_g, rwkv_ln_b))
    o_b = _rwkv(rw, vecs, rwkv_w2.astype(BF16), rwkv_a2.astype(BF16), rwkv_g2.astype(BF16), batch, seq)

    x1 = _merge(x2, o_a, o_b, gab, w_proj_a.astype(BF16), w_proj_b.astype(BF16), w_out.astype(BF16), tm=512)
    dff = w_down.shape[0]
    out = _ffn(x1, row(ffn_norm_g), w_up.astype(BF16), conv_w.astype(F32), row(conv_b), w_down.astype(BF16),
               seq, tm=1024, tf=dff // 2)
    return out.reshape(batch, seq, d)


def kernel(x, attn_norm_g, w_in, rel_bias, q_norm_g, k_norm_g, cmp_pe_k, cmp_w1_k, cmp_w2_k, cmp_pe_v, cmp_w1_v,
           cmp_w2_v, rwkv_mu, rwkv_w0, rwkv_w2, rwkv_a0, rwkv_a2, rwkv_g2, rwkv_k_k, rwkv_k_a, rwkv_r_k,
           rwkv_ln_g, rwkv_ln_b, w_proj_a, w_proj_b, w_out, ffn_norm_g, w_up, conv_w, conv_b, w_down):
    per_layer = (attn_norm_g, w_in, None, q_norm_g, k_norm_g, cmp_pe_k, cmp_w1_k, cmp_w2_k, cmp_pe_v, cmp_w1_v,
                 cmp_w2_v, rwkv_mu, rwkv_w0, rwkv_w2, rwkv_a0, rwkv_a2, rwkv_g2, rwkv_k_k, rwkv_k_a, rwkv_r_k,
                 rwkv_ln_g, rwkv_ln_b, w_proj_a, w_proj_b, w_out, ffn_norm_g, w_up, conv_w, conv_b, w_down)
    for l in range(attn_norm_g.shape[0]):
        args = [rel_bias if p is None else p[l] for p in per_layer]
        x = _layer(x, *args)
    return x
```

```python
import functools
import math

import numpy as np
import jax
import jax.numpy as jnp
from jax import lax
from jax.experimental import pallas as pl
from jax.experimental.pallas import tpu as pltpu

F32 = jnp.float32
BF16 = jnp.bfloat16

HEAD_DIM = 64
NSA_HEADS = 8
NSA_KV_GROUPS = 2
NSA_GROUP = NSA_HEADS // NSA_KV_GROUPS
NSA_WIDTH = NSA_HEADS * HEAD_DIM
NSA_KV_WIDTH = NSA_KV_GROUPS * HEAD_DIM
CMP_BLOCK = 32
CMP_STRIDE = 16
CMP_HIDDEN = 256
SLC_BLOCK = 64
SLC_TOPK = 16
OVERLAP_W = (1, 2, 2, 2, 1)
WINDOW = 512
REL_BUCKETS = 32
REL_MAX_DIST = 128
RWKV_HEADS = 8
RWKV_WIDTH = RWKV_HEADS * HEAD_DIM
LORA_W = 64
LORA_A = 64
LORA_G = 128
RWKV_IN_WIDTH = 3 * RWKV_WIDTH + LORA_W + LORA_A + LORA_G
GN_EPS = 64e-5
CONV_WIDTH = 3
RMS_EPS = 1e-6
NEG_INF = -1e30
FORCE = 1e9
LOG2E = math.log2(math.e)
M_FLOOR = -1e20

LANES = 128
VMEM_LIMIT = 56 * 1024 * 1024

Q_TILE = 256
K_TILE = 256
VT_ROWS = 80
CHUNK = 128
RWKV_SEQS = 4
HALO = 16

_NT = (((1,), (1,)), ((), ()))


def _params(*sem):
    return pltpu.CompilerParams(dimension_semantics=sem, vmem_limit_bytes=VMEM_LIMIT)


def _mm(a, b):
    return jnp.dot(a.astype(BF16), b.astype(BF16), preferred_element_type=F32)


def _mm_nt(a, b):
    return lax.dot_general(a.astype(BF16), b.astype(BF16), _NT, preferred_element_type=F32)


def _mm_f32(a, b):
    return jnp.dot(a, b, preferred_element_type=F32, precision=lax.Precision.HIGHEST)


def _rms(x, g):
    return x * lax.rsqrt(jnp.mean(x * x, axis=-1, keepdims=True) + RMS_EPS) * g


def _kv_prep(kv, gk, row0, ksx_ref, kwn_ref, vs_ref, vw_ref):
    tm = kv.shape[0]
    lane = lax.broadcasted_iota(jnp.int32, (tm, LANES), 1)
    lo = lane < HEAD_DIM

    def norm2(x):
        x2 = x * x
        s_lo = jnp.sum(jnp.where(lo, x2, 0.0), axis=-1, keepdims=True)
        s_hi = jnp.sum(jnp.where(lo, 0.0, x2), axis=-1, keepdims=True)
        ms = jnp.where(lo, s_lo, s_hi) * (1.0 / HEAD_DIM)
        return x * lax.rsqrt(ms + RMS_EPS) * gk

    ks = norm2(kv[:, 0:LANES])
    kw = norm2(kv[:, 2 * LANES:3 * LANES])
    row = row0 + lax.broadcasted_iota(jnp.int32, (tm, LANES), 0)
    onehot = jnp.where(lane - HEAD_DIM == row // SLC_BLOCK, 1.0, 0.0)
    ksx_ref[0] = jnp.where(lo, ks, onehot).astype(BF16)
    ksx_ref[1] = jnp.where(lo, pltpu.roll(ks, HEAD_DIM, axis=1), onehot).astype(BF16)
    flag_col = jnp.where(lane == HEAD_DIM, 1.0, 0.0)
    kwn_ref[0] = jnp.where(lo, kw, flag_col).astype(BF16)
    kwn_ref[1] = jnp.where(lo, pltpu.roll(kw, HEAD_DIM, axis=1), flag_col).astype(BF16)
    tail = jnp.where(lax.broadcasted_iota(jnp.int32, (VT_ROWS - HEAD_DIM, K_TILE), 0) == 0, 1.0, 0.0)
    for col, ref in ((1, vs_ref), (3, vw_ref)):
        for t in range(tm // K_TILE):
            vt = kv[t * K_TILE:(t + 1) * K_TILE, col * LANES:(col + 1) * LANES].T
            for g in range(NSA_KV_GROUPS):
                ref[g, t] = jnp.concatenate([vt[g * HEAD_DIM:(g + 1) * HEAD_DIM], tail], axis=0).astype(BF16)


def _inproj_kernel(seq, x_ref, g_ref, w_ref, gk_ref, rw_ref, q_ref, kvc_ref, gab_ref, gn_ref,
                   ksx_ref, kwn_ref, vs_ref, vw_ref):
    tm = x_ref.shape[0]
    h = _rms(x_ref[...], g_ref[...]).astype(BF16)
    off = 0
    for ref in (rw_ref, q_ref, kvc_ref, None, gab_ref, gn_ref):
        n = 4 * NSA_KV_WIDTH if ref is None else ref.shape[-1]
        y = jnp.dot(h, w_ref[:, off:off + n], preferred_element_type=F32)
        if ref is None:
            _kv_prep(y, gk_ref[...], lax.rem(pl.program_id(0) * tm, seq), ksx_ref, kwn_ref, vs_ref, vw_ref)
        else:
            ref[...] = y.astype(ref.dtype)
        off += n


def _inproj(x2, g, w, gk2, widths, dtypes, seq, tm):
    n, d = x2.shape
    vt_spec = lambda: pl.BlockSpec((2, tm // K_TILE, VT_ROWS, K_TILE), lambda i: (0, i, 0, 0))
    vt_shape = jax.ShapeDtypeStruct((2, n // K_TILE, VT_ROWS, K_TILE), BF16)
    key_spec = lambda: pl.BlockSpec((2, tm, LANES), lambda i: (0, i, 0))
    key_shape = jax.ShapeDtypeStruct((2, n, LANES), BF16)
    return pl.pallas_call(
        functools.partial(_inproj_kernel, seq),
        grid=(n // tm,),
        in_specs=[pl.BlockSpec((tm, d), lambda i: (i, 0)),
                  pl.BlockSpec((1, d), lambda i: (0, 0)),
                  pl.BlockSpec(w.shape, lambda i: (0, 0), pipeline_mode=pl.Buffered(1)),
                  pl.BlockSpec((1, LANES), lambda i: (0, 0))],
        out_specs=[pl.BlockSpec((tm, wd), lambda i: (i, 0)) for wd in widths]
        + [key_spec(), key_spec(), vt_spec(), vt_spec()],
        out_shape=[jax.ShapeDtypeStruct((n, wd), dt) for wd, dt in zip(widths, dtypes)]
        + [key_shape, key_shape, vt_shape, vt_shape],
        compiler_params=_params("parallel"),
        name="inproj",
    )(x2, g, w, gk2)


def _compress_kernel(k_ref, v_ref, pe_ref, w1_ref, w2_ref, gk_ref, kc_ref, vc_ref):
    nchunk = k_ref.shape[0] // CMP_STRIDE
    span = CMP_BLOCK // CMP_STRIDE
    G = NSA_KV_GROUPS
    acc = [[[jnp.zeros((nchunk, CMP_HIDDEN), F32) for _ in range(span)] for _ in range(G)] for _ in range(2)]
    for p in range(CMP_STRIDE):
        for s, ref in enumerate((k_ref, v_ref)):
            x = ref[pl.ds(p, nchunk, stride=CMP_STRIDE), :]
            for g in range(G):
                xs = x[:, g * HEAD_DIM:(g + 1) * HEAD_DIM]
                for half in range(span):
                    q = half * CMP_STRIDE + p
                    acc[s][g][half] = acc[s][g][half] + _mm(xs + pe_ref[s, q:q + 1, :], w1_ref[s, q])
    for s, out_ref in ((0, kc_ref), (1, vc_ref)):
        for g in range(G):
            hid = acc[s][g][0]
            for half in range(1, span):
                hid = hid + pltpu.roll(acc[s][g][half], nchunk - half, axis=0)
            out = _mm(jax.nn.gelu(hid), w2_ref[s])
            out_ref[0, g] = _rms(out, gk_ref[...]) if s == 0 else out


def _compress(kv, pe, w1, w2, gk, batch, seq):
    nchunk = seq // CMP_STRIDE
    out_spec = lambda: pl.BlockSpec((1, NSA_KV_GROUPS, nchunk, HEAD_DIM), lambda b: (b, 0, 0, 0))
    out_shape = jax.ShapeDtypeStruct((batch, NSA_KV_GROUPS, nchunk, HEAD_DIM), F32)
    const = lambda a: pl.BlockSpec(a.shape, lambda b: (0,) * a.ndim)
    return pl.pallas_call(
        _compress_kernel,
        grid=(batch,),
        in_specs=[pl.BlockSpec((seq, NSA_KV_WIDTH), lambda b: (b, 0)), pl.BlockSpec((seq, NSA_KV_WIDTH), lambda b: (b, 1)),
                  const(pe), const(w1), const(w2), const(gk)],
        out_specs=[out_spec(), out_spec()],
        out_shape=[out_shape, out_shape],
        compiler_params=_params("parallel"),
        name="compress",
    )(kv, kv, pe, w1, w2, gk)


def _rank_select(score_t, cur_t):
    nb = score_t.shape[0]
    jj = lax.broadcasted_iota(jnp.int32, score_t.shape, 0)
    rank = jnp.zeros(score_t.shape, F32)
    for j in range(nb):
        row = score_t[j:j + 1, :]
        gt = jnp.where(row > score_t, 1.0, 0.0)
        ge = jnp.where(row >= score_t, 1.0, 0.0)
        rank = rank + jnp.where(jj > j, ge, gt)
    keep = jnp.where(rank < float(min(SLC_TOPK, nb)), 1.0, 0.0) * jnp.where(jj <= cur_t, 1.0, 0.0)
    return jnp.where(keep > 0.5, 0.0, NEG_INF)


def _win_masked(tq, tk):
    return [o for o in range(-(WINDOW // tk), 0) if tq - 1 - o * tk >= WINDOW]


def _scores(ks, ws):
    return [jnp.dot(k, w, preferred_element_type=F32) for k, w in zip(ks, ws)]


def _softmax_update(m_ref, acc_ref, slots, s, vts):
    m_old = [m_ref[i] for i in slots]
    ncol = s[0].shape[1] // LANES
    m_cols = [[] for _ in slots]
    p_cols = [[] for _ in slots]
    for c in range(ncol):
        cols = slice(c * LANES, (c + 1) * LANES)
        for n, x in enumerate(s):
            xc = x[:, cols]
            mc = jnp.maximum(m_old[n][:, cols], jnp.max(xc, axis=0, keepdims=True))
            m_cols[n].append(mc)
            p_cols[n].append(jnp.exp2(xc - mc).astype(BF16))
    m_new = [jnp.concatenate(mc, axis=1) for mc in m_cols]
    pv = [jnp.dot(vt, jnp.concatenate(pc, axis=1), preferred_element_type=F32) for vt, pc in zip(vts, p_cols)]
    acc = [jnp.exp2(mo - mn) * acc_ref[i] + z for mo, mn, i, z in zip(m_old, m_new, slots, pv)]
    for i, mn, ac in zip(slots, m_new, acc):
        m_ref[i] = mn
        acc_ref[i] = ac


def _attn_kernel(q_ref, gn_ref, kc_ref, vct_ref, ksx_ref, vst_ref, kwn_ref, vwt_ref, bc_ref, bn_ref, wm_ref,
                 gq_ref, wov_ref, o_ref, m_ref, acc_ref, sa_ref, sb_ref):
    tq = q_ref.shape[0]
    tk = K_TILE
    nblk = wov_ref.shape[0]
    G, R = NSA_KV_GROUPS, NSA_GROUP
    qt = pl.program_id(1)
    n_q = tq // tk
    n_b = -(-REL_MAX_DIST // tk)
    j0 = qt * n_q
    qT = q_ref[...].astype(F32).T
    gates = jax.nn.sigmoid(gn_ref[...].T)
    gq = gq_ref[...]

    qn, qx, negs, o_cmp = [], [], [], []
    for g in range(G):
        xs = []
        for h in range(g * R, (g + 1) * R):
            x = qT[h * HEAD_DIM:(h + 1) * HEAD_DIM]
            x = x * lax.rsqrt(jnp.mean(x * x, axis=0, keepdims=True) + RMS_EPS) * gq * (HEAD_DIM ** -0.5 * LOG2E)
            xs.append(x.astype(BF16))
        qn.append(jnp.concatenate(xs, axis=1))

    for g in range(G):
        s = jnp.dot(kc_ref[0, g].astype(BF16), qn[g], preferred_element_type=F32)
        p_cols = []
        for c in range(R * tq // LANES):
            cols = slice(c * LANES, (c + 1) * LANES)
            sc = s[:, cols] + bc_ref[0, g, :, cols]
            e = jnp.exp2(sc - jnp.maximum(jnp.max(sc, axis=0, keepdims=True), M_FLOOR))
            p_cols.append(e * (1.0 / jnp.maximum(jnp.sum(e, axis=0, keepdims=True), 1e-30)))
        o_cmp.append(jnp.dot(vct_ref[0, g].astype(BF16), jnp.concatenate(p_cols, axis=1).astype(BF16),
                             preferred_element_type=F32))
        per_head = tq // LANES
        psum = jnp.concatenate([sum(p_cols[r * per_head + t] for r in range(R)) for t in range(per_head)], axis=1)
        imp = _mm_f32(wov_ref[...], psum)

        blk = lax.broadcasted_iota(jnp.int32, imp.shape, 0)
        cur = (qt * tq + lax.broadcasted_iota(jnp.int32, imp.shape, 1)) // SLC_BLOCK
        forced = (blk == 0) | (blk == cur) | (blk == cur - 1)
        score = jnp.where(forced, FORCE, jnp.where(blk <= cur, imp, -FORCE))
        neg = _rank_select(score, cur)
        if nblk < LANES - HEAD_DIM:
            neg = jnp.concatenate([neg, jnp.zeros((LANES - HEAD_DIM - nblk, tq), F32)], axis=0)
        neg = neg.astype(BF16)
        negs.append(neg)
        qx.append(jnp.concatenate([qn[g], jnp.concatenate([neg] * R, axis=1)], axis=0))

    m_ref[...] = jnp.full(m_ref.shape, M_FLOOR, F32)
    acc_ref[...] = jnp.zeros(acc_ref.shape, F32)

    def invalid_before(level):
        return jnp.where(qt >= level, 0.0, NEG_INF)

    def slc_queries(g, level):
        if level == 0:
            return qx[g]
        rows = jnp.minimum(negs[g], invalid_before(level).astype(BF16))
        return jnp.concatenate([qn[g], jnp.concatenate([rows] * R, axis=1)], axis=0)

    def win_queries(g, level):
        flag_row = lax.broadcasted_iota(jnp.int32, (LANES - HEAD_DIM, R * tq), 0) == 0
        extra = jnp.where(flag_row, invalid_before(level), 0.0).astype(BF16)
        return jnp.concatenate([qn[g], extra], axis=0)

    def slc_probs(o, bias):
        j = jnp.maximum(j0 + o, 0)
        sl = pl.ds(pl.multiple_of(j * tk, tk), tk)
        return [(g, ksx_ref[g, sl, :], vst_ref[g, j], slc_queries(g, max(-(o // n_q), 0)), bias.at[g])
                for g in range(G)]

    def win_probs(o, bias, per_group):
        j = jnp.maximum(j0 + o, 0)
        sl = pl.ds(pl.multiple_of(j * tk, tk), tk)
        pick = lambda g: None if bias is None else (bias.at[g] if per_group else bias)
        return [(G + g, kwn_ref[g, sl, :], vwt_ref[g, j], win_queries(g, max(-(o // n_q), 0)), pick(g))
                for g in range(G)]

    n_far = jnp.maximum(j0 - n_b, 0)
    groups = list(range(G))

    def far_scores(j, buf, gs=groups):
        sl = pl.ds(pl.multiple_of(j * tk, tk), tk)
        for g, x in zip(gs, _scores([ksx_ref[g, sl, :] for g in gs], [qx[g] for g in gs])):
            buf[g] = x

    def far_update(j, buf, gs=groups):
        _softmax_update(m_ref, acc_ref, gs, [buf[g] for g in gs], [vst_ref[g, j] for g in gs])

    far_scores(0, sa_ref)

    def far_body(i, carry):
        for g in groups:
            far_scores(2 * i + 1, sb_ref, [g])
            far_update(2 * i, sa_ref, [g])
        for g in groups:
            far_scores(2 * i + 2, sa_ref, [g])
            far_update(2 * i + 1, sb_ref, [g])
        return carry

    lax.fori_loop(0, n_far // 2, far_body, 0)

    @pl.when(n_far % 2 == 1)
    def _():
        far_update(n_far - 1, sa_ref)

    masked = _win_masked(tq, tk)
    rounds = []
    for o in range(-(WINDOW // tk), n_q):
        if o >= -n_b:
            rounds.append(slc_probs(o, bn_ref.at[o + n_b]) + win_probs(o, bn_ref.at[o + n_b], True))
        elif o in masked:
            rounds.append(win_probs(o, wm_ref.at[masked.index(o)], False))
        else:
            rounds.append(win_probs(o, None, False))

    def round_scores(probs):
        sc = _scores([p[1] for p in probs], [p[3] for p in probs])
        return [x if p[4] is None else x + p[4][...] for x, p in zip(sc, probs)]

    sc = round_scores(rounds[0])
    for i, probs in enumerate(rounds):
        sc_next = round_scores(rounds[i + 1]) if i + 1 < len(rounds) else None
        _softmax_update(m_ref, acc_ref, [p[0] for p in probs], sc, [p[2] for p in probs])
        sc = sc_next

    outs = []
    for g in range(G):
        acc = acc_ref[g]
        o_slc = acc[:HEAD_DIM] / acc[HEAD_DIM:HEAD_DIM + 1]
        acc = acc_ref[G + g]
        o_win = acc[:HEAD_DIM] / acc[HEAD_DIM:HEAD_DIM + 1]
        for r in range(R):
            h = g * R + r
            cols = slice(r * tq, (r + 1) * tq)
            outs.append(gates[3 * h:3 * h + 1] * o_cmp[g][:, cols] + gates[3 * h + 1:3 * h + 2] * o_slc[:, cols]
                        + gates[3 * h + 2:3 * h + 3] * o_win[:, cols])
    o_ref[...] = jnp.concatenate(outs, axis=0).T.astype(o_ref.dtype)


def _attention(q, gn, kc, vct, ksx, vst, kwn, vwt, bias_c, bias_near, win_mask, gq, wov, batch, seq):
    n = q.shape[0]
    tq = Q_TILE
    nq = seq // tq
    ncmp = kc.shape[2]
    row = lambda b, i: (b * nq + i, 0)
    whole = lambda b, i: (0, b, 0)
    tiles = lambda b, i: (0, b, 0, 0)
    const = lambda a: pl.BlockSpec(a.shape, lambda b, i: (0,) * a.ndim, pipeline_mode=pl.Buffered(1))
    return pl.pallas_call(
        _attn_kernel,
        grid=(batch, nq),
        in_specs=[pl.BlockSpec((tq, NSA_WIDTH), row),
                  pl.BlockSpec((tq, LANES), row),
                  pl.BlockSpec((1, NSA_KV_GROUPS, ncmp, HEAD_DIM), lambda b, i: (b, 0, 0, 0)),
                  pl.BlockSpec((1, NSA_KV_GROUPS, HEAD_DIM, ncmp), lambda b, i: (b, 0, 0, 0)),
                  pl.BlockSpec((2, seq, LANES), whole),
                  pl.BlockSpec((2, seq // K_TILE, VT_ROWS, K_TILE), tiles),
                  pl.BlockSpec((2, seq, LANES), whole),
                  pl.BlockSpec((2, seq // K_TILE, VT_ROWS, K_TILE), tiles),
                  pl.BlockSpec((1, NSA_KV_GROUPS, ncmp, NSA_GROUP * tq), lambda b, i: (i, 0, 0, 0)),
                  const(bias_near), const(win_mask), const(gq), const(wov)],
        out_specs=pl.BlockSpec((tq, NSA_WIDTH), row),
        out_shape=jax.ShapeDtypeStruct((n, NSA_WIDTH), BF16),
        scratch_shapes=[pltpu.VMEM((2 * NSA_KV_GROUPS, 1, NSA_GROUP * tq), F32),
                        pltpu.VMEM((2 * NSA_KV_GROUPS, VT_ROWS, NSA_GROUP * tq), F32),
                        pltpu.VMEM((NSA_KV_GROUPS, K_TILE, NSA_GROUP * tq), F32),
                        pltpu.VMEM((NSA_KV_GROUPS, K_TILE, NSA_GROUP * tq), F32)],
        compiler_params=_params("parallel", "parallel"),
        name="nsa_attention",
    )(q, gn, kc, vct, ksx, vst, kwn, vwt, bias_c, bias_near, win_mask, gq, wov)


def _split_bf16(z, parts):
    out = []
    for _ in range(parts - 1):
        hi = z.astype(BF16)
        out.append(hi)
        z = z - hi.astype(F32)
    return out + [z.astype(BF16)]


def _head_sums(z, ones):
    width = ones.shape[0]
    nb = z.shape[1] // width
    rows = z.shape[0]
    zb = z.astype(BF16)
    stacked = jnp.concatenate([zb[:, m * width:(m + 1) * width] for m in range(nb)], axis=0)
    sums = jnp.dot(stacked, ones, preferred_element_type=F32)
    return jnp.concatenate([sums[m * rows:(m + 1) * rows] for m in range(nb)], axis=1)


def _softplus(z):
    return jnp.maximum(z, 0.0) + jnp.log(1.0 + jnp.exp(-jnp.abs(z)))


def _rwkv_chunk(x, prev, st, mu_ref, w0_ref, w2_ref, a0_ref, a2_ref, g2_ref, kk_ref, ka_ref, rk_ref,
                lng_ref, lnb_ref, ones_ref):
    L = x.shape[0]
    W = RWKV_WIDTH
    N = HEAD_DIM
    row_id = lax.broadcasted_iota(jnp.int32, x.shape, 0)
    shifted = jnp.where(row_id == 0, prev, pltpu.roll(x, 1, axis=0))
    xl = x + (shifted - x) * mu_ref[...]
    yield None
    r = xl[:, 0:W]
    k = xl[:, W:2 * W]
    v = xl[:, 2 * W:3 * W]
    xw = xl[:, 3 * W:3 * W + LORA_W]
    xa = xl[:, 3 * W + LORA_W:3 * W + LORA_W + LORA_A]
    xg = xl[:, 3 * W + LORA_W + LORA_A:]
    w = -_softplus(-(w0_ref[...] + _mm(jnp.tanh(xw), w2_ref[...]))) - 0.5
    ld = -jnp.exp(w)
    a = jax.nn.sigmoid(a0_ref[...] + _mm(xa, a2_ref[...]))
    gate = _mm(jax.nn.sigmoid(xg), g2_ref[...])
    kkv = k * kk_ref[...]
    k2 = k * (1.0 + (a - 1.0) * ka_ref[...])
    yield None

    ti = lax.broadcasted_iota(jnp.int32, (L, L), 0)
    si = lax.broadcasted_iota(jnp.int32, (L, L), 1)
    incl = si <= ti
    strict = si < ti
    tri = jnp.where(incl, 1.0, 0.0).astype(BF16)
    cl3 = jnp.dot(tri, jnp.concatenate(_split_bf16(ld, 3), axis=1), preferred_element_type=F32)
    cl = cl3[:, :W] + (cl3[:, W:2 * W] + cl3[:, 2 * W:])
    cl_end = cl[L - 1:L, :]
    yield None
    e_pos = jnp.exp(cl)
    e_neg = jnp.exp(-cl)
    e_prev = jnp.exp(cl - ld)
    e_end = jnp.exp(cl_end - cl)
    eye = jnp.where(ti == si, 1.0, 0.0)
    yield None

    hsum = lambda z: _head_sums(z, ones_ref[...])

    kk_n = kkv * lax.rsqrt(jnp.maximum(hsum(kkv * kkv), 1e-24))
    bv = kk_n * a
    yield None
    a_t = (-kk_n * e_prev).astype(BF16)
    b_t = (bv * e_neg).astype(BF16)
    k_t = (k2 * e_neg).astype(BF16)
    r_t = (r * e_pos).astype(BF16)
    yield None
    v_b = v.astype(BF16)
    k_e = k2 * e_end
    b_e = bv * e_end

    yield "elementwise done"

    P = range(W // LANES)
    ps = [slice(m * LANES, (m + 1) * LANES) for m in P]
    first = lax.broadcasted_iota(jnp.int32, (1, LANES), 1) < N
    same_head = (lax.broadcasted_iota(jnp.int32, (LANES, LANES), 0) // N
                 == lax.broadcasted_iota(jnp.int32, (LANES, LANES), 1) // N)

    def diag_rows(x):
        zero = jnp.zeros_like(x)
        return jnp.concatenate([jnp.where(first, x, zero), jnp.where(first, zero, x)], axis=0)

    def diag_blocks(x):
        zero = jnp.zeros_like(x[:, :LANES])
        return jnp.concatenate([jnp.concatenate([x[:, :LANES], zero], axis=1),
                                jnp.concatenate([zero, x[:, LANES:]], axis=1)], axis=0)

    ke_t = [k_e[:, s].T for s in ps]
    be_t = [b_e[:, s].T for s in ps]
    p_end = [e_pos[:, s].T[:, L - 1:L] for s in ps]

    lhs = [jnp.concatenate([a_t[:, s], r_t[:, s]], axis=0) for s in ps]
    rhs = [jnp.concatenate([k_t[:, s], b_t[:, s]], axis=0) for s in ps]
    zero_b = jnp.zeros_like(lhs[0])
    aa = [[lax.dot_general(jnp.where(first, x, zero_b) if j == 0 else jnp.where(first, zero_b, x), y, _NT,
                           preferred_element_type=F32) for j in range(2)] for x, y in zip(lhs, rhs)]
    cat2 = lambda f: [jnp.concatenate([f(pair[0]), f(pair[1])], axis=1) for pair in aa]
    a_ak = cat2(lambda x: jnp.where(strict, x[:L, :L], 0.0))
    a_ab = cat2(lambda x: jnp.where(strict, x[:L, L:], 0.0))
    a_rk = cat2(lambda x: jnp.where(incl, x[L:, :L], 0.0))
    a_rb = cat2(lambda x: jnp.where(incl, x[L:, L:], 0.0))
    yield None

    eye2 = jnp.concatenate([eye, eye], axis=1)
    tinv = [eye2 + x for x in a_ab]
    pw = [_mm(x, diag_blocks(x)) for x in a_ab]
    yield None
    span = 2
    while 2 * span < L:
        both = [_mm(jnp.concatenate([t, p], axis=0), diag_blocks(p)) for t, p in zip(tinv, pw)]
        tinv = [t + x[:L] for t, x in zip(tinv, both)]
        pw = [x[L:] for x in both]
        yield None
        span *= 2
    tinv = [t + _mm(t, diag_blocks(p)) for t, p in zip(tinv, pw)]
    yield None

    v_d = [diag_rows(v_b[:, s]) for s in ps]
    av = [_mm(jnp.concatenate([a_ak[m], a_rk[m]], axis=0), v_d[m]) for m in P]
    tw = [_mm(tinv[m], jnp.concatenate([diag_rows(a_t[:, ps[m]]), diag_rows(av[m][:L].astype(BF16))], axis=1))
          for m in P]
    kv_loc = [jnp.where(same_head, _mm(ke_t[m], v_b[:, ps[m]]), 0.0) for m in P]
    yield None

    ws = [_mm(jnp.concatenate([tw[m][:, :LANES].astype(BF16), r_t[:, ps[m]]], axis=0), st[m]) for m in P]
    u = [ws[m][:L] + tw[m][:, LANES:] for m in P]
    yield None
    y = [ws[m][L:] + av[m][L:] + _mm(a_rb[m], diag_rows(u[m])) for m in P]
    st_new = [p_end[m] * st[m] + kv_loc[m] + jnp.where(same_head, _mm(be_t[m], u[m]), 0.0) for m in P]
    yield None

    y = jnp.concatenate(y, axis=1)
    d = y - hsum(y) * (1.0 / N)
    yn = d * lax.rsqrt(hsum(d * d) * (1.0 / N) + GN_EPS) * lng_ref[...] + lnb_ref[...]
    bonus = hsum(r * k2 * rk_ref[...])
    yield (yn + bonus * v) * gate, st_new


def _rwkv_kernel(x_ref, xp_ref, *refs):
    *param_refs, o_ref, st_ref = refs
    c = pl.program_id(1)

    @pl.when(c == 0)
    def _():
        st_ref[...] = jnp.zeros(st_ref.shape, F32)

    chunks = []
    for i in range(x_ref.shape[0]):
        prev = jnp.where(c > 0, xp_ref[i, xp_ref.shape[1] - 1:, :], 0.0)
        st = [st_ref[i, m] for m in range(st_ref.shape[1])]
        chunks.append(_rwkv_chunk(x_ref[i], prev, st, *param_refs))
    def advance(gen, until_result):
        item = next(gen)
        return (isinstance(item, tuple), item) if until_result else (item == "elementwise done", item)

    while not advance(chunks[0], False)[0]:
        pass
    outs, states = [], []
    for i, chunk in enumerate(chunks):
        following = chunks[i + 1] if i + 1 < len(chunks) else None
        result = None
        while result is None or following is not None:
            if result is None:
                done, item = advance(chunk, True)
                result = item if done else None
            if following is not None and advance(following, False)[0]:
                following = None
        outs.append(result[0].astype(o_ref.dtype))
        states.append(jnp.stack(result[1]))
    o_ref[...] = jnp.stack(outs)
    st_ref[...] = jnp.stack(states)


def _rwkv(rw, vecs, w2, a2, g2, batch, seq):
    width = rw.shape[1]
    L = min(CHUNK, seq)
    nc = seq // L
    nb = RWKV_SEQS if batch % RWKV_SEQS == 0 else 1
    sub = 8
    rw3 = rw.reshape(batch, seq, width)
    vec = lambda wd: pl.BlockSpec((1, wd), lambda b, c: (0, 0))
    mat = lambda m: pl.BlockSpec(m.shape, lambda b, c: (0, 0))
    mu, w0, a0, kk, ka, rk, lng, lnb = vecs
    head = np.arange(2 * LANES) // HEAD_DIM
    ones = jnp.asarray(head[:, None] == head[None, :], BF16)
    out = pl.pallas_call(
        _rwkv_kernel,
        grid=(batch // nb, nc),
        in_specs=[pl.BlockSpec((nb, L, width), lambda b, c: (b, c, 0)),
                  pl.BlockSpec((nb, sub, width), lambda b, c: (b, jnp.maximum(c * (L // sub) - 1, 0), 0)),
                  vec(width), vec(RWKV_WIDTH), mat(w2), vec(RWKV_WIDTH), mat(a2), mat(g2),
                  vec(RWKV_WIDTH), vec(RWKV_WIDTH), vec(RWKV_WIDTH), vec(RWKV_WIDTH), vec(RWKV_WIDTH), mat(ones)],
        out_specs=pl.BlockSpec((nb, L, RWKV_WIDTH), lambda b, c: (b, c, 0)),
        out_shape=jax.ShapeDtypeStruct((batch, seq, RWKV_WIDTH), BF16),
        scratch_shapes=[pltpu.VMEM((nb, RWKV_WIDTH // LANES, LANES, LANES), F32)],
        compiler_params=_params("parallel", "arbitrary"),
        name="rwkv7",
    )(rw3, rw3, mu, w0, w2, a0, a2, g2, kk, ka, rk, lng, lnb, ones)
    return out.reshape(batch * seq, RWKV_WIDTH)


def _merge_kernel(x_ref, oa_ref, ob_ref, gab_ref, wpa_ref, wpb_ref, wo_ref, o_ref):
    d = x_ref.shape[1]
    pa = jnp.dot(oa_ref[...], wpa_ref[...], preferred_element_type=F32)
    pb = jnp.dot(ob_ref[...], wpb_ref[...], preferred_element_type=F32)
    gab = gab_ref[...].astype(F32)
    merged = jax.nn.sigmoid(gab[:, :d]) * pa + jax.nn.sigmoid(gab[:, d:]) * pb
    o_ref[...] = x_ref[...] + _mm(merged, wo_ref[...])


def _merge(x2, oa, ob, gab, wpa, wpb, wo, tm):
    n, d = x2.shape
    row = lambda wd: pl.BlockSpec((tm, wd), lambda i: (i, 0))
    mat = lambda m: pl.BlockSpec(m.shape, lambda i: (0, 0))
    return pl.pallas_call(
        _merge_kernel,
        grid=(n // tm,),
        in_specs=[row(d), row(oa.shape[1]), row(ob.shape[1]), row(gab.shape[1]), mat(wpa), mat(wpb), mat(wo)],
        out_specs=row(d),
        out_shape=jax.ShapeDtypeStruct((n, d), F32),
        compiler_params=_params("parallel"),
        name="merge",
    )(x2, oa, ob, gab, wpa, wpb, wo)


def _ffn_kernel(seq, x_ref, xh_ref, g_ref, wv_ref, wg_ref, cwv_ref, cwg_ref, cbv_ref, cbg_ref, wd_ref, o_ref,
                h_ref, uv_ref, ug_ref):
    tm = x_ref.shape[0]
    i = pl.program_id(0)
    f = pl.program_id(1)

    @pl.when(f == 0)
    def _():
        first = lax.rem(i * tm, seq) == 0
        halo = jnp.where(first, 0.0, _rms(xh_ref[...], g_ref[...]))
        h_ref[0:HALO, :] = halo.astype(BF16)
        h_ref[HALO:, :] = _rms(x_ref[...], g_ref[...]).astype(BF16)

    h = h_ref[...]
    uv_ref[...] = jnp.dot(h, wv_ref[...], preferred_element_type=F32)
    ug_ref[...] = jnp.dot(h, wg_ref[...], preferred_element_type=F32)

    def conv(u_ref, cw_ref, cb_ref):
        acc = cb_ref[...] + cw_ref[0:1, :] * u_ref[pl.ds(HALO - 2, tm), :]
        acc = acc + cw_ref[1:2, :] * u_ref[pl.ds(HALO - 1, tm), :]
        return acc + cw_ref[2:3, :] * u_ref[pl.ds(HALO, tm), :]

    val = conv(uv_ref, cwv_ref, cbv_ref)
    gt = conv(ug_ref, cwg_ref, cbg_ref)
    y = _mm(gt * jax.nn.sigmoid(gt) * val, wd_ref[...])

    @pl.when(f == 0)
    def _():
        o_ref[...] = x_ref[...] + y

    @pl.when(f > 0)
    def _():
        o_ref[...] = o_ref[...] + y


def _ffn(x1, g, w_up, conv_w, conv_b, w_down, seq, tm, tf):
    n, d = x1.shape
    dff = w_down.shape[0]
    nf = dff // tf
    return pl.pallas_call(
        functools.partial(_ffn_kernel, seq),
        grid=(n // tm, nf),
        in_specs=[pl.BlockSpec((tm, d), lambda i, f: (i, 0)),
                  pl.BlockSpec((HALO, d), lambda i, f: (jnp.maximum(i * (tm // HALO) - 1, 0), 0)),
                  pl.BlockSpec((1, d), lambda i, f: (0, 0)),
                  pl.BlockSpec((d, tf), lambda i, f: (0, f)),
                  pl.BlockSpec((d, tf), lambda i, f: (0, nf + f)),
                  pl.BlockSpec((CONV_WIDTH, tf), lambda i, f: (0, f)),
                  pl.BlockSpec((CONV_WIDTH, tf), lambda i, f: (0, nf + f)),
                  pl.BlockSpec((1, tf), lambda i, f: (0, f)),
                  pl.BlockSpec((1, tf), lambda i, f: (0, nf + f)),
                  pl.BlockSpec((tf, d), lambda i, f: (f, 0))],
        out_specs=pl.BlockSpec((tm, d), lambda i, f: (i, 0)),
        out_shape=jax.ShapeDtypeStruct((n, d), F32),
        scratch_shapes=[pltpu.VMEM((tm + HALO, d), BF16),
                        pltpu.VMEM((tm + HALO, tf), F32),
                        pltpu.VMEM((tm + HALO, tf), F32)],
        compiler_params=_params("parallel", "arbitrary"),
        name="convffn",
    )(x1, x1, g, w_up, w_up, conv_w, conv_w, conv_b, conv_b, w_down)


def _t5_bucket(dist):
    n = np.maximum(dist, 0)
    max_exact = REL_BUCKETS // 2
    ratio = np.log(np.maximum(n, 1).astype(np.float32) / max_exact) / math.log(REL_MAX_DIST / max_exact)
    large = np.minimum(max_exact + (ratio * (REL_BUCKETS - max_exact)).astype(np.int32), REL_BUCKETS - 1)
    return np.where(n < max_exact, n, large).astype(np.int32)


def _bias_tables(rel_bias, seq, ncmp):
    tq, tk = Q_TILE, K_TILE
    G, R = NSA_KV_GROUPS, NSA_GROUP
    nq = seq // tq
    step = tq // CMP_STRIDE
    tab = rel_bias.astype(F32)

    off = (nq - 1) * step
    d_c = np.arange(tq)[None, :] - (np.arange(ncmp + off)[:, None] - off) * CMP_STRIDE - (CMP_BLOCK - 1)
    n_b = -(-REL_MAX_DIST // tk)
    d0 = np.arange(tq)[None, :] - np.arange(tk)[:, None]
    d_n = np.stack([d0 - o * tk for o in range(-n_b, tq // tk)])

    buckets = np.concatenate([_t5_bucket(d_c).reshape(-1), _t5_bucket(d_n).reshape(-1)])
    onehot = (jnp.asarray(buckets)[:, None] == jnp.arange(REL_BUCKETS, dtype=jnp.int32)[None, :]).astype(F32)
    vals = jnp.dot(onehot, tab * LOG2E, precision=lax.Precision.HIGHEST, preferred_element_type=F32).T
    by_group = lambda x: x.reshape(G, R, -1, tq).transpose(0, 2, 1, 3).reshape(G, -1, R * tq)
    tile_r = lambda m: np.tile(m, (1, R))

    base = jnp.where(jnp.asarray(tile_r(d_c >= 0)), by_group(vals[:, :d_c.size]), NEG_INF)
    bias_c = jnp.stack([base[:, off - i * step:off - i * step + ncmp] for i in range(nq)])

    far = jnp.repeat(tab[REL_BUCKETS - 1] * LOG2E, tq).reshape(G, 1, R * tq)
    near = by_group(vals[:, d_c.size:]) - far
    near = jnp.where(jnp.asarray(tile_r(d_n.reshape(-1, tq) >= 0)), near, NEG_INF)
    bias_near = near.reshape(G, d_n.shape[0], tk, R * tq).transpose(1, 0, 2, 3)
    win = np.stack([tile_r(np.where(d0 - o * tk < WINDOW, 0.0, NEG_INF)) for o in _win_masked(tq, tk)])
    return bias_c, bias_near, jnp.asarray(win.astype(np.float32))


def _overlap_matrix(ncmp_pad, n_slc, n_cmp):
    m = np.zeros((ncmp_pad, n_slc), np.float32)
    ratio = SLC_BLOCK // CMP_STRIDE
    for j in range(n_slc):
        for o, wgt in enumerate(OVERLAP_W):
            cidx = ratio * j + o - (CMP_BLOCK // CMP_STRIDE - 1)
            if 0 <= cidx < n_cmp:
                m[cidx, j] += wgt
    return jnp.asarray(m.T)


def _layer(x, attn_norm_g, w_in, rel_bias, q_norm_g, k_norm_g, cmp_pe_k, cmp_w1_k, cmp_w2_k,
           cmp_pe_v, cmp_w1_v, cmp_w2_v, rwkv_mu, rwkv_w0, rwkv_w2, rwkv_a0, rwkv_a2, rwkv_g2,
           rwkv_k_k, rwkv_k_a, rwkv_r_k, rwkv_ln_g, rwkv_ln_b, w_proj_a, w_proj_b, w_out,
           ffn_norm_g, w_up, conv_w, conv_b, w_down):
    batch, seq, d = x.shape
    n = batch * seq
    assert seq % Q_TILE == 0 and Q_TILE % K_TILE == 0 and WINDOW % K_TILE == 0
    assert seq % CMP_STRIDE == 0
    assert seq // SLC_BLOCK <= LANES - HEAD_DIM
    x2 = x.reshape(n, d)
    row = lambda a: a.reshape(1, -1).astype(F32)

    kvw = 6 * NSA_KV_WIDTH
    o_q, o_kv, o_gn = 0, NSA_WIDTH, NSA_WIDTH + kvw
    o_rw = o_gn + 3 * NSA_HEADS
    o_gab = o_rw + RWKV_IN_WIDTH
    gn_pad = jnp.zeros((d, LANES - 3 * NSA_HEADS), w_in.dtype)
    w_cat = jnp.concatenate([w_in[:, o_rw:o_gab], w_in[:, o_q:o_kv], w_in[:, o_kv:o_gn], w_in[:, o_gab:],
                             w_in[:, o_gn:o_rw], gn_pad], axis=1).astype(BF16)
    widths = (RWKV_IN_WIDTH, NSA_WIDTH, 2 * NSA_KV_WIDTH, 2 * d, LANES)
    dtypes = (F32, BF16, F32, BF16, F32)
    gk = row(k_norm_g)
    rw, q, kv_cmp, gab, gn, ksx, kwn, vst, vwt = _inproj(
        x2, row(attn_norm_g), w_cat, jnp.concatenate([gk, gk], axis=1), widths, dtypes, seq, tm=512)

    nchunk = seq // CMP_STRIDE
    n_cmp = nchunk - CMP_BLOCK // CMP_STRIDE + 1
    pe = jnp.stack([cmp_pe_k, cmp_pe_v]).astype(F32)
    w1 = jnp.stack([cmp_w1_k, cmp_w1_v]).astype(BF16).reshape(2, CMP_BLOCK, HEAD_DIM, CMP_HIDDEN)
    w2 = jnp.stack([cmp_w2_k, cmp_w2_v]).astype(BF16)
    kc, vc = _compress(kv_cmp, pe, w1, w2, gk, batch, seq)

    bias_c, bias_near, win_mask = _bias_tables(rel_bias, seq, nchunk)
    wov = _overlap_matrix(nchunk, seq // SLC_BLOCK, n_cmp)
    o_a = _attention(q, gn, kc, jnp.swapaxes(vc, -1, -2), ksx, vst, kwn, vwt, bias_c, bias_near, win_mask,
                     q_norm_g.reshape(-1, 1).astype(F32), wov, batch, seq)

    vecs = tuple(row(a) for a in (rwkv_mu, rwkv_w0, rwkv_a0, rwkv_k_k, rwkv_k_a, rwkv_r_k, rwkv_ln_g, rwkv_ln_b))
    o_b = _rwkv(rw, vecs, rwkv_w2.astype(BF16), rwkv_a2.astype(BF16), rwkv_g2.astype(BF16), batch, seq)

    x1 = _merge(x2, o_a, o_b, gab, w_proj_a.astype(BF16), w_proj_b.astype(BF16), w_out.astype(BF16), tm=512)
    dff = w_down.shape[0]
    out = _ffn(x1, row(ffn_norm_g), w_up.astype(BF16), conv_w.astype(F32), row(conv_b), w_down.astype(BF16),
               seq, tm=1024, tf=dff // 2)
    return out.reshape(batch, seq, d)


def kernel(x, attn_norm_g, w_in, rel_bias, q_norm_g, k_norm_g, cmp_pe_k, cmp_w1_k, cmp_w2_k, cmp_pe_v, cmp_w1_v,
           cmp_w2_v, rwkv_mu, rwkv_w0, rwkv_w2, rwkv_a0, rwkv_a2, rwkv_g2, rwkv_k_k, rwkv_k_a, rwkv_r_k,
           rwkv_ln_g, rwkv_ln_b, w_proj_a, w_proj_b, w_out, ffn_norm_g, w_up, conv_w, conv_b, w_down):
    per_layer = (attn_norm_g, w_in, None, q_norm_g, k_norm_g, cmp_pe_k, cmp_w1_k, cmp_w2_k, cmp_pe_v, cmp_w1_v,
                 cmp_w2_v, rwkv_mu, rwkv_w0, rwkv_w2, rwkv_a0, rwkv_a2, rwkv_g2, rwkv_k_k, rwkv_k_a, rwkv_r_k,
                 rwkv_ln_g, rwkv_ln_b, w_proj_a, w_proj_b, w_out, ffn_norm_g, w_up, conv_w, conv_b, w_down)
    for l in range(attn_norm_g.shape[0]):
        args = [rel_bias if p is None else p[l] for p in per_layer]
        x = _layer(x, *args)
    return x
```

```python
import functools
import math

import numpy as np
import jax
import jax.numpy as jnp
from jax import lax
from jax.experimental import pallas as pl
from jax.experimental.pallas import tpu as pltpu

F32 = jnp.float32
BF16 = jnp.bfloat16

HEAD_DIM = 64
NSA_HEADS = 8
NSA_KV_GROUPS = 2
NSA_GROUP = NSA_HEADS // NSA_KV_GROUPS
NSA_WIDTH = NSA_HEADS * HEAD_DIM
NSA_KV_WIDTH = NSA_KV_GROUPS * HEAD_DIM
CMP_BLOCK = 32
CMP_STRIDE = 16
CMP_HIDDEN = 256
SLC_BLOCK = 64
SLC_TOPK = 16
OVERLAP_W = (1, 2, 2, 2, 1)
WINDOW = 512
REL_BUCKETS = 32
REL_MAX_DIST = 128
RWKV_HEADS = 8
RWKV_WIDTH = RWKV_HEADS * HEAD_DIM
LORA_W = 64
LORA_A = 64
LORA_G = 128
RWKV_IN_WIDTH = 3 * RWKV_WIDTH + LORA_W + LORA_A + LORA_G
GN_EPS = 64e-5
CONV_WIDTH = 3
RMS_EPS = 1e-6
NEG_INF = -1e30
FORCE = 1e9
LOG2E = math.log2(math.e)
M_FLOOR = -1e20

LANES = 128
VMEM_LIMIT = 56 * 1024 * 1024

Q_TILE = 256
K_TILE = 256
VT_ROWS = 80
CHUNK = 128
RWKV_SEQS = 4
HALO = 16

_NT = (((1,), (1,)), ((), ()))


def _params(*sem):
    return pltpu.CompilerParams(dimension_semantics=sem, vmem_limit_bytes=VMEM_LIMIT)


def _mm(a, b):
    return jnp.dot(a.astype(BF16), b.astype(BF16), preferred_element_type=F32)


def _mm_nt(a, b):
    return lax.dot_general(a.astype(BF16), b.astype(BF16), _NT, preferred_element_type=F32)


def _mm_f32(a, b):
    return jnp.dot(a, b, preferred_element_type=F32, precision=lax.Precision.HIGHEST)


def _rms(x, g):
    return x * lax.rsqrt(jnp.mean(x * x, axis=-1, keepdims=True) + RMS_EPS) * g


def _kv_prep(kv, gk, row0, ksx_ref, kwn_ref, vs_ref, vw_ref):
    tm = kv.shape[0]
    lane = lax.broadcasted_iota(jnp.int32, (tm, LANES), 1)
    lo = lane < HEAD_DIM

    def norm2(x):
        x2 = x * x
        s_lo = jnp.sum(jnp.where(lo, x2, 0.0), axis=-1, keepdims=True)
        s_hi = jnp.sum(jnp.where(lo, 0.0, x2), axis=-1, keepdims=True)
        ms = jnp.where(lo, s_lo, s_hi) * (1.0 / HEAD_DIM)
        return x * lax.rsqrt(ms + RMS_EPS) * gk

    ks = norm2(kv[:, 0:LANES])
    kw = norm2(kv[:, 2 * LANES:3 * LANES])
    row = row0 + lax.broadcasted_iota(jnp.int32, (tm, LANES), 0)
    onehot = jnp.where(lane - HEAD_DIM == row // SLC_BLOCK, 1.0, 0.0)
    ksx_ref[0] = jnp.where(lo, ks, onehot).astype(BF16)
    ksx_ref[1] = jnp.where(lo, pltpu.roll(ks, HEAD_DIM, axis=1), onehot).astype(BF16)
    flag_col = jnp.where(lane == HEAD_DIM, 1.0, 0.0)
    kwn_ref[0] = jnp.where(lo, kw, flag_col).astype(BF16)
    kwn_ref[1] = jnp.where(lo, pltpu.roll(kw, HEAD_DIM, axis=1), flag_col).astype(BF16)
    tail = jnp.where(lax.broadcasted_iota(jnp.int32, (VT_ROWS - HEAD_DIM, K_TILE), 0) == 0, 1.0, 0.0)
    for col, ref in ((1, vs_ref), (3, vw_ref)):
        for t in range(tm // K_TILE):
            vt = kv[t * K_TILE:(t + 1) * K_TILE, col * LANES:(col + 1) * LANES].T
            for g in range(NSA_KV_GROUPS):
                ref[g, t] = jnp.concatenate([vt[g * HEAD_DIM:(g + 1) * HEAD_DIM], tail], axis=0).astype(BF16)


def _inproj_kernel(seq, x_ref, g_ref, w_ref, gk_ref, rw_ref, q_ref, kvc_ref, gab_ref, gn_ref,
                   ksx_ref, kwn_ref, vs_ref, vw_ref):
    tm = x_ref.shape[0]
    h = _rms(x_ref[...], g_ref[...]).astype(BF16)
    off = 0
    for ref in (rw_ref, q_ref, kvc_ref, None, gab_ref, gn_ref):
        n = 4 * NSA_KV_WIDTH if ref is None else ref.shape[-1]
        y = jnp.dot(h, w_ref[:, off:off + n], preferred_element_type=F32)
        if ref is None:
            _kv_prep(y, gk_ref[...], lax.rem(pl.program_id(0) * tm, seq), ksx_ref, kwn_ref, vs_ref, vw_ref)
        else:
            ref[...] = y.astype(ref.dtype)
        off += n


def _inproj(x2, g, w, gk2, widths, dtypes, seq, tm):
    n, d = x2.shape
    vt_spec = lambda: pl.BlockSpec((2, tm // K_TILE, VT_ROWS, K_TILE), lambda i: (0, i, 0, 0))
    vt_shape = jax.ShapeDtypeStruct((2, n // K_TILE, VT_ROWS, K_TILE), BF16)
    key_spec = lambda: pl.BlockSpec((2, tm, LANES), lambda i: (0, i, 0))
    key_shape = jax.ShapeDtypeStruct((2, n, LANES), BF16)
    return pl.pallas_call(
        functools.partial(_inproj_kernel, seq),
        grid=(n // tm,),
        in_specs=[pl.BlockSpec((tm, d), lambda i: (i, 0)),
                  pl.BlockSpec((1, d), lambda i: (0, 0)),
                  pl.BlockSpec(w.shape, lambda i: (0, 0), pipeline_mode=pl.Buffered(1)),
                  pl.BlockSpec((1, LANES), lambda i: (0, 0))],
        out_specs=[pl.BlockSpec((tm, wd), lambda i: (i, 0)) for wd in widths]
        + [key_spec(), key_spec(), vt_spec(), vt_spec()],
        out_shape=[jax.ShapeDtypeStruct((n, wd), dt) for wd, dt in zip(widths, dtypes)]
        + [key_shape, key_shape, vt_shape, vt_shape],
        compiler_params=_params("parallel"),
        name="inproj",
    )(x2, g, w, gk2)


def _compress_kernel(k_ref, v_ref, pe_ref, w1_ref, w2_ref, gk_ref, kc_ref, vc_ref):
    nchunk = k_ref.shape[0] // CMP_STRIDE
    span = CMP_BLOCK // CMP_STRIDE
    G = NSA_KV_GROUPS
    acc = [[[jnp.zeros((nchunk, CMP_HIDDEN), F32) for _ in range(span)] for _ in range(G)] for _ in range(2)]
    for p in range(CMP_STRIDE):
        for s, ref in enumerate((k_ref, v_ref)):
            x = ref[pl.ds(p, nchunk, stride=CMP_STRIDE), :]
            for g in range(G):
                xs = x[:, g * HEAD_DIM:(g + 1) * HEAD_DIM]
                for half in range(span):
                    q = half * CMP_STRIDE + p
                    acc[s][g][half] = acc[s][g][half] + _mm(xs + pe_ref[s, q:q + 1, :], w1_ref[s, q])
    for s, out_ref in ((0, kc_ref), (1, vc_ref)):
        for g in range(G):
            hid = acc[s][g][0]
            for half in range(1, span):
                hid = hid + pltpu.roll(acc[s][g][half], nchunk - half, axis=0)
            out = _mm(jax.nn.gelu(hid), w2_ref[s])
            out_ref[0, g] = _rms(out, gk_ref[...]) if s == 0 else out


def _compress(kv, pe, w1, w2, gk, batch, seq):
    nchunk = seq // CMP_STRIDE
    out_spec = lambda: pl.BlockSpec((1, NSA_KV_GROUPS, nchunk, HEAD_DIM), lambda b: (b, 0, 0, 0))
    out_shape = jax.ShapeDtypeStruct((batch, NSA_KV_GROUPS, nchunk, HEAD_DIM), F32)
    const = lambda a: pl.BlockSpec(a.shape, lambda b: (0,) * a.ndim)
    return pl.pallas_call(
        _compress_kernel,
        grid=(batch,),
        in_specs=[pl.BlockSpec((seq, NSA_KV_WIDTH), lambda b: (b, 0)), pl.BlockSpec((seq, NSA_KV_WIDTH), lambda b: (b, 1)),
                  const(pe), const(w1), const(w2), const(gk)],
        out_specs=[out_spec(), out_spec()],
        out_shape=[out_shape, out_shape],
        compiler_params=_params("parallel"),
        name="compress",
    )(kv, kv, pe, w1, w2, gk)


def _rank_select(scores, cur_t, n_live, rank_ref):
    nb, t = scores[0].shape
    sub = 8
    groups = range(nb // sub)
    rank_ref[...] = jnp.zeros(rank_ref.shape, F32)
    below = lax.broadcasted_iota(jnp.int32, (sub, t), 0)
    for gj in groups:
        @pl.when(gj * sub < n_live)
        def _():
            for n, x in enumerate(scores):
                xs = [x[g * sub:(g + 1) * sub] for g in groups]
                inc = [jnp.zeros((sub, t), F32) for _ in groups]
                for jl in range(sub):
                    j = gj * sub + jl
                    row = jnp.broadcast_to(x[j:j + 1, :], (sub, t))
                    for g in groups:
                        if g > gj:
                            one = jnp.where(row >= xs[g], 1.0, 0.0)
                        elif g < gj:
                            one = jnp.where(row > xs[g], 1.0, 0.0)
                        else:
                            one = jnp.where(below > jl, jnp.where(row >= xs[g], 1.0, 0.0),
                                            jnp.where(row > xs[g], 1.0, 0.0))
                        inc[g] = inc[g] + one
                rank_ref[n] = rank_ref[n] + jnp.concatenate(inc, axis=0)
    jj = lax.broadcasted_iota(jnp.int32, (nb, t), 0)
    live = jj <= cur_t
    return [jnp.where(live, jnp.where(rank_ref[n] < float(min(SLC_TOPK, nb)), 0.0, NEG_INF), NEG_INF)
            for n in range(len(scores))]


def _win_masked(tq, tk):
    return [o for o in range(-(WINDOW // tk), 0) if tq - 1 - o * tk >= WINDOW]


def _scores(ks, ws):
    return [jnp.dot(k, w, preferred_element_type=F32) for k, w in zip(ks, ws)]


def _softmax_update(m_ref, acc_ref, slots, s, vts):
    m_old = [m_ref[i] for i in slots]
    ncol = s[0].shape[1] // LANES
    m_cols = [[] for _ in slots]
    p_cols = [[] for _ in slots]
    for c in range(ncol):
        cols = slice(c * LANES, (c + 1) * LANES)
        for n, x in enumerate(s):
            xc = x[:, cols]
            mc = jnp.maximum(m_old[n][:, cols], jnp.max(xc, axis=0, keepdims=True))
            m_cols[n].append(mc)
            p_cols[n].append(jnp.exp2(xc - mc).astype(BF16))
    m_new = [jnp.concatenate(mc, axis=1) for mc in m_cols]
    pv = [jnp.dot(vt, jnp.concatenate(pc, axis=1), preferred_element_type=F32) for vt, pc in zip(vts, p_cols)]
    acc = [jnp.exp2(mo - mn) * acc_ref[i] + z for mo, mn, i, z in zip(m_old, m_new, slots, pv)]
    for i, mn, ac in zip(slots, m_new, acc):
        m_ref[i] = mn
        acc_ref[i] = ac


def _attn_kernel(q_ref, gn_ref, kc_ref, vct_ref, ksx_ref, vst_ref, kwn_ref, vwt_ref, bc_ref, bn_ref, wm_ref,
                 gq_ref, wov_ref, o_ref, m_ref, acc_ref, sa_ref, sb_ref, rank_ref):
    tq = q_ref.shape[0]
    tk = K_TILE
    nblk = wov_ref.shape[0]
    G, R = NSA_KV_GROUPS, NSA_GROUP
    qt = pl.program_id(1)
    n_q = tq // tk
    n_b = -(-REL_MAX_DIST // tk)
    j0 = qt * n_q
    qT = q_ref[...].astype(F32).T
    gates = jax.nn.sigmoid(gn_ref[...].T)
    gq = gq_ref[...]

    qn, qx, negs, o_cmp, scores = [], [], [], [], []
    for g in range(G):
        xs = []
        for h in range(g * R, (g + 1) * R):
            x = qT[h * HEAD_DIM:(h + 1) * HEAD_DIM]
            x = x * lax.rsqrt(jnp.mean(x * x, axis=0, keepdims=True) + RMS_EPS) * gq * (HEAD_DIM ** -0.5 * LOG2E)
            xs.append(x.astype(BF16))
        qn.append(jnp.concatenate(xs, axis=1))

    for g in range(G):
        s = jnp.dot(kc_ref[0, g].astype(BF16), qn[g], preferred_element_type=F32)
        p_cols = []
        for c in range(R * tq // LANES):
            cols = slice(c * LANES, (c + 1) * LANES)
            sc = s[:, cols] + bc_ref[0, g, :, cols]
            e = jnp.exp2(sc - jnp.maximum(jnp.max(sc, axis=0, keepdims=True), M_FLOOR))
            p_cols.append(e * (1.0 / jnp.maximum(jnp.sum(e, axis=0, keepdims=True), 1e-30)))
        o_cmp.append(jnp.dot(vct_ref[0, g].astype(BF16), jnp.concatenate(p_cols, axis=1).astype(BF16),
                             preferred_element_type=F32))
        per_head = tq // LANES
        psum = jnp.concatenate([sum(p_cols[r * per_head + t] for r in range(R)) for t in range(per_head)], axis=1)
        imp = _mm_f32(wov_ref[...], psum)

        blk = lax.broadcasted_iota(jnp.int32, imp.shape, 0)
        cur = (qt * tq + lax.broadcasted_iota(jnp.int32, imp.shape, 1)) // SLC_BLOCK
        forced = (blk == 0) | (blk == cur) | (blk == cur - 1)
        scores.append(jnp.where(forced, FORCE, jnp.where(blk <= cur, imp, -FORCE)))

    cur = (qt * tq + lax.broadcasted_iota(jnp.int32, (nblk, tq), 1)) // SLC_BLOCK
    for g, neg in enumerate(_rank_select(scores, cur, (qt + 1) * (tq // SLC_BLOCK), rank_ref)):
        if nblk < LANES - HEAD_DIM:
            neg = jnp.concatenate([neg, jnp.zeros((LANES - HEAD_DIM - nblk, tq), F32)], axis=0)
        neg = neg.astype(BF16)
        negs.append(neg)
        qx.append(jnp.concatenate([qn[g], jnp.concatenate([neg] * R, axis=1)], axis=0))

    m_ref[...] = jnp.full(m_ref.shape, M_FLOOR, F32)
    acc_ref[...] = jnp.zeros(acc_ref.shape, F32)

    def invalid_before(level):
        return jnp.where(qt >= level, 0.0, NEG_INF)

    def slc_queries(g, level):
        if level == 0:
            return qx[g]
        rows = jnp.minimum(negs[g], invalid_before(level).astype(BF16))
        return jnp.concatenate([qn[g], jnp.concatenate([rows] * R, axis=1)], axis=0)

    def win_queries(g, level):
        flag_row = lax.broadcasted_iota(jnp.int32, (LANES - HEAD_DIM, R * tq), 0) == 0
        extra = jnp.where(flag_row, invalid_before(level), 0.0).astype(BF16)
        return jnp.concatenate([qn[g], extra], axis=0)

    def slc_probs(o, bias):
        j = jnp.maximum(j0 + o, 0)
        sl = pl.ds(pl.multiple_of(j * tk, tk), tk)
        return [(g, ksx_ref[g, sl, :], vst_ref[g, j], slc_queries(g, max(-(o // n_q), 0)), bias.at[g])
                for g in range(G)]

    def win_probs(o, bias, per_group):
        j = jnp.maximum(j0 + o, 0)
        sl = pl.ds(pl.multiple_of(j * tk, tk), tk)
        pick = lambda g: None if bias is None else (bias.at[g] if per_group else bias)
        return [(G + g, kwn_ref[g, sl, :], vwt_ref[g, j], win_queries(g, max(-(o // n_q), 0)), pick(g))
                for g in range(G)]

    n_far = jnp.maximum(j0 - n_b, 0)
    groups = list(range(G))

    def far_scores(j, buf, gs=groups):
        sl = pl.ds(pl.multiple_of(j * tk, tk), tk)
        for g, x in zip(gs, _scores([ksx_ref[g, sl, :] for g in gs], [qx[g] for g in gs])):
            buf[g] = x

    def far_update(j, buf, gs=groups):
        _softmax_update(m_ref, acc_ref, gs, [buf[g] for g in gs], [vst_ref[g, j] for g in gs])

    far_scores(0, sa_ref)

    def far_body(i, carry):
        for g in groups:
            far_scores(2 * i + 1, sb_ref, [g])
            far_update(2 * i, sa_ref, [g])
        for g in groups:
            far_scores(2 * i + 2, sa_ref, [g])
            far_update(2 * i + 1, sb_ref, [g])
        return carry

    lax.fori_loop(0, n_far // 2, far_body, 0)

    @pl.when(n_far % 2 == 1)
    def _():
        far_update(n_far - 1, sa_ref)

    masked = _win_masked(tq, tk)
    rounds = []
    for o in range(-(WINDOW // tk), n_q):
        if o >= -n_b:
            rounds.append(slc_probs(o, bn_ref.at[o + n_b]) + win_probs(o, bn_ref.at[o + n_b], True))
        elif o in masked:
            rounds.append(win_probs(o, wm_ref.at[masked.index(o)], False))
        else:
            rounds.append(win_probs(o, None, False))

    def round_scores(probs):
        sc = _scores([p[1] for p in probs], [p[3] for p in probs])
        return [x if p[4] is None else x + p[4][...] for x, p in zip(sc, probs)]

    sc = round_scores(rounds[0])
    for i, probs in enumerate(rounds):
        sc_next = round_scores(rounds[i + 1]) if i + 1 < len(rounds) else None
        _softmax_update(m_ref, acc_ref, [p[0] for p in probs], sc, [p[2] for p in probs])
        sc = sc_next

    outs = []
    for g in range(G):
        acc = acc_ref[g]
        o_slc = acc[:HEAD_DIM] / acc[HEAD_DIM:HEAD_DIM + 1]
        acc = acc_ref[G + g]
        o_win = acc[:HEAD_DIM] / acc[HEAD_DIM:HEAD_DIM + 1]
        for r in range(R):
            h = g * R + r
            cols = slice(r * tq, (r + 1) * tq)
            outs.append(gates[3 * h:3 * h + 1] * o_cmp[g][:, cols] + gates[3 * h + 1:3 * h + 2] * o_slc[:, cols]
                        + gates[3 * h + 2:3 * h + 3] * o_win[:, cols])
    o_ref[...] = jnp.concatenate(outs, axis=0).T.astype(o_ref.dtype)


def _attention(q, gn, kc, vct, ksx, vst, kwn, vwt, bias_c, bias_near, win_mask, gq, wov, batch, seq):
    n = q.shape[0]
    tq = Q_TILE
    nq = seq // tq
    ncmp = kc.shape[2]
    row = lambda b, i: (b * nq + i, 0)
    whole = lambda b, i: (0, b, 0)
    tiles = lambda b, i: (0, b, 0, 0)
    const = lambda a: pl.BlockSpec(a.shape, lambda b, i: (0,) * a.ndim, pipeline_mode=pl.Buffered(1))
    return pl.pallas_call(
        _attn_kernel,
        grid=(batch, nq),
        in_specs=[pl.BlockSpec((tq, NSA_WIDTH), row),
                  pl.BlockSpec((tq, LANES), row),
                  pl.BlockSpec((1, NSA_KV_GROUPS, ncmp, HEAD_DIM), lambda b, i: (b, 0, 0, 0)),
                  pl.BlockSpec((1, NSA_KV_GROUPS, HEAD_DIM, ncmp), lambda b, i: (b, 0, 0, 0)),
                  pl.BlockSpec((2, seq, LANES), whole),
                  pl.BlockSpec((2, seq // K_TILE, VT_ROWS, K_TILE), tiles),
                  pl.BlockSpec((2, seq, LANES), whole),
                  pl.BlockSpec((2, seq // K_TILE, VT_ROWS, K_TILE), tiles),
                  pl.BlockSpec((1, NSA_KV_GROUPS, ncmp, NSA_GROUP * tq), lambda b, i: (i, 0, 0, 0)),
                  const(bias_near), const(win_mask), const(gq), const(wov)],
        out_specs=pl.BlockSpec((tq, NSA_WIDTH), row),
        out_shape=jax.ShapeDtypeStruct((n, NSA_WIDTH), BF16),
        scratch_shapes=[pltpu.VMEM((2 * NSA_KV_GROUPS, 1, NSA_GROUP * tq), F32),
                        pltpu.VMEM((2 * NSA_KV_GROUPS, VT_ROWS, NSA_GROUP * tq), F32),
                        pltpu.VMEM((NSA_KV_GROUPS, K_TILE, NSA_GROUP * tq), F32),
                        pltpu.VMEM((NSA_KV_GROUPS, K_TILE, NSA_GROUP * tq), F32),
                        pltpu.VMEM((NSA_KV_GROUPS, seq // SLC_BLOCK, tq), F32)],
        compiler_params=_params("parallel", "parallel"),
        name="nsa_attention",
    )(q, gn, kc, vct, ksx, vst, kwn, vwt, bias_c, bias_near, win_mask, gq, wov)


def _split_bf16(z, parts):
    out = []
    for _ in range(parts - 1):
        hi = z.astype(BF16)
        out.append(hi)
        z = z - hi.astype(F32)
    return out + [z.astype(BF16)]


def _head_sums(z, ones):
    width = ones.shape[0]
    nb = z.shape[1] // width
    rows = z.shape[0]
    zb = z.astype(BF16)
    stacked = jnp.concatenate([zb[:, m * width:(m + 1) * width] for m in range(nb)], axis=0)
    sums = jnp.dot(stacked, ones, preferred_element_type=F32)
    return jnp.concatenate([sums[m * rows:(m + 1) * rows] for m in range(nb)], axis=1)


def _softplus(z):
    return jnp.maximum(z, 0.0) + jnp.log(1.0 + jnp.exp(-jnp.abs(z)))


def _rwkv_chunk(x, prev, st, mu_ref, w0_ref, w2_ref, a0_ref, a2_ref, g2_ref, kk_ref, ka_ref, rk_ref,
                lng_ref, lnb_ref, ones_ref):
    L = x.shape[0]
    W = RWKV_WIDTH
    N = HEAD_DIM
    row_id = lax.broadcasted_iota(jnp.int32, x.shape, 0)
    shifted = jnp.where(row_id == 0, prev, pltpu.roll(x, 1, axis=0))
    xl = x + (shifted - x) * mu_ref[...]
    yield None
    r = xl[:, 0:W]
    k = xl[:, W:2 * W]
    v = xl[:, 2 * W:3 * W]
    xw = xl[:, 3 * W:3 * W + LORA_W]
    xa = xl[:, 3 * W + LORA_W:3 * W + LORA_W + LORA_A]
    xg = xl[:, 3 * W + LORA_W + LORA_A:]
    w = -_softplus(-(w0_ref[...] + _mm(jnp.tanh(xw), w2_ref[...]))) - 0.5
    ld = -jnp.exp(w)
    a = jax.nn.sigmoid(a0_ref[...] + _mm(xa, a2_ref[...]))
    gate = _mm(jax.nn.sigmoid(xg), g2_ref[...])
    kkv = k * kk_ref[...]
    k2 = k * (1.0 + (a - 1.0) * ka_ref[...])
    yield None

    ti = lax.broadcasted_iota(jnp.int32, (L, L), 0)
    si = lax.broadcasted_iota(jnp.int32, (L, L), 1)
    incl = si <= ti
    strict = si < ti
    tri = jnp.where(incl, 1.0, 0.0).astype(BF16)
    cl3 = jnp.dot(tri, jnp.concatenate(_split_bf16(ld, 3), axis=1), preferred_element_type=F32)
    cl = cl3[:, :W] + (cl3[:, W:2 * W] + cl3[:, 2 * W:])
    cl_end = cl[L - 1:L, :]
    yield None
    e_pos = jnp.exp(cl)
    e_neg = jnp.exp(-cl)
    e_prev = jnp.exp(cl - ld)
    e_end = jnp.exp(cl_end - cl)
    eye = jnp.where(ti == si, 1.0, 0.0)
    yield None

    hsum = lambda z: _head_sums(z, ones_ref[...])

    kk_n = kkv * lax.rsqrt(jnp.maximum(hsum(kkv * kkv), 1e-24))
    bv = kk_n * a
    yield None
    a_t = (-kk_n * e_prev).astype(BF16)
    b_t = (bv * e_neg).astype(BF16)
    k_t = (k2 * e_neg).astype(BF16)
    r_t = (r * e_pos).astype(BF16)
    yield None
    v_b = v.astype(BF16)
    k_e = k2 * e_end
    b_e = bv * e_end

    yield "elementwise done"

    P = range(W // LANES)
    ps = [slice(m * LANES, (m + 1) * LANES) for m in P]
    first = lax.broadcasted_iota(jnp.int32, (1, LANES), 1) < N
    same_head = (lax.broadcasted_iota(jnp.int32, (LANES, LANES), 0) // N
                 == lax.broadcasted_iota(jnp.int32, (LANES, LANES), 1) // N)

    def diag_rows(x):
        zero = jnp.zeros_like(x)
        return jnp.concatenate([jnp.where(first, x, zero), jnp.where(first, zero, x)], axis=0)

    def diag_blocks(x):
        zero = jnp.zeros_like(x[:, :LANES])
        return jnp.concatenate([jnp.concatenate([x[:, :LANES], zero], axis=1),
                                jnp.concatenate([zero, x[:, LANES:]], axis=1)], axis=0)

    ke_t = [k_e[:, s].T for s in ps]
    be_t = [b_e[:, s].T for s in ps]
    p_end = [e_pos[:, s].T[:, L - 1:L] for s in ps]

    lhs = [jnp.concatenate([a_t[:, s], r_t[:, s]], axis=0) for s in ps]
    rhs = [jnp.concatenate([k_t[:, s], b_t[:, s]], axis=0) for s in ps]
    zero_b = jnp.zeros_like(lhs[0])
    aa = [[lax.dot_general(jnp.where(first, x, zero_b) if j == 0 else jnp.where(first, zero_b, x), y, _NT,
                           preferred_element_type=F32) for j in range(2)] for x, y in zip(lhs, rhs)]
    cat2 = lambda f: [jnp.concatenate([f(pair[0]), f(pair[1])], axis=1) for pair in aa]
    a_ak = cat2(lambda x: jnp.where(strict, x[:L, :L], 0.0))
    a_ab = cat2(lambda x: jnp.where(strict, x[:L, L:], 0.0))
    a_rk = cat2(lambda x: jnp.where(incl, x[L:, :L], 0.0))
    a_rb = cat2(lambda x: jnp.where(incl, x[L:, L:], 0.0))
    yield None

    eye2 = jnp.concatenate([eye, eye], axis=1)
    tinv = [eye2 + x for x in a_ab]
    pw = [_mm(x, diag_blocks(x)) for x in a_ab]
    yield None
    span = 2
    while 2 * span < L:
        both = [_mm(jnp.concatenate([t, p], axis=0), diag_blocks(p)) for t, p in zip(tinv, pw)]
        tinv = [t + x[:L] for t, x in zip(tinv, both)]
        pw = [x[L:] for x in both]
        yield None
        span *= 2
    tinv = [t + _mm(t, diag_blocks(p)) for t, p in zip(tinv, pw)]
    yield None

    v_d = [diag_rows(v_b[:, s]) for s in ps]
    av = [_mm(jnp.concatenate([a_ak[m], a_rk[m]], axis=0), v_d[m]) for m in P]
    tw = [_mm(tinv[m], jnp.concatenate([diag_rows(a_t[:, ps[m]]), diag_rows(av[m][:L].astype(BF16))], axis=1))
          for m in P]
    kv_loc = [jnp.where(same_head, _mm(ke_t[m], v_b[:, ps[m]]), 0.0) for m in P]
    yield None

    ws = [_mm(jnp.concatenate([tw[m][:, :LANES].astype(BF16), r_t[:, ps[m]]], axis=0), st[m]) for m in P]
    u = [ws[m][:L] + tw[m][:, LANES:] for m in P]
    yield None
    y = [ws[m][L:] + av[m][L:] + _mm(a_rb[m], diag_rows(u[m])) for m in P]
    st_new = [p_end[m] * st[m] + kv_loc[m] + jnp.where(same_head, _mm(be_t[m], u[m]), 0.0) for m in P]
    yield None

    y = jnp.concatenate(y, axis=1)
    d = y - hsum(y) * (1.0 / N)
    yn = d * lax.rsqrt(hsum(d * d) * (1.0 / N) + GN_EPS) * lng_ref[...] + lnb_ref[...]
    bonus = hsum(r * k2 * rk_ref[...])
    yield (yn + bonus * v) * gate, st_new


def _rwkv_kernel(x_ref, xp_ref, *refs):
    *param_refs, o_ref, st_ref = refs
    c = pl.program_id(1)

    @pl.when(c == 0)
    def _():
        st_ref[...] = jnp.zeros(st_ref.shape, F32)

    chunks = []
    for i in range(x_ref.shape[0]):
        prev = jnp.where(c > 0, xp_ref[i, xp_ref.shape[1] - 1:, :], 0.0)
        st = [st_ref[i, m] for m in range(st_ref.shape[1])]
        chunks.append(_rwkv_chunk(x_ref[i], prev, st, *param_refs))
    def advance(gen, until_result):
        item = next(gen)
        return (isinstance(item, tuple), item) if until_result else (item == "elementwise done", item)

    while not advance(chunks[0], False)[0]:
        pass
    outs, states = [], []
    for i, chunk in enumerate(chunks):
        following = chunks[i + 1] if i + 1 < len(chunks) else None
        result = None
        while result is None or following is not None:
            if result is None:
                done, item = advance(chunk, True)
                result = item if done else None
            if following is not None and advance(following, False)[0]:
                following = None
        outs.append(result[0].astype(o_ref.dtype))
        states.append(jnp.stack(result[1]))
    o_ref[...] = jnp.stack(outs)
    st_ref[...] = jnp.stack(states)


def _rwkv(rw, vecs, w2, a2, g2, batch, seq):
    width = rw.shape[1]
    L = min(CHUNK, seq)
    nc = seq // L
    nb = RWKV_SEQS if batch % RWKV_SEQS == 0 else 1
    sub = 8
    rw3 = rw.reshape(batch, seq, width)
    vec = lambda wd: pl.BlockSpec((1, wd), lambda b, c: (0, 0))
    mat = lambda m: pl.BlockSpec(m.shape, lambda b, c: (0, 0))
    mu, w0, a0, kk, ka, rk, lng, lnb = vecs
    head = np.arange(2 * LANES) // HEAD_DIM
    ones = jnp.asarray(head[:, None] == head[None, :], BF16)
    out = pl.pallas_call(
        _rwkv_kernel,
        grid=(batch // nb, nc),
        in_specs=[pl.BlockSpec((nb, L, width), lambda b, c: (b, c, 0)),
                  pl.BlockSpec((nb, sub, width), lambda b, c: (b, jnp.maximum(c * (L // sub) - 1, 0), 0)),
                  vec(width), vec(RWKV_WIDTH), mat(w2), vec(RWKV_WIDTH), mat(a2), mat(g2),
                  vec(RWKV_WIDTH), vec(RWKV_WIDTH), vec(RWKV_WIDTH), vec(RWKV_WIDTH), vec(RWKV_WIDTH), mat(ones)],
        out_specs=pl.BlockSpec((nb, L, RWKV_WIDTH), lambda b, c: (b, c, 0)),
        out_shape=jax.ShapeDtypeStruct((batch, seq, RWKV_WIDTH), BF16),
        scratch_shapes=[pltpu.VMEM((nb, RWKV_WIDTH // LANES, LANES, LANES), F32)],
        compiler_params=_params("parallel", "arbitrary"),
        name="rwkv7",
    )(rw3, rw3, mu, w0, w2, a0, a2, g2, kk, ka, rk, lng, lnb, ones)
    return out.reshape(batch * seq, RWKV_WIDTH)


def _merge_kernel(x_ref, oa_ref, ob_ref, gab_ref, wpa_ref, wpb_ref, wo_ref, o_ref):
    d = x_ref.shape[1]
    pa = jnp.dot(oa_ref[...], wpa_ref[...], preferred_element_type=F32)
    pb = jnp.dot(ob_ref[...], wpb_ref[...], preferred_element_type=F32)
    gab = gab_ref[...].astype(F32)
    merged = jax.nn.sigmoid(gab[:, :d]) * pa + jax.nn.sigmoid(gab[:, d:]) * pb
    o_ref[...] = x_ref[...] + _mm(merged, wo_ref[...])


def _merge(x2, oa, ob, gab, wpa, wpb, wo, tm):
    n, d = x2.shape
    row = lambda wd: pl.BlockSpec((tm, wd), lambda i: (i, 0))
    mat = lambda m: pl.BlockSpec(m.shape, lambda i: (0, 0))
    return pl.pallas_call(
        _merge_kernel,
        grid=(n // tm,),
        in_specs=[row(d), row(oa.shape[1]), row(ob.shape[1]), row(gab.shape[1]), mat(wpa), mat(wpb), mat(wo)],
        out_specs=row(d),
        out_shape=jax.ShapeDtypeStruct((n, d), F32),
        compiler_params=_params("parallel"),
        name="merge",
    )(x2, oa, ob, gab, wpa, wpb, wo)


def _ffn_kernel(seq, x_ref, xh_ref, g_ref, wv_ref, wg_ref, cwv_ref, cwg_ref, cbv_ref, cbg_ref, wd_ref, o_ref,
                h_ref, uv_ref, ug_ref):
    tm = x_ref.shape[0]
    i = pl.program_id(0)
    f = pl.program_id(1)

    @pl.when(f == 0)
    def _():
        first = lax.rem(i * tm, seq) == 0
        halo = jnp.where(first, 0.0, _rms(xh_ref[...], g_ref[...]))
        h_ref[0:HALO, :] = halo.astype(BF16)
        h_ref[HALO:, :] = _rms(x_ref[...], g_ref[...]).astype(BF16)

    h = h_ref[...]
    uv_ref[...] = jnp.dot(h, wv_ref[...], preferred_element_type=F32)
    ug_ref[...] = jnp.dot(h, wg_ref[...], preferred_element_type=F32)

    def conv(u_ref, cw_ref, cb_ref):
        acc = cb_ref[...] + cw_ref[0:1, :] * u_ref[pl.ds(HALO - 2, tm), :]
        acc = acc + cw_ref[1:2, :] * u_ref[pl.ds(HALO - 1, tm), :]
        return acc + cw_ref[2:3, :] * u_ref[pl.ds(HALO, tm), :]

    val = conv(uv_ref, cwv_ref, cbv_ref)
    gt = conv(ug_ref, cwg_ref, cbg_ref)
    y = _mm(gt * jax.nn.sigmoid(gt) * val, wd_ref[...])

    @pl.when(f == 0)
    def _():
        o_ref[...] = x_ref[...] + y

    @pl.when(f > 0)
    def _():
        o_ref[...] = o_ref[...] + y


def _ffn(x1, g, w_up, conv_w, conv_b, w_down, seq, tm, tf):
    n, d = x1.shape
    dff = w_down.shape[0]
    nf = dff // tf
    return pl.pallas_call(
        functools.partial(_ffn_kernel, seq),
        grid=(n // tm, nf),
        in_specs=[pl.BlockSpec((tm, d), lambda i, f: (i, 0)),
                  pl.BlockSpec((HALO, d), lambda i, f: (jnp.maximum(i * (tm // HALO) - 1, 0), 0)),
                  pl.BlockSpec((1, d), lambda i, f: (0, 0)),
                  pl.BlockSpec((d, tf), lambda i, f: (0, f)),
                  pl.BlockSpec((d, tf), lambda i, f: (0, nf + f)),
                  pl.BlockSpec((CONV_WIDTH, tf), lambda i, f: (0, f)),
                  pl.BlockSpec((CONV_WIDTH, tf), lambda i, f: (0, nf + f)),
                  pl.BlockSpec((1, tf), lambda i, f: (0, f)),
                  pl.BlockSpec((1, tf), lambda i, f: (0, nf + f)),
                  pl.BlockSpec((tf, d), lambda i, f: (f, 0))],
        out_specs=pl.BlockSpec((tm, d), lambda i, f: (i, 0)),
        out_shape=jax.ShapeDtypeStruct((n, d), F32),
        scratch_shapes=[pltpu.VMEM((tm + HALO, d), BF16),
                        pltpu.VMEM((tm + HALO, tf), F32),
                        pltpu.VMEM((tm + HALO, tf), F32)],
        compiler_params=_params("parallel", "arbitrary"),
        name="convffn",
    )(x1, x1, g, w_up, w_up, conv_w, conv_w, conv_b, conv_b, w_down)


def _t5_bucket(dist):
    n = np.maximum(dist, 0)
    max_exact = REL_BUCKETS // 2
    ratio = np.log(np.maximum(n, 1).astype(np.float32) / max_exact) / math.log(REL_MAX_DIST / max_exact)
    large = np.minimum(max_exact + (ratio * (REL_BUCKETS - max_exact)).astype(np.int32), REL_BUCKETS - 1)
    return np.where(n < max_exact, n, large).astype(np.int32)


def _bias_tables(rel_bias, seq, ncmp):
    tq, tk = Q_TILE, K_TILE
    G, R = NSA_KV_GROUPS, NSA_GROUP
    nq = seq // tq
    step = tq // CMP_STRIDE
    tab = rel_bias.astype(F32)

    off = (nq - 1) * step
    d_c = np.arange(tq)[None, :] - (np.arange(ncmp + off)[:, None] - off) * CMP_STRIDE - (CMP_BLOCK - 1)
    n_b = -(-REL_MAX_DIST // tk)
    d0 = np.arange(tq)[None, :] - np.arange(tk)[:, None]
    d_n = np.stack([d0 - o * tk for o in range(-n_b, tq // tk)])

    buckets = np.concatenate([_t5_bucket(d_c).reshape(-1), _t5_bucket(d_n).reshape(-1)])
    onehot = (jnp.asarray(buckets)[:, None] == jnp.arange(REL_BUCKETS, dtype=jnp.int32)[None, :]).astype(F32)
    vals = jnp.dot(onehot, tab * LOG2E, precision=lax.Precision.HIGHEST, preferred_element_type=F32).T
    by_group = lambda x: x.reshape(G, R, -1, tq).transpose(0, 2, 1, 3).reshape(G, -1, R * tq)
    tile_r = lambda m: np.tile(m, (1, R))

    base = jnp.where(jnp.asarray(tile_r(d_c >= 0)), by_group(vals[:, :d_c.size]), NEG_INF)
    bias_c = jnp.stack([base[:, off - i * step:off - i * step + ncmp] for i in range(nq)])

    far = jnp.repeat(tab[REL_BUCKETS - 1] * LOG2E, tq).reshape(G, 1, R * tq)
    near = by_group(vals[:, d_c.size:]) - far
    near = jnp.where(jnp.asarray(tile_r(d_n.reshape(-1, tq) >= 0)), near, NEG_INF)
    bias_near = near.reshape(G, d_n.shape[0], tk, R * tq).transpose(1, 0, 2, 3)
    win = np.stack([tile_r(np.where(d0 - o * tk < WINDOW, 0.0, NEG_INF)) for o in _win_masked(tq, tk)])
    return bias_c, bias_near, jnp.asarray(win.astype(np.float32))


def _overlap_matrix(ncmp_pad, n_slc, n_cmp):
    m = np.zeros((ncmp_pad, n_slc), np.float32)
    ratio = SLC_BLOCK // CMP_STRIDE
    for j in range(n_slc):
        for o, wgt in enumerate(OVERLAP_W):
            cidx = ratio * j + o - (CMP_BLOCK // CMP_STRIDE - 1)
            if 0 <= cidx < n_cmp:
                m[cidx, j] += wgt
    return jnp.asarray(m.T)


def _layer(x, attn_norm_g, w_in, rel_bias, q_norm_g, k_norm_g, cmp_pe_k, cmp_w1_k, cmp_w2_k,
           cmp_pe_v, cmp_w1_v, cmp_w2_v, rwkv_mu, rwkv_w0, rwkv_w2, rwkv_a0, rwkv_a2, rwkv_g2,
           rwkv_k_k, rwkv_k_a, rwkv_r_k, rwkv_ln_g, rwkv_ln_b, w_proj_a, w_proj_b, w_out,
           ffn_norm_g, w_up, conv_w, conv_b, w_down):
    batch, seq, d = x.shape
    n = batch * seq
    assert seq % Q_TILE == 0 and Q_TILE % K_TILE == 0 and WINDOW % K_TILE == 0
    assert seq % CMP_STRIDE == 0
    assert seq // SLC_BLOCK <= LANES - HEAD_DIM
    x2 = x.reshape(n, d)
    row = lambda a: a.reshape(1, -1).astype(F32)

    kvw = 6 * NSA_KV_WIDTH
    o_q, o_kv, o_gn = 0, NSA_WIDTH, NSA_WIDTH + kvw
    o_rw = o_gn + 3 * NSA_HEADS
    o_gab = o_rw + RWKV_IN_WIDTH
    w16 = w_in.astype(BF16)
    gn_pad = jnp.zeros((d, LANES - 3 * NSA_HEADS), BF16)
    w_cat = jnp.concatenate([w16[:, o_rw:o_gab], w16[:, o_q:o_kv], w16[:, o_kv:o_gn], w16[:, o_gab:],
                             w16[:, o_gn:o_rw], gn_pad], axis=1)
    widths = (RWKV_IN_WIDTH, NSA_WIDTH, 2 * NSA_KV_WIDTH, 2 * d, LANES)
    dtypes = (F32, BF16, F32, BF16, F32)
    gk = row(k_norm_g)
    rw, q, kv_cmp, gab, gn, ksx, kwn, vst, vwt = _inproj(
        x2, row(attn_norm_g), w_cat, jnp.concatenate([gk, gk], axis=1), widths, dtypes, seq, tm=512)

    nchunk = seq // CMP_STRIDE
    n_cmp = nchunk - CMP_BLOCK // CMP_STRIDE + 1
    pe = jnp.stack([cmp_pe_k, cmp_pe_v]).astype(F32)
    w1 = jnp.stack([cmp_w1_k, cmp_w1_v]).astype(BF16).reshape(2, CMP_BLOCK, HEAD_DIM, CMP_HIDDEN)
    w2 = jnp.stack([cmp_w2_k, cmp_w2_v]).astype(BF16)
    kc, vc = _compress(kv_cmp, pe, w1, w2, gk, batch, seq)

    bias_c, bias_near, win_mask = _bias_tables(rel_bias, seq, nchunk)
    wov = _overlap_matrix(nchunk, seq // SLC_BLOCK, n_cmp)
    o_a = _attention(q, gn, kc, jnp.swapaxes(vc, -1, -2), ksx, vst, kwn, vwt, bias_c, bias_near, win_mask,
                     q_norm_g.reshape(-1, 1).astype(F32), wov, batch, seq)

    vecs = tuple(row(a) for a in (rwkv_mu, rwkv_w0, rwkv_a0, rwkv_k_k, rwkv_k_a, rwkv_r_k, rwkv_ln_g, rwkv_ln_b))
    o_b = _rwkv(rw, vecs, rwkv_w2.astype(BF16), rwkv_a2.astype(BF16), rwkv_g2.astype(BF16), batch, seq)

    x1 = _merge(x2, o_a, o_b, gab, w_proj_a.astype(BF16), w_proj_b.astype(BF16), w_out.astype(BF16), tm=512)
    dff = w_down.shape[0]
    out = _ffn(x1, row(ffn_norm_g), w_up.astype(BF16), conv_w.astype(F32), row(conv_b), w_down.astype(BF16),
               seq, tm=1024, tf=dff // 2)
    return out.reshape(batch, seq, d)


def kernel(x, attn_norm_g, w_in, rel_bias, q_norm_g, k_norm_g, cmp_pe_k, cmp_w1_k, cmp_w2_k, cmp_pe_v, cmp_w1_v,
           cmp_w2_v, rwkv_mu, rwkv_w0, rwkv_w2, rwkv_a0, rwkv_a2, rwkv_g2, rwkv_k_k, rwkv_k_a, rwkv_r_k,
           rwkv_ln_g, rwkv_ln_b, w_proj_a, w_proj_b, w_out, ffn_norm_g, w_up, conv_w, conv_b, w_down):
    per_layer = (attn_norm_g, w_in, None, q_norm_g, k_norm_g, cmp_pe_k, cmp_w1_k, cmp_w2_k, cmp_pe_v, cmp_w1_v,
                 cmp_w2_v, rwkv_mu, rwkv_w0, rwkv_w2, rwkv_a0, rwkv_a2, rwkv_g2, rwkv_k_k, rwkv_k_a, rwkv_r_k,
                 rwkv_ln_g, rwkv_ln_b, w_proj_a, w_proj_b, w_out, ffn_norm_g, w_up, conv_w, conv_b, w_down)
    for l in range(attn_norm_g.shape[0]):
        args = [rel_bias if p is None else p[l] for p in per_layer]
        x = _layer(x, *args)
    return x
```

```python
import functools
import math

import numpy as np
import jax
import jax.numpy as jnp
from jax import lax
from jax.experimental import pallas as pl
from jax.experimental.pallas import tpu as pltpu

F32 = jnp.float32
BF16 = jnp.bfloat16

HEAD_DIM = 64
NSA_HEADS = 8
NSA_KV_GROUPS = 2
NSA_GROUP = NSA_HEADS // NSA_KV_GROUPS
NSA_WIDTH = NSA_HEADS * HEAD_DIM
NSA_KV_WIDTH = NSA_KV_GROUPS * HEAD_DIM
CMP_BLOCK = 32
CMP_STRIDE = 16
CMP_HIDDEN = 256
SLC_BLOCK = 64
SLC_TOPK = 16
OVERLAP_W = (1, 2, 2, 2, 1)
WINDOW = 512
REL_BUCKETS = 32
REL_MAX_DIST = 128
RWKV_HEADS = 8
RWKV_WIDTH = RWKV_HEADS * HEAD_DIM
LORA_W = 64
LORA_A = 64
LORA_G = 128
RWKV_IN_WIDTH = 3 * RWKV_WIDTH + LORA_W + LORA_A + LORA_G
GN_EPS = 64e-5
CONV_WIDTH = 3
RMS_EPS = 1e-6
NEG_INF = -1e30
FORCE = 1e9
LOG2E = math.log2(math.e)
M_FLOOR = -1e20

LANES = 128
VMEM_LIMIT = 56 * 1024 * 1024

Q_TILE = 256
K_TILE = 256
VT_ROWS = 80
CHUNK = 128
RWKV_SEQS = 4
HALO = 16
PROJ_ROWS = 512
FFN_ROWS = 1024
FFN_COL_SPLITS = 2

_NT = (((1,), (1,)), ((), ()))


def _params(*sem):
    return pltpu.CompilerParams(dimension_semantics=sem, vmem_limit_bytes=VMEM_LIMIT)


def _mm(a, b):
    return jnp.dot(a.astype(BF16), b.astype(BF16), preferred_element_type=F32)


def _mm_f32(a, b):
    return jnp.dot(a, b, preferred_element_type=F32, precision=lax.Precision.HIGHEST)


def _rms(x, g):
    return x * lax.rsqrt(jnp.mean(x * x, axis=-1, keepdims=True) + RMS_EPS) * g


def _kv_prep(kv, gk, row0, ksx_ref, kwn_ref, vs_ref, vw_ref):
    tm = kv.shape[0]
    lane = lax.broadcasted_iota(jnp.int32, (tm, LANES), 1)
    lo = lane < HEAD_DIM

    def norm2(x):
        x2 = x * x
        s_lo = jnp.sum(jnp.where(lo, x2, 0.0), axis=-1, keepdims=True)
        s_hi = jnp.sum(jnp.where(lo, 0.0, x2), axis=-1, keepdims=True)
        ms = jnp.where(lo, s_lo, s_hi) * (1.0 / HEAD_DIM)
        return x * lax.rsqrt(ms + RMS_EPS) * gk

    ks = norm2(kv[:, 0:LANES])
    kw = norm2(kv[:, 2 * LANES:3 * LANES])
    row = row0 + lax.broadcasted_iota(jnp.int32, (tm, LANES), 0)
    onehot = jnp.where(lane - HEAD_DIM == row // SLC_BLOCK, 1.0, 0.0)
    ksx_ref[0] = jnp.where(lo, ks, onehot).astype(BF16)
    ksx_ref[1] = jnp.where(lo, pltpu.roll(ks, HEAD_DIM, axis=1), onehot).astype(BF16)
    flag_col = jnp.where(lane == HEAD_DIM, 1.0, 0.0)
    kwn_ref[0] = jnp.where(lo, kw, flag_col).astype(BF16)
    kwn_ref[1] = jnp.where(lo, pltpu.roll(kw, HEAD_DIM, axis=1), flag_col).astype(BF16)
    tail = jnp.where(lax.broadcasted_iota(jnp.int32, (VT_ROWS - HEAD_DIM, K_TILE), 0) == 0, 1.0, 0.0)
    for col, ref in ((1, vs_ref), (3, vw_ref)):
        for t in range(tm // K_TILE):
            vt = kv[t * K_TILE:(t + 1) * K_TILE, col * LANES:(col + 1) * LANES].T
            for g in range(NSA_KV_GROUPS):
                ref[g, t] = jnp.concatenate([vt[g * HEAD_DIM:(g + 1) * HEAD_DIM], tail], axis=0).astype(BF16)


def _inproj_kernel(seq, x_ref, g_ref, w_ref, gk_ref, rw_ref, q_ref, kvc_ref, gab_ref, gn_ref,
                   ksx_ref, kwn_ref, vs_ref, vw_ref):
    tm = x_ref.shape[0]
    h = _rms(x_ref[...], g_ref[...]).astype(BF16)
    off = 0
    for ref in (rw_ref, q_ref, kvc_ref, None, gab_ref, gn_ref):
        n = 4 * NSA_KV_WIDTH if ref is None else ref.shape[-1]
        y = jnp.dot(h, w_ref[:, off:off + n], preferred_element_type=F32)
        if ref is None:
            _kv_prep(y, gk_ref[...], lax.rem(pl.program_id(0) * tm, seq), ksx_ref, kwn_ref, vs_ref, vw_ref)
        else:
            ref[...] = y.astype(ref.dtype)
        off += n


def _inproj(x2, g, w, gk2, widths, dtypes, seq, tm):
    n, d = x2.shape
    vt_spec = lambda: pl.BlockSpec((2, tm // K_TILE, VT_ROWS, K_TILE), lambda i: (0, i, 0, 0))
    vt_shape = jax.ShapeDtypeStruct((2, n // K_TILE, VT_ROWS, K_TILE), BF16)
    key_spec = lambda: pl.BlockSpec((2, tm, LANES), lambda i: (0, i, 0))
    key_shape = jax.ShapeDtypeStruct((2, n, LANES), BF16)
    return pl.pallas_call(
        functools.partial(_inproj_kernel, seq),
        grid=(n // tm,),
        in_specs=[pl.BlockSpec((tm, d), lambda i: (i, 0)),
                  pl.BlockSpec((1, d), lambda i: (0, 0)),
                  pl.BlockSpec(w.shape, lambda i: (0, 0), pipeline_mode=pl.Buffered(1)),
                  pl.BlockSpec((1, LANES), lambda i: (0, 0))],
        out_specs=[pl.BlockSpec((tm, wd), lambda i: (i, 0)) for wd in widths]
        + [key_spec(), key_spec(), vt_spec(), vt_spec()],
        out_shape=[jax.ShapeDtypeStruct((n, wd), dt) for wd, dt in zip(widths, dtypes)]
        + [key_shape, key_shape, vt_shape, vt_shape],
        compiler_params=_params("parallel"),
        name="inproj",
    )(x2, g, w, gk2)


def _compress_kernel(k_ref, v_ref, pe_ref, w1_ref, w2_ref, gk_ref, kc_ref, vc_ref):
    nchunk = k_ref.shape[0] // CMP_STRIDE
    span = CMP_BLOCK // CMP_STRIDE
    G = NSA_KV_GROUPS
    acc = [[[jnp.zeros((nchunk, CMP_HIDDEN), F32) for _ in range(span)] for _ in range(G)] for _ in range(2)]
    for p in range(CMP_STRIDE):
        for s, ref in enumerate((k_ref, v_ref)):
            x = ref[pl.ds(p, nchunk, stride=CMP_STRIDE), :]
            for g in range(G):
                xs = x[:, g * HEAD_DIM:(g + 1) * HEAD_DIM]
                for half in range(span):
                    q = half * CMP_STRIDE + p
                    acc[s][g][half] = acc[s][g][half] + _mm(xs + pe_ref[s, q:q + 1, :], w1_ref[s, q])
    for s, out_ref in ((0, kc_ref), (1, vc_ref)):
        for g in range(G):
            hid = acc[s][g][0]
            for half in range(1, span):
                hid = hid + pltpu.roll(acc[s][g][half], nchunk - half, axis=0)
            out = _mm(jax.nn.gelu(hid), w2_ref[s])
            out_ref[0, g] = _rms(out, gk_ref[...]) if s == 0 else out


def _compress(kv, pe, w1, w2, gk, batch, seq):
    nchunk = seq // CMP_STRIDE
    out_spec = lambda: pl.BlockSpec((1, NSA_KV_GROUPS, nchunk, HEAD_DIM), lambda b: (b, 0, 0, 0))
    out_shape = jax.ShapeDtypeStruct((batch, NSA_KV_GROUPS, nchunk, HEAD_DIM), F32)
    const = lambda a: pl.BlockSpec(a.shape, lambda b: (0,) * a.ndim)
    return pl.pallas_call(
        _compress_kernel,
        grid=(batch,),
        in_specs=[pl.BlockSpec((seq, NSA_KV_WIDTH), lambda b: (b, 0)), pl.BlockSpec((seq, NSA_KV_WIDTH), lambda b: (b, 1)),
                  const(pe), const(w1), const(w2), const(gk)],
        out_specs=[out_spec(), out_spec()],
        out_shape=[out_shape, out_shape],
        compiler_params=_params("parallel"),
        name="compress",
    )(kv, kv, pe, w1, w2, gk)


def _rank_select(scores, cur_t, n_live, rank_ref):
    nb, t = scores[0].shape
    sub = 8
    groups = range(nb // sub)
    rank_ref[...] = jnp.zeros(rank_ref.shape, F32)
    below = lax.broadcasted_iota(jnp.int32, (sub, t), 0)
    for gj in groups:
        @pl.when(gj * sub < n_live)
        def _():
            for n, x in enumerate(scores):
                xs = [x[g * sub:(g + 1) * sub] for g in groups]
                inc = [jnp.zeros((sub, t), F32) for _ in groups]
                for jl in range(sub):
                    j = gj * sub + jl
                    row = jnp.broadcast_to(x[j:j + 1, :], (sub, t))
                    for g in groups:
                        if g > gj:
                            one = jnp.where(row >= xs[g], 1.0, 0.0)
                        elif g < gj:
                            one = jnp.where(row > xs[g], 1.0, 0.0)
                        else:
                            one = jnp.where(below > jl, jnp.where(row >= xs[g], 1.0, 0.0),
                                            jnp.where(row > xs[g], 1.0, 0.0))
                        inc[g] = inc[g] + one
                rank_ref[n] = rank_ref[n] + jnp.concatenate(inc, axis=0)
    jj = lax.broadcasted_iota(jnp.int32, (nb, t), 0)
    live = jj <= cur_t
    return [jnp.where(live, jnp.where(rank_ref[n] < float(min(SLC_TOPK, nb)), 0.0, NEG_INF), NEG_INF)
            for n in range(len(scores))]


def _win_masked(tq, tk):
    return [o for o in range(-(WINDOW // tk), 0) if tq - 1 - o * tk >= WINDOW]


def _scores(ks, ws):
    return [jnp.dot(k, w, preferred_element_type=F32) for k, w in zip(ks, ws)]


def _softmax_update(m_ref, acc_ref, slots, s, vts):
    m_old = [m_ref[i] for i in slots]
    ncol = s[0].shape[1] // LANES
    m_cols = [[] for _ in slots]
    p_cols = [[] for _ in slots]
    for c in range(ncol):
        cols = slice(c * LANES, (c + 1) * LANES)
        for n, x in enumerate(s):
            xc = x[:, cols]
            mc = jnp.maximum(m_old[n][:, cols], jnp.max(xc, axis=0, keepdims=True))
            m_cols[n].append(mc)
            p_cols[n].append(jnp.exp2(xc - mc).astype(BF16))
    m_new = [jnp.concatenate(mc, axis=1) for mc in m_cols]
    pv = [jnp.dot(vt, jnp.concatenate(pc, axis=1), preferred_element_type=F32) for vt, pc in zip(vts, p_cols)]
    acc = [jnp.exp2(mo - mn) * acc_ref[i] + z for mo, mn, i, z in zip(m_old, m_new, slots, pv)]
    for i, mn, ac in zip(slots, m_new, acc):
        m_ref[i] = mn
        acc_ref[i] = ac


def _attn_kernel(q_ref, gn_ref, kc_ref, vct_ref, ksx_ref, vst_ref, kwn_ref, vwt_ref, bc_ref, bn_ref, wm_ref,
                 gq_ref, wov_ref, o_ref, m_ref, acc_ref, sa_ref, sb_ref, rank_ref):
    tq = q_ref.shape[0]
    tk = K_TILE
    nblk = wov_ref.shape[0]
    G, R = NSA_KV_GROUPS, NSA_GROUP
    qt = pl.program_id(1)
    n_q = tq // tk
    n_b = -(-REL_MAX_DIST // tk)
    j0 = qt * n_q
    qT = q_ref[...].astype(F32).T
    gates = jax.nn.sigmoid(gn_ref[...].T)
    gq = gq_ref[...]

    qn, qx, negs, o_cmp, scores = [], [], [], [], []
    ncmp = kc_ref.shape[2]
    step = tq // CMP_STRIDE
    bc_rows = pl.ds(pl.multiple_of(bc_ref.shape[1] - ncmp - qt * step, 8), ncmp)
    for g in range(G):
        xs = []
        for h in range(g * R, (g + 1) * R):
            x = qT[h * HEAD_DIM:(h + 1) * HEAD_DIM]
            x = x * lax.rsqrt(jnp.mean(x * x, axis=0, keepdims=True) + RMS_EPS) * gq * (HEAD_DIM ** -0.5 * LOG2E)
            xs.append(x.astype(BF16))
        qn.append(jnp.concatenate(xs, axis=1))

    for g in range(G):
        s = jnp.dot(kc_ref[0, g].astype(BF16), qn[g], preferred_element_type=F32)
        p_cols = []
        for c in range(R * tq // LANES):
            cols = slice(c * LANES, (c + 1) * LANES)
            sc = s[:, cols] + bc_ref[g, bc_rows, cols]
            e = jnp.exp2(sc - jnp.maximum(jnp.max(sc, axis=0, keepdims=True), M_FLOOR))
            p_cols.append(e * (1.0 / jnp.maximum(jnp.sum(e, axis=0, keepdims=True), 1e-30)))
        o_cmp.append(jnp.dot(vct_ref[0, g].astype(BF16), jnp.concatenate(p_cols, axis=1).astype(BF16),
                             preferred_element_type=F32))
        per_head = tq // LANES
        psum = jnp.concatenate([sum(p_cols[r * per_head + t] for r in range(R)) for t in range(per_head)], axis=1)
        imp = _mm_f32(wov_ref[...], psum)

        blk = lax.broadcasted_iota(jnp.int32, imp.shape, 0)
        cur = (qt * tq + lax.broadcasted_iota(jnp.int32, imp.shape, 1)) // SLC_BLOCK
        forced = (blk == 0) | (blk == cur) | (blk == cur - 1)
        scores.append(jnp.where(forced, FORCE, jnp.where(blk <= cur, imp, -FORCE)))

    cur = (qt * tq + lax.broadcasted_iota(jnp.int32, (nblk, tq), 1)) // SLC_BLOCK
    for g, neg in enumerate(_rank_select(scores, cur, (qt + 1) * (tq // SLC_BLOCK), rank_ref)):
        if nblk < LANES - HEAD_DIM:
            neg = jnp.concatenate([neg, jnp.zeros((LANES - HEAD_DIM - nblk, tq), F32)], axis=0)
        neg = neg.astype(BF16)
        negs.append(neg)
        qx.append(jnp.concatenate([qn[g], jnp.concatenate([neg] * R, axis=1)], axis=0))

    m_ref[...] = jnp.full(m_ref.shape, M_FLOOR, F32)
    acc_ref[...] = jnp.zeros(acc_ref.shape, F32)

    def invalid_before(level):
        return jnp.where(qt >= level, 0.0, NEG_INF)

    def slc_queries(g, level):
        if level == 0:
            return qx[g]
        rows = jnp.minimum(negs[g], invalid_before(level).astype(BF16))
        return jnp.concatenate([qn[g], jnp.concatenate([rows] * R, axis=1)], axis=0)

    def win_queries(g, level):
        flag_row = lax.broadcasted_iota(jnp.int32, (LANES - HEAD_DIM, R * tq), 0) == 0
        extra = jnp.where(flag_row, invalid_before(level), 0.0).astype(BF16)
        return jnp.concatenate([qn[g], extra], axis=0)

    def slc_probs(o, bias):
        j = jnp.maximum(j0 + o, 0)
        sl = pl.ds(pl.multiple_of(j * tk, tk), tk)
        return [(g, ksx_ref[g, sl, :], vst_ref[g, j], slc_queries(g, max(-(o // n_q), 0)), bias.at[g])
                for g in range(G)]

    def win_probs(o, bias, per_group):
        j = jnp.maximum(j0 + o, 0)
        sl = pl.ds(pl.multiple_of(j * tk, tk), tk)
        pick = lambda g: None if bias is None else (bias.at[g] if per_group else bias)
        return [(G + g, kwn_ref[g, sl, :], vwt_ref[g, j], win_queries(g, max(-(o // n_q), 0)), pick(g))
                for g in range(G)]

    n_far = jnp.maximum(j0 - n_b, 0)
    groups = list(range(G))

    def far_scores(j, buf, gs=groups):
        sl = pl.ds(pl.multiple_of(j * tk, tk), tk)
        for g, x in zip(gs, _scores([ksx_ref[g, sl, :] for g in gs], [qx[g] for g in gs])):
            buf[g] = x

    def far_update(j, buf, gs=groups):
        _softmax_update(m_ref, acc_ref, gs, [buf[g] for g in gs], [vst_ref[g, j] for g in gs])

    far_scores(0, sa_ref)

    def far_body(i, carry):
        for g in groups:
            far_scores(2 * i + 1, sb_ref, [g])
            far_update(2 * i, sa_ref, [g])
        for g in groups:
            far_scores(2 * i + 2, sa_ref, [g])
            far_update(2 * i + 1, sb_ref, [g])
        return carry

    lax.fori_loop(0, n_far // 2, far_body, 0)

    @pl.when(n_far % 2 == 1)
    def _():
        far_update(n_far - 1, sa_ref)

    masked = _win_masked(tq, tk)
    rounds = []
    for o in range(-(WINDOW // tk), n_q):
        if o >= -n_b:
            rounds.append(slc_probs(o, bn_ref.at[o + n_b]) + win_probs(o, bn_ref.at[o + n_b], True))
        elif o in masked:
            rounds.append(win_probs(o, wm_ref.at[masked.index(o)], False))
        else:
            rounds.append(win_probs(o, None, False))

    def round_scores(probs):
        sc = _scores([p[1] for p in probs], [p[3] for p in probs])
        return [x if p[4] is None else x + p[4][...] for x, p in zip(sc, probs)]

    sc = round_scores(rounds[0])
    for i, probs in enumerate(rounds):
        sc_next = round_scores(rounds[i + 1]) if i + 1 < len(rounds) else None
        _softmax_update(m_ref, acc_ref, [p[0] for p in probs], sc, [p[2] for p in probs])
        sc = sc_next

    outs = []
    for g in range(G):
        acc = acc_ref[g]
        o_slc = acc[:HEAD_DIM] / acc[HEAD_DIM:HEAD_DIM + 1]
        acc = acc_ref[G + g]
        o_win = acc[:HEAD_DIM] / acc[HEAD_DIM:HEAD_DIM + 1]
        for r in range(R):
            h = g * R + r
            cols = slice(r * tq, (r + 1) * tq)
            outs.append(gates[3 * h:3 * h + 1] * o_cmp[g][:, cols] + gates[3 * h + 1:3 * h + 2] * o_slc[:, cols]
                        + gates[3 * h + 2:3 * h + 3] * o_win[:, cols])
    o_ref[...] = jnp.concatenate(outs, axis=0).T.astype(o_ref.dtype)


def _attention(q, gn, kc, vct, ksx, vst, kwn, vwt, bias_c, bias_near, win_mask, gq, wov, batch, seq):
    n = q.shape[0]
    tq = Q_TILE
    nq = seq // tq
    ncmp = kc.shape[2]
    row = lambda b, i: (b * nq + i, 0)
    whole = lambda b, i: (0, b, 0)
    tiles = lambda b, i: (0, b, 0, 0)
    const = lambda a: pl.BlockSpec(a.shape, lambda b, i: (0,) * a.ndim, pipeline_mode=pl.Buffered(1))
    return pl.pallas_call(
        _attn_kernel,
        grid=(batch, nq),
        in_specs=[pl.BlockSpec((tq, NSA_WIDTH), row),
                  pl.BlockSpec((tq, LANES), row),
                  pl.BlockSpec((1, NSA_KV_GROUPS, ncmp, HEAD_DIM), lambda b, i: (b, 0, 0, 0)),
                  pl.BlockSpec((1, NSA_KV_GROUPS, HEAD_DIM, ncmp), lambda b, i: (b, 0, 0, 0)),
                  pl.BlockSpec((2, seq, LANES), whole),
                  pl.BlockSpec((2, seq // K_TILE, VT_ROWS, K_TILE), tiles),
                  pl.BlockSpec((2, seq, LANES), whole),
                  pl.BlockSpec((2, seq // K_TILE, VT_ROWS, K_TILE), tiles),
                  const(bias_c),
                  const(bias_near), const(win_mask), const(gq), const(wov)],
        out_specs=pl.BlockSpec((tq, NSA_WIDTH), row),
        out_shape=jax.ShapeDtypeStruct((n, NSA_WIDTH), BF16),
        scratch_shapes=[pltpu.VMEM((2 * NSA_KV_GROUPS, 1, NSA_GROUP * tq), F32),
                        pltpu.VMEM((2 * NSA_KV_GROUPS, VT_ROWS, NSA_GROUP * tq), F32),
                        pltpu.VMEM((NSA_KV_GROUPS, K_TILE, NSA_GROUP * tq), F32),
                        pltpu.VMEM((NSA_KV_GROUPS, K_TILE, NSA_GROUP * tq), F32),
                        pltpu.VMEM((NSA_KV_GROUPS, seq // SLC_BLOCK, tq), F32)],
        compiler_params=_params("parallel", "parallel"),
        name="nsa_attention",
    )(q, gn, kc, vct, ksx, vst, kwn, vwt, bias_c, bias_near, win_mask, gq, wov)


def _split_bf16(z, parts):
    out = []
    for _ in range(parts - 1):
        hi = z.astype(BF16)
        out.append(hi)
        z = z - hi.astype(F32)
    return out + [z.astype(BF16)]


def _head_sums(z, ones):
    width = ones.shape[0]
    nb = z.shape[1] // width
    rows = z.shape[0]
    zb = z.astype(BF16)
    stacked = jnp.concatenate([zb[:, m * width:(m + 1) * width] for m in range(nb)], axis=0)
    sums = jnp.dot(stacked, ones, preferred_element_type=F32)
    return jnp.concatenate([sums[m * rows:(m + 1) * rows] for m in range(nb)], axis=1)


def _softplus(z):
    return jnp.maximum(z, 0.0) + jnp.log(1.0 + jnp.exp(-jnp.abs(z)))


def _rwkv_chunk(x, prev, st, mu_ref, w0_ref, w2_ref, a0_ref, a2_ref, g2_ref, kk_ref, ka_ref, rk_ref,
                lng_ref, lnb_ref, ones_ref):
    L = x.shape[0]
    W = RWKV_WIDTH
    N = HEAD_DIM
    row_id = lax.broadcasted_iota(jnp.int32, x.shape, 0)
    shifted = jnp.where(row_id == 0, prev, pltpu.roll(x, 1, axis=0))
    xl = x + (shifted - x) * mu_ref[...]
    yield None
    r = xl[:, 0:W]
    k = xl[:, W:2 * W]
    v = xl[:, 2 * W:3 * W]
    xw = xl[:, 3 * W:3 * W + LORA_W]
    xa = xl[:, 3 * W + LORA_W:3 * W + LORA_W + LORA_A]
    xg = xl[:, 3 * W + LORA_W + LORA_A:]
    w = -_softplus(-(w0_ref[...] + _mm(jnp.tanh(xw), w2_ref[...]))) - 0.5
    ld = -jnp.exp(w)
    a = jax.nn.sigmoid(a0_ref[...] + _mm(xa, a2_ref[...]))
    gate = _mm(jax.nn.sigmoid(xg), g2_ref[...])
    kkv = k * kk_ref[...]
    k2 = k * (1.0 + (a - 1.0) * ka_ref[...])
    yield None

    ti = lax.broadcasted_iota(jnp.int32, (L, L), 0)
    si = lax.broadcasted_iota(jnp.int32, (L, L), 1)
    incl = si <= ti
    strict = si < ti
    tri = jnp.where(incl, 1.0, 0.0).astype(BF16)
    cl3 = jnp.dot(tri, jnp.concatenate(_split_bf16(ld, 3), axis=1), preferred_element_type=F32)
    cl = cl3[:, :W] + (cl3[:, W:2 * W] + cl3[:, 2 * W:])
    cl_end = cl[L - 1:L, :]
    yield None
    e_pos = jnp.exp(cl)
    e_neg = jnp.exp(-cl)
    e_prev = jnp.exp(cl - ld)
    e_end = jnp.exp(cl_end - cl)
    eye = jnp.where(ti == si, 1.0, 0.0)
    yield None

    hsum = lambda z: _head_sums(z, ones_ref[...])

    kk_n = kkv * lax.rsqrt(jnp.maximum(hsum(kkv * kkv), 1e-24))
    bv = kk_n * a
    yield None
    a_t = (-kk_n * e_prev).astype(BF16)
    b_t = (bv * e_neg).astype(BF16)
    k_t = (k2 * e_neg).astype(BF16)
    r_t = (r * e_pos).astype(BF16)
    yield None
    v_b = v.astype(BF16)
    k_e = k2 * e_end
    b_e = bv * e_end

    yield "elementwise done"

    P = range(W // LANES)
    ps = [slice(m * LANES, (m + 1) * LANES) for m in P]
    first = lax.broadcasted_iota(jnp.int32, (1, LANES), 1) < N
    same_head = (lax.broadcasted_iota(jnp.int32, (LANES, LANES), 0) // N
                 == lax.broadcasted_iota(jnp.int32, (LANES, LANES), 1) // N)

    def diag_rows(x):
        zero = jnp.zeros_like(x)
        return jnp.concatenate([jnp.where(first, x, zero), jnp.where(first, zero, x)], axis=0)

    def diag_blocks(x):
        zero = jnp.zeros_like(x[:, :LANES])
        return jnp.concatenate([jnp.concatenate([x[:, :LANES], zero], axis=1),
                                jnp.concatenate([zero, x[:, LANES:]], axis=1)], axis=0)

    ke_t = [k_e[:, s].T for s in ps]
    be_t = [b_e[:, s].T for s in ps]
    p_end = [e_pos[:, s].T[:, L - 1:L] for s in ps]

    lhs = [jnp.concatenate([a_t[:, s], r_t[:, s]], axis=0) for s in ps]
    rhs = [jnp.concatenate([k_t[:, s], b_t[:, s]], axis=0) for s in ps]
    zero_b = jnp.zeros_like(lhs[0])
    aa = [[lax.dot_general(jnp.where(first, x, zero_b) if j == 0 else jnp.where(first, zero_b, x), y, _NT,
                           preferred_element_type=F32) for j in range(2)] for x, y in zip(lhs, rhs)]
    cat2 = lambda f: [jnp.concatenate([f(pair[0]), f(pair[1])], axis=1) for pair in aa]
    a_ak = cat2(lambda x: jnp.where(strict, x[:L, :L], 0.0))
    a_ab = cat2(lambda x: jnp.where(strict, x[:L, L:], 0.0))
    a_rk = cat2(lambda x: jnp.where(incl, x[L:, :L], 0.0))
    a_rb = cat2(lambda x: jnp.where(incl, x[L:, L:], 0.0))
    yield None

    eye2 = jnp.concatenate([eye, eye], axis=1)
    tinv = [eye2 + x for x in a_ab]
    pw = [_mm(x, diag_blocks(x)) for x in a_ab]
    yield None
    span = 2
    while 2 * span < L:
        both = [_mm(jnp.concatenate([t, p], axis=0), diag_blocks(p)) for t, p in zip(tinv, pw)]
        tinv = [t + x[:L] for t, x in zip(tinv, both)]
        pw = [x[L:] for x in both]
        yield None
        span *= 2
    tinv = [t + _mm(t, diag_blocks(p)) for t, p in zip(tinv, pw)]
    yield None

    v_d = [diag_rows(v_b[:, s]) for s in ps]
    av = [_mm(jnp.concatenate([a_ak[m], a_rk[m]], axis=0), v_d[m]) for m in P]
    tw = [_mm(tinv[m], jnp.concatenate([diag_rows(a_t[:, ps[m]]), diag_rows(av[m][:L].astype(BF16))], axis=1))
          for m in P]
    kv_loc = [jnp.where(same_head, _mm(ke_t[m], v_b[:, ps[m]]), 0.0) for m in P]
    yield None

    ws = [_mm(jnp.concatenate([tw[m][:, :LANES].astype(BF16), r_t[:, ps[m]]], axis=0), st[m]) for m in P]
    u = [ws[m][:L] + tw[m][:, LANES:] for m in P]
    yield None
    y = [ws[m][L:] + av[m][L:] + _mm(a_rb[m], diag_rows(u[m])) for m in P]
    st_new = [p_end[m] * st[m] + kv_loc[m] + jnp.where(same_head, _mm(be_t[m], u[m]), 0.0) for m in P]
    yield None

    y = jnp.concatenate(y, axis=1)
    d = y - hsum(y) * (1.0 / N)
    yn = d * lax.rsqrt(hsum(d * d) * (1.0 / N) + GN_EPS) * lng_ref[...] + lnb_ref[...]
    bonus = hsum(r * k2 * rk_ref[...])
    yield (yn + bonus * v) * gate, st_new


def _rwkv_kernel(x_ref, xp_ref, *refs):
    *param_refs, o_ref, st_ref = refs
    c = pl.program_id(1)

    @pl.when(c == 0)
    def _():
        st_ref[...] = jnp.zeros(st_ref.shape, F32)

    chunks = []
    for i in range(x_ref.shape[0]):
        prev = jnp.where(c > 0, xp_ref[i, xp_ref.shape[1] - 1:, :], 0.0)
        st = [st_ref[i, m] for m in range(st_ref.shape[1])]
        chunks.append(_rwkv_chunk(x_ref[i], prev, st, *param_refs))
    def advance(gen, until_result):
        item = next(gen)
        return (isinstance(item, tuple), item) if until_result else (item == "elementwise done", item)

    while not advance(chunks[0], False)[0]:
        pass
    outs, states = [], []
    for i, chunk in enumerate(chunks):
        following = chunks[i + 1] if i + 1 < len(chunks) else None
        result = None
        while result is None or following is not None:
            if result is None:
                done, item = advance(chunk, True)
                result = item if done else None
            if following is not None and advance(following, False)[0]:
                following = None
        outs.append(result[0].astype(o_ref.dtype))
        states.append(jnp.stack(result[1]))
    o_ref[...] = jnp.stack(outs)
    st_ref[...] = jnp.stack(states)


def _rwkv(rw, vecs, w2, a2, g2, batch, seq):
    width = rw.shape[1]
    L = min(CHUNK, seq)
    nc = seq // L
    nb = RWKV_SEQS if batch % RWKV_SEQS == 0 else 1
    sub = 8
    rw3 = rw.reshape(batch, seq, width)
    vec = lambda wd: pl.BlockSpec((1, wd), lambda b, c: (0, 0))
    mat = lambda m: pl.BlockSpec(m.shape, lambda b, c: (0, 0))
    mu, w0, a0, kk, ka, rk, lng, lnb = vecs
    head = np.arange(2 * LANES) // HEAD_DIM
    ones = jnp.asarray(head[:, None] == head[None, :], BF16)
    out = pl.pallas_call(
        _rwkv_kernel,
        grid=(batch // nb, nc),
        in_specs=[pl.BlockSpec((nb, L, width), lambda b, c: (b, c, 0)),
                  pl.BlockSpec((nb, sub, width), lambda b, c: (b, jnp.maximum(c * (L // sub) - 1, 0), 0)),
                  vec(width), vec(RWKV_WIDTH), mat(w2), vec(RWKV_WIDTH), mat(a2), mat(g2),
                  vec(RWKV_WIDTH), vec(RWKV_WIDTH), vec(RWKV_WIDTH), vec(RWKV_WIDTH), vec(RWKV_WIDTH), mat(ones)],
        out_specs=pl.BlockSpec((nb, L, RWKV_WIDTH), lambda b, c: (b, c, 0)),
        out_shape=jax.ShapeDtypeStruct((batch, seq, RWKV_WIDTH), BF16),
        scratch_shapes=[pltpu.VMEM((nb, RWKV_WIDTH // LANES, LANES, LANES), F32)],
        compiler_params=_params("parallel", "arbitrary"),
        name="rwkv7",
    )(rw3, rw3, mu, w0, w2, a0, a2, g2, kk, ka, rk, lng, lnb, ones)
    return out.reshape(batch * seq, RWKV_WIDTH)


def _merge_kernel(x_ref, oa_ref, ob_ref, gab_ref, wpa_ref, wpb_ref, wo_ref, o_ref):
    d = x_ref.shape[1]
    pa = jnp.dot(oa_ref[...], wpa_ref[...], preferred_element_type=F32)
    pb = jnp.dot(ob_ref[...], wpb_ref[...], preferred_element_type=F32)
    gab = gab_ref[...].astype(F32)
    merged = jax.nn.sigmoid(gab[:, :d]) * pa + jax.nn.sigmoid(gab[:, d:]) * pb
    o_ref[...] = x_ref[...] + _mm(merged, wo_ref[...])


def _merge(x2, oa, ob, gab, wpa, wpb, wo, tm):
    n, d = x2.shape
    row = lambda wd: pl.BlockSpec((tm, wd), lambda i: (i, 0))
    mat = lambda m: pl.BlockSpec(m.shape, lambda i: (0, 0))
    return pl.pallas_call(
        _merge_kernel,
        grid=(n // tm,),
        in_specs=[row(d), row(oa.shape[1]), row(ob.shape[1]), row(gab.shape[1]), mat(wpa), mat(wpb), mat(wo)],
        out_specs=row(d),
        out_shape=jax.ShapeDtypeStruct((n, d), F32),
        compiler_params=_params("parallel"),
        name="merge",
    )(x2, oa, ob, gab, wpa, wpb, wo)


def _ffn_kernel(seq, x_ref, xh_ref, g_ref, wv_ref, wg_ref, cwv_ref, cwg_ref, cbv_ref, cbg_ref, wd_ref, o_ref,
                h_ref, uv_ref, ug_ref):
    tm = x_ref.shape[0]
    i = pl.program_id(0)
    f = pl.program_id(1)

    @pl.when(f == 0)
    def _():
        first = lax.rem(i * tm, seq) == 0
        halo = jnp.where(first, 0.0, _rms(xh_ref[...], g_ref[...]))
        h_ref[0:HALO, :] = halo.astype(BF16)
        h_ref[HALO:, :] = _rms(x_ref[...], g_ref[...]).astype(BF16)

    h = h_ref[...]
    uv_ref[...] = jnp.dot(h, wv_ref[...], preferred_element_type=F32)
    ug_ref[...] = jnp.dot(h, wg_ref[...], preferred_element_type=F32)

    def conv(u_ref, cw_ref, cb_ref):
        acc = cb_ref[...] + cw_ref[0:1, :] * u_ref[pl.ds(HALO - 2, tm), :]
        acc = acc + cw_ref[1:2, :] * u_ref[pl.ds(HALO - 1, tm), :]
        return acc + cw_ref[2:3, :] * u_ref[pl.ds(HALO, tm), :]

    val = conv(uv_ref, cwv_ref, cbv_ref)
    gt = conv(ug_ref, cwg_ref, cbg_ref)
    y = _mm(gt * jax.nn.sigmoid(gt) * val, wd_ref[...])

    @pl.when(f == 0)
    def _():
        o_ref[...] = x_ref[...] + y

    @pl.when(f > 0)
    def _():
        o_ref[...] = o_ref[...] + y


def _ffn(x1, g, w_up, conv_w, conv_b, w_down, seq, tm, tf):
    n, d = x1.shape
    dff = w_down.shape[0]
    nf = dff // tf
    return pl.pallas_call(
        functools.partial(_ffn_kernel, seq),
        grid=(n // tm, nf),
        in_specs=[pl.BlockSpec((tm, d), lambda i, f: (i, 0)),
                  pl.BlockSpec((HALO, d), lambda i, f: (jnp.maximum(i * (tm // HALO) - 1, 0), 0)),
                  pl.BlockSpec((1, d), lambda i, f: (0, 0)),
                  pl.BlockSpec((d, tf), lambda i, f: (0, f)),
                  pl.BlockSpec((d, tf), lambda i, f: (0, nf + f)),
                  pl.BlockSpec((CONV_WIDTH, tf), lambda i, f: (0, f)),
                  pl.BlockSpec((CONV_WIDTH, tf), lambda i, f: (0, nf + f)),
                  pl.BlockSpec((1, tf), lambda i, f: (0, f)),
                  pl.BlockSpec((1, tf), lambda i, f: (0, nf + f)),
                  pl.BlockSpec((tf, d), lambda i, f: (f, 0))],
        out_specs=pl.BlockSpec((tm, d), lambda i, f: (i, 0)),
        out_shape=jax.ShapeDtypeStruct((n, d), F32),
        scratch_shapes=[pltpu.VMEM((tm + HALO, d), BF16),
                        pltpu.VMEM((tm + HALO, tf), F32),
                        pltpu.VMEM((tm + HALO, tf), F32)],
        compiler_params=_params("parallel", "arbitrary"),
        name="convffn",
    )(x1, x1, g, w_up, w_up, conv_w, conv_w, conv_b, conv_b, w_down)


def _t5_bucket(dist):
    n = np.maximum(dist, 0)
    max_exact = REL_BUCKETS // 2
    ratio = np.log(np.maximum(n, 1).astype(np.float32) / max_exact) / math.log(REL_MAX_DIST / max_exact)
    large = np.minimum(max_exact + (ratio * (REL_BUCKETS - max_exact)).astype(np.int32), REL_BUCKETS - 1)
    return np.where(n < max_exact, n, large).astype(np.int32)


def _bias_tables(rel_bias, seq, ncmp):
    tq, tk = Q_TILE, K_TILE
    G, R = NSA_KV_GROUPS, NSA_GROUP
    nq = seq // tq
    step = tq // CMP_STRIDE
    tab = rel_bias.astype(F32)

    off = (nq - 1) * step
    d_c = np.arange(tq)[None, :] - (np.arange(ncmp + off)[:, None] - off) * CMP_STRIDE - (CMP_BLOCK - 1)
    n_b = -(-REL_MAX_DIST // tk)
    d0 = np.arange(tq)[None, :] - np.arange(tk)[:, None]
    d_n = np.stack([d0 - o * tk for o in range(-n_b, tq // tk)])

    buckets = np.concatenate([_t5_bucket(d_c).reshape(-1), _t5_bucket(d_n).reshape(-1)])
    onehot = (jnp.asarray(buckets)[:, None] == jnp.arange(REL_BUCKETS, dtype=jnp.int32)[None, :]).astype(F32)
    vals = jnp.dot(onehot, tab * LOG2E, precision=lax.Precision.HIGHEST, preferred_element_type=F32).T
    by_group = lambda x: x.reshape(G, R, -1, tq).transpose(0, 2, 1, 3).reshape(G, -1, R * tq)
    tile_r = lambda m: np.tile(m, (1, R))

    base = jnp.where(jnp.asarray(tile_r(d_c >= 0)), by_group(vals[:, :d_c.size]), NEG_INF)

    far = jnp.repeat(tab[REL_BUCKETS - 1] * LOG2E, tq).reshape(G, 1, R * tq)
    near = by_group(vals[:, d_c.size:]) - far
    near = jnp.where(jnp.asarray(tile_r(d_n.reshape(-1, tq) >= 0)), near, NEG_INF)
    bias_near = near.reshape(G, d_n.shape[0], tk, R * tq).transpose(1, 0, 2, 3)
    win = np.stack([tile_r(np.where(d0 - o * tk < WINDOW, 0.0, NEG_INF)) for o in _win_masked(tq, tk)])
    return base, bias_near, jnp.asarray(win.astype(np.float32))


def _overlap_matrix(ncmp_pad, n_slc, n_cmp):
    m = np.zeros((ncmp_pad, n_slc), np.float32)
    ratio = SLC_BLOCK // CMP_STRIDE
    for j in range(n_slc):
        for o, wgt in enumerate(OVERLAP_W):
            cidx = ratio * j + o - (CMP_BLOCK // CMP_STRIDE - 1)
            if 0 <= cidx < n_cmp:
                m[cidx, j] += wgt
    return jnp.asarray(m.T)


def _layer(x, attn_norm_g, w_in, rel_bias, q_norm_g, k_norm_g, cmp_pe_k, cmp_w1_k, cmp_w2_k,
           cmp_pe_v, cmp_w1_v, cmp_w2_v, rwkv_mu, rwkv_w0, rwkv_w2, rwkv_a0, rwkv_a2, rwkv_g2,
           rwkv_k_k, rwkv_k_a, rwkv_r_k, rwkv_ln_g, rwkv_ln_b, w_proj_a, w_proj_b, w_out,
           ffn_norm_g, w_up, conv_w, conv_b, w_down):
    batch, seq, d = x.shape
    n = batch * seq
    assert seq % Q_TILE == 0 and Q_TILE % K_TILE == 0 and WINDOW % K_TILE == 0
    assert seq % CMP_STRIDE == 0 and seq % FFN_ROWS == 0 and seq % PROJ_ROWS == 0 and PROJ_ROWS % K_TILE == 0
    assert seq // SLC_BLOCK <= LANES - HEAD_DIM
    x2 = x.reshape(n, d)
    row = lambda a: a.reshape(1, -1).astype(F32)

    kvw = 6 * NSA_KV_WIDTH
    o_q, o_kv, o_gn = 0, NSA_WIDTH, NSA_WIDTH + kvw
    o_rw = o_gn + 3 * NSA_HEADS
    o_gab = o_rw + RWKV_IN_WIDTH
    w16 = w_in.astype(BF16)
    gn_pad = jnp.zeros((d, LANES - 3 * NSA_HEADS), BF16)
    w_cat = jnp.concatenate([w16[:, o_rw:o_gab], w16[:, o_q:o_kv], w16[:, o_kv:o_gn], w16[:, o_gab:],
                             w16[:, o_gn:o_rw], gn_pad], axis=1)
    widths = (RWKV_IN_WIDTH, NSA_WIDTH, 2 * NSA_KV_WIDTH, 2 * d, LANES)
    dtypes = (F32, BF16, F32, BF16, F32)
    gk = row(k_norm_g)
    rw, q, kv_cmp, gab, gn, ksx, kwn, vst, vwt = _inproj(
        x2, row(attn_norm_g), w_cat, jnp.concatenate([gk, gk], axis=1), widths, dtypes, seq, tm=PROJ_ROWS)

    nchunk = seq // CMP_STRIDE
    n_cmp = nchunk - CMP_BLOCK // CMP_STRIDE + 1
    pe = jnp.stack([cmp_pe_k, cmp_pe_v]).astype(F32)
    w1 = jnp.stack([cmp_w1_k, cmp_w1_v]).astype(BF16).reshape(2, CMP_BLOCK, HEAD_DIM, CMP_HIDDEN)
    w2 = jnp.stack([cmp_w2_k, cmp_w2_v]).astype(BF16)
    kc, vc = _compress(kv_cmp, pe, w1, w2, gk, batch, seq)

    bias_c, bias_near, win_mask = _bias_tables(rel_bias, seq, nchunk)
    wov = _overlap_matrix(nchunk, seq // SLC_BLOCK, n_cmp)
    o_a = _attention(q, gn, kc, jnp.swapaxes(vc, -1, -2), ksx, vst, kwn, vwt, bias_c, bias_near, win_mask,
                     q_norm_g.reshape(-1, 1).astype(F32), wov, batch, seq)

    vecs = tuple(row(a) for a in (rwkv_mu, rwkv_w0, rwkv_a0, rwkv_k_k, rwkv_k_a, rwkv_r_k, rwkv_ln_g, rwkv_ln_b))
    o_b = _rwkv(rw, vecs, rwkv_w2.astype(BF16), rwkv_a2.astype(BF16), rwkv_g2.astype(BF16), batch, seq)

    x1 = _merge(x2, o_a, o_b, gab, w_proj_a.astype(BF16), w_proj_b.astype(BF16), w_out.astype(BF16), tm=PROJ_ROWS)
    dff = w_down.shape[0]
    out = _ffn(x1, row(ffn_norm_g), w_up.astype(BF16), conv_w.astype(F32), row(conv_b), w_down.astype(BF16),
               seq, tm=FFN_ROWS, tf=dff // FFN_COL_SPLITS)
    return out.reshape(batch, seq, d)


def kernel(x, attn_norm_g, w_in, rel_bias, q_norm_g, k_norm_g, cmp_pe_k, cmp_w1_k, cmp_w2_k, cmp_pe_v, cmp_w1_v,
           cmp_w2_v, rwkv_mu, rwkv_w0, rwkv_w2, rwkv_a0, rwkv_a2, rwkv_g2, rwkv_k_k, rwkv_k_a, rwkv_r_k,
           rwkv_ln_g, rwkv_ln_b, w_proj_a, w_proj_b, w_out, ffn_norm_g, w_up, conv_w, conv_b, w_down):
    per_layer = (attn_norm_g, w_in, None, q_norm_g, k_norm_g, cmp_pe_k, cmp_w1_k, cmp_w2_k, cmp_pe_v, cmp_w1_v,
                 cmp_w2_v, rwkv_mu, rwkv_w0, rwkv_w2, rwkv_a0, rwkv_a2, rwkv_g2, rwkv_k_k, rwkv_k_a, rwkv_r_k,
                 rwkv_ln_g, rwkv_ln_b, w_proj_a, w_proj_b, w_out, ffn_norm_g, w_up, conv_w, conv_b, w_down)
    for l in range(attn_norm_g.shape[0]):
        args = [rel_bias if p is None else p[l] for p in per_layer]
        x = _layer(x, *args)
    return x
```

```python
import functools
import math

import numpy as np
import jax
import jax.numpy as jnp
from jax import lax
from jax.experimental import pallas as pl
from jax.experimental.pallas import tpu as pltpu

F32 = jnp.float32
BF16 = jnp.bfloat16

HEAD_DIM = 64
NSA_HEADS = 8
NSA_KV_GROUPS = 2
NSA_GROUP = NSA_HEADS // NSA_KV_GROUPS
NSA_WIDTH = NSA_HEADS * HEAD_DIM
NSA_KV_WIDTH = NSA_KV_GROUPS * HEAD_DIM
CMP_BLOCK = 32
CMP_STRIDE = 16
CMP_HIDDEN = 256
SLC_BLOCK = 64
SLC_TOPK = 16
OVERLAP_W = (1, 2, 2, 2, 1)
WINDOW = 512
REL_BUCKETS = 32
REL_MAX_DIST = 128
RWKV_HEADS = 8
RWKV_WIDTH = RWKV_HEADS * HEAD_DIM
LORA_W = 64
LORA_A = 64
LORA_G = 128
RWKV_IN_WIDTH = 3 * RWKV_WIDTH + LORA_W + LORA_A + LORA_G
GN_EPS = 64e-5
CONV_WIDTH = 3
RMS_EPS = 1e-6
NEG_INF = -1e30
FORCE = 1e9
LOG2E = math.log2(math.e)
M_FLOOR = -1e20

LANES = 128
VMEM_LIMIT = 56 * 1024 * 1024

Q_TILE = 256
K_TILE = 256
VT_ROWS = 80
CHUNK = 128
RWKV_SEQS = 4
HALO = 16
PROJ_ROWS = 512
FFN_ROWS = 1024
FFN_COL_SPLITS = 2

_NT = (((1,), (1,)), ((), ()))


def _params(*sem):
    return pltpu.CompilerParams(dimension_semantics=sem, vmem_limit_bytes=VMEM_LIMIT)


def _mm(a, b):
    return jnp.dot(a.astype(BF16), b.astype(BF16), preferred_element_type=F32)


def _mm_f32(a, b):
    return jnp.dot(a, b, preferred_element_type=F32, precision=lax.Precision.HIGHEST)


def _rms(x, g):
    return x * lax.rsqrt(jnp.mean(x * x, axis=-1, keepdims=True) + RMS_EPS) * g


def _kv_prep(kv, gk, row0, ksx_ref, kwn_ref, vs_ref, vw_ref):
    tm = kv.shape[0]
    lane = lax.broadcasted_iota(jnp.int32, (tm, LANES), 1)
    lo = lane < HEAD_DIM

    def norm2(x):
        x2 = x * x
        s_lo = jnp.sum(jnp.where(lo, x2, 0.0), axis=-1, keepdims=True)
        s_hi = jnp.sum(jnp.where(lo, 0.0, x2), axis=-1, keepdims=True)
        ms = jnp.where(lo, s_lo, s_hi) * (1.0 / HEAD_DIM)
        return x * lax.rsqrt(ms + RMS_EPS) * gk

    ks = norm2(kv[:, 0:LANES])
    kw = norm2(kv[:, 2 * LANES:3 * LANES])
    row = row0 + lax.broadcasted_iota(jnp.int32, (tm, LANES), 0)
    onehot = jnp.where(lane - HEAD_DIM == row // SLC_BLOCK, 1.0, 0.0)
    ksx_ref[0] = jnp.where(lo, ks, onehot).astype(BF16)
    ksx_ref[1] = jnp.where(lo, pltpu.roll(ks, HEAD_DIM, axis=1), onehot).astype(BF16)
    flag_col = jnp.where(lane == HEAD_DIM, 1.0, 0.0)
    kwn_ref[0] = jnp.where(lo, kw, flag_col).astype(BF16)
    kwn_ref[1] = jnp.where(lo, pltpu.roll(kw, HEAD_DIM, axis=1), flag_col).astype(BF16)
    tail = jnp.where(lax.broadcasted_iota(jnp.int32, (VT_ROWS - HEAD_DIM, K_TILE), 0) == 0, 1.0, 0.0)
    for col, ref in ((1, vs_ref), (3, vw_ref)):
        for t in range(tm // K_TILE):
            vt = kv[t * K_TILE:(t + 1) * K_TILE, col * LANES:(col + 1) * LANES].T
            for g in range(NSA_KV_GROUPS):
                ref[g, t] = jnp.concatenate([vt[g * HEAD_DIM:(g + 1) * HEAD_DIM], tail], axis=0).astype(BF16)


def _inproj_kernel(seq, x_ref, g_ref, w_ref, gk_ref, rw_ref, q_ref, kvc_ref, gab_ref, gn_ref,
                   ksx_ref, kwn_ref, vs_ref, vw_ref):
    tm = x_ref.shape[0]
    h = _rms(x_ref[...], g_ref[...]).astype(BF16)
    off = 0
    for ref in (rw_ref, q_ref, kvc_ref, None, gab_ref, gn_ref):
        n = 4 * NSA_KV_WIDTH if ref is None else ref.shape[-1]
        y = jnp.dot(h, w_ref[:, off:off + n], preferred_element_type=F32)
        if ref is None:
            _kv_prep(y, gk_ref[...], lax.rem(pl.program_id(0) * tm, seq), ksx_ref, kwn_ref, vs_ref, vw_ref)
        else:
            ref[...] = y.astype(ref.dtype)
        off += n


def _inproj(x2, g, w, gk2, widths, dtypes, seq, tm):
    n, d = x2.shape
    vt_spec = lambda: pl.BlockSpec((2, tm // K_TILE, VT_ROWS, K_TILE), lambda i: (0, i, 0, 0))
    vt_shape = jax.ShapeDtypeStruct((2, n // K_TILE, VT_ROWS, K_TILE), BF16)
    key_spec = lambda: pl.BlockSpec((2, tm, LANES), lambda i: (0, i, 0))
    key_shape = jax.ShapeDtypeStruct((2, n, LANES), BF16)
    return pl.pallas_call(
        functools.partial(_inproj_kernel, seq),
        grid=(n // tm,),
        in_specs=[pl.BlockSpec((tm, d), lambda i: (i, 0)),
                  pl.BlockSpec((1, d), lambda i: (0, 0)),
                  pl.BlockSpec(w.shape, lambda i: (0, 0), pipeline_mode=pl.Buffered(1)),
                  pl.BlockSpec((1, LANES), lambda i: (0, 0))],
        out_specs=[pl.BlockSpec((tm, wd), lambda i: (i, 0)) for wd in widths]
        + [key_spec(), key_spec(), vt_spec(), vt_spec()],
        out_shape=[jax.ShapeDtypeStruct((n, wd), dt) for wd, dt in zip(widths, dtypes)]
        + [key_shape, key_shape, vt_shape, vt_shape],
        compiler_params=_params("parallel"),
        name="inproj",
    )(x2, g, w, gk2)


def _compress_kernel(k_ref, v_ref, pe_ref, w1_ref, w2_ref, gk_ref, kc_ref, vc_ref):
    nchunk = k_ref.shape[0] // CMP_STRIDE
    span = CMP_BLOCK // CMP_STRIDE
    G = NSA_KV_GROUPS
    acc = [[[jnp.zeros((nchunk, CMP_HIDDEN), F32) for _ in range(span)] for _ in range(G)] for _ in range(2)]
    for p in range(CMP_STRIDE):
        for s, ref in enumerate((k_ref, v_ref)):
            x = ref[pl.ds(p, nchunk, stride=CMP_STRIDE), :]
            for g in range(G):
                xs = x[:, g * HEAD_DIM:(g + 1) * HEAD_DIM]
                for half in range(span):
                    q = half * CMP_STRIDE + p
                    acc[s][g][half] = acc[s][g][half] + _mm(xs + pe_ref[s, q:q + 1, :], w1_ref[s, q])
    for s, out_ref in ((0, kc_ref), (1, vc_ref)):
        for g in range(G):
            hid = acc[s][g][0]
            for half in range(1, span):
                hid = hid + pltpu.roll(acc[s][g][half], nchunk - half, axis=0)
            out = _mm(jax.nn.gelu(hid), w2_ref[s])
            out_ref[0, g] = _rms(out, gk_ref[...]) if s == 0 else out


def _compress(kv, pe, w1, w2, gk, batch, seq):
    nchunk = seq // CMP_STRIDE
    out_spec = lambda: pl.BlockSpec((1, NSA_KV_GROUPS, nchunk, HEAD_DIM), lambda b: (b, 0, 0, 0))
    out_shape = jax.ShapeDtypeStruct((batch, NSA_KV_GROUPS, nchunk, HEAD_DIM), F32)
    const = lambda a: pl.BlockSpec(a.shape, lambda b: (0,) * a.ndim)
    return pl.pallas_call(
        _compress_kernel,
        grid=(batch,),
        in_specs=[pl.BlockSpec((seq, NSA_KV_WIDTH), lambda b: (b, 0)), pl.BlockSpec((seq, NSA_KV_WIDTH), lambda b: (b, 1)),
                  const(pe), const(w1), const(w2), const(gk)],
        out_specs=[out_spec(), out_spec()],
        out_shape=[out_shape, out_shape],
        compiler_params=_params("parallel"),
        name="compress",
    )(kv, kv, pe, w1, w2, gk)


def _rank_select(scores, cur_t, n_live, rank_ref):
    nb, t = scores[0].shape
    sub = 8
    groups = range(nb // sub)
    rank_ref[...] = jnp.zeros(rank_ref.shape, F32)
    below = lax.broadcasted_iota(jnp.int32, (sub, t), 0)
    for gj in groups:
        @pl.when(gj * sub < n_live)
        def _():
            for n, x in enumerate(scores):
                xs = [x[g * sub:(g + 1) * sub] for g in groups]
                inc = [jnp.zeros((sub, t), F32) for _ in groups]
                for jl in range(sub):
                    j = gj * sub + jl
                    row = jnp.broadcast_to(x[j:j + 1, :], (sub, t))
                    for g in groups:
                        if g > gj:
                            one = jnp.where(row >= xs[g], 1.0, 0.0)
                        elif g < gj:
                            one = jnp.where(row > xs[g], 1.0, 0.0)
                        else:
                            one = jnp.where(below > jl, jnp.where(row >= xs[g], 1.0, 0.0),
                                            jnp.where(row > xs[g], 1.0, 0.0))
                        inc[g] = inc[g] + one
                rank_ref[n] = rank_ref[n] + jnp.concatenate(inc, axis=0)
    jj = lax.broadcasted_iota(jnp.int32, (nb, t), 0)
    live = jj <= cur_t
    return [jnp.where(live, jnp.where(rank_ref[n] < float(min(SLC_TOPK, nb)), 0.0, NEG_INF), NEG_INF)
            for n in range(len(scores))]


def _win_masked(tq, tk):
    return [o for o in range(-(WINDOW // tk), 0) if tq - 1 - o * tk >= WINDOW]


def _scores(ks, ws):
    return [jnp.dot(k, w, preferred_element_type=F32) for k, w in zip(ks, ws)]


def _softmax_update(m_ref, acc_ref, slots, s, vts):
    m_old = [m_ref[i] for i in slots]
    ncol = s[0].shape[1] // LANES
    m_cols = [[] for _ in slots]
    p_cols = [[] for _ in slots]
    for c in range(ncol):
        cols = slice(c * LANES, (c + 1) * LANES)
        for n, x in enumerate(s):
            xc = x[:, cols]
            mc = jnp.maximum(m_old[n][:, cols], jnp.max(xc, axis=0, keepdims=True))
            m_cols[n].append(mc)
            p_cols[n].append(jnp.exp2(xc - mc).astype(BF16))
    m_new = [jnp.concatenate(mc, axis=1) for mc in m_cols]
    pv = [jnp.dot(vt, jnp.concatenate(pc, axis=1), preferred_element_type=F32) for vt, pc in zip(vts, p_cols)]
    acc = [jnp.exp2(mo - mn) * acc_ref[i] + z for mo, mn, i, z in zip(m_old, m_new, slots, pv)]
    for i, mn, ac in zip(slots, m_new, acc):
        m_ref[i] = mn
        acc_ref[i] = ac


def _attn_kernel(q_ref, gn_ref, kc_ref, vct_ref, ksx_ref, vst_ref, kwn_ref, vwt_ref, bc_ref, bn_ref, wm_ref,
                 gq_ref, wov_ref, o_ref, m_ref, acc_ref, sa_ref, sb_ref, rank_ref):
    tq = q_ref.shape[0]
    tk = K_TILE
    nblk = wov_ref.shape[0]
    G, R = NSA_KV_GROUPS, NSA_GROUP
    qt = pl.program_id(1)
    n_q = tq // tk
    n_b = -(-REL_MAX_DIST // tk)
    j0 = qt * n_q
    qT = q_ref[...].astype(F32).T
    gates = jax.nn.sigmoid(gn_ref[...].T)
    gq = gq_ref[...]

    qn, qx, negs, o_cmp, scores = [], [], [], [], []
    ncmp = kc_ref.shape[2]
    step = tq // CMP_STRIDE
    bc_rows = pl.ds(pl.multiple_of(bc_ref.shape[1] - ncmp - qt * step, 8), ncmp)
    for g in range(G):
        xs = []
        for h in range(g * R, (g + 1) * R):
            x = qT[h * HEAD_DIM:(h + 1) * HEAD_DIM]
            x = x * lax.rsqrt(jnp.mean(x * x, axis=0, keepdims=True) + RMS_EPS) * gq * (HEAD_DIM ** -0.5 * LOG2E)
            xs.append(x.astype(BF16))
        qn.append(jnp.concatenate(xs, axis=1))

    for g in range(G):
        s = jnp.dot(kc_ref[0, g].astype(BF16), qn[g], preferred_element_type=F32)
        p_cols = []
        for c in range(R * tq // LANES):
            cols = slice(c * LANES, (c + 1) * LANES)
            sc = s[:, cols] + bc_ref[g, bc_rows, cols]
            e = jnp.exp2(sc - jnp.maximum(jnp.max(sc, axis=0, keepdims=True), M_FLOOR))
            p_cols.append(e * (1.0 / jnp.maximum(jnp.sum(e, axis=0, keepdims=True), 1e-30)))
        o_cmp.append(jnp.dot(vct_ref[0, g].astype(BF16), jnp.concatenate(p_cols, axis=1).astype(BF16),
                             preferred_element_type=F32))
        per_head = tq // LANES
        psum = jnp.concatenate([sum(p_cols[r * per_head + t] for r in range(R)) for t in range(per_head)], axis=1)
        imp = _mm_f32(wov_ref[...], psum)

        blk = lax.broadcasted_iota(jnp.int32, imp.shape, 0)
        cur = (qt * tq + lax.broadcasted_iota(jnp.int32, imp.shape, 1)) // SLC_BLOCK
        forced = (blk == 0) | (blk == cur) | (blk == cur - 1)
        scores.append(jnp.where(forced, FORCE, jnp.where(blk <= cur, imp, -FORCE)))

    cur = (qt * tq + lax.broadcasted_iota(jnp.int32, (nblk, tq), 1)) // SLC_BLOCK
    for g, neg in enumerate(_rank_select(scores, cur, (qt + 1) * (tq // SLC_BLOCK), rank_ref)):
        if nblk < LANES - HEAD_DIM:
            neg = jnp.concatenate([neg, jnp.zeros((LANES - HEAD_DIM - nblk, tq), F32)], axis=0)
        neg = neg.astype(BF16)
        negs.append(neg)
        qx.append(jnp.concatenate([qn[g], jnp.concatenate([neg] * R, axis=1)], axis=0))

    m_ref[...] = jnp.full(m_ref.shape, M_FLOOR, F32)
    acc_ref[...] = jnp.zeros(acc_ref.shape, F32)

    def invalid_before(level):
        return jnp.where(qt >= level, 0.0, NEG_INF)

    def slc_queries(g, level):
        if level == 0:
            return qx[g]
        rows = jnp.minimum(negs[g], invalid_before(level).astype(BF16))
        return jnp.concatenate([qn[g], jnp.concatenate([rows] * R, axis=1)], axis=0)

    def win_queries(g, level):
        flag_row = lax.broadcasted_iota(jnp.int32, (LANES - HEAD_DIM, R * tq), 0) == 0
        extra = jnp.where(flag_row, invalid_before(level), 0.0).astype(BF16)
        return jnp.concatenate([qn[g], extra], axis=0)

    def slc_probs(o, bias):
        j = jnp.maximum(j0 + o, 0)
        sl = pl.ds(pl.multiple_of(j * tk, tk), tk)
        return [(g, ksx_ref[g, sl, :], vst_ref[g, j], slc_queries(g, max(-(o // n_q), 0)), bias.at[g])
                for g in range(G)]

    def win_probs(o, bias, per_group):
        j = jnp.maximum(j0 + o, 0)
        sl = pl.ds(pl.multiple_of(j * tk, tk), tk)
        pick = lambda g: None if bias is None else (bias.at[g] if per_group else bias)
        return [(G + g, kwn_ref[g, sl, :], vwt_ref[g, j], win_queries(g, max(-(o // n_q), 0)), pick(g))
                for g in range(G)]

    n_far = jnp.maximum(j0 - n_b, 0)
    groups = list(range(G))

    def far_scores(j, buf, gs=groups):
        sl = pl.ds(pl.multiple_of(j * tk, tk), tk)
        for g, x in zip(gs, _scores([ksx_ref[g, sl, :] for g in gs], [qx[g] for g in gs])):
            buf[g] = x

    def far_update(j, buf, gs=groups):
        _softmax_update(m_ref, acc_ref, gs, [buf[g] for g in gs], [vst_ref[g, j] for g in gs])

    far_scores(0, sa_ref)

    def far_body(i, carry):
        for g in groups:
            far_scores(2 * i + 1, sb_ref, [g])
            far_update(2 * i, sa_ref, [g])
        for g in groups:
            far_scores(2 * i + 2, sa_ref, [g])
            far_update(2 * i + 1, sb_ref, [g])
        return carry

    lax.fori_loop(0, n_far // 2, far_body, 0)

    @pl.when(n_far % 2 == 1)
    def _():
        far_update(n_far - 1, sa_ref)

    masked = _win_masked(tq, tk)
    rounds = []
    for o in range(-(WINDOW // tk), n_q):
        if o >= -n_b:
            rounds.append(slc_probs(o, bn_ref.at[o + n_b]) + win_probs(o, bn_ref.at[o + n_b], True))
        elif o in masked:
            rounds.append(win_probs(o, wm_ref.at[masked.index(o)], False))
        else:
            rounds.append(win_probs(o, None, False))

    def round_scores(probs):
        sc = _scores([p[1] for p in probs], [p[3] for p in probs])
        return [x if p[4] is None else x + p[4][...] for x, p in zip(sc, probs)]

    sc = round_scores(rounds[0])
    for i, probs in enumerate(rounds):
        sc_next = round_scores(rounds[i + 1]) if i + 1 < len(rounds) else None
        _softmax_update(m_ref, acc_ref, [p[0] for p in probs], sc, [p[2] for p in probs])
        sc = sc_next

    outs = []
    for g in range(G):
        acc = acc_ref[g]
        o_slc = acc[:HEAD_DIM] / acc[HEAD_DIM:HEAD_DIM + 1]
        acc = acc_ref[G + g]
        o_win = acc[:HEAD_DIM] / acc[HEAD_DIM:HEAD_DIM + 1]
        for r in range(R):
            h = g * R + r
            cols = slice(r * tq, (r + 1) * tq)
            outs.append(gates[3 * h:3 * h + 1] * o_cmp[g][:, cols] + gates[3 * h + 1:3 * h + 2] * o_slc[:, cols]
                        + gates[3 * h + 2:3 * h + 3] * o_win[:, cols])
    o_ref[...] = jnp.concatenate(outs, axis=0).T.astype(o_ref.dtype)


def _attention(q, gn, kc, vct, ksx, vst, kwn, vwt, bias_c, bias_near, win_mask, gq, wov, batch, seq):
    n = q.shape[0]
    tq = Q_TILE
    nq = seq // tq
    ncmp = kc.shape[2]
    row = lambda b, i: (b * nq + i, 0)
    whole = lambda b, i: (0, b, 0)
    tiles = lambda b, i: (0, b, 0, 0)
    const = lambda a: pl.BlockSpec(a.shape, lambda b, i: (0,) * a.ndim, pipeline_mode=pl.Buffered(1))
    return pl.pallas_call(
        _attn_kernel,
        grid=(batch, nq),
        in_specs=[pl.BlockSpec((tq, NSA_WIDTH), row),
                  pl.BlockSpec((tq, LANES), row),
                  pl.BlockSpec((1, NSA_KV_GROUPS, ncmp, HEAD_DIM), lambda b, i: (b, 0, 0, 0)),
                  pl.BlockSpec((1, NSA_KV_GROUPS, HEAD_DIM, ncmp), lambda b, i: (b, 0, 0, 0)),
                  pl.BlockSpec((2, seq, LANES), whole),
                  pl.BlockSpec((2, seq // K_TILE, VT_ROWS, K_TILE), tiles),
                  pl.BlockSpec((2, seq, LANES), whole),
                  pl.BlockSpec((2, seq // K_TILE, VT_ROWS, K_TILE), tiles),
                  const(bias_c),
                  const(bias_near), const(win_mask), const(gq), const(wov)],
        out_specs=pl.BlockSpec((tq, NSA_WIDTH), row),
        out_shape=jax.ShapeDtypeStruct((n, NSA_WIDTH), BF16),
        scratch_shapes=[pltpu.VMEM((2 * NSA_KV_GROUPS, 1, NSA_GROUP * tq), F32),
                        pltpu.VMEM((2 * NSA_KV_GROUPS, VT_ROWS, NSA_GROUP * tq), F32),
                        pltpu.VMEM((NSA_KV_GROUPS, K_TILE, NSA_GROUP * tq), F32),
                        pltpu.VMEM((NSA_KV_GROUPS, K_TILE, NSA_GROUP * tq), F32),
                        pltpu.VMEM((NSA_KV_GROUPS, seq // SLC_BLOCK, tq), F32)],
        compiler_params=_params("parallel", "parallel"),
        name="nsa_attention",
    )(q, gn, kc, vct, ksx, vst, kwn, vwt, bias_c, bias_near, win_mask, gq, wov)


def _split_bf16(z, parts):
    out = []
    for _ in range(parts - 1):
        hi = z.astype(BF16)
        out.append(hi)
        z = z - hi.astype(F32)
    return out + [z.astype(BF16)]


def _head_sums(z, ones):
    width = ones.shape[0]
    nb = z.shape[1] // width
    rows = z.shape[0]
    zb = z.astype(BF16)
    stacked = jnp.concatenate([zb[:, m * width:(m + 1) * width] for m in range(nb)], axis=0)
    sums = jnp.dot(stacked, ones, preferred_element_type=F32)
    return jnp.concatenate([sums[m * rows:(m + 1) * rows] for m in range(nb)], axis=1)


def _softplus(z):
    return jnp.maximum(z, 0.0) + jnp.log(1.0 + jnp.exp(-jnp.abs(z)))


def _rwkv_chunk(x, prev, st, mu_ref, w0_ref, w2_ref, a0_ref, a2_ref, g2_ref, kk_ref, ka_ref, rk_ref,
                lng_ref, lnb_ref, ones_ref):
    L = x.shape[0]
    W = RWKV_WIDTH
    N = HEAD_DIM
    row_id = lax.broadcasted_iota(jnp.int32, x.shape, 0)
    shifted = jnp.where(row_id == 0, prev, pltpu.roll(x, 1, axis=0))
    xl = x + (shifted - x) * mu_ref[...]
    yield None
    r = xl[:, 0:W]
    k = xl[:, W:2 * W]
    v = xl[:, 2 * W:3 * W]
    xw = xl[:, 3 * W:3 * W + LORA_W]
    xa = xl[:, 3 * W + LORA_W:3 * W + LORA_W + LORA_A]
    xg = xl[:, 3 * W + LORA_W + LORA_A:]
    w = -_softplus(-(w0_ref[...] + _mm(jnp.tanh(xw), w2_ref[...]))) - 0.5
    ld = -jnp.exp(w)
    a = jax.nn.sigmoid(a0_ref[...] + _mm(xa, a2_ref[...]))
    gate = _mm(jax.nn.sigmoid(xg), g2_ref[...])
    kkv = k * kk_ref[...]
    k2 = k * (1.0 + (a - 1.0) * ka_ref[...])
    yield None

    ti = lax.broadcasted_iota(jnp.int32, (L, L), 0)
    si = lax.broadcasted_iota(jnp.int32, (L, L), 1)
    incl = si <= ti
    strict = si < ti
    tri = jnp.where(incl, 1.0, 0.0).astype(BF16)
    cl3 = jnp.dot(tri, jnp.concatenate(_split_bf16(ld, 3), axis=1), preferred_element_type=F32)
    cl = cl3[:, :W] + (cl3[:, W:2 * W] + cl3[:, 2 * W:])
    cl_end = cl[L - 1:L, :]
    yield None
    e_pos = jnp.exp(cl)
    e_neg = jnp.exp(-cl)
    e_prev = jnp.exp(cl - ld)
    e_end = jnp.exp(cl_end - cl)
    eye = jnp.where(ti == si, 1.0, 0.0)
    yield None

    hsum = lambda z: _head_sums(z, ones_ref[...])

    kk_n = kkv * lax.rsqrt(jnp.maximum(hsum(kkv * kkv), 1e-24))
    bv = kk_n * a
    yield None
    a_t = (-kk_n * e_prev).astype(BF16)
    b_t = (bv * e_neg).astype(BF16)
    k_t = (k2 * e_neg).astype(BF16)
    r_t = (r * e_pos).astype(BF16)
    yield None
    v_b = v.astype(BF16)
    k_e = k2 * e_end
    b_e = bv * e_end

    yield "elementwise done"

    P = range(W // LANES)
    ps = [slice(m * LANES, (m + 1) * LANES) for m in P]
    first = lax.broadcasted_iota(jnp.int32, (1, LANES), 1) < N
    same_head = (lax.broadcasted_iota(jnp.int32, (LANES, LANES), 0) // N
                 == lax.broadcasted_iota(jnp.int32, (LANES, LANES), 1) // N)

    def diag_rows(x):
        zero = jnp.zeros_like(x)
        return jnp.concatenate([jnp.where(first, x, zero), jnp.where(first, zero, x)], axis=0)

    def diag_blocks(x):
        zero = jnp.zeros_like(x[:, :LANES])
        return jnp.concatenate([jnp.concatenate([x[:, :LANES], zero], axis=1),
                                jnp.concatenate([zero, x[:, LANES:]], axis=1)], axis=0)

    ke_t = [k_e[:, s].T for s in ps]
    be_t = [b_e[:, s].T for s in ps]
    p_end = [e_pos[:, s].T[:, L - 1:L] for s in ps]

    lhs = [jnp.concatenate([a_t[:, s], r_t[:, s]], axis=0) for s in ps]
    rhs = [jnp.concatenate([k_t[:, s], b_t[:, s]], axis=0) for s in ps]
    zero_b = jnp.zeros_like(lhs[0])
    aa = [[lax.dot_general(jnp.where(first, x, zero_b) if j == 0 else jnp.where(first, zero_b, x), y, _NT,
                           preferred_element_type=F32) for j in range(2)] for x, y in zip(lhs, rhs)]
    cat2 = lambda f: [jnp.concatenate([f(pair[0]), f(pair[1])], axis=1) for pair in aa]
    a_ak = cat2(lambda x: jnp.where(strict, x[:L, :L], 0.0))
    a_ab = cat2(lambda x: jnp.where(strict, x[:L, L:], 0.0))
    a_rk = cat2(lambda x: jnp.where(incl, x[L:, :L], 0.0))
    a_rb = cat2(lambda x: jnp.where(incl, x[L:, L:], 0.0))
    yield None

    eye2 = jnp.concatenate([eye, eye], axis=1)
    tinv = [eye2 + x for x in a_ab]
    pw = [_mm(x, diag_blocks(x)) for x in a_ab]
    yield None
    span = 2
    while 2 * span < L:
        both = [_mm(jnp.concatenate([t, p], axis=0), diag_blocks(p)) for t, p in zip(tinv, pw)]
        tinv = [t + x[:L] for t, x in zip(tinv, both)]
        pw = [x[L:] for x in both]
        yield None
        span *= 2
    tinv = [t + _mm(t, diag_blocks(p)) for t, p in zip(tinv, pw)]
    yield None

    v_d = [diag_rows(v_b[:, s]) for s in ps]
    av = [_mm(jnp.concatenate([a_ak[m], a_rk[m]], axis=0), v_d[m]) for m in P]
    tw = [_mm(tinv[m], jnp.concatenate([diag_rows(a_t[:, ps[m]]), diag_rows(av[m][:L].astype(BF16))], axis=1))
          for m in P]
    kv_loc = [jnp.where(same_head, _mm(ke_t[m], v_b[:, ps[m]]), 0.0) for m in P]
    yield None

    ws = [_mm(jnp.concatenate([tw[m][:, :LANES].astype(BF16), r_t[:, ps[m]]], axis=0), st[m]) for m in P]
    u = [ws[m][:L] + tw[m][:, LANES:] for m in P]
    yield None
    y = [ws[m][L:] + av[m][L:] + _mm(a_rb[m], diag_rows(u[m])) for m in P]
    st_new = [p_end[m] * st[m] + kv_loc[m] + jnp.where(same_head, _mm(be_t[m], u[m]), 0.0) for m in P]
    yield None

    y = jnp.concatenate(y, axis=1)
    d = y - hsum(y) * (1.0 / N)
    yn = d * lax.rsqrt(hsum(d * d) * (1.0 / N) + GN_EPS) * lng_ref[...] + lnb_ref[...]
    bonus = hsum(r * k2 * rk_ref[...])
    yield (yn + bonus * v) * gate, st_new


def _rwkv_kernel(x_ref, xp_ref, *refs):
    *param_refs, o_ref, st_ref = refs
    c = pl.program_id(1)

    @pl.when(c == 0)
    def _():
        st_ref[...] = jnp.zeros(st_ref.shape, F32)

    chunks = []
    for i in range(x_ref.shape[0]):
        prev = jnp.where(c > 0, xp_ref[i, xp_ref.shape[1] - 1:, :], 0.0)
        st = [st_ref[i, m] for m in range(st_ref.shape[1])]
        chunks.append(_rwkv_chunk(x_ref[i], prev, st, *param_refs))
    def advance(gen, until_result):
        item = next(gen)
        return (isinstance(item, tuple), item) if until_result else (item == "elementwise done", item)

    while not advance(chunks[0], False)[0]:
        pass
    outs, states = [], []
    for i, chunk in enumerate(chunks):
        following = chunks[i + 1] if i + 1 < len(chunks) else None
        result = None
        while result is None or following is not None:
            if result is None:
                done, item = advance(chunk, True)
                result = item if done else None
            if following is not None and advance(following, False)[0]:
                following = None
        outs.append(result[0].astype(o_ref.dtype))
        states.append(jnp.stack(result[1]))
    o_ref[...] = jnp.stack(outs)
    st_ref[...] = jnp.stack(states)


def _rwkv(rw, vecs, w2, a2, g2, batch, seq):
    width = rw.shape[1]
    L = min(CHUNK, seq)
    nc = seq // L
    nb = RWKV_SEQS if batch % RWKV_SEQS == 0 else 1
    sub = 8
    rw3 = rw.reshape(batch, seq, width)
    vec = lambda wd: pl.BlockSpec((1, wd), lambda b, c: (0, 0))
    mat = lambda m: pl.BlockSpec(m.shape, lambda b, c: (0, 0))
    mu, w0, a0, kk, ka, rk, lng, lnb = vecs
    head = np.arange(2 * LANES) // HEAD_DIM
    ones = jnp.asarray(head[:, None] == head[None, :], BF16)
    out = pl.pallas_call(
        _rwkv_kernel,
        grid=(batch // nb, nc),
        in_specs=[pl.BlockSpec((nb, L, width), lambda b, c: (b, c, 0)),
                  pl.BlockSpec((nb, sub, width), lambda b, c: (b, jnp.maximum(c * (L // sub) - 1, 0), 0)),
                  vec(width), vec(RWKV_WIDTH), mat(w2), vec(RWKV_WIDTH), mat(a2), mat(g2),
                  vec(RWKV_WIDTH), vec(RWKV_WIDTH), vec(RWKV_WIDTH), vec(RWKV_WIDTH), vec(RWKV_WIDTH), mat(ones)],
        out_specs=pl.BlockSpec((nb, L, RWKV_WIDTH), lambda b, c: (b, c, 0)),
        out_shape=jax.ShapeDtypeStruct((batch, seq, RWKV_WIDTH), BF16),
        scratch_shapes=[pltpu.VMEM((nb, RWKV_WIDTH // LANES, LANES, LANES), F32)],
        compiler_params=_params("parallel", "arbitrary"),
        name="rwkv7",
    )(rw3, rw3, mu, w0, w2, a0, a2, g2, kk, ka, rk, lng, lnb, ones)
    return out.reshape(batch * seq, RWKV_WIDTH)


def _merge_kernel(x_ref, oa_ref, ob_ref, gab_ref, wpa_ref, wpb_ref, wo_ref, o_ref):
    d = x_ref.shape[1]
    pa = jnp.dot(oa_ref[...], wpa_ref[...], preferred_element_type=F32)
    pb = jnp.dot(ob_ref[...], wpb_ref[...], preferred_element_type=F32)
    gab = gab_ref[...].astype(F32)
    merged = jax.nn.sigmoid(gab[:, :d]) * pa + jax.nn.sigmoid(gab[:, d:]) * pb
    o_ref[...] = x_ref[...] + _mm(merged, wo_ref[...])


def _merge(x2, oa, ob, gab, wpa, wpb, wo, tm):
    n, d = x2.shape
    row = lambda wd: pl.BlockSpec((tm, wd), lambda i: (i, 0))
    mat = lambda m: pl.BlockSpec(m.shape, lambda i: (0, 0))
    return pl.pallas_call(
        _merge_kernel,
        grid=(n // tm,),
        in_specs=[row(d), row(oa.shape[1]), row(ob.shape[1]), row(gab.shape[1]), mat(wpa), mat(wpb), mat(wo)],
        out_specs=row(d),
        out_shape=jax.ShapeDtypeStruct((n, d), F32),
        compiler_params=_params("parallel"),
        name="merge",
    )(x2, oa, ob, gab, wpa, wpb, wo)


def _ffn_kernel(seq, x_ref, xh_ref, g_ref, wv_ref, wg_ref, cwv_ref, cwg_ref, cbv_ref, cbg_ref, wd_ref, o_ref,
                h_ref, uv_ref, ug_ref):
    tm = x_ref.shape[0]
    i = pl.program_id(0)
    f = pl.program_id(1)

    @pl.when(f == 0)
    def _():
        first = lax.rem(i * tm, seq) == 0
        halo = jnp.where(first, 0.0, _rms(xh_ref[...], g_ref[...]))
        h_ref[0:HALO, :] = halo.astype(BF16)
        h_ref[HALO:, :] = _rms(x_ref[...], g_ref[...]).astype(BF16)

    h = h_ref[...]
    uv_ref[...] = jnp.dot(h, wv_ref[...], preferred_element_type=F32)
    ug_ref[...] = jnp.dot(h, wg_ref[...], preferred_element_type=F32)

    def conv(u_ref, cw_ref, cb_ref):
        acc = cb_ref[...] + cw_ref[0:1, :] * u_ref[pl.ds(HALO - 2, tm), :]
        acc = acc + cw_ref[1:2, :] * u_ref[pl.ds(HALO - 1, tm), :]
        return acc + cw_ref[2:3, :] * u_ref[pl.ds(HALO, tm), :]

    val = conv(uv_ref, cwv_ref, cbv_ref)
    gt = conv(ug_ref, cwg_ref, cbg_ref)
    y = _mm(gt * jax.nn.sigmoid(gt) * val, wd_ref[...])

    @pl.when(f == 0)
    def _():
        o_ref[...] = x_ref[...] + y

    @pl.when(f > 0)
    def _():
        o_ref[...] = o_ref[...] + y


def _ffn(x1, g, w_up, conv_w, conv_b, w_down, seq, tm, tf):
    n, d = x1.shape
    dff = w_down.shape[0]
    nf = dff // tf
    return pl.pallas_call(
        functools.partial(_ffn_kernel, seq),
        grid=(n // tm, nf),
        in_specs=[pl.BlockSpec((tm, d), lambda i, f: (i, 0)),
                  pl.BlockSpec((HALO, d), lambda i, f: (jnp.maximum(i * (tm // HALO) - 1, 0), 0)),
                  pl.BlockSpec((1, d), lambda i, f: (0, 0)),
                  pl.BlockSpec((d, tf), lambda i, f: (0, f)),
                  pl.BlockSpec((d, tf), lambda i, f: (0, nf + f)),
                  pl.BlockSpec((CONV_WIDTH, tf), lambda i, f: (0, f)),
                  pl.BlockSpec((CONV_WIDTH, tf), lambda i, f: (0, nf + f)),
                  pl.BlockSpec((1, tf), lambda i, f: (0, f)),
                  pl.BlockSpec((1, tf), lambda i, f: (0, nf + f)),
                  pl.BlockSpec((tf, d), lambda i, f: (f, 0))],
        out_specs=pl.BlockSpec((tm, d), lambda i, f: (i, 0)),
        out_shape=jax.ShapeDtypeStruct((n, d), F32),
        scratch_shapes=[pltpu.VMEM((tm + HALO, d), BF16),
                        pltpu.VMEM((tm + HALO, tf), F32),
                        pltpu.VMEM((tm + HALO, tf), F32)],
        compiler_params=_params("parallel", "arbitrary"),
        name="convffn",
    )(x1, x1, g, w_up, w_up, conv_w, conv_w, conv_b, conv_b, w_down)


def _t5_bucket(dist):
    n = np.maximum(dist, 0)
    max_exact = REL_BUCKETS // 2
    ratio = np.log(np.maximum(n, 1).astype(np.float32) / max_exact) / math.log(REL_MAX_DIST / max_exact)
    large = np.minimum(max_exact + (ratio * (REL_BUCKETS - max_exact)).astype(np.int32), REL_BUCKETS - 1)
    return np.where(n < max_exact, n, large).astype(np.int32)


def _bias_tables(rel_bias, seq, ncmp):
    tq, tk = Q_TILE, K_TILE
    G, R = NSA_KV_GROUPS, NSA_GROUP
    nq = seq // tq
    step = tq // CMP_STRIDE
    tab = rel_bias.astype(F32)

    off = (nq - 1) * step
    d_c = np.arange(tq)[None, :] - (np.arange(ncmp + off)[:, None] - off) * CMP_STRIDE - (CMP_BLOCK - 1)
    n_b = -(-REL_MAX_DIST // tk)
    d0 = np.arange(tq)[None, :] - np.arange(tk)[:, None]
    d_n = np.stack([d0 - o * tk for o in range(-n_b, tq // tk)])

    buckets = np.concatenate([_t5_bucket(d_c).reshape(-1), _t5_bucket(d_n).reshape(-1)])
    onehot = (jnp.arange(REL_BUCKETS, dtype=jnp.int32)[:, None] == jnp.asarray(buckets)[None, :]).astype(F32)
    vals = jnp.dot(tab.T * LOG2E, onehot, precision=lax.Precision.HIGHEST, preferred_element_type=F32)
    by_group = lambda x: x.reshape(G, R, -1, tq).transpose(0, 2, 1, 3).reshape(G, -1, R * tq)
    tile_r = lambda m: np.tile(m, (1, R))

    base = jnp.where(jnp.asarray(tile_r(d_c >= 0)), by_group(vals[:, :d_c.size]), NEG_INF)

    far = jnp.repeat(tab[REL_BUCKETS - 1] * LOG2E, tq).reshape(G, 1, R * tq)
    near = by_group(vals[:, d_c.size:]) - far
    near = jnp.where(jnp.asarray(tile_r(d_n.reshape(-1, tq) >= 0)), near, NEG_INF)
    bias_near = near.reshape(G, d_n.shape[0], tk, R * tq).transpose(1, 0, 2, 3)
    win = np.stack([tile_r(np.where(d0 - o * tk < WINDOW, 0.0, NEG_INF)) for o in _win_masked(tq, tk)])
    return base, bias_near, jnp.asarray(win.astype(np.float32))


def _overlap_matrix(ncmp_pad, n_slc, n_cmp):
    m = np.zeros((ncmp_pad, n_slc), np.float32)
    ratio = SLC_BLOCK // CMP_STRIDE
    for j in range(n_slc):
        for o, wgt in enumerate(OVERLAP_W):
            cidx = ratio * j + o - (CMP_BLOCK // CMP_STRIDE - 1)
            if 0 <= cidx < n_cmp:
                m[cidx, j] += wgt
    return jnp.asarray(m.T)


def _layer(x, attn_norm_g, w_in, rel_bias, q_norm_g, k_norm_g, cmp_pe_k, cmp_w1_k, cmp_w2_k,
           cmp_pe_v, cmp_w1_v, cmp_w2_v, rwkv_mu, rwkv_w0, rwkv_w2, rwkv_a0, rwkv_a2, rwkv_g2,
           rwkv_k_k, rwkv_k_a, rwkv_r_k, rwkv_ln_g, rwkv_ln_b, w_proj_a, w_proj_b, w_out,
           ffn_norm_g, w_up, conv_w, conv_b, w_down):
    batch, seq, d = x.shape
    n = batch * seq
    assert seq % Q_TILE == 0 and Q_TILE % K_TILE == 0 and WINDOW % K_TILE == 0
    assert seq % CMP_STRIDE == 0 and seq % FFN_ROWS == 0 and seq % PROJ_ROWS == 0 and PROJ_ROWS % K_TILE == 0
    assert seq // SLC_BLOCK <= LANES - HEAD_DIM
    x2 = x.reshape(n, d)
    row = lambda a: a.reshape(1, -1).astype(F32)

    kvw = 6 * NSA_KV_WIDTH
    o_q, o_kv, o_gn = 0, NSA_WIDTH, NSA_WIDTH + kvw
    o_rw = o_gn + 3 * NSA_HEADS
    o_gab = o_rw + RWKV_IN_WIDTH
    w16 = w_in.astype(BF16)
    gn_pad = jnp.zeros((d, LANES - 3 * NSA_HEADS), BF16)
    w_cat = jnp.concatenate([w16[:, o_rw:o_gab], w16[:, o_q:o_kv], w16[:, o_kv:o_gn], w16[:, o_gab:],
                             w16[:, o_gn:o_rw], gn_pad], axis=1)
    widths = (RWKV_IN_WIDTH, NSA_WIDTH, 2 * NSA_KV_WIDTH, 2 * d, LANES)
    dtypes = (F32, BF16, F32, BF16, F32)
    gk = row(k_norm_g)
    rw, q, kv_cmp, gab, gn, ksx, kwn, vst, vwt = _inproj(
        x2, row(attn_norm_g), w_cat, jnp.concatenate([gk, gk], axis=1), widths, dtypes, seq, tm=PROJ_ROWS)

    nchunk = seq // CMP_STRIDE
    n_cmp = nchunk - CMP_BLOCK // CMP_STRIDE + 1
    pe = jnp.stack([cmp_pe_k, cmp_pe_v]).astype(F32)
    w1 = jnp.stack([cmp_w1_k, cmp_w1_v]).astype(BF16).reshape(2, CMP_BLOCK, HEAD_DIM, CMP_HIDDEN)
    w2 = jnp.stack([cmp_w2_k, cmp_w2_v]).astype(BF16)
    kc, vc = _compress(kv_cmp, pe, w1, w2, gk, batch, seq)

    bias_c, bias_near, win_mask = _bias_tables(rel_bias, seq, nchunk)
    wov = _overlap_matrix(nchunk, seq // SLC_BLOCK, n_cmp)
    o_a = _attention(q, gn, kc, jnp.swapaxes(vc, -1, -2), ksx, vst, kwn, vwt, bias_c, bias_near, win_mask,
                     q_norm_g.reshape(-1, 1).astype(F32), wov, batch, seq)

    vecs = tuple(row(a) for a in (rwkv_mu, rwkv_w0, rwkv_a0, rwkv_k_k, rwkv_k_a, rwkv_r_k, rwkv_ln_g, rwkv_ln_b))
    o_b = _rwkv(rw, vecs, rwkv_w2.astype(BF16), rwkv_a2.astype(BF16), rwkv_g2.astype(BF16), batch, seq)

    x1 = _merge(x2, o_a, o_b, gab, w_proj_a.astype(BF16), w_proj_b.astype(BF16), w_out.astype(BF16), tm=FFN_ROWS)
    dff = w_down.shape[0]
    out = _ffn(x1, row(ffn_norm_g), w_up.astype(BF16), conv_w.astype(F32), row(conv_b), w_down.astype(BF16),
               seq, tm=FFN_ROWS, tf=dff // FFN_COL_SPLITS)
    return out.reshape(batch, seq, d)


def kernel(x, attn_norm_g, w_in, rel_bias, q_norm_g, k_norm_g, cmp_pe_k, cmp_w1_k, cmp_w2_k, cmp_pe_v, cmp_w1_v,
           cmp_w2_v, rwkv_mu, rwkv_w0, rwkv_w2, rwkv_a0, rwkv_a2, rwkv_g2, rwkv_k_k, rwkv_k_a, rwkv_r_k,
           rwkv_ln_g, rwkv_ln_b, w_proj_a, w_proj_b, w_out, ffn_norm_g, w_up, conv_w, conv_b, w_down):
    per_layer = (attn_norm_g, w_in, None, q_norm_g, k_norm_g, cmp_pe_k, cmp_w1_k, cmp_w2_k, cmp_pe_v, cmp_w1_v,
                 cmp_w2_v, rwkv_mu, rwkv_w0, rwkv_w2, rwkv_a0, rwkv_a2, rwkv_g2, rwkv_k_k, rwkv_k_a, rwkv_r_k,
                 rwkv_ln_g, rwkv_ln_b, w_proj_a, w_proj_b, w_out, ffn_norm_g, w_up, conv_w, conv_b, w_down)
    for l in range(attn_norm_g.shape[0]):
        args = [rel_bias if p is None else p[l] for p in per_layer]
        x = _layer(x, *args)
    return x
```

```python
import functools
import math

import numpy as np
import jax
import jax.numpy as jnp
from jax import lax
from jax.experimental import pallas as pl
from jax.experimental.pallas import tpu as pltpu

F32 = jnp.float32
BF16 = jnp.bfloat16

HEAD_DIM = 64
NSA_HEADS = 8
NSA_KV_GROUPS = 2
NSA_GROUP = NSA_HEADS // NSA_KV_GROUPS
NSA_WIDTH = NSA_HEADS * HEAD_DIM
NSA_KV_WIDTH = NSA_KV_GROUPS * HEAD_DIM
CMP_BLOCK = 32
CMP_STRIDE = 16
CMP_HIDDEN = 256
SLC_BLOCK = 64
SLC_TOPK = 16
OVERLAP_W = (1, 2, 2, 2, 1)
WINDOW = 512
REL_BUCKETS = 32
REL_MAX_DIST = 128
RWKV_HEADS = 8
RWKV_WIDTH = RWKV_HEADS * HEAD_DIM
LORA_W = 64
LORA_A = 64
LORA_G = 128
RWKV_IN_WIDTH = 3 * RWKV_WIDTH + LORA_W + LORA_A + LORA_G
GN_EPS = 64e-5
CONV_WIDTH = 3
RMS_EPS = 1e-6
NEG_INF = -1e30
FORCE = 1e9
LOG2E = math.log2(math.e)
M_FLOOR = -1e20

LANES = 128
VMEM_LIMIT = 56 * 1024 * 1024

Q_TILE = 256
K_TILE = 256
VT_ROWS = 80
CHUNK = 128
RWKV_SEQS = 4
CMP_PACK = 4
HALO = 16
PROJ_ROWS = 512
FFN_ROWS = 1024
FFN_COL_SPLITS = 2

_NT = (((1,), (1,)), ((), ()))


def _params(*sem):
    return pltpu.CompilerParams(dimension_semantics=sem, vmem_limit_bytes=VMEM_LIMIT)


def _mm(a, b):
    return jnp.dot(a.astype(BF16), b.astype(BF16), preferred_element_type=F32)


def _mm_f32(a, b):
    return jnp.dot(a, b, preferred_element_type=F32, precision=lax.Precision.HIGHEST)


def _rms(x, g):
    return x * lax.rsqrt(jnp.mean(x * x, axis=-1, keepdims=True) + RMS_EPS) * g


def _kv_prep(kv, gk, row0, ksx_ref, kwn_ref, vs_ref, vw_ref):
    tm = kv.shape[0]
    lane = lax.broadcasted_iota(jnp.int32, (tm, LANES), 1)
    lo = lane < HEAD_DIM

    def norm2(x):
        x2 = x * x
        s_lo = jnp.sum(jnp.where(lo, x2, 0.0), axis=-1, keepdims=True)
        s_hi = jnp.sum(jnp.where(lo, 0.0, x2), axis=-1, keepdims=True)
        ms = jnp.where(lo, s_lo, s_hi) * (1.0 / HEAD_DIM)
        return x * lax.rsqrt(ms + RMS_EPS) * gk

    ks = norm2(kv[:, 0:LANES])
    kw = norm2(kv[:, 2 * LANES:3 * LANES])
    row = row0 + lax.broadcasted_iota(jnp.int32, (tm, LANES), 0)
    onehot = jnp.where(lane - HEAD_DIM == row // SLC_BLOCK, 1.0, 0.0)
    ksx_ref[0] = jnp.where(lo, ks, onehot).astype(BF16)
    ksx_ref[1] = jnp.where(lo, pltpu.roll(ks, HEAD_DIM, axis=1), onehot).astype(BF16)
    flag_col = jnp.where(lane == HEAD_DIM, 1.0, 0.0)
    kwn_ref[0] = jnp.where(lo, kw, flag_col).astype(BF16)
    kwn_ref[1] = jnp.where(lo, pltpu.roll(kw, HEAD_DIM, axis=1), flag_col).astype(BF16)
    tail = jnp.where(lax.broadcasted_iota(jnp.int32, (VT_ROWS - HEAD_DIM, K_TILE), 0) == 0, 1.0, 0.0)
    for col, ref in ((1, vs_ref), (3, vw_ref)):
        for t in range(tm // K_TILE):
            vt = kv[t * K_TILE:(t + 1) * K_TILE, col * LANES:(col + 1) * LANES].T
            for g in range(NSA_KV_GROUPS):
                ref[g, t] = jnp.concatenate([vt[g * HEAD_DIM:(g + 1) * HEAD_DIM], tail], axis=0).astype(BF16)


def _inproj_kernel(seq, x_ref, g_ref, w_ref, gk_ref, rw_ref, q_ref, kvc_ref, gab_ref, gn_ref,
                   ksx_ref, kwn_ref, vs_ref, vw_ref):
    tm = x_ref.shape[0]
    h = _rms(x_ref[...], g_ref[...]).astype(BF16)
    off = 0
    for ref in (rw_ref, q_ref, kvc_ref, None, gab_ref, gn_ref):
        n = 4 * NSA_KV_WIDTH if ref is None else ref.shape[-1]
        y = jnp.dot(h, w_ref[:, off:off + n], preferred_element_type=F32)
        if ref is None:
            _kv_prep(y, gk_ref[...], lax.rem(pl.program_id(0) * tm, seq), ksx_ref, kwn_ref, vs_ref, vw_ref)
        else:
            ref[...] = y.astype(ref.dtype)
        off += n


def _inproj(x2, g, w, gk2, widths, dtypes, seq, tm):
    n, d = x2.shape
    vt_spec = lambda: pl.BlockSpec((2, tm // K_TILE, VT_ROWS, K_TILE), lambda i: (0, i, 0, 0))
    vt_shape = jax.ShapeDtypeStruct((2, n // K_TILE, VT_ROWS, K_TILE), BF16)
    key_spec = lambda: pl.BlockSpec((2, tm, LANES), lambda i: (0, i, 0))
    key_shape = jax.ShapeDtypeStruct((2, n, LANES), BF16)
    return pl.pallas_call(
        functools.partial(_inproj_kernel, seq),
        grid=(n // tm,),
        in_specs=[pl.BlockSpec((tm, d), lambda i: (i, 0)),
                  pl.BlockSpec((1, d), lambda i: (0, 0)),
                  pl.BlockSpec(w.shape, lambda i: (0, 0), pipeline_mode=pl.Buffered(1)),
                  pl.BlockSpec((1, LANES), lambda i: (0, 0))],
        out_specs=[pl.BlockSpec((tm, wd), lambda i: (i, 0)) for wd in widths]
        + [key_spec(), key_spec(), vt_spec(), vt_spec()],
        out_shape=[jax.ShapeDtypeStruct((n, wd), dt) for wd, dt in zip(widths, dtypes)]
        + [key_shape, key_shape, vt_shape, vt_shape],
        compiler_params=_params("parallel"),
        name="inproj",
    )(x2, g, w, gk2)


def _compress_kernel(k_ref, v_ref, pe_ref, w1_ref, w2_ref, gk_ref, kc_ref, vc_ref):
    nchunk = k_ref.shape[0] // CMP_STRIDE
    span = CMP_BLOCK // CMP_STRIDE
    G = NSA_KV_GROUPS
    lo = lax.broadcasted_iota(jnp.int32, (nchunk, LANES), 1) < HEAD_DIM
    acc = [[[jnp.zeros((nchunk, CMP_HIDDEN), F32) for _ in range(span)] for _ in range(G)] for _ in range(2)]
    for p0 in range(0, CMP_STRIDE, CMP_PACK):
        for s, ref in enumerate((k_ref, v_ref)):
            xs = [ref[pl.ds(p0 + i, nchunk, stride=CMP_STRIDE), :] for i in range(CMP_PACK)]
            for half in range(span):
                q0 = half * CMP_STRIDE + p0
                xq = [xs[i] + pe_ref[s, q0 + i:q0 + i + 1, :] for i in range(CMP_PACK)]
                for g in range(G):
                    pairs = [jnp.where(lo, xq[i], pltpu.roll(xq[i + 1], HEAD_DIM, axis=1)) if g == 0 else
                             jnp.where(lo, pltpu.roll(xq[i], HEAD_DIM, axis=1), xq[i + 1])
                             for i in range(0, CMP_PACK, 2)]
                    acc[s][g][half] = acc[s][g][half] + _mm(jnp.concatenate(pairs, axis=1),
                                                            w1_ref[s, q0 // CMP_PACK])
    for s, out_ref in ((0, kc_ref), (1, vc_ref)):
        for g in range(G):
            hid = acc[s][g][0]
            for half in range(1, span):
                hid = hid + pltpu.roll(acc[s][g][half], nchunk - half, axis=0)
            out = _mm(jax.nn.gelu(hid), w2_ref[s])
            out_ref[0, g] = _rms(out, gk_ref[...]) if s == 0 else out


def _compress(kv, pe, w1, w2, gk, batch, seq):
    nchunk = seq // CMP_STRIDE
    out_spec = lambda: pl.BlockSpec((1, NSA_KV_GROUPS, nchunk, HEAD_DIM), lambda b: (b, 0, 0, 0))
    out_shape = jax.ShapeDtypeStruct((batch, NSA_KV_GROUPS, nchunk, HEAD_DIM), F32)
    const = lambda a: pl.BlockSpec(a.shape, lambda b: (0,) * a.ndim)
    return pl.pallas_call(
        _compress_kernel,
        grid=(batch,),
        in_specs=[pl.BlockSpec((seq, NSA_KV_WIDTH), lambda b: (b, 0)), pl.BlockSpec((seq, NSA_KV_WIDTH), lambda b: (b, 1)),
                  const(pe), const(w1), const(w2), const(gk)],
        out_specs=[out_spec(), out_spec()],
        out_shape=[out_shape, out_shape],
        compiler_params=_params("parallel"),
        name="compress",
    )(kv, kv, pe, w1, w2, gk)


def _rank_select(scores, cur_t, n_live, rank_ref):
    nb, t = scores[0].shape
    sub = 8
    groups = range(nb // sub)
    rank_ref[...] = jnp.zeros(rank_ref.shape, F32)
    below = lax.broadcasted_iota(jnp.int32, (sub, t), 0)
    for gj in groups:
        @pl.when(gj * sub < n_live)
        def _():
            for n, x in enumerate(scores):
                xs = [x[g * sub:(g + 1) * sub] for g in groups]
                inc = [jnp.zeros((sub, t), F32) for _ in groups]
                for jl in range(sub):
                    j = gj * sub + jl
                    row = jnp.broadcast_to(x[j:j + 1, :], (sub, t))
                    for g in groups:
                        if g > gj:
                            one = jnp.where(row >= xs[g], 1.0, 0.0)
                        elif g < gj:
                            one = jnp.where(row > xs[g], 1.0, 0.0)
                        else:
                            one = jnp.where(below > jl, jnp.where(row >= xs[g], 1.0, 0.0),
                                            jnp.where(row > xs[g], 1.0, 0.0))
                        inc[g] = inc[g] + one
                rank_ref[n] = rank_ref[n] + jnp.concatenate(inc, axis=0)
    jj = lax.broadcasted_iota(jnp.int32, (nb, t), 0)
    live = jj <= cur_t
    return [jnp.where(live, jnp.where(rank_ref[n] < float(min(SLC_TOPK, nb)), 0.0, NEG_INF), NEG_INF)
            for n in range(len(scores))]


def _win_masked(tq, tk):
    return [o for o in range(-(WINDOW // tk), 0) if tq - 1 - o * tk >= WINDOW]


def _scores(ks, ws):
    return [jnp.dot(k, w, preferred_element_type=F32) for k, w in zip(ks, ws)]


def _softmax_update(m_ref, acc_ref, slots, s, vts):
    m_old = [m_ref[i] for i in slots]
    ncol = s[0].shape[1] // LANES
    m_cols = [[] for _ in slots]
    p_cols = [[] for _ in slots]
    for c in range(ncol):
        cols = slice(c * LANES, (c + 1) * LANES)
        for n, x in enumerate(s):
            xc = x[:, cols]
            mc = jnp.maximum(m_old[n][:, cols], jnp.max(xc, axis=0, keepdims=True))
            m_cols[n].append(mc)
            p_cols[n].append(jnp.exp2(xc - mc).astype(BF16))
    m_new = [jnp.concatenate(mc, axis=1) for mc in m_cols]
    pv = [jnp.dot(vt, jnp.concatenate(pc, axis=1), preferred_element_type=F32) for vt, pc in zip(vts, p_cols)]
    acc = [jnp.exp2(mo - mn) * acc_ref[i] + z for mo, mn, i, z in zip(m_old, m_new, slots, pv)]
    for i, mn, ac in zip(slots, m_new, acc):
        m_ref[i] = mn
        acc_ref[i] = ac


def _attn_kernel(q_ref, gn_ref, kc_ref, vct_ref, ksx_ref, vst_ref, kwn_ref, vwt_ref, bc_ref, bn_ref, wm_ref,
                 gq_ref, wov_ref, o_ref, m_ref, acc_ref, sa_ref, sb_ref, rank_ref):
    tq = q_ref.shape[0]
    tk = K_TILE
    nblk = wov_ref.shape[0]
    G, R = NSA_KV_GROUPS, NSA_GROUP
    qt = pl.program_id(1)
    n_q = tq // tk
    n_b = -(-REL_MAX_DIST // tk)
    j0 = qt * n_q
    qT = q_ref[...].astype(F32).T
    gates = jax.nn.sigmoid(gn_ref[...].T)
    gq = gq_ref[...]

    qn, qx, negs, o_cmp, scores = [], [], [], [], []
    ncmp = kc_ref.shape[2]
    step = tq // CMP_STRIDE
    bc_rows = pl.ds(pl.multiple_of(bc_ref.shape[2] - ncmp - qt * step, 8), ncmp)
    per_head = tq // LANES
    for g in range(G):
        xs = []
        for h in range(g * R, (g + 1) * R):
            x = qT[h * HEAD_DIM:(h + 1) * HEAD_DIM]
            x = x * lax.rsqrt(jnp.mean(x * x, axis=0, keepdims=True) + RMS_EPS) * gq * (HEAD_DIM ** -0.5 * LOG2E)
            xs.append(x.astype(BF16))
        qn.append(jnp.concatenate(xs, axis=1))

    for g in range(G):
        s = jnp.dot(kc_ref[0, g].astype(BF16), qn[g], preferred_element_type=F32)
        p_cols = []
        for c in range(R * tq // LANES):
            cols = slice(c * LANES, (c + 1) * LANES)
            t_cols = slice(c % per_head * LANES, (c % per_head + 1) * LANES)
            sc = s[:, cols] + bc_ref[g, c // per_head, bc_rows, t_cols]
            e = jnp.exp2(sc - jnp.maximum(jnp.max(sc, axis=0, keepdims=True), M_FLOOR))
            p_cols.append(e * (1.0 / jnp.maximum(jnp.sum(e, axis=0, keepdims=True), 1e-30)))
        o_cmp.append(jnp.dot(vct_ref[0, g].astype(BF16), jnp.concatenate(p_cols, axis=1).astype(BF16),
                             preferred_element_type=F32))
        psum = jnp.concatenate([sum(p_cols[r * per_head + t] for r in range(R)) for t in range(per_head)], axis=1)
        imp = _mm_f32(wov_ref[...], psum)

        blk = lax.broadcasted_iota(jnp.int32, imp.shape, 0)
        cur = (qt * tq + lax.broadcasted_iota(jnp.int32, imp.shape, 1)) // SLC_BLOCK
        forced = (blk == 0) | (blk == cur) | (blk == cur - 1)
        scores.append(jnp.where(forced, FORCE, jnp.where(blk <= cur, imp, -FORCE)))

    cur = (qt * tq + lax.broadcasted_iota(jnp.int32, (nblk, tq), 1)) // SLC_BLOCK
    for g, neg in enumerate(_rank_select(scores, cur, (qt + 1) * (tq // SLC_BLOCK), rank_ref)):
        if nblk < LANES - HEAD_DIM:
            neg = jnp.concatenate([neg, jnp.zeros((LANES - HEAD_DIM - nblk, tq), F32)], axis=0)
        neg = neg.astype(BF16)
        negs.append(neg)
        qx.append(jnp.concatenate([qn[g], jnp.concatenate([neg] * R, axis=1)], axis=0))

    m_ref[...] = jnp.full(m_ref.shape, M_FLOOR, F32)
    acc_ref[...] = jnp.zeros(acc_ref.shape, F32)

    def invalid_before(level):
        return jnp.where(qt >= level, 0.0, NEG_INF)

    def slc_queries(g, level):
        if level == 0:
            return qx[g]
        rows = jnp.minimum(negs[g], invalid_before(level).astype(BF16))
        return jnp.concatenate([qn[g], jnp.concatenate([rows] * R, axis=1)], axis=0)

    def win_queries(g, level):
        flag_row = lax.broadcasted_iota(jnp.int32, (LANES - HEAD_DIM, R * tq), 0) == 0
        extra = jnp.where(flag_row, invalid_before(level), 0.0).astype(BF16)
        return jnp.concatenate([qn[g], extra], axis=0)

    def slc_probs(o, bias):
        j = jnp.maximum(j0 + o, 0)
        sl = pl.ds(pl.multiple_of(j * tk, tk), tk)
        return [(g, ksx_ref[g, sl, :], vst_ref[g, j], slc_queries(g, max(-(o // n_q), 0)),
                 functools.partial(bias, g)) for g in range(G)]

    def win_probs(o, bias):
        j = jnp.maximum(j0 + o, 0)
        sl = pl.ds(pl.multiple_of(j * tk, tk), tk)
        return [(G + g, kwn_ref[g, sl, :], vwt_ref[g, j], win_queries(g, max(-(o // n_q), 0)),
                 None if bias is None else functools.partial(bias, g)) for g in range(G)]

    n_far = jnp.maximum(j0 - n_b, 0)
    groups = list(range(G))

    def far_scores(j, buf, gs=groups):
        sl = pl.ds(pl.multiple_of(j * tk, tk), tk)
        for g, x in zip(gs, _scores([ksx_ref[g, sl, :] for g in gs], [qx[g] for g in gs])):
            buf[g] = x

    def far_update(j, buf, gs=groups):
        _softmax_update(m_ref, acc_ref, gs, [buf[g] for g in gs], [vst_ref[g, j] for g in gs])

    far_scores(0, sa_ref)

    def far_body(i, carry):
        for g in groups:
            far_scores(2 * i + 1, sb_ref, [g])
            far_update(2 * i, sa_ref, [g])
        for g in groups:
            far_scores(2 * i + 2, sa_ref, [g])
            far_update(2 * i + 1, sb_ref, [g])
        return carry

    lax.fori_loop(0, n_far // 2, far_body, 0)

    @pl.when(n_far % 2 == 1)
    def _():
        far_update(n_far - 1, sa_ref)

    masked = _win_masked(tq, tk)
    rounds = []
    for o in range(-(WINDOW // tk), n_q):
        if o >= -n_b:
            near = lambda g, r, i=o + n_b: bn_ref[g, r, i]
            rounds.append(slc_probs(o, near) + win_probs(o, near))
        elif o in masked:
            rounds.append(win_probs(o, lambda g, r, i=masked.index(o): wm_ref[i]))
        else:
            rounds.append(win_probs(o, None))

    def round_scores(probs):
        sc = _scores([p[1] for p in probs], [p[3] for p in probs])
        return [x if p[4] is None else
                jnp.concatenate([x[:, r * tq:(r + 1) * tq] + p[4](r) for r in range(R)], axis=1)
                for x, p in zip(sc, probs)]

    sc = round_scores(rounds[0])
    for i, probs in enumerate(rounds):
        sc_next = round_scores(rounds[i + 1]) if i + 1 < len(rounds) else None
        _softmax_update(m_ref, acc_ref, [p[0] for p in probs], sc, [p[2] for p in probs])
        sc = sc_next

    outs = []
    for g in range(G):
        acc = acc_ref[g]
        o_slc = acc[:HEAD_DIM] / acc[HEAD_DIM:HEAD_DIM + 1]
        acc = acc_ref[G + g]
        o_win = acc[:HEAD_DIM] / acc[HEAD_DIM:HEAD_DIM + 1]
        for r in range(R):
            h = g * R + r
            cols = slice(r * tq, (r + 1) * tq)
            outs.append(gates[3 * h:3 * h + 1] * o_cmp[g][:, cols] + gates[3 * h + 1:3 * h + 2] * o_slc[:, cols]
                        + gates[3 * h + 2:3 * h + 3] * o_win[:, cols])
    o_ref[...] = jnp.concatenate(outs, axis=0).T.astype(o_ref.dtype)


def _attention(q, gn, kc, vct, ksx, vst, kwn, vwt, bias_c, bias_near, win_mask, gq, wov, batch, seq):
    n = q.shape[0]
    tq = Q_TILE
    nq = seq // tq
    ncmp = kc.shape[2]
    row = lambda b, i: (b * nq + i, 0)
    whole = lambda b, i: (0, b, 0)
    tiles = lambda b, i: (0, b, 0, 0)
    const = lambda a: pl.BlockSpec(a.shape, lambda b, i: (0,) * a.ndim, pipeline_mode=pl.Buffered(1))
    return pl.pallas_call(
        _attn_kernel,
        grid=(batch, nq),
        in_specs=[pl.BlockSpec((tq, NSA_WIDTH), row),
                  pl.BlockSpec((tq, LANES), row),
                  pl.BlockSpec((1, NSA_KV_GROUPS, ncmp, HEAD_DIM), lambda b, i: (b, 0, 0, 0)),
                  pl.BlockSpec((1, NSA_KV_GROUPS, HEAD_DIM, ncmp), lambda b, i: (b, 0, 0, 0)),
                  pl.BlockSpec((2, seq, LANES), whole),
                  pl.BlockSpec((2, seq // K_TILE, VT_ROWS, K_TILE), tiles),
                  pl.BlockSpec((2, seq, LANES), whole),
                  pl.BlockSpec((2, seq // K_TILE, VT_ROWS, K_TILE), tiles),
                  const(bias_c),
                  const(bias_near), const(win_mask), const(gq), const(wov)],
        out_specs=pl.BlockSpec((tq, NSA_WIDTH), row),
        out_shape=jax.ShapeDtypeStruct((n, NSA_WIDTH), BF16),
        scratch_shapes=[pltpu.VMEM((2 * NSA_KV_GROUPS, 1, NSA_GROUP * tq), F32),
                        pltpu.VMEM((2 * NSA_KV_GROUPS, VT_ROWS, NSA_GROUP * tq), F32),
                        pltpu.VMEM((NSA_KV_GROUPS, K_TILE, NSA_GROUP * tq), F32),
                        pltpu.VMEM((NSA_KV_GROUPS, K_TILE, NSA_GROUP * tq), F32),
                        pltpu.VMEM((NSA_KV_GROUPS, seq // SLC_BLOCK, tq), F32)],
        compiler_params=_params("parallel", "parallel"),
        name="nsa_attention",
    )(q, gn, kc, vct, ksx, vst, kwn, vwt, bias_c, bias_near, win_mask, gq, wov)


def _split_bf16(z, parts):
    out = []
    for _ in range(parts - 1):
        hi = z.astype(BF16)
        out.append(hi)
        z = z - hi.astype(F32)
    return out + [z.astype(BF16)]


def _head_sums(z, ones):
    width = ones.shape[0]
    nb = z.shape[1] // width
    rows = z.shape[0]
    zb = z.astype(BF16)
    stacked = jnp.concatenate([zb[:, m * width:(m + 1) * width] for m in range(nb)], axis=0)
    sums = jnp.dot(stacked, ones, preferred_element_type=F32)
    return jnp.concatenate([sums[m * rows:(m + 1) * rows] for m in range(nb)], axis=1)


def _softplus(z):
    return jnp.maximum(z, 0.0) + jnp.log(1.0 + jnp.exp(-jnp.abs(z)))


def _rwkv_chunk(x, prev, st, mu_ref, w0_ref, w2_ref, a0_ref, a2_ref, g2_ref, kk_ref, ka_ref, rk_ref,
                lng_ref, lnb_ref, ones_ref):
    L = x.shape[0]
    W = RWKV_WIDTH
    N = HEAD_DIM
    row_id = lax.broadcasted_iota(jnp.int32, x.shape, 0)
    shifted = jnp.where(row_id == 0, prev, pltpu.roll(x, 1, axis=0))
    xl = x + (shifted - x) * mu_ref[...]
    yield None
    r = xl[:, 0:W]
    k = xl[:, W:2 * W]
    v = xl[:, 2 * W:3 * W]
    xw = xl[:, 3 * W:3 * W + LORA_W]
    xa = xl[:, 3 * W + LORA_W:3 * W + LORA_W + LORA_A]
    xg = xl[:, 3 * W + LORA_W + LORA_A:]
    w = -_softplus(-(w0_ref[...] + _mm(jnp.tanh(xw), w2_ref[...]))) - 0.5
    ld = -jnp.exp(w)
    a = jax.nn.sigmoid(a0_ref[...] + _mm(xa, a2_ref[...]))
    gate = _mm(jax.nn.sigmoid(xg), g2_ref[...])
    kkv = k * kk_ref[...]
    k2 = k * (1.0 + (a - 1.0) * ka_ref[...])
    yield None

    ti = lax.broadcasted_iota(jnp.int32, (L, L), 0)
    si = lax.broadcasted_iota(jnp.int32, (L, L), 1)
    incl = si <= ti
    strict = si < ti
    tri = jnp.where(incl, 1.0, 0.0).astype(BF16)
    cl3 = jnp.dot(tri, jnp.concatenate(_split_bf16(ld, 3), axis=1), preferred_element_type=F32)
    cl = cl3[:, :W] + (cl3[:, W:2 * W] + cl3[:, 2 * W:])
    cl_end = cl[L - 1:L, :]
    yield None
    e_pos = jnp.exp(cl)
    e_neg = jnp.exp(-cl)
    e_prev = jnp.exp(cl - ld)
    e_end = jnp.exp(cl_end - cl)
    eye = jnp.where(ti == si, 1.0, 0.0)
    yield None

    hsum = lambda z: _head_sums(z, ones_ref[...])

    kk_n = kkv * lax.rsqrt(jnp.maximum(hsum(kkv * kkv), 1e-24))
    bv = kk_n * a
    yield None
    a_t = (-kk_n * e_prev).astype(BF16)
    b_t = (bv * e_neg).astype(BF16)
    k_t = (k2 * e_neg).astype(BF16)
    r_t = (r * e_pos).astype(BF16)
    yield None
    v_b = v.astype(BF16)
    k_e = k2 * e_end
    b_e = bv * e_end

    yield "elementwise done"

    P = range(W // LANES)
    ps = [slice(m * LANES, (m + 1) * LANES) for m in P]
    first = lax.broadcasted_iota(jnp.int32, (1, LANES), 1) < N
    same_head = (lax.broadcasted_iota(jnp.int32, (LANES, LANES), 0) // N
                 == lax.broadcasted_iota(jnp.int32, (LANES, LANES), 1) // N)

    def diag_rows(x):
        zero = jnp.zeros_like(x)
        return jnp.concatenate([jnp.where(first, x, zero), jnp.where(first, zero, x)], axis=0)

    def diag_blocks(x):
        zero = jnp.zeros_like(x[:, :LANES])
        return jnp.concatenate([jnp.concatenate([x[:, :LANES], zero], axis=1),
                                jnp.concatenate([zero, x[:, LANES:]], axis=1)], axis=0)

    ke_t = [k_e[:, s].T for s in ps]
    be_t = [b_e[:, s].T for s in ps]
    p_end = [e_pos[:, s].T[:, L - 1:L] for s in ps]

    lhs = [jnp.concatenate([a_t[:, s], r_t[:, s]], axis=0) for s in ps]
    rhs = [jnp.concatenate([k_t[:, s], b_t[:, s]], axis=0) for s in ps]
    zero_b = jnp.zeros_like(lhs[0])
    aa = [[lax.dot_general(jnp.where(first, x, zero_b) if j == 0 else jnp.where(first, zero_b, x), y, _NT,
                           preferred_element_type=F32) for j in range(2)] for x, y in zip(lhs, rhs)]
    cat2 = lambda f: [jnp.concatenate([f(pair[0]), f(pair[1])], axis=1) for pair in aa]
    a_ak = cat2(lambda x: jnp.where(strict, x[:L, :L], 0.0))
    a_ab = cat2(lambda x: jnp.where(strict, x[:L, L:], 0.0))
    a_rk = cat2(lambda x: jnp.where(incl, x[L:, :L], 0.0))
    a_rb = cat2(lambda x: jnp.where(incl, x[L:, L:], 0.0))
    yield None

    eye2 = jnp.concatenate([eye, eye], axis=1)
    tinv = [eye2 + x for x in a_ab]
    pw = [_mm(x, diag_blocks(x)) for x in a_ab]
    yield None
    span = 2
    while 2 * span < L:
        both = [_mm(jnp.concatenate([t, p], axis=0), diag_blocks(p)) for t, p in zip(tinv, pw)]
        tinv = [t + x[:L] for t, x in zip(tinv, both)]
        pw = [x[L:] for x in both]
        yield None
        span *= 2
    tinv = [t + _mm(t, diag_blocks(p)) for t, p in zip(tinv, pw)]
    yield None

    v_d = [diag_rows(v_b[:, s]) for s in ps]
    av = [_mm(jnp.concatenate([a_ak[m], a_rk[m]], axis=0), v_d[m]) for m in P]
    tw = [_mm(tinv[m], jnp.concatenate([diag_rows(a_t[:, ps[m]]), diag_rows(av[m][:L].astype(BF16))], axis=1))
          for m in P]
    kv_loc = [jnp.where(same_head, _mm(ke_t[m], v_b[:, ps[m]]), 0.0) for m in P]
    yield None

    ws = [_mm(jnp.concatenate([tw[m][:, :LANES].astype(BF16), r_t[:, ps[m]]], axis=0), st[m]) for m in P]
    u = [ws[m][:L] + tw[m][:, LANES:] for m in P]
    yield None
    y = [ws[m][L:] + av[m][L:] + _mm(a_rb[m], diag_rows(u[m])) for m in P]
    st_new = [p_end[m] * st[m] + kv_loc[m] + jnp.where(same_head, _mm(be_t[m], u[m]), 0.0) for m in P]
    yield None

    y = jnp.concatenate(y, axis=1)
    d = y - hsum(y) * (1.0 / N)
    yn = d * lax.rsqrt(hsum(d * d) * (1.0 / N) + GN_EPS) * lng_ref[...] + lnb_ref[...]
    bonus = hsum(r * k2 * rk_ref[...])
    yield (yn + bonus * v) * gate, st_new


def _rwkv_kernel(x_ref, xp_ref, *refs):
    *param_refs, o_ref, st_ref = refs
    c = pl.program_id(1)

    @pl.when(c == 0)
    def _():
        st_ref[...] = jnp.zeros(st_ref.shape, F32)

    chunks = []
    for i in range(x_ref.shape[0]):
        prev = jnp.where(c > 0, xp_ref[i, xp_ref.shape[1] - 1:, :], 0.0)
        st = [st_ref[i, m] for m in range(st_ref.shape[1])]
        chunks.append(_rwkv_chunk(x_ref[i], prev, st, *param_refs))
    def advance(gen, until_result):
        item = next(gen)
        return (isinstance(item, tuple), item) if until_result else (item == "elementwise done", item)

    while not advance(chunks[0], False)[0]:
        pass
    outs, states = [], []
    for i, chunk in enumerate(chunks):
        following = chunks[i + 1] if i + 1 < len(chunks) else None
        result = None
        while result is None or following is not None:
            if result is None:
                done, item = advance(chunk, True)
                result = item if done else None
            if following is not None and advance(following, False)[0]:
                following = None
        outs.append(result[0].astype(o_ref.dtype))
        states.append(jnp.stack(result[1]))
    o_ref[...] = jnp.stack(outs)
    st_ref[...] = jnp.stack(states)


def _rwkv(rw, vecs, w2, a2, g2, batch, seq):
    width = rw.shape[1]
    L = min(CHUNK, seq)
    nc = seq // L
    nb = RWKV_SEQS if batch % RWKV_SEQS == 0 else 1
    sub = 8
    rw3 = rw.reshape(batch, seq, width)
    vec = lambda wd: pl.BlockSpec((1, wd), lambda b, c: (0, 0))
    mat = lambda m: pl.BlockSpec(m.shape, lambda b, c: (0, 0))
    mu, w0, a0, kk, ka, rk, lng, lnb = vecs
    head = np.arange(2 * LANES) // HEAD_DIM
    ones = jnp.asarray(head[:, None] == head[None, :], BF16)
    out = pl.pallas_call(
        _rwkv_kernel,
        grid=(batch // nb, nc),
        in_specs=[pl.BlockSpec((nb, L, width), lambda b, c: (b, c, 0)),
                  pl.BlockSpec((nb, sub, width), lambda b, c: (b, jnp.maximum(c * (L // sub) - 1, 0), 0)),
                  vec(width), vec(RWKV_WIDTH), mat(w2), vec(RWKV_WIDTH), mat(a2), mat(g2),
                  vec(RWKV_WIDTH), vec(RWKV_WIDTH), vec(RWKV_WIDTH), vec(RWKV_WIDTH), vec(RWKV_WIDTH), mat(ones)],
        out_specs=pl.BlockSpec((nb, L, RWKV_WIDTH), lambda b, c: (b, c, 0)),
        out_shape=jax.ShapeDtypeStruct((batch, seq, RWKV_WIDTH), BF16),
        scratch_shapes=[pltpu.VMEM((nb, RWKV_WIDTH // LANES, LANES, LANES), F32)],
        compiler_params=_params("parallel", "arbitrary"),
        name="rwkv7",
    )(rw3, rw3, mu, w0, w2, a0, a2, g2, kk, ka, rk, lng, lnb, ones)
    return out.reshape(batch * seq, RWKV_WIDTH)


def _merge_kernel(x_ref, oa_ref, ob_ref, gab_ref, wpa_ref, wpb_ref, wo_ref, o_ref):
    d = x_ref.shape[1]
    pa = jnp.dot(oa_ref[...], wpa_ref[...], preferred_element_type=F32)
    pb = jnp.dot(ob_ref[...], wpb_ref[...], preferred_element_type=F32)
    gab = gab_ref[...].astype(F32)
    merged = jax.nn.sigmoid(gab[:, :d]) * pa + jax.nn.sigmoid(gab[:, d:]) * pb
    o_ref[...] = x_ref[...] + _mm(merged, wo_ref[...])


def _merge(x2, oa, ob, gab, wpa, wpb, wo, tm):
    n, d = x2.shape
    row = lambda wd: pl.BlockSpec((tm, wd), lambda i: (i, 0))
    mat = lambda m: pl.BlockSpec(m.shape, lambda i: (0, 0))
    return pl.pallas_call(
        _merge_kernel,
        grid=(n // tm,),
        in_specs=[row(d), row(oa.shape[1]), row(ob.shape[1]), row(gab.shape[1]), mat(wpa), mat(wpb), mat(wo)],
        out_specs=row(d),
        out_shape=jax.ShapeDtypeStruct((n, d), F32),
        compiler_params=_params("parallel"),
        name="merge",
    )(x2, oa, ob, gab, wpa, wpb, wo)


def _ffn_kernel(seq, x_ref, xh_ref, g_ref, wv_ref, wg_ref, cwv_ref, cwg_ref, cbv_ref, cbg_ref, wd_ref, o_ref,
                h_ref, uv_ref, ug_ref):
    tm = x_ref.shape[0]
    i = pl.program_id(0)
    f = pl.program_id(1)

    @pl.when(f == 0)
    def _():
        first = lax.rem(i * tm, seq) == 0
        halo = jnp.where(first, 0.0, _rms(xh_ref[...], g_ref[...]))
        h_ref[0:HALO, :] = halo.astype(BF16)
        h_ref[HALO:, :] = _rms(x_ref[...], g_ref[...]).astype(BF16)

    h = h_ref[...]
    uv_ref[...] = jnp.dot(h, wv_ref[...], preferred_element_type=F32)
    ug_ref[...] = jnp.dot(h, wg_ref[...], preferred_element_type=F32)

    def conv(u_ref, cw_ref, cb_ref):
        acc = cb_ref[...] + cw_ref[0:1, :] * u_ref[pl.ds(HALO - 2, tm), :]
        acc = acc + cw_ref[1:2, :] * u_ref[pl.ds(HALO - 1, tm), :]
        return acc + cw_ref[2:3, :] * u_ref[pl.ds(HALO, tm), :]

    val = conv(uv_ref, cwv_ref, cbv_ref)
    gt = conv(ug_ref, cwg_ref, cbg_ref)
    y = _mm(gt * jax.nn.sigmoid(gt) * val, wd_ref[...])

    @pl.when(f == 0)
    def _():
        o_ref[...] = x_ref[...] + y

    @pl.when(f > 0)
    def _():
        o_ref[...] = o_ref[...] + y


def _ffn(x1, g, w_up, conv_w, conv_b, w_down, seq, tm, tf):
    n, d = x1.shape
    dff = w_down.shape[0]
    nf = dff // tf
    return pl.pallas_call(
        functools.partial(_ffn_kernel, seq),
        grid=(n // tm, nf),
        in_specs=[pl.BlockSpec((tm, d), lambda i, f: (i, 0)),
                  pl.BlockSpec((HALO, d), lambda i, f: (jnp.maximum(i * (tm // HALO) - 1, 0), 0)),
                  pl.BlockSpec((1, d), lambda i, f: (0, 0)),
                  pl.BlockSpec((d, tf), lambda i, f: (0, f)),
                  pl.BlockSpec((d, tf), lambda i, f: (0, nf + f)),
                  pl.BlockSpec((CONV_WIDTH, tf), lambda i, f: (0, f)),
                  pl.BlockSpec((CONV_WIDTH, tf), lambda i, f: (0, nf + f)),
                  pl.BlockSpec((1, tf), lambda i, f: (0, f)),
                  pl.BlockSpec((1, tf), lambda i, f: (0, nf + f)),
                  pl.BlockSpec((tf, d), lambda i, f: (f, 0))],
        out_specs=pl.BlockSpec((tm, d), lambda i, f: (i, 0)),
        out_shape=jax.ShapeDtypeStruct((n, d), F32),
        scratch_shapes=[pltpu.VMEM((tm + HALO, d), BF16),
                        pltpu.VMEM((tm + HALO, tf), F32),
                        pltpu.VMEM((tm + HALO, tf), F32)],
        compiler_params=_params("parallel", "arbitrary"),
        name="convffn",
    )(x1, x1, g, w_up, w_up, conv_w, conv_w, conv_b, conv_b, w_down)


def _t5_bucket(dist):
    n = np.maximum(dist, 0)
    max_exact = REL_BUCKETS // 2
    ratio = np.log(np.maximum(n, 1).astype(np.float32) / max_exact) / math.log(REL_MAX_DIST / max_exact)
    large = np.minimum(max_exact + (ratio * (REL_BUCKETS - max_exact)).astype(np.int32), REL_BUCKETS - 1)
    return np.where(n < max_exact, n, large).astype(np.int32)


def _bias_tables(rel_bias, seq, ncmp):
    tq, tk = Q_TILE, K_TILE
    G, R = NSA_KV_GROUPS, NSA_GROUP
    nq = seq // tq
    step = tq // CMP_STRIDE
    tab = rel_bias.astype(F32)

    off = (nq - 1) * step
    d_c = np.arange(tq)[None, :] - (np.arange(ncmp + off)[:, None] - off) * CMP_STRIDE - (CMP_BLOCK - 1)
    n_b = -(-REL_MAX_DIST // tk)
    d0 = np.arange(tq)[None, :] - np.arange(tk)[:, None]
    d_n = np.stack([d0 - o * tk for o in range(-n_b, tq // tk)])

    buckets = np.concatenate([_t5_bucket(d_c).reshape(-1), _t5_bucket(d_n).reshape(-1)])
    onehot = (jnp.arange(REL_BUCKETS, dtype=jnp.int32)[:, None] == jnp.asarray(buckets)[None, :]).astype(F32)
    vals = jnp.dot(tab.T * LOG2E, onehot, precision=lax.Precision.HIGHEST, preferred_element_type=F32)
    base = jnp.where(jnp.asarray(d_c >= 0), vals[:, :d_c.size].reshape(G, R, ncmp + off, tq), NEG_INF)

    far = (tab[REL_BUCKETS - 1] * LOG2E).reshape(G, R, 1, 1, 1)
    near = vals[:, d_c.size:].reshape(G, R, d_n.shape[0], tk, tq) - far
    bias_near = jnp.where(jnp.asarray(d_n >= 0), near, NEG_INF)
    win = np.stack([np.where(d0 - o * tk < WINDOW, 0.0, NEG_INF) for o in _win_masked(tq, tk)])
    return base, bias_near, jnp.asarray(win.astype(np.float32))


def _overlap_matrix(ncmp_pad, n_slc, n_cmp):
    m = np.zeros((ncmp_pad, n_slc), np.float32)
    ratio = SLC_BLOCK // CMP_STRIDE
    for j in range(n_slc):
        for o, wgt in enumerate(OVERLAP_W):
            cidx = ratio * j + o - (CMP_BLOCK // CMP_STRIDE - 1)
            if 0 <= cidx < n_cmp:
                m[cidx, j] += wgt
    return jnp.asarray(m.T)


def _layer(x, attn_norm_g, w_in, rel_bias, q_norm_g, k_norm_g, cmp_pe_k, cmp_w1_k, cmp_w2_k,
           cmp_pe_v, cmp_w1_v, cmp_w2_v, rwkv_mu, rwkv_w0, rwkv_w2, rwkv_a0, rwkv_a2, rwkv_g2,
           rwkv_k_k, rwkv_k_a, rwkv_r_k, rwkv_ln_g, rwkv_ln_b, w_proj_a, w_proj_b, w_out,
           ffn_norm_g, w_up, conv_w, conv_b, w_down):
    batch, seq, d = x.shape
    n = batch * seq
    assert seq % Q_TILE == 0 and Q_TILE % K_TILE == 0 and WINDOW % K_TILE == 0
    assert seq % CMP_STRIDE == 0 and seq % FFN_ROWS == 0 and seq % PROJ_ROWS == 0 and PROJ_ROWS % K_TILE == 0
    assert seq // SLC_BLOCK <= LANES - HEAD_DIM
    assert NSA_KV_WIDTH == LANES and CMP_PACK * HEAD_DIM == 2 * LANES
    x2 = x.reshape(n, d)
    row = lambda a: a.reshape(1, -1).astype(F32)

    kvw = 6 * NSA_KV_WIDTH
    o_q, o_kv, o_gn = 0, NSA_WIDTH, NSA_WIDTH + kvw
    o_rw = o_gn + 3 * NSA_HEADS
    o_gab = o_rw + RWKV_IN_WIDTH
    w16 = w_in.astype(BF16)
    gn_pad = jnp.zeros((d, LANES - 3 * NSA_HEADS), BF16)
    w_cat = jnp.concatenate([w16[:, o_rw:o_gab], w16[:, o_q:o_kv], w16[:, o_kv:o_gn], w16[:, o_gab:],
                             w16[:, o_gn:o_rw], gn_pad], axis=1)
    widths = (RWKV_IN_WIDTH, NSA_WIDTH, 2 * NSA_KV_WIDTH, 2 * d, LANES)
    dtypes = (F32, BF16, F32, BF16, F32)
    gk = row(k_norm_g)
    rw, q, kv_cmp, gab, gn, ksx, kwn, vst, vwt = _inproj(
        x2, row(attn_norm_g), w_cat, jnp.concatenate([gk, gk], axis=1), widths, dtypes, seq, tm=PROJ_ROWS)

    nchunk = seq // CMP_STRIDE
    n_cmp = nchunk - CMP_BLOCK // CMP_STRIDE + 1
    pe = jnp.stack([cmp_pe_k, cmp_pe_v]).astype(F32)
    pe = jnp.concatenate([pe] * NSA_KV_GROUPS, axis=-1)
    w1 = jnp.stack([cmp_w1_k, cmp_w1_v]).astype(BF16)
    w1 = w1.reshape(2, CMP_BLOCK // CMP_PACK, CMP_PACK * HEAD_DIM, CMP_HIDDEN)
    w2 = jnp.stack([cmp_w2_k, cmp_w2_v]).astype(BF16)
    kc, vc = _compress(kv_cmp, pe, w1, w2, gk, batch, seq)

    bias_c, bias_near, win_mask = _bias_tables(rel_bias, seq, nchunk)
    wov = _overlap_matrix(nchunk, seq // SLC_BLOCK, n_cmp)
    o_a = _attention(q, gn, kc, jnp.swapaxes(vc, -1, -2), ksx, vst, kwn, vwt, bias_c, bias_near, win_mask,
                     q_norm_g.reshape(-1, 1).astype(F32), wov, batch, seq)

    vecs = tuple(row(a) for a in (rwkv_mu, rwkv_w0, rwkv_a0, rwkv_k_k, rwkv_k_a, rwkv_r_k, rwkv_ln_g, rwkv_ln_b))
    o_b = _rwkv(rw, vecs, rwkv_w2.astype(BF16), rwkv_a2.astype(BF16), rwkv_g2.astype(BF16), batch, seq)

    x1 = _merge(x2, o_a, o_b, gab, w_proj_a.astype(BF16), w_proj_b.astype(BF16), w_out.astype(BF16), tm=FFN_ROWS)
    dff = w_down.shape[0]
    out = _ffn(x1, row(ffn_norm_g), w_up.astype(BF16), conv_w.astype(F32), row(conv_b), w_down.astype(BF16),
               seq, tm=FFN_ROWS, tf=dff // FFN_COL_SPLITS)
    return out.reshape(batch, seq, d)


def kernel(x, attn_norm_g, w_in, rel_bias, q_norm_g, k_norm_g, cmp_pe_k, cmp_w1_k, cmp_w2_k, cmp_pe_v, cmp_w1_v,
           cmp_w2_v, rwkv_mu, rwkv_w0, rwkv_w2, rwkv_a0, rwkv_a2, rwkv_g2, rwkv_k_k, rwkv_k_a, rwkv_r_k,
           rwkv_ln_g, rwkv_ln_b, w_proj_a, w_proj_b, w_out, ffn_norm_g, w_up, conv_w, conv_b, w_down):
    per_layer = (attn_norm_g, w_in, None, q_norm_g, k_norm_g, cmp_pe_k, cmp_w1_k, cmp_w2_k, cmp_pe_v, cmp_w1_v,
                 cmp_w2_v, rwkv_mu, rwkv_w0, rwkv_w2, rwkv_a0, rwkv_a2, rwkv_g2, rwkv_k_k, rwkv_k_a, rwkv_r_k,
                 rwkv_ln_g, rwkv_ln_b, w_proj_a, w_proj_b, w_out, ffn_norm_g, w_up, conv_w, conv_b, w_down)
    for l in range(attn_norm_g.shape[0]):
        args = [rel_bias if p is None else p[l] for p in per_layer]
        x = _layer(x, *args)
    return x
```

```python
import functools
import math

import numpy as np
import jax
import jax.numpy as jnp
from jax import lax
from jax.experimental import pallas as pl
from jax.experimental.pallas import tpu as pltpu

F32 = jnp.float32
BF16 = jnp.bfloat16

HEAD_DIM = 64
NSA_HEADS = 8
NSA_KV_GROUPS = 2
NSA_GROUP = NSA_HEADS // NSA_KV_GROUPS
NSA_WIDTH = NSA_HEADS * HEAD_DIM
NSA_KV_WIDTH = NSA_KV_GROUPS * HEAD_DIM
CMP_BLOCK = 32
CMP_STRIDE = 16
CMP_HIDDEN = 256
SLC_BLOCK = 64
SLC_TOPK = 16
OVERLAP_W = (1, 2, 2, 2, 1)
WINDOW = 512
REL_BUCKETS = 32
REL_MAX_DIST = 128
RWKV_HEADS = 8
RWKV_WIDTH = RWKV_HEADS * HEAD_DIM
LORA_W = 64
LORA_A = 64
LORA_G = 128
RWKV_IN_WIDTH = 3 * RWKV_WIDTH + LORA_W + LORA_A + LORA_G
GN_EPS = 64e-5
CONV_WIDTH = 3
RMS_EPS = 1e-6
NEG_INF = -1e30
FORCE = 1e9
LOG2E = math.log2(math.e)
M_FLOOR = -1e20

LANES = 128
VMEM_LIMIT = 56 * 1024 * 1024

Q_TILE = 256
K_TILE = 256
VT_ROWS = 80
CHUNK = 128
RWKV_SEQS = 4
CMP_PACK = 4
HALO = 16
PROJ_ROWS = 512
FFN_ROWS = 1024
FFN_COL_SPLITS = 2

_NT = (((1,), (1,)), ((), ()))


def _params(*sem):
    return pltpu.CompilerParams(dimension_semantics=sem, vmem_limit_bytes=VMEM_LIMIT)


def _mm(a, b):
    return jnp.dot(a.astype(BF16), b.astype(BF16), preferred_element_type=F32)


def _mm_f32(a, b):
    return jnp.dot(a, b, preferred_element_type=F32, precision=lax.Precision.HIGHEST)


def _rms(x, g):
    return x * lax.rsqrt(jnp.mean(x * x, axis=-1, keepdims=True) + RMS_EPS) * g


def _kv_prep(kv, gk, row0, ksx_ref, kwn_ref, vs_ref, vw_ref):
    tm = kv.shape[0]
    lane = lax.broadcasted_iota(jnp.int32, (tm, LANES), 1)
    lo = lane < HEAD_DIM

    def norm2(x):
        x2 = x * x
        s_lo = jnp.sum(jnp.where(lo, x2, 0.0), axis=-1, keepdims=True)
        s_hi = jnp.sum(jnp.where(lo, 0.0, x2), axis=-1, keepdims=True)
        ms = jnp.where(lo, s_lo, s_hi) * (1.0 / HEAD_DIM)
        return x * lax.rsqrt(ms + RMS_EPS) * gk

    ks = norm2(kv[:, 0:LANES])
    kw = norm2(kv[:, 2 * LANES:3 * LANES])
    row = row0 + lax.broadcasted_iota(jnp.int32, (tm, LANES), 0)
    onehot = jnp.where(lane - HEAD_DIM == row // SLC_BLOCK, 1.0, 0.0)
    ksx_ref[0] = jnp.where(lo, ks, onehot).astype(BF16)
    ksx_ref[1] = jnp.where(lo, pltpu.roll(ks, HEAD_DIM, axis=1), onehot).astype(BF16)
    flag_col = jnp.where(lane == HEAD_DIM, 1.0, 0.0)
    kwn_ref[0] = jnp.where(lo, kw, flag_col).astype(BF16)
    kwn_ref[1] = jnp.where(lo, pltpu.roll(kw, HEAD_DIM, axis=1), flag_col).astype(BF16)
    tail = jnp.where(lax.broadcasted_iota(jnp.int32, (VT_ROWS - HEAD_DIM, K_TILE), 0) == 0, 1.0, 0.0)
    for col, ref in ((1, vs_ref), (3, vw_ref)):
        for t in range(tm // K_TILE):
            vt = kv[t * K_TILE:(t + 1) * K_TILE, col * LANES:(col + 1) * LANES].T
            for g in range(NSA_KV_GROUPS):
                ref[g, t] = jnp.concatenate([vt[g * HEAD_DIM:(g + 1) * HEAD_DIM], tail], axis=0).astype(BF16)


def _inproj_kernel(seq, x_ref, g_ref, w_rw, w_q, w_kvc, w_kv, w_gab, w_gn, gk_ref, rw_ref, q_ref, kvc_ref,
                   gab_ref, gn_ref, ksx_ref, kwn_ref, vs_ref, vw_ref):
    tm = x_ref.shape[0]
    h = _rms(x_ref[...], g_ref[...]).astype(BF16)
    for w_ref, ref in ((w_rw, rw_ref), (w_q, q_ref), (w_kvc, kvc_ref), (w_kv, None), (w_gab, gab_ref),
                       (w_gn, gn_ref)):
        y = jnp.dot(h, w_ref[...], preferred_element_type=F32)
        if ref is None:
            _kv_prep(y, gk_ref[...], lax.rem(pl.program_id(0) * tm, seq), ksx_ref, kwn_ref, vs_ref, vw_ref)
        else:
            ref[...] = y.astype(ref.dtype)


def _inproj(x2, g, ws, gk2, widths, dtypes, seq, tm):
    n, d = x2.shape
    vt_spec = lambda: pl.BlockSpec((2, tm // K_TILE, VT_ROWS, K_TILE), lambda i: (0, i, 0, 0))
    vt_shape = jax.ShapeDtypeStruct((2, n // K_TILE, VT_ROWS, K_TILE), BF16)
    key_spec = lambda: pl.BlockSpec((2, tm, LANES), lambda i: (0, i, 0))
    key_shape = jax.ShapeDtypeStruct((2, n, LANES), BF16)
    return pl.pallas_call(
        functools.partial(_inproj_kernel, seq),
        grid=(n // tm,),
        in_specs=[pl.BlockSpec((tm, d), lambda i: (i, 0)),
                  pl.BlockSpec((1, d), lambda i: (0, 0))]
        + [pl.BlockSpec(w.shape, lambda i: (0, 0), pipeline_mode=pl.Buffered(1)) for w in ws]
        + [pl.BlockSpec((1, LANES), lambda i: (0, 0))],
        out_specs=[pl.BlockSpec((tm, wd), lambda i: (i, 0)) for wd in widths]
        + [key_spec(), key_spec(), vt_spec(), vt_spec()],
        out_shape=[jax.ShapeDtypeStruct((n, wd), dt) for wd, dt in zip(widths, dtypes)]
        + [key_shape, key_shape, vt_shape, vt_shape],
        compiler_params=_params("parallel"),
        name="inproj",
    )(x2, g, *ws, gk2)


def _compress_kernel(k_ref, v_ref, pe_ref, w1_ref, w2_ref, gk_ref, kc_ref, vc_ref):
    nchunk = k_ref.shape[0] // CMP_STRIDE
    span = CMP_BLOCK // CMP_STRIDE
    G = NSA_KV_GROUPS
    lo = lax.broadcasted_iota(jnp.int32, (nchunk, LANES), 1) < HEAD_DIM
    acc = [[[jnp.zeros((nchunk, CMP_HIDDEN), F32) for _ in range(span)] for _ in range(G)] for _ in range(2)]
    for p0 in range(0, CMP_STRIDE, CMP_PACK):
        for s, ref in enumerate((k_ref, v_ref)):
            xs = [ref[pl.ds(p0 + i, nchunk, stride=CMP_STRIDE), :] for i in range(CMP_PACK)]
            for half in range(span):
                q0 = half * CMP_STRIDE + p0
                xq = [xs[i] + pe_ref[s, q0 + i:q0 + i + 1, :] for i in range(CMP_PACK)]
                for g in range(G):
                    pairs = [jnp.where(lo, xq[i], pltpu.roll(xq[i + 1], HEAD_DIM, axis=1)) if g == 0 else
                             jnp.where(lo, pltpu.roll(xq[i], HEAD_DIM, axis=1), xq[i + 1])
                             for i in range(0, CMP_PACK, 2)]
                    acc[s][g][half] = acc[s][g][half] + _mm(jnp.concatenate(pairs, axis=1),
                                                            w1_ref[s, q0 // CMP_PACK])
    for s, out_ref in ((0, kc_ref), (1, vc_ref)):
        for g in range(G):
            hid = acc[s][g][0]
            for half in range(1, span):
                hid = hid + pltpu.roll(acc[s][g][half], nchunk - half, axis=0)
            out = _mm(jax.nn.gelu(hid), w2_ref[s])
            out_ref[0, g] = _rms(out, gk_ref[...]) if s == 0 else out


def _compress(kv, pe, w1, w2, gk, batch, seq):
    nchunk = seq // CMP_STRIDE
    out_spec = lambda: pl.BlockSpec((1, NSA_KV_GROUPS, nchunk, HEAD_DIM), lambda b: (b, 0, 0, 0))
    out_shape = jax.ShapeDtypeStruct((batch, NSA_KV_GROUPS, nchunk, HEAD_DIM), F32)
    const = lambda a: pl.BlockSpec(a.shape, lambda b: (0,) * a.ndim)
    return pl.pallas_call(
        _compress_kernel,
        grid=(batch,),
        in_specs=[pl.BlockSpec((seq, NSA_KV_WIDTH), lambda b: (b, 0)), pl.BlockSpec((seq, NSA_KV_WIDTH), lambda b: (b, 1)),
                  const(pe), const(w1), const(w2), const(gk)],
        out_specs=[out_spec(), out_spec()],
        out_shape=[out_shape, out_shape],
        compiler_params=_params("parallel"),
        name="compress",
    )(kv, kv, pe, w1, w2, gk)


def _rank_select(scores, cur_t, n_live, rank_ref):
    nb, t = scores[0].shape
    sub = 8
    groups = range(nb // sub)
    rank_ref[...] = jnp.zeros(rank_ref.shape, F32)
    below = lax.broadcasted_iota(jnp.int32, (sub, t), 0)
    for gj in groups:
        @pl.when(gj * sub < n_live)
        def _():
            for n, x in enumerate(scores):
                xs = [x[g * sub:(g + 1) * sub] for g in groups]
                inc = [jnp.zeros((sub, t), F32) for _ in groups]
                for jl in range(sub):
                    j = gj * sub + jl
                    row = jnp.broadcast_to(x[j:j + 1, :], (sub, t))
                    for g in groups:
                        if g > gj:
                            one = jnp.where(row >= xs[g], 1.0, 0.0)
                        elif g < gj:
                            one = jnp.where(row > xs[g], 1.0, 0.0)
                        else:
                            one = jnp.where(below > jl, jnp.where(row >= xs[g], 1.0, 0.0),
                                            jnp.where(row > xs[g], 1.0, 0.0))
                        inc[g] = inc[g] + one
                rank_ref[n] = rank_ref[n] + jnp.concatenate(inc, axis=0)
    jj = lax.broadcasted_iota(jnp.int32, (nb, t), 0)
    live = jj <= cur_t
    return [jnp.where(live, jnp.where(rank_ref[n] < float(min(SLC_TOPK, nb)), 0.0, NEG_INF), NEG_INF)
            for n in range(len(scores))]


def _win_masked(tq, tk):
    return [o for o in range(-(WINDOW // tk), 0) if tq - 1 - o * tk >= WINDOW]


def _scores(ks, ws):
    return [jnp.dot(k, w, preferred_element_type=F32) for k, w in zip(ks, ws)]


def _softmax_update(m_ref, acc_ref, slots, s, vts):
    m_old = [m_ref[i] for i in slots]
    ncol = s[0].shape[1] // LANES
    m_cols = [[] for _ in slots]
    p_cols = [[] for _ in slots]
    for c in range(ncol):
        cols = slice(c * LANES, (c + 1) * LANES)
        for n, x in enumerate(s):
            xc = x[:, cols]
            mc = jnp.maximum(m_old[n][:, cols], jnp.max(xc, axis=0, keepdims=True))
            m_cols[n].append(mc)
            p_cols[n].append(jnp.exp2(xc - mc).astype(BF16))
    m_new = [jnp.concatenate(mc, axis=1) for mc in m_cols]
    pv = [jnp.dot(vt, jnp.concatenate(pc, axis=1), preferred_element_type=F32) for vt, pc in zip(vts, p_cols)]
    acc = [jnp.exp2(mo - mn) * acc_ref[i] + z for mo, mn, i, z in zip(m_old, m_new, slots, pv)]
    for i, mn, ac in zip(slots, m_new, acc):
        m_ref[i] = mn
        acc_ref[i] = ac


def _attn_kernel(q_ref, gn_ref, kc_ref, vct_ref, ksx_ref, vst_ref, kwn_ref, vwt_ref, bc_ref, bn_ref, wm_ref,
                 gq_ref, wov_ref, o_ref, m_ref, acc_ref, sa_ref, sb_ref, rank_ref):
    tq = q_ref.shape[0]
    tk = K_TILE
    nblk = wov_ref.shape[0]
    G, R = NSA_KV_GROUPS, NSA_GROUP
    qt = pl.program_id(1)
    n_q = tq // tk
    n_b = -(-REL_MAX_DIST // tk)
    j0 = qt * n_q
    qT = q_ref[...].astype(F32).T
    gates = jax.nn.sigmoid(gn_ref[...].T)
    gq = gq_ref[...]

    qn, qx, negs, o_cmp, scores = [], [], [], [], []
    ncmp = kc_ref.shape[2]
    step = tq // CMP_STRIDE
    bc_rows = pl.ds(pl.multiple_of(bc_ref.shape[2] - ncmp - qt * step, 8), ncmp)
    per_head = tq // LANES
    for g in range(G):
        xs = []
        for h in range(g * R, (g + 1) * R):
            x = qT[h * HEAD_DIM:(h + 1) * HEAD_DIM]
            x = x * lax.rsqrt(jnp.mean(x * x, axis=0, keepdims=True) + RMS_EPS) * gq * (HEAD_DIM ** -0.5 * LOG2E)
            xs.append(x.astype(BF16))
        qn.append(jnp.concatenate(xs, axis=1))

    for g in range(G):
        s = jnp.dot(kc_ref[0, g].astype(BF16), qn[g], preferred_element_type=F32)
        p_cols = []
        for c in range(R * tq // LANES):
            cols = slice(c * LANES, (c + 1) * LANES)
            t_cols = slice(c % per_head * LANES, (c % per_head + 1) * LANES)
            sc = s[:, cols] + bc_ref[g, c // per_head, bc_rows, t_cols]
            e = jnp.exp2(sc - jnp.maximum(jnp.max(sc, axis=0, keepdims=True), M_FLOOR))
            p_cols.append(e * (1.0 / jnp.maximum(jnp.sum(e, axis=0, keepdims=True), 1e-30)))
        o_cmp.append(jnp.dot(vct_ref[0, g].astype(BF16), jnp.concatenate(p_cols, axis=1).astype(BF16),
                             preferred_element_type=F32))
        psum = jnp.concatenate([sum(p_cols[r * per_head + t] for r in range(R)) for t in range(per_head)], axis=1)
        imp = _mm_f32(wov_ref[...], psum)

        blk = lax.broadcasted_iota(jnp.int32, imp.shape, 0)
        cur = (qt * tq + lax.broadcasted_iota(jnp.int32, imp.shape, 1)) // SLC_BLOCK
        forced = (blk == 0) | (blk == cur) | (blk == cur - 1)
        scores.append(jnp.where(forced, FORCE, jnp.where(blk <= cur, imp, -FORCE)))

    cur = (qt * tq + lax.broadcasted_iota(jnp.int32, (nblk, tq), 1)) // SLC_BLOCK
    for g, neg in enumerate(_rank_select(scores, cur, (qt + 1) * (tq // SLC_BLOCK), rank_ref)):
        if nblk < LANES - HEAD_DIM:
            neg = jnp.concatenate([neg, jnp.zeros((LANES - HEAD_DIM - nblk, tq), F32)], axis=0)
        neg = neg.astype(BF16)
        negs.append(neg)
        qx.append(jnp.concatenate([qn[g], jnp.concatenate([neg] * R, axis=1)], axis=0))

    m_ref[...] = jnp.full(m_ref.shape, M_FLOOR, F32)
    acc_ref[...] = jnp.zeros(acc_ref.shape, F32)

    def invalid_before(level):
        return jnp.where(qt >= level, 0.0, NEG_INF)

    def slc_queries(g, level):
        if level == 0:
            return qx[g]
        rows = jnp.minimum(negs[g], invalid_before(level).astype(BF16))
        return jnp.concatenate([qn[g], jnp.concatenate([rows] * R, axis=1)], axis=0)

    def win_queries(g, level):
        flag_row = lax.broadcasted_iota(jnp.int32, (LANES - HEAD_DIM, R * tq), 0) == 0
        extra = jnp.where(flag_row, invalid_before(level), 0.0).astype(BF16)
        return jnp.concatenate([qn[g], extra], axis=0)

    def slc_probs(o, bias):
        j = jnp.maximum(j0 + o, 0)
        sl = pl.ds(pl.multiple_of(j * tk, tk), tk)
        return [(g, ksx_ref[g, sl, :], vst_ref[g, j], slc_queries(g, max(-(o // n_q), 0)),
                 functools.partial(bias, g)) for g in range(G)]

    def win_probs(o, bias):
        j = jnp.maximum(j0 + o, 0)
        sl = pl.ds(pl.multiple_of(j * tk, tk), tk)
        return [(G + g, kwn_ref[g, sl, :], vwt_ref[g, j], win_queries(g, max(-(o // n_q), 0)),
                 None if bias is None else functools.partial(bias, g)) for g in range(G)]

    n_far = jnp.maximum(j0 - n_b, 0)
    groups = list(range(G))

    def far_scores(j, buf, gs=groups):
        sl = pl.ds(pl.multiple_of(j * tk, tk), tk)
        for g, x in zip(gs, _scores([ksx_ref[g, sl, :] for g in gs], [qx[g] for g in gs])):
            buf[g] = x

    def far_update(j, buf, gs=groups):
        _softmax_update(m_ref, acc_ref, gs, [buf[g] for g in gs], [vst_ref[g, j] for g in gs])

    far_scores(0, sa_ref)

    def far_body(i, carry):
        for g in groups:
            far_scores(2 * i + 1, sb_ref, [g])
            far_update(2 * i, sa_ref, [g])
        for g in groups:
            far_scores(2 * i + 2, sa_ref, [g])
            far_update(2 * i + 1, sb_ref, [g])
        return carry

    lax.fori_loop(0, n_far // 2, far_body, 0)

    @pl.when(n_far % 2 == 1)
    def _():
        far_update(n_far - 1, sa_ref)

    masked = _win_masked(tq, tk)
    rounds = []
    for o in range(-(WINDOW // tk), n_q):
        if o >= -n_b:
            near = lambda g, r, i=o + n_b: bn_ref[g, r, i]
            rounds.append(slc_probs(o, near) + win_probs(o, near))
        elif o in masked:
            rounds.append(win_probs(o, lambda g, r, i=masked.index(o): wm_ref[i]))
        else:
            rounds.append(win_probs(o, None))

    def round_scores(probs):
        sc = _scores([p[1] for p in probs], [p[3] for p in probs])
        return [x if p[4] is None else
                jnp.concatenate([x[:, r * tq:(r + 1) * tq] + p[4](r) for r in range(R)], axis=1)
                for x, p in zip(sc, probs)]

    sc = round_scores(rounds[0])
    for i, probs in enumerate(rounds):
        sc_next = round_scores(rounds[i + 1]) if i + 1 < len(rounds) else None
        _softmax_update(m_ref, acc_ref, [p[0] for p in probs], sc, [p[2] for p in probs])
        sc = sc_next

    outs = []
    for g in range(G):
        acc = acc_ref[g]
        o_slc = acc[:HEAD_DIM] / acc[HEAD_DIM:HEAD_DIM + 1]
        acc = acc_ref[G + g]
        o_win = acc[:HEAD_DIM] / acc[HEAD_DIM:HEAD_DIM + 1]
        for r in range(R):
            h = g * R + r
            cols = slice(r * tq, (r + 1) * tq)
            outs.append(gates[3 * h:3 * h + 1] * o_cmp[g][:, cols] + gates[3 * h + 1:3 * h + 2] * o_slc[:, cols]
                        + gates[3 * h + 2:3 * h + 3] * o_win[:, cols])
    o_ref[...] = jnp.concatenate(outs, axis=0).T.astype(o_ref.dtype)


def _attention(q, gn, kc, vct, ksx, vst, kwn, vwt, bias_c, bias_near, win_mask, gq, wov, batch, seq):
    n = q.shape[0]
    tq = Q_TILE
    nq = seq // tq
    ncmp = kc.shape[2]
    row = lambda b, i: (b * nq + i, 0)
    whole = lambda b, i: (0, b, 0)
    tiles = lambda b, i: (0, b, 0, 0)
    const = lambda a: pl.BlockSpec(a.shape, lambda b, i: (0,) * a.ndim, pipeline_mode=pl.Buffered(1))
    return pl.pallas_call(
        _attn_kernel,
        grid=(batch, nq),
        in_specs=[pl.BlockSpec((tq, NSA_WIDTH), row),
                  pl.BlockSpec((tq, LANES), row),
                  pl.BlockSpec((1, NSA_KV_GROUPS, ncmp, HEAD_DIM), lambda b, i: (b, 0, 0, 0)),
                  pl.BlockSpec((1, NSA_KV_GROUPS, HEAD_DIM, ncmp), lambda b, i: (b, 0, 0, 0)),
                  pl.BlockSpec((2, seq, LANES), whole),
                  pl.BlockSpec((2, seq // K_TILE, VT_ROWS, K_TILE), tiles),
                  pl.BlockSpec((2, seq, LANES), whole),
                  pl.BlockSpec((2, seq // K_TILE, VT_ROWS, K_TILE), tiles),
                  const(bias_c),
                  const(bias_near), const(win_mask), const(gq), const(wov)],
        out_specs=pl.BlockSpec((tq, NSA_WIDTH), row),
        out_shape=jax.ShapeDtypeStruct((n, NSA_WIDTH), BF16),
        scratch_shapes=[pltpu.VMEM((2 * NSA_KV_GROUPS, 1, NSA_GROUP * tq), F32),
                        pltpu.VMEM((2 * NSA_KV_GROUPS, VT_ROWS, NSA_GROUP * tq), F32),
                        pltpu.VMEM((NSA_KV_GROUPS, K_TILE, NSA_GROUP * tq), F32),
                        pltpu.VMEM((NSA_KV_GROUPS, K_TILE, NSA_GROUP * tq), F32),
                        pltpu.VMEM((NSA_KV_GROUPS, seq // SLC_BLOCK, tq), F32)],
        compiler_params=_params("parallel", "parallel"),
        name="nsa_attention",
    )(q, gn, kc, vct, ksx, vst, kwn, vwt, bias_c, bias_near, win_mask, gq, wov)


def _split_bf16(z, parts):
    out = []
    for _ in range(parts - 1):
        hi = z.astype(BF16)
        out.append(hi)
        z = z - hi.astype(F32)
    return out + [z.astype(BF16)]


def _head_sums(z, ones):
    width = ones.shape[0]
    nb = z.shape[1] // width
    rows = z.shape[0]
    zb = z.astype(BF16)
    stacked = jnp.concatenate([zb[:, m * width:(m + 1) * width] for m in range(nb)], axis=0)
    sums = jnp.dot(stacked, ones, preferred_element_type=F32)
    return jnp.concatenate([sums[m * rows:(m + 1) * rows] for m in range(nb)], axis=1)


def _softplus(z):
    return jnp.maximum(z, 0.0) + jnp.log(1.0 + jnp.exp(-jnp.abs(z)))


def _rwkv_chunk(x, prev, st, mu_ref, w0_ref, w2_ref, a0_ref, a2_ref, g2_ref, kk_ref, ka_ref, rk_ref,
                lng_ref, lnb_ref, ones_ref):
    L = x.shape[0]
    W = RWKV_WIDTH
    N = HEAD_DIM
    row_id = lax.broadcasted_iota(jnp.int32, x.shape, 0)
    shifted = jnp.where(row_id == 0, prev, pltpu.roll(x, 1, axis=0))
    xl = x + (shifted - x) * mu_ref[...]
    yield None
    r = xl[:, 0:W]
    k = xl[:, W:2 * W]
    v = xl[:, 2 * W:3 * W]
    xw = xl[:, 3 * W:3 * W + LORA_W]
    xa = xl[:, 3 * W + LORA_W:3 * W + LORA_W + LORA_A]
    xg = xl[:, 3 * W + LORA_W + LORA_A:]
    w = -_softplus(-(w0_ref[...] + _mm(jnp.tanh(xw), w2_ref[...]))) - 0.5
    ld = -jnp.exp(w)
    a = jax.nn.sigmoid(a0_ref[...] + _mm(xa, a2_ref[...]))
    gate = _mm(jax.nn.sigmoid(xg), g2_ref[...])
    kkv = k * kk_ref[...]
    k2 = k * (1.0 + (a - 1.0) * ka_ref[...])
    yield None

    ti = lax.broadcasted_iota(jnp.int32, (L, L), 0)
    si = lax.broadcasted_iota(jnp.int32, (L, L), 1)
    incl = si <= ti
    strict = si < ti
    tri = jnp.where(incl, 1.0, 0.0).astype(BF16)
    cl3 = jnp.dot(tri, jnp.concatenate(_split_bf16(ld, 3), axis=1), preferred_element_type=F32)
    cl = cl3[:, :W] + (cl3[:, W:2 * W] + cl3[:, 2 * W:])
    cl_end = cl[L - 1:L, :]
    yield None
    e_pos = jnp.exp(cl)
    e_neg = jnp.exp(-cl)
    e_prev = jnp.exp(cl - ld)
    e_end = jnp.exp(cl_end - cl)
    eye = jnp.where(ti == si, 1.0, 0.0)
    yield None

    hsum = lambda z: _head_sums(z, ones_ref[...])

    kk_n = kkv * lax.rsqrt(jnp.maximum(hsum(kkv * kkv), 1e-24))
    bv = kk_n * a
    yield None
    a_t = (-kk_n * e_prev).astype(BF16)
    b_t = (bv * e_neg).astype(BF16)
    k_t = (k2 * e_neg).astype(BF16)
    r_t = (r * e_pos).astype(BF16)
    yield None
    v_b = v.astype(BF16)
    k_e = k2 * e_end
    b_e = bv * e_end

    yield "elementwise done"

    P = range(W // LANES)
    ps = [slice(m * LANES, (m + 1) * LANES) for m in P]
    first = lax.broadcasted_iota(jnp.int32, (1, LANES), 1) < N
    same_head = (lax.broadcasted_iota(jnp.int32, (LANES, LANES), 0) // N
                 == lax.broadcasted_iota(jnp.int32, (LANES, LANES), 1) // N)

    def diag_rows(x):
        zero = jnp.zeros_like(x)
        return jnp.concatenate([jnp.where(first, x, zero), jnp.where(first, zero, x)], axis=0)

    def diag_blocks(x):
        zero = jnp.zeros_like(x[:, :LANES])
        return jnp.concatenate([jnp.concatenate([x[:, :LANES], zero], axis=1),
                                jnp.concatenate([zero, x[:, LANES:]], axis=1)], axis=0)

    ke_t = [k_e[:, s].T for s in ps]
    be_t = [b_e[:, s].T for s in ps]
    p_end = [e_pos[:, s].T[:, L - 1:L] for s in ps]

    lhs = [jnp.concatenate([a_t[:, s], r_t[:, s]], axis=0) for s in ps]
    rhs = [jnp.concatenate([k_t[:, s], b_t[:, s]], axis=0) for s in ps]
    zero_b = jnp.zeros_like(lhs[0])
    aa = [[lax.dot_general(jnp.where(first, x, zero_b) if j == 0 else jnp.where(first, zero_b, x), y, _NT,
                           preferred_element_type=F32) for j in range(2)] for x, y in zip(lhs, rhs)]
    cat2 = lambda f: [jnp.concatenate([f(pair[0]), f(pair[1])], axis=1) for pair in aa]
    a_ak = cat2(lambda x: jnp.where(strict, x[:L, :L], 0.0))
    a_ab = cat2(lambda x: jnp.where(strict, x[:L, L:], 0.0))
    a_rk = cat2(lambda x: jnp.where(incl, x[L:, :L], 0.0))
    a_rb = cat2(lambda x: jnp.where(incl, x[L:, L:], 0.0))
    yield None

    eye2 = jnp.concatenate([eye, eye], axis=1)
    tinv = [eye2 + x for x in a_ab]
    pw = [_mm(x, diag_blocks(x)) for x in a_ab]
    yield None
    span = 2
    while 2 * span < L:
        both = [_mm(jnp.concatenate([t, p], axis=0), diag_blocks(p)) for t, p in zip(tinv, pw)]
        tinv = [t + x[:L] for t, x in zip(tinv, both)]
        pw = [x[L:] for x in both]
        yield None
        span *= 2
    tinv = [t + _mm(t, diag_blocks(p)) for t, p in zip(tinv, pw)]
    yield None

    v_d = [diag_rows(v_b[:, s]) for s in ps]
    av = [_mm(jnp.concatenate([a_ak[m], a_rk[m]], axis=0), v_d[m]) for m in P]
    tw = [_mm(tinv[m], jnp.concatenate([diag_rows(a_t[:, ps[m]]), diag_rows(av[m][:L].astype(BF16))], axis=1))
          for m in P]
    kv_loc = [jnp.where(same_head, _mm(ke_t[m], v_b[:, ps[m]]), 0.0) for m in P]
    yield None

    ws = [_mm(jnp.concatenate([tw[m][:, :LANES].astype(BF16), r_t[:, ps[m]]], axis=0), st[m]) for m in P]
    u = [ws[m][:L] + tw[m][:, LANES:] for m in P]
    yield None
    y = [ws[m][L:] + av[m][L:] + _mm(a_rb[m], diag_rows(u[m])) for m in P]
    st_new = [p_end[m] * st[m] + kv_loc[m] + jnp.where(same_head, _mm(be_t[m], u[m]), 0.0) for m in P]
    yield None

    y = jnp.concatenate(y, axis=1)
    d = y - hsum(y) * (1.0 / N)
    yn = d * lax.rsqrt(hsum(d * d) * (1.0 / N) + GN_EPS) * lng_ref[...] + lnb_ref[...]
    bonus = hsum(r * k2 * rk_ref[...])
    yield (yn + bonus * v) * gate, st_new


def _rwkv_kernel(x_ref, xp_ref, *refs):
    *param_refs, o_ref, st_ref = refs
    c = pl.program_id(1)

    @pl.when(c == 0)
    def _():
        st_ref[...] = jnp.zeros(st_ref.shape, F32)

    chunks = []
    for i in range(x_ref.shape[0]):
        prev = jnp.where(c > 0, xp_ref[i, xp_ref.shape[1] - 1:, :], 0.0)
        st = [st_ref[i, m] for m in range(st_ref.shape[1])]
        chunks.append(_rwkv_chunk(x_ref[i], prev, st, *param_refs))
    def advance(gen, until_result):
        item = next(gen)
        return (isinstance(item, tuple), item) if until_result else (item == "elementwise done", item)

    while not advance(chunks[0], False)[0]:
        pass
    outs, states = [], []
    for i, chunk in enumerate(chunks):
        following = chunks[i + 1] if i + 1 < len(chunks) else None
        result = None
        while result is None or following is not None:
            if result is None:
                done, item = advance(chunk, True)
                result = item if done else None
            if following is not None and advance(following, False)[0]:
                following = None
        outs.append(result[0].astype(o_ref.dtype))
        states.append(jnp.stack(result[1]))
    o_ref[...] = jnp.stack(outs)
    st_ref[...] = jnp.stack(states)


def _rwkv(rw, vecs, w2, a2, g2, batch, seq):
    width = rw.shape[1]
    L = min(CHUNK, seq)
    nc = seq // L
    nb = RWKV_SEQS if batch % RWKV_SEQS == 0 else 1
    sub = 8
    rw3 = rw.reshape(batch, seq, width)
    vec = lambda wd: pl.BlockSpec((1, wd), lambda b, c: (0, 0))
    mat = lambda m: pl.BlockSpec(m.shape, lambda b, c: (0, 0))
    mu, w0, a0, kk, ka, rk, lng, lnb = vecs
    head = np.arange(2 * LANES) // HEAD_DIM
    ones = jnp.asarray(head[:, None] == head[None, :], BF16)
    out = pl.pallas_call(
        _rwkv_kernel,
        grid=(batch // nb, nc),
        in_specs=[pl.BlockSpec((nb, L, width), lambda b, c: (b, c, 0)),
                  pl.BlockSpec((nb, sub, width), lambda b, c: (b, jnp.maximum(c * (L // sub) - 1, 0), 0)),
                  vec(width), vec(RWKV_WIDTH), mat(w2), vec(RWKV_WIDTH), mat(a2), mat(g2),
                  vec(RWKV_WIDTH), vec(RWKV_WIDTH), vec(RWKV_WIDTH), vec(RWKV_WIDTH), vec(RWKV_WIDTH), mat(ones)],
        out_specs=pl.BlockSpec((nb, L, RWKV_WIDTH), lambda b, c: (b, c, 0)),
        out_shape=jax.ShapeDtypeStruct((batch, seq, RWKV_WIDTH), BF16),
        scratch_shapes=[pltpu.VMEM((nb, RWKV_WIDTH // LANES, LANES, LANES), F32)],
        compiler_params=_params("parallel", "arbitrary"),
        name="rwkv7",
    )(rw3, rw3, mu, w0, w2, a0, a2, g2, kk, ka, rk, lng, lnb, ones)
    return out


def _merge_kernel(x_ref, oa_ref, ob_ref, gab_ref, wpa_ref, wpb_ref, wo_ref, o_ref):
    d = x_ref.shape[1]
    pa = jnp.dot(oa_ref[...], wpa_ref[...], preferred_element_type=F32)
    pb = jnp.dot(ob_ref[0], wpb_ref[...], preferred_element_type=F32)
    gab = gab_ref[...].astype(F32)
    merged = jax.nn.sigmoid(gab[:, :d]) * pa + jax.nn.sigmoid(gab[:, d:]) * pb
    o_ref[...] = x_ref[...] + _mm(merged, wo_ref[...])


def _merge(x2, oa, ob, gab, wpa, wpb, wo, tm):
    n, d = x2.shape
    per_seq = ob.shape[1] // tm
    row = lambda wd: pl.BlockSpec((tm, wd), lambda i: (i, 0))
    mat = lambda m: pl.BlockSpec(m.shape, lambda i: (0, 0))
    ob_spec = pl.BlockSpec((1, tm, ob.shape[2]), lambda i: (i // per_seq, i % per_seq, 0))
    return pl.pallas_call(
        _merge_kernel,
        grid=(n // tm,),
        in_specs=[row(d), row(oa.shape[1]), ob_spec, row(gab.shape[1]), mat(wpa), mat(wpb), mat(wo)],
        out_specs=row(d),
        out_shape=jax.ShapeDtypeStruct((n, d), F32),
        compiler_params=_params("parallel"),
        name="merge",
    )(x2, oa, ob, gab, wpa, wpb, wo)


def _ffn_kernel(seq, x_ref, xh_ref, g_ref, wv_ref, wg_ref, cwv_ref, cwg_ref, cbv_ref, cbg_ref, wd_ref, o_ref,
                h_ref, uv_ref, ug_ref):
    tm = x_ref.shape[0]
    i = pl.program_id(0)
    f = pl.program_id(1)

    @pl.when(f == 0)
    def _():
        first = lax.rem(i * tm, seq) == 0
        halo = jnp.where(first, 0.0, _rms(xh_ref[...], g_ref[...]))
        h_ref[0:HALO, :] = halo.astype(BF16)
        h_ref[HALO:, :] = _rms(x_ref[...], g_ref[...]).astype(BF16)

    h = h_ref[...]
    uv_ref[...] = jnp.dot(h, wv_ref[...], preferred_element_type=F32)
    ug_ref[...] = jnp.dot(h, wg_ref[...], preferred_element_type=F32)

    def conv(u_ref, cw_ref, cb_ref):
        acc = cb_ref[...] + cw_ref[0:1, :] * u_ref[pl.ds(HALO - 2, tm), :]
        acc = acc + cw_ref[1:2, :] * u_ref[pl.ds(HALO - 1, tm), :]
        return acc + cw_ref[2:3, :] * u_ref[pl.ds(HALO, tm), :]

    val = conv(uv_ref, cwv_ref, cbv_ref)
    gt = conv(ug_ref, cwg_ref, cbg_ref)
    y = _mm(gt * jax.nn.sigmoid(gt) * val, wd_ref[...])

    @pl.when(f == 0)
    def _():
        o_ref[...] = x_ref[...] + y

    @pl.when(f > 0)
    def _():
        o_ref[...] = o_ref[...] + y


def _ffn(x1, g, w_up, conv_w, conv_b, w_down, seq, tm, tf):
    n, d = x1.shape
    dff = w_down.shape[0]
    nf = dff // tf
    return pl.pallas_call(
        functools.partial(_ffn_kernel, seq),
        grid=(n // tm, nf),
        in_specs=[pl.BlockSpec((tm, d), lambda i, f: (i, 0)),
                  pl.BlockSpec((HALO, d), lambda i, f: (jnp.maximum(i * (tm // HALO) - 1, 0), 0)),
                  pl.BlockSpec((1, d), lambda i, f: (0, 0)),
                  pl.BlockSpec((d, tf), lambda i, f: (0, f)),
                  pl.BlockSpec((d, tf), lambda i, f: (0, nf + f)),
                  pl.BlockSpec((CONV_WIDTH, tf), lambda i, f: (0, f)),
                  pl.BlockSpec((CONV_WIDTH, tf), lambda i, f: (0, nf + f)),
                  pl.BlockSpec((1, tf), lambda i, f: (0, f)),
                  pl.BlockSpec((1, tf), lambda i, f: (0, nf + f)),
                  pl.BlockSpec((tf, d), lambda i, f: (f, 0))],
        out_specs=pl.BlockSpec((tm, d), lambda i, f: (i, 0)),
        out_shape=jax.ShapeDtypeStruct((n, d), F32),
        scratch_shapes=[pltpu.VMEM((tm + HALO, d), BF16),
                        pltpu.VMEM((tm + HALO, tf), F32),
                        pltpu.VMEM((tm + HALO, tf), F32)],
        compiler_params=_params("parallel", "arbitrary"),
        name="convffn",
    )(x1, x1, g, w_up, w_up, conv_w, conv_w, conv_b, conv_b, w_down)


def _t5_bucket(dist):
    n = np.maximum(dist, 0)
    max_exact = REL_BUCKETS // 2
    ratio = np.log(np.maximum(n, 1).astype(np.float32) / max_exact) / math.log(REL_MAX_DIST / max_exact)
    large = np.minimum(max_exact + (ratio * (REL_BUCKETS - max_exact)).astype(np.int32), REL_BUCKETS - 1)
    return np.where(n < max_exact, n, large).astype(np.int32)


def _bias_tables(rel_bias, seq, ncmp):
    tq, tk = Q_TILE, K_TILE
    G, R = NSA_KV_GROUPS, NSA_GROUP
    nq = seq // tq
    step = tq // CMP_STRIDE
    tab = rel_bias.astype(F32)

    off = (nq - 1) * step
    d_c = np.arange(tq)[None, :] - (np.arange(ncmp + off)[:, None] - off) * CMP_STRIDE - (CMP_BLOCK - 1)
    n_b = -(-REL_MAX_DIST // tk)
    d0 = np.arange(tq)[None, :] - np.arange(tk)[:, None]
    d_n = np.stack([d0 - o * tk for o in range(-n_b, tq // tk)])

    buckets = np.concatenate([_t5_bucket(d_c).reshape(-1), _t5_bucket(d_n).reshape(-1)])
    onehot = (jnp.arange(REL_BUCKETS, dtype=jnp.int32)[:, None] == jnp.asarray(buckets)[None, :]).astype(F32)
    vals = jnp.dot(tab.T * LOG2E, onehot, precision=lax.Precision.HIGHEST, preferred_element_type=F32)
    base = jnp.where(jnp.asarray(d_c >= 0), vals[:, :d_c.size].reshape(G, R, ncmp + off, tq), NEG_INF)

    far = (tab[REL_BUCKETS - 1] * LOG2E).reshape(G, R, 1, 1, 1)
    near = vals[:, d_c.size:].reshape(G, R, d_n.shape[0], tk, tq) - far
    bias_near = jnp.where(jnp.asarray(d_n >= 0), near, NEG_INF)
    win = np.stack([np.where(d0 - o * tk < WINDOW, 0.0, NEG_INF) for o in _win_masked(tq, tk)])
    return base, bias_near, jnp.asarray(win.astype(np.float32))


def _overlap_matrix(ncmp_pad, n_slc, n_cmp):
    m = np.zeros((ncmp_pad, n_slc), np.float32)
    ratio = SLC_BLOCK // CMP_STRIDE
    for j in range(n_slc):
        for o, wgt in enumerate(OVERLAP_W):
            cidx = ratio * j + o - (CMP_BLOCK // CMP_STRIDE - 1)
            if 0 <= cidx < n_cmp:
                m[cidx, j] += wgt
    return jnp.asarray(m.T)


def _layer(x, attn_norm_g, w_in, rel_bias, q_norm_g, k_norm_g, cmp_pe_k, cmp_w1_k, cmp_w2_k,
           cmp_pe_v, cmp_w1_v, cmp_w2_v, rwkv_mu, rwkv_w0, rwkv_w2, rwkv_a0, rwkv_a2, rwkv_g2,
           rwkv_k_k, rwkv_k_a, rwkv_r_k, rwkv_ln_g, rwkv_ln_b, w_proj_a, w_proj_b, w_out,
           ffn_norm_g, w_up, conv_w, conv_b, w_down):
    batch, seq, d = x.shape
    n = batch * seq
    assert seq % Q_TILE == 0 and Q_TILE % K_TILE == 0 and WINDOW % K_TILE == 0
    assert seq % CMP_STRIDE == 0 and seq % FFN_ROWS == 0 and seq % PROJ_ROWS == 0 and PROJ_ROWS % K_TILE == 0
    assert seq // SLC_BLOCK <= LANES - HEAD_DIM
    assert NSA_KV_WIDTH == LANES and CMP_PACK * HEAD_DIM == 2 * LANES
    x2 = x.reshape(n, d)
    row = lambda a: a.reshape(1, -1).astype(F32)

    o_kvc = NSA_WIDTH
    o_kv = o_kvc + 2 * NSA_KV_WIDTH
    o_gn = o_kv + 4 * NSA_KV_WIDTH
    o_rw = o_gn + 3 * NSA_HEADS
    o_gab = o_rw + RWKV_IN_WIDTH
    w16 = w_in.astype(BF16)
    gn_pad = jnp.zeros((d, LANES - 3 * NSA_HEADS), BF16)
    ws = (w16[:, o_rw:o_gab], w16[:, :o_kvc], w16[:, o_kvc:o_kv], w16[:, o_kv:o_gn], w16[:, o_gab:],
          jnp.concatenate([w16[:, o_gn:o_rw], gn_pad], axis=1))
    widths = (RWKV_IN_WIDTH, NSA_WIDTH, 2 * NSA_KV_WIDTH, 2 * d, LANES)
    dtypes = (F32, BF16, F32, BF16, F32)
    gk = row(k_norm_g)
    rw, q, kv_cmp, gab, gn, ksx, kwn, vst, vwt = _inproj(
        x2, row(attn_norm_g), ws, jnp.concatenate([gk, gk], axis=1), widths, dtypes, seq, tm=PROJ_ROWS)

    nchunk = seq // CMP_STRIDE
    n_cmp = nchunk - CMP_BLOCK // CMP_STRIDE + 1
    pe = jnp.stack([cmp_pe_k, cmp_pe_v]).astype(F32)
    pe = jnp.concatenate([pe] * NSA_KV_GROUPS, axis=-1)
    w1 = jnp.stack([cmp_w1_k, cmp_w1_v]).astype(BF16)
    w1 = w1.reshape(2, CMP_BLOCK // CMP_PACK, CMP_PACK * HEAD_DIM, CMP_HIDDEN)
    w2 = jnp.stack([cmp_w2_k, cmp_w2_v]).astype(BF16)
    kc, vc = _compress(kv_cmp, pe, w1, w2, gk, batch, seq)

    bias_c, bias_near, win_mask = _bias_tables(rel_bias, seq, nchunk)
    wov = _overlap_matrix(nchunk, seq // SLC_BLOCK, n_cmp)
    o_a = _attention(q, gn, kc, jnp.swapaxes(vc, -1, -2), ksx, vst, kwn, vwt, bias_c, bias_near, win_mask,
                     q_norm_g.reshape(-1, 1).astype(F32), wov, batch, seq)

    vecs = tuple(row(a) for a in (rwkv_mu, rwkv_w0, rwkv_a0, rwkv_k_k, rwkv_k_a, rwkv_r_k, rwkv_ln_g, rwkv_ln_b))
    o_b = _rwkv(rw, vecs, rwkv_w2.astype(BF16), rwkv_a2.astype(BF16), rwkv_g2.astype(BF16), batch, seq)

    x1 = _merge(x2, o_a, o_b, gab, w_proj_a.astype(BF16), w_proj_b.astype(BF16), w_out.astype(BF16), tm=FFN_ROWS)
    dff = w_down.shape[0]
    out = _ffn(x1, row(ffn_norm_g), w_up.astype(BF16), conv_w.astype(F32), row(conv_b), w_down.astype(BF16),
               seq, tm=FFN_ROWS, tf=dff // FFN_COL_SPLITS)
    return out.reshape(batch, seq, d)


def kernel(x, attn_norm_g, w_in, rel_bias, q_norm_g, k_norm_g, cmp_pe_k, cmp_w1_k, cmp_w2_k, cmp_pe_v, cmp_w1_v,
           cmp_w2_v, rwkv_mu, rwkv_w0, rwkv_w2, rwkv_a0, rwkv_a2, rwkv_g2, rwkv_k_k, rwkv_k_a, rwkv_r_k,
           rwkv_ln_g, rwkv_ln_b, w_proj_a, w_proj_b, w_out, ffn_norm_g, w_up, conv_w, conv_b, w_down):
    per_layer = (attn_norm_g, w_in, None, q_norm_g, k_norm_g, cmp_pe_k, cmp_w1_k, cmp_w2_k, cmp_pe_v, cmp_w1_v,
                 cmp_w2_v, rwkv_mu, rwkv_w0, rwkv_w2, rwkv_a0, rwkv_a2, rwkv_g2, rwkv_k_k, rwkv_k_a, rwkv_r_k,
                 rwkv_ln_g, rwkv_ln_b, w_proj_a, w_proj_b, w_out, ffn_norm_g, w_up, conv_w, conv_b, w_down)
    for l in range(attn_norm_g.shape[0]):
        args = [rel_bias if p is None else p[l] for p in per_layer]
        x = _layer(x, *args)
    return x
```

```python
import functools
import math

import numpy as np
import jax
import jax.numpy as jnp
from jax import lax
from jax.experimental import pallas as pl
from jax.experimental.pallas import tpu as pltpu

F32 = jnp.float32
BF16 = jnp.bfloat16

HEAD_DIM = 64
NSA_HEADS = 8
NSA_KV_GROUPS = 2
NSA_GROUP = NSA_HEADS // NSA_KV_GROUPS
NSA_WIDTH = NSA_HEADS * HEAD_DIM
NSA_KV_WIDTH = NSA_KV_GROUPS * HEAD_DIM
CMP_BLOCK = 32
CMP_STRIDE = 16
CMP_HIDDEN = 256
SLC_BLOCK = 64
SLC_TOPK = 16
OVERLAP_W = (1, 2, 2, 2, 1)
WINDOW = 512
REL_BUCKETS = 32
REL_MAX_DIST = 128
RWKV_HEADS = 8
RWKV_WIDTH = RWKV_HEADS * HEAD_DIM
LORA_W = 64
LORA_A = 64
LORA_G = 128
RWKV_IN_WIDTH = 3 * RWKV_WIDTH + LORA_W + LORA_A + LORA_G
GN_EPS = 64e-5
CONV_WIDTH = 3
RMS_EPS = 1e-6
NEG_INF = -1e30
FORCE = 1e9
LOG2E = math.log2(math.e)
M_FLOOR = -1e20

LANES = 128
VMEM_LIMIT = 56 * 1024 * 1024

Q_TILE = 256
K_TILE = 256
VT_ROWS = 80
CHUNK = 128
RWKV_SEQS = 4
CMP_PACK = 4
HALO = 16
PROJ_ROWS = 512
FFN_ROWS = 1024
FFN_COL_SPLITS = 2

_NT = (((1,), (1,)), ((), ()))


def _params(*sem):
    return pltpu.CompilerParams(dimension_semantics=sem, vmem_limit_bytes=VMEM_LIMIT)


def _mm(a, b):
    return jnp.dot(a.astype(BF16), b.astype(BF16), preferred_element_type=F32)


def _mm_f32(a, b):
    return jnp.dot(a, b, preferred_element_type=F32, precision=lax.Precision.HIGHEST)


def _rms(x, g):
    return x * lax.rsqrt(jnp.mean(x * x, axis=-1, keepdims=True) + RMS_EPS) * g


def _kv_prep(kv, gk, row0, ksx_ref, kwn_ref, vs_ref, vw_ref):
    tm = kv.shape[0]
    lane = lax.broadcasted_iota(jnp.int32, (tm, LANES), 1)
    lo = lane < HEAD_DIM

    def norm2(x):
        x2 = x * x
        s_lo = jnp.sum(jnp.where(lo, x2, 0.0), axis=-1, keepdims=True)
        s_hi = jnp.sum(jnp.where(lo, 0.0, x2), axis=-1, keepdims=True)
        ms = jnp.where(lo, s_lo, s_hi) * (1.0 / HEAD_DIM)
        return x * lax.rsqrt(ms + RMS_EPS) * gk

    ks = norm2(kv[:, 0:LANES])
    kw = norm2(kv[:, 2 * LANES:3 * LANES])
    row = row0 + lax.broadcasted_iota(jnp.int32, (tm, LANES), 0)
    onehot = jnp.where(lane - HEAD_DIM == row // SLC_BLOCK, 1.0, 0.0)
    ksx_ref[0] = jnp.where(lo, ks, onehot).astype(BF16)
    ksx_ref[1] = jnp.where(lo, pltpu.roll(ks, HEAD_DIM, axis=1), onehot).astype(BF16)
    flag_col = jnp.where(lane == HEAD_DIM, 1.0, 0.0)
    kwn_ref[0] = jnp.where(lo, kw, flag_col).astype(BF16)
    kwn_ref[1] = jnp.where(lo, pltpu.roll(kw, HEAD_DIM, axis=1), flag_col).astype(BF16)
    tail = jnp.where(lax.broadcasted_iota(jnp.int32, (VT_ROWS - HEAD_DIM, K_TILE), 0) == 0, 1.0, 0.0)
    for col, ref in ((1, vs_ref), (3, vw_ref)):
        for t in range(tm // K_TILE):
            vt = kv[t * K_TILE:(t + 1) * K_TILE, col * LANES:(col + 1) * LANES].T
            for g in range(NSA_KV_GROUPS):
                ref[g, t] = jnp.concatenate([vt[g * HEAD_DIM:(g + 1) * HEAD_DIM], tail], axis=0).astype(BF16)


def _inproj_kernel(seq, x_ref, g_ref, w_rw, w_q, w_kvc, w_kv, w_gab, w_gn, gk_ref, gq_ref, rw_ref, q_ref, kvc_ref,
                   gab_ref, gn_ref, ksx_ref, kwn_ref, vs_ref, vw_ref):
    tm = x_ref.shape[0]
    h = _rms(x_ref[...], g_ref[...]).astype(BF16)
    for w_ref, ref in ((w_rw, rw_ref), (w_q, q_ref), (w_kvc, kvc_ref), (w_kv, None), (w_gab, gab_ref),
                       (w_gn, gn_ref)):
        y = jnp.dot(h, w_ref[...], preferred_element_type=F32)
        if ref is None:
            _kv_prep(y, gk_ref[...], lax.rem(pl.program_id(0) * tm, seq), ksx_ref, kwn_ref, vs_ref, vw_ref)
        elif ref is q_ref:
            for t in range(tm // Q_TILE):
                yt = y[t * Q_TILE:(t + 1) * Q_TILE].T
                heads = []
                for hd in range(NSA_HEADS):
                    x = yt[hd * HEAD_DIM:(hd + 1) * HEAD_DIM]
                    x = x * lax.rsqrt(jnp.mean(x * x, axis=0, keepdims=True) + RMS_EPS)
                    heads.append(x * gq_ref[...] * (HEAD_DIM ** -0.5 * LOG2E))
                q_ref[t] = jnp.concatenate(heads, axis=0).astype(q_ref.dtype)
        elif ref is gn_ref:
            for t in range(tm // Q_TILE):
                gn_ref[t] = jax.nn.sigmoid(y[t * Q_TILE:(t + 1) * Q_TILE].T)
        else:
            ref[...] = y.astype(ref.dtype)


def _inproj(x2, g, ws, gk2, gq, seq, tm):
    n, d = x2.shape
    rows = lambda wd, dt: (pl.BlockSpec((tm, wd), lambda i: (i, 0)), jax.ShapeDtypeStruct((n, wd), dt))
    tiles = lambda r, c, dt: (pl.BlockSpec((tm // c, r, c), lambda i: (i, 0, 0)),
                              jax.ShapeDtypeStruct((n // c, r, c), dt))
    keys = (pl.BlockSpec((2, tm, LANES), lambda i: (0, i, 0)), jax.ShapeDtypeStruct((2, n, LANES), BF16))
    vts = (pl.BlockSpec((2, tm // K_TILE, VT_ROWS, K_TILE), lambda i: (0, i, 0, 0)),
           jax.ShapeDtypeStruct((2, n // K_TILE, VT_ROWS, K_TILE), BF16))
    outs = [rows(RWKV_IN_WIDTH, F32), tiles(NSA_WIDTH, Q_TILE, BF16), rows(2 * NSA_KV_WIDTH, F32),
            rows(2 * d, BF16), tiles(LANES, Q_TILE, F32), keys, keys, vts, vts]
    return pl.pallas_call(
        functools.partial(_inproj_kernel, seq),
        grid=(n // tm,),
        in_specs=[pl.BlockSpec((tm, d), lambda i: (i, 0)),
                  pl.BlockSpec((1, d), lambda i: (0, 0))]
        + [pl.BlockSpec(w.shape, lambda i: (0, 0), pipeline_mode=pl.Buffered(1)) for w in ws]
        + [pl.BlockSpec((1, LANES), lambda i: (0, 0)), pl.BlockSpec((HEAD_DIM, 1), lambda i: (0, 0))],
        out_specs=[o[0] for o in outs],
        out_shape=[o[1] for o in outs],
        compiler_params=_params("parallel"),
        name="inproj",
    )(x2, g, *ws, gk2, gq)


def _compress_kernel(k_ref, v_ref, pe_ref, w1_ref, w2_ref, gk_ref, kc_ref, vc_ref):
    nchunk = k_ref.shape[0] // CMP_STRIDE
    span = CMP_BLOCK // CMP_STRIDE
    G = NSA_KV_GROUPS
    lo = lax.broadcasted_iota(jnp.int32, (nchunk, LANES), 1) < HEAD_DIM
    acc = [[[jnp.zeros((nchunk, CMP_HIDDEN), F32) for _ in range(span)] for _ in range(G)] for _ in range(2)]
    for p0 in range(0, CMP_STRIDE, CMP_PACK):
        for s, ref in enumerate((k_ref, v_ref)):
            xs = [ref[pl.ds(p0 + i, nchunk, stride=CMP_STRIDE), :] for i in range(CMP_PACK)]
            for half in range(span):
                q0 = half * CMP_STRIDE + p0
                xq = [xs[i] + pe_ref[s, q0 + i:q0 + i + 1, :] for i in range(CMP_PACK)]
                for g in range(G):
                    pairs = [jnp.where(lo, xq[i], pltpu.roll(xq[i + 1], HEAD_DIM, axis=1)) if g == 0 else
                             jnp.where(lo, pltpu.roll(xq[i], HEAD_DIM, axis=1), xq[i + 1])
                             for i in range(0, CMP_PACK, 2)]
                    acc[s][g][half] = acc[s][g][half] + _mm(jnp.concatenate(pairs, axis=1),
                                                            w1_ref[s, q0 // CMP_PACK])
    for s, out_ref in ((0, kc_ref), (1, vc_ref)):
        for g in range(G):
            hid = acc[s][g][0]
            for half in range(1, span):
                hid = hid + pltpu.roll(acc[s][g][half], nchunk - half, axis=0)
            out = _mm(jax.nn.gelu(hid), w2_ref[s])
            out_ref[0, g] = _rms(out, gk_ref[...]) if s == 0 else out


def _compress(kv, pe, w1, w2, gk, batch, seq):
    nchunk = seq // CMP_STRIDE
    out_spec = lambda: pl.BlockSpec((1, NSA_KV_GROUPS, nchunk, HEAD_DIM), lambda b: (b, 0, 0, 0))
    out_shape = jax.ShapeDtypeStruct((batch, NSA_KV_GROUPS, nchunk, HEAD_DIM), F32)
    const = lambda a: pl.BlockSpec(a.shape, lambda b: (0,) * a.ndim)
    return pl.pallas_call(
        _compress_kernel,
        grid=(batch,),
        in_specs=[pl.BlockSpec((seq, NSA_KV_WIDTH), lambda b: (b, 0)), pl.BlockSpec((seq, NSA_KV_WIDTH), lambda b: (b, 1)),
                  const(pe), const(w1), const(w2), const(gk)],
        out_specs=[out_spec(), out_spec()],
        out_shape=[out_shape, out_shape],
        compiler_params=_params("parallel"),
        name="compress",
    )(kv, kv, pe, w1, w2, gk)


def _rank_select(scores, cur_t, n_live, rank_ref):
    nb, t = scores[0].shape
    sub = 8
    groups = range(nb // sub)
    rank_ref[...] = jnp.zeros(rank_ref.shape, F32)
    below = lax.broadcasted_iota(jnp.int32, (sub, t), 0)
    for gj in groups:
        @pl.when(gj * sub < n_live)
        def _():
            for n, x in enumerate(scores):
                xs = [x[g * sub:(g + 1) * sub] for g in groups]
                inc = [jnp.zeros((sub, t), F32) for _ in groups]
                for jl in range(sub):
                    j = gj * sub + jl
                    row = jnp.broadcast_to(x[j:j + 1, :], (sub, t))
                    for g in groups:
                        if g > gj:
                            one = jnp.where(row >= xs[g], 1.0, 0.0)
                        elif g < gj:
                            one = jnp.where(row > xs[g], 1.0, 0.0)
                        else:
                            one = jnp.where(below > jl, jnp.where(row >= xs[g], 1.0, 0.0),
                                            jnp.where(row > xs[g], 1.0, 0.0))
                        inc[g] = inc[g] + one
                rank_ref[n] = rank_ref[n] + jnp.concatenate(inc, axis=0)
    jj = lax.broadcasted_iota(jnp.int32, (nb, t), 0)
    live = jj <= cur_t
    return [jnp.where(live, jnp.where(rank_ref[n] < float(min(SLC_TOPK, nb)), 0.0, NEG_INF), NEG_INF)
            for n in range(len(scores))]


def _win_masked(tq, tk):
    return [o for o in range(-(WINDOW // tk), 0) if tq - 1 - o * tk >= WINDOW]


def _scores(ks, ws):
    return [jnp.dot(k, w, preferred_element_type=F32) for k, w in zip(ks, ws)]


def _softmax_update(m_ref, acc_ref, slots, s, vts):
    m_old = [m_ref[i] for i in slots]
    ncol = s[0].shape[1] // LANES
    m_cols = [[] for _ in slots]
    p_cols = [[] for _ in slots]
    for c in range(ncol):
        cols = slice(c * LANES, (c + 1) * LANES)
        for n, x in enumerate(s):
            xc = x[:, cols]
            mc = jnp.maximum(m_old[n][:, cols], jnp.max(xc, axis=0, keepdims=True))
            m_cols[n].append(mc)
            p_cols[n].append(jnp.exp2(xc - mc).astype(BF16))
    m_new = [jnp.concatenate(mc, axis=1) for mc in m_cols]
    pv = [jnp.dot(vt, jnp.concatenate(pc, axis=1), preferred_element_type=F32) for vt, pc in zip(vts, p_cols)]
    acc = [jnp.exp2(mo - mn) * acc_ref[i] + z for mo, mn, i, z in zip(m_old, m_new, slots, pv)]
    for i, mn, ac in zip(slots, m_new, acc):
        m_ref[i] = mn
        acc_ref[i] = ac


def _attn_kernel(q_ref, gn_ref, kc_ref, vct_ref, ksx_ref, vst_ref, kwn_ref, vwt_ref, bc_ref, bn_ref, wm_ref,
                 wov_ref, o_ref, m_ref, acc_ref, sa_ref, sb_ref, rank_ref):
    tq = q_ref.shape[2]
    tk = K_TILE
    nblk = wov_ref.shape[0]
    G, R = NSA_KV_GROUPS, NSA_GROUP
    qt = pl.program_id(1)
    n_q = tq // tk
    n_b = -(-REL_MAX_DIST // tk)
    j0 = qt * n_q
    gates = gn_ref[0]

    qn, qx, negs, o_cmp, scores = [], [], [], [], []
    ncmp = kc_ref.shape[2]
    step = tq // CMP_STRIDE
    bc_rows = pl.ds(pl.multiple_of(bc_ref.shape[2] - ncmp - qt * step, 8), ncmp)
    per_head = tq // LANES
    for g in range(G):
        qn.append(jnp.concatenate([q_ref[0, h * HEAD_DIM:(h + 1) * HEAD_DIM, :] for h in range(g * R, (g + 1) * R)],
                                  axis=1))

    for g in range(G):
        s = jnp.dot(kc_ref[0, g].astype(BF16), qn[g], preferred_element_type=F32)
        p_cols = []
        for c in range(R * tq // LANES):
            cols = slice(c * LANES, (c + 1) * LANES)
            t_cols = slice(c % per_head * LANES, (c % per_head + 1) * LANES)
            sc = s[:, cols] + bc_ref[g, c // per_head, bc_rows, t_cols]
            e = jnp.exp2(sc - jnp.maximum(jnp.max(sc, axis=0, keepdims=True), M_FLOOR))
            p_cols.append(e * (1.0 / jnp.maximum(jnp.sum(e, axis=0, keepdims=True), 1e-30)))
        o_cmp.append(jnp.dot(vct_ref[0, g].astype(BF16), jnp.concatenate(p_cols, axis=1).astype(BF16),
                             preferred_element_type=F32))
        psum = jnp.concatenate([sum(p_cols[r * per_head + t] for r in range(R)) for t in range(per_head)], axis=1)
        imp = _mm_f32(wov_ref[...], psum)

        blk = lax.broadcasted_iota(jnp.int32, imp.shape, 0)
        cur = (qt * tq + lax.broadcasted_iota(jnp.int32, imp.shape, 1)) // SLC_BLOCK
        forced = (blk == 0) | (blk == cur) | (blk == cur - 1)
        scores.append(jnp.where(forced, FORCE, jnp.where(blk <= cur, imp, -FORCE)))

    cur = (qt * tq + lax.broadcasted_iota(jnp.int32, (nblk, tq), 1)) // SLC_BLOCK
    for g, neg in enumerate(_rank_select(scores, cur, (qt + 1) * (tq // SLC_BLOCK), rank_ref)):
        if nblk < LANES - HEAD_DIM:
            neg = jnp.concatenate([neg, jnp.zeros((LANES - HEAD_DIM - nblk, tq), F32)], axis=0)
        neg = neg.astype(BF16)
        negs.append(neg)
        qx.append(jnp.concatenate([qn[g], jnp.concatenate([neg] * R, axis=1)], axis=0))

    m_ref[...] = jnp.full(m_ref.shape, M_FLOOR, F32)
    acc_ref[...] = jnp.zeros(acc_ref.shape, F32)

    def invalid_before(level):
        return jnp.where(qt >= level, 0.0, NEG_INF)

    def slc_queries(g, level):
        if level == 0:
            return qx[g]
        rows = jnp.minimum(negs[g], invalid_before(level).astype(BF16))
        return jnp.concatenate([qn[g], jnp.concatenate([rows] * R, axis=1)], axis=0)

    def win_queries(g, level):
        flag_row = lax.broadcasted_iota(jnp.int32, (LANES - HEAD_DIM, R * tq), 0) == 0
        extra = jnp.where(flag_row, invalid_before(level), 0.0).astype(BF16)
        return jnp.concatenate([qn[g], extra], axis=0)

    def slc_probs(o, bias):
        j = jnp.maximum(j0 + o, 0)
        sl = pl.ds(pl.multiple_of(j * tk, tk), tk)
        return [(g, ksx_ref[g, sl, :], vst_ref[g, j], slc_queries(g, max(-(o // n_q), 0)),
                 functools.partial(bias, g)) for g in range(G)]

    def win_probs(o, bias):
        j = jnp.maximum(j0 + o, 0)
        sl = pl.ds(pl.multiple_of(j * tk, tk), tk)
        return [(G + g, kwn_ref[g, sl, :], vwt_ref[g, j], win_queries(g, max(-(o // n_q), 0)),
                 None if bias is None else functools.partial(bias, g)) for g in range(G)]

    n_far = jnp.maximum(j0 - n_b, 0)
    groups = list(range(G))

    def far_scores(j, buf, gs=groups):
        sl = pl.ds(pl.multiple_of(j * tk, tk), tk)
        for g, x in zip(gs, _scores([ksx_ref[g, sl, :] for g in gs], [qx[g] for g in gs])):
            buf[g] = x

    def far_update(j, buf, gs=groups):
        _softmax_update(m_ref, acc_ref, gs, [buf[g] for g in gs], [vst_ref[g, j] for g in gs])

    far_scores(0, sa_ref)

    def far_body(i, carry):
        for g in groups:
            far_scores(2 * i + 1, sb_ref, [g])
            far_update(2 * i, sa_ref, [g])
        for g in groups:
            far_scores(2 * i + 2, sa_ref, [g])
            far_update(2 * i + 1, sb_ref, [g])
        return carry

    lax.fori_loop(0, n_far // 2, far_body, 0)

    @pl.when(n_far % 2 == 1)
    def _():
        far_update(n_far - 1, sa_ref)

    masked = _win_masked(tq, tk)
    rounds = []
    for o in range(-(WINDOW // tk), n_q):
        if o >= -n_b:
            near = lambda g, r, i=o + n_b: bn_ref[g, r, i]
            rounds.append(slc_probs(o, near) + win_probs(o, near))
        elif o in masked:
            rounds.append(win_probs(o, lambda g, r, i=masked.index(o): wm_ref[i]))
        else:
            rounds.append(win_probs(o, None))

    def round_scores(probs):
        sc = _scores([p[1] for p in probs], [p[3] for p in probs])
        return [x if p[4] is None else
                jnp.concatenate([x[:, r * tq:(r + 1) * tq] + p[4](r) for r in range(R)], axis=1)
                for x, p in zip(sc, probs)]

    sc = round_scores(rounds[0])
    for i, probs in enumerate(rounds):
        sc_next = round_scores(rounds[i + 1]) if i + 1 < len(rounds) else None
        _softmax_update(m_ref, acc_ref, [p[0] for p in probs], sc, [p[2] for p in probs])
        sc = sc_next

    outs = []
    for g in range(G):
        acc = acc_ref[g]
        o_slc = acc[:HEAD_DIM] / acc[HEAD_DIM:HEAD_DIM + 1]
        acc = acc_ref[G + g]
        o_win = acc[:HEAD_DIM] / acc[HEAD_DIM:HEAD_DIM + 1]
        for r in range(R):
            h = g * R + r
            cols = slice(r * tq, (r + 1) * tq)
            outs.append(gates[3 * h:3 * h + 1] * o_cmp[g][:, cols] + gates[3 * h + 1:3 * h + 2] * o_slc[:, cols]
                        + gates[3 * h + 2:3 * h + 3] * o_win[:, cols])
    o_ref[...] = jnp.concatenate(outs, axis=0).T.astype(o_ref.dtype)


def _attention(q, gn, kc, vct, ksx, vst, kwn, vwt, bias_c, bias_near, win_mask, wov, batch, seq):
    n = batch * seq
    tq = Q_TILE
    nq = seq // tq
    ncmp = kc.shape[2]
    row = lambda b, i: (b * nq + i, 0)
    qtile = lambda b, i: (b * nq + i, 0, 0)
    whole = lambda b, i: (0, b, 0)
    tiles = lambda b, i: (0, b, 0, 0)
    const = lambda a: pl.BlockSpec(a.shape, lambda b, i: (0,) * a.ndim, pipeline_mode=pl.Buffered(1))
    return pl.pallas_call(
        _attn_kernel,
        grid=(batch, nq),
        in_specs=[pl.BlockSpec((1, NSA_WIDTH, tq), qtile),
                  pl.BlockSpec((1, LANES, tq), qtile),
                  pl.BlockSpec((1, NSA_KV_GROUPS, ncmp, HEAD_DIM), lambda b, i: (b, 0, 0, 0)),
                  pl.BlockSpec((1, NSA_KV_GROUPS, HEAD_DIM, ncmp), lambda b, i: (b, 0, 0, 0)),
                  pl.BlockSpec((2, seq, LANES), whole),
                  pl.BlockSpec((2, seq // K_TILE, VT_ROWS, K_TILE), tiles),
                  pl.BlockSpec((2, seq, LANES), whole),
                  pl.BlockSpec((2, seq // K_TILE, VT_ROWS, K_TILE), tiles),
                  const(bias_c),
                  const(bias_near), const(win_mask), const(wov)],
        out_specs=pl.BlockSpec((tq, NSA_WIDTH), row),
        out_shape=jax.ShapeDtypeStruct((n, NSA_WIDTH), BF16),
        scratch_shapes=[pltpu.VMEM((2 * NSA_KV_GROUPS, 1, NSA_GROUP * tq), F32),
                        pltpu.VMEM((2 * NSA_KV_GROUPS, VT_ROWS, NSA_GROUP * tq), F32),
                        pltpu.VMEM((NSA_KV_GROUPS, K_TILE, NSA_GROUP * tq), F32),
                        pltpu.VMEM((NSA_KV_GROUPS, K_TILE, NSA_GROUP * tq), F32),
                        pltpu.VMEM((NSA_KV_GROUPS, seq // SLC_BLOCK, tq), F32)],
        compiler_params=_params("parallel", "parallel"),
        name="nsa_attention",
    )(q, gn, kc, vct, ksx, vst, kwn, vwt, bias_c, bias_near, win_mask, wov)


def _split_bf16(z, parts):
    out = []
    for _ in range(parts - 1):
        hi = z.astype(BF16)
        out.append(hi)
        z = z - hi.astype(F32)
    return out + [z.astype(BF16)]


def _head_sums(z, ones):
    width = ones.shape[0]
    nb = z.shape[1] // width
    rows = z.shape[0]
    zb = z.astype(BF16)
    stacked = jnp.concatenate([zb[:, m * width:(m + 1) * width] for m in range(nb)], axis=0)
    sums = jnp.dot(stacked, ones, preferred_element_type=F32)
    return jnp.concatenate([sums[m * rows:(m + 1) * rows] for m in range(nb)], axis=1)


def _softplus(z):
    return jnp.maximum(z, 0.0) + jnp.log(1.0 + jnp.exp(-jnp.abs(z)))


def _rwkv_chunk(x, prev, st, mu_ref, w0_ref, w2_ref, a0_ref, a2_ref, g2_ref, kk_ref, ka_ref, rk_ref,
                lng_ref, lnb_ref, ones_ref):
    L = x.shape[0]
    W = RWKV_WIDTH
    N = HEAD_DIM
    row_id = lax.broadcasted_iota(jnp.int32, x.shape, 0)
    shifted = jnp.where(row_id == 0, prev, pltpu.roll(x, 1, axis=0))
    xl = x + (shifted - x) * mu_ref[...]
    yield None
    r = xl[:, 0:W]
    k = xl[:, W:2 * W]
    v = xl[:, 2 * W:3 * W]
    xw = xl[:, 3 * W:3 * W + LORA_W]
    xa = xl[:, 3 * W + LORA_W:3 * W + LORA_W + LORA_A]
    xg = xl[:, 3 * W + LORA_W + LORA_A:]
    w = -_softplus(-(w0_ref[...] + _mm(jnp.tanh(xw), w2_ref[...]))) - 0.5
    ld = -jnp.exp(w)
    a = jax.nn.sigmoid(a0_ref[...] + _mm(xa, a2_ref[...]))
    gate = _mm(jax.nn.sigmoid(xg), g2_ref[...])
    kkv = k * kk_ref[...]
    k2 = k * (1.0 + (a - 1.0) * ka_ref[...])
    yield None

    ti = lax.broadcasted_iota(jnp.int32, (L, L), 0)
    si = lax.broadcasted_iota(jnp.int32, (L, L), 1)
    incl = si <= ti
    strict = si < ti
    tri = jnp.where(incl, 1.0, 0.0).astype(BF16)
    cl3 = jnp.dot(tri, jnp.concatenate(_split_bf16(ld, 3), axis=1), preferred_element_type=F32)
    cl = cl3[:, :W] + (cl3[:, W:2 * W] + cl3[:, 2 * W:])
    cl_end = cl[L - 1:L, :]
    yield None
    e_pos = jnp.exp(cl)
    e_neg = jnp.exp(-cl)
    e_prev = jnp.exp(cl - ld)
    e_end = jnp.exp(cl_end - cl)
    eye = jnp.where(ti == si, 1.0, 0.0)
    yield None

    hsum = lambda z: _head_sums(z, ones_ref[...])

    kk_n = kkv * lax.rsqrt(jnp.maximum(hsum(kkv * kkv), 1e-24))
    bv = kk_n * a
    yield None
    a_t = (-kk_n * e_prev).astype(BF16)
    b_t = (bv * e_neg).astype(BF16)
    k_t = (k2 * e_neg).astype(BF16)
    r_t = (r * e_pos).astype(BF16)
    yield None
    v_b = v.astype(BF16)
    k_e = k2 * e_end
    b_e = bv * e_end

    yield "elementwise done"

    P = range(W // LANES)
    ps = [slice(m * LANES, (m + 1) * LANES) for m in P]
    first = lax.broadcasted_iota(jnp.int32, (1, LANES), 1) < N
    same_head = (lax.broadcasted_iota(jnp.int32, (LANES, LANES), 0) // N
                 == lax.broadcasted_iota(jnp.int32, (LANES, LANES), 1) // N)

    def diag_rows(x):
        zero = jnp.zeros_like(x)
        return jnp.concatenate([jnp.where(first, x, zero), jnp.where(first, zero, x)], axis=0)

    def diag_blocks(x):
        zero = jnp.zeros_like(x[:, :LANES])
        return jnp.concatenate([jnp.concatenate([x[:, :LANES], zero], axis=1),
                                jnp.concatenate([zero, x[:, LANES:]], axis=1)], axis=0)

    ke_t = [k_e[:, s].T for s in ps]
    be_t = [b_e[:, s].T for s in ps]
    p_end = [e_pos[:, s].T[:, L - 1:L] for s in ps]

    lhs = [jnp.concatenate([a_t[:, s], r_t[:, s]], axis=0) for s in ps]
    rhs = [jnp.concatenate([k_t[:, s], b_t[:, s]], axis=0) for s in ps]
    zero_b = jnp.zeros_like(lhs[0])
    aa = [[lax.dot_general(jnp.where(first, x, zero_b) if j == 0 else jnp.where(first, zero_b, x), y, _NT,
                           preferred_element_type=F32) for j in range(2)] for x, y in zip(lhs, rhs)]
    cat2 = lambda f: [jnp.concatenate([f(pair[0]), f(pair[1])], axis=1) for pair in aa]
    a_ak = cat2(lambda x: jnp.where(strict, x[:L, :L], 0.0))
    a_ab = cat2(lambda x: jnp.where(strict, x[:L, L:], 0.0))
    a_rk = cat2(lambda x: jnp.where(incl, x[L:, :L], 0.0))
    a_rb = cat2(lambda x: jnp.where(incl, x[L:, L:], 0.0))
    yield None

    eye2 = jnp.concatenate([eye, eye], axis=1)
    tinv = [eye2 + x for x in a_ab]
    pw = [_mm(x, diag_blocks(x)) for x in a_ab]
    yield None
    span = 2
    while 2 * span < L:
        both = [_mm(jnp.concatenate([t, p], axis=0), diag_blocks(p)) for t, p in zip(tinv, pw)]
        tinv = [t + x[:L] for t, x in zip(tinv, both)]
        pw = [x[L:] for x in both]
        yield None
        span *= 2
    tinv = [t + _mm(t, diag_blocks(p)) for t, p in zip(tinv, pw)]
    yield None

    v_d = [diag_rows(v_b[:, s]) for s in ps]
    av = [_mm(jnp.concatenate([a_ak[m], a_rk[m]], axis=0), v_d[m]) for m in P]
    tw = [_mm(tinv[m], jnp.concatenate([diag_rows(a_t[:, ps[m]]), diag_rows(av[m][:L].astype(BF16))], axis=1))
          for m in P]
    kv_loc = [jnp.where(same_head, _mm(ke_t[m], v_b[:, ps[m]]), 0.0) for m in P]
    yield None

    ws = [_mm(jnp.concatenate([tw[m][:, :LANES].astype(BF16), r_t[:, ps[m]]], axis=0), st[m]) for m in P]
    u = [ws[m][:L] + tw[m][:, LANES:] for m in P]
    yield None
    y = [ws[m][L:] + av[m][L:] + _mm(a_rb[m], diag_rows(u[m])) for m in P]
    st_new = [p_end[m] * st[m] + kv_loc[m] + jnp.where(same_head, _mm(be_t[m], u[m]), 0.0) for m in P]
    yield None

    y = jnp.concatenate(y, axis=1)
    d = y - hsum(y) * (1.0 / N)
    yn = d * lax.rsqrt(hsum(d * d) * (1.0 / N) + GN_EPS) * lng_ref[...] + lnb_ref[...]
    bonus = hsum(r * k2 * rk_ref[...])
    yield (yn + bonus * v) * gate, st_new


def _rwkv_kernel(x_ref, xp_ref, *refs):
    *param_refs, o_ref, st_ref = refs
    c = pl.program_id(1)

    @pl.when(c == 0)
    def _():
        st_ref[...] = jnp.zeros(st_ref.shape, F32)

    chunks = []
    for i in range(x_ref.shape[0]):
        prev = jnp.where(c > 0, xp_ref[i, xp_ref.shape[1] - 1:, :], 0.0)
        st = [st_ref[i, m] for m in range(st_ref.shape[1])]
        chunks.append(_rwkv_chunk(x_ref[i], prev, st, *param_refs))
    def advance(gen, until_result):
        item = next(gen)
        return (isinstance(item, tuple), item) if until_result else (item == "elementwise done", item)

    while not advance(chunks[0], False)[0]:
        pass
    outs, states = [], []
    for i, chunk in enumerate(chunks):
        following = chunks[i + 1] if i + 1 < len(chunks) else None
        result = None
        while result is None or following is not None:
            if result is None:
                done, item = advance(chunk, True)
                result = item if done else None
            if following is not None and advance(following, False)[0]:
                following = None
        outs.append(result[0].astype(o_ref.dtype))
        states.append(jnp.stack(result[1]))
    o_ref[...] = jnp.stack(outs)
    st_ref[...] = jnp.stack(states)


def _rwkv(rw, vecs, w2, a2, g2, batch, seq):
    width = rw.shape[1]
    L = min(CHUNK, seq)
    nc = seq // L
    nb = RWKV_SEQS if batch % RWKV_SEQS == 0 else 1
    sub = 8
    rw3 = rw.reshape(batch, seq, width)
    vec = lambda wd: pl.BlockSpec((1, wd), lambda b, c: (0, 0))
    mat = lambda m: pl.BlockSpec(m.shape, lambda b, c: (0, 0))
    mu, w0, a0, kk, ka, rk, lng, lnb = vecs
    head = np.arange(2 * LANES) // HEAD_DIM
    ones = jnp.asarray(head[:, None] == head[None, :], BF16)
    out = pl.pallas_call(
        _rwkv_kernel,
        grid=(batch // nb, nc),
        in_specs=[pl.BlockSpec((nb, L, width), lambda b, c: (b, c, 0)),
                  pl.BlockSpec((nb, sub, width), lambda b, c: (b, jnp.maximum(c * (L // sub) - 1, 0), 0)),
                  vec(width), vec(RWKV_WIDTH), mat(w2), vec(RWKV_WIDTH), mat(a2), mat(g2),
                  vec(RWKV_WIDTH), vec(RWKV_WIDTH), vec(RWKV_WIDTH), vec(RWKV_WIDTH), vec(RWKV_WIDTH), mat(ones)],
        out_specs=pl.BlockSpec((nb, L, RWKV_WIDTH), lambda b, c: (b, c, 0)),
        out_shape=jax.ShapeDtypeStruct((batch, seq, RWKV_WIDTH), BF16),
        scratch_shapes=[pltpu.VMEM((nb, RWKV_WIDTH // LANES, LANES, LANES), F32)],
        compiler_params=_params("parallel", "arbitrary"),
        name="rwkv7",
    )(rw3, rw3, mu, w0, w2, a0, a2, g2, kk, ka, rk, lng, lnb, ones)
    return out


def _merge_kernel(x_ref, oa_ref, ob_ref, gab_ref, wpa_ref, wpb_ref, wo_ref, o_ref):
    d = x_ref.shape[1]
    pa = jnp.dot(oa_ref[...], wpa_ref[...], preferred_element_type=F32)
    pb = jnp.dot(ob_ref[0], wpb_ref[...], preferred_element_type=F32)
    gab = gab_ref[...].astype(F32)
    merged = jax.nn.sigmoid(gab[:, :d]) * pa + jax.nn.sigmoid(gab[:, d:]) * pb
    o_ref[...] = x_ref[...] + _mm(merged, wo_ref[...])


def _merge(x2, oa, ob, gab, wpa, wpb, wo, tm):
    n, d = x2.shape
    per_seq = ob.shape[1] // tm
    row = lambda wd: pl.BlockSpec((tm, wd), lambda i: (i, 0))
    mat = lambda m: pl.BlockSpec(m.shape, lambda i: (0, 0))
    ob_spec = pl.BlockSpec((1, tm, ob.shape[2]), lambda i: (i // per_seq, i % per_seq, 0))
    return pl.pallas_call(
        _merge_kernel,
        grid=(n // tm,),
        in_specs=[row(d), row(oa.shape[1]), ob_spec, row(gab.shape[1]), mat(wpa), mat(wpb), mat(wo)],
        out_specs=row(d),
        out_shape=jax.ShapeDtypeStruct((n, d), F32),
        compiler_params=_params("parallel"),
        name="merge",
    )(x2, oa, ob, gab, wpa, wpb, wo)


def _ffn_kernel(seq, x_ref, xh_ref, g_ref, wv_ref, wg_ref, cwv_ref, cwg_ref, cbv_ref, cbg_ref, wd_ref, o_ref,
                h_ref, uv_ref, ug_ref):
    tm = x_ref.shape[0]
    i = pl.program_id(0)
    f = pl.program_id(1)

    @pl.when(f == 0)
    def _():
        first = lax.rem(i * tm, seq) == 0
        halo = jnp.where(first, 0.0, _rms(xh_ref[...], g_ref[...]))
        h_ref[0:HALO, :] = halo.astype(BF16)
        h_ref[HALO:, :] = _rms(x_ref[...], g_ref[...]).astype(BF16)

    h = h_ref[...]
    uv_ref[...] = jnp.dot(h, wv_ref[...], preferred_element_type=F32)
    ug_ref[...] = jnp.dot(h, wg_ref[...], preferred_element_type=F32)

    def conv(u_ref, cw_ref, cb_ref):
        acc = cb_ref[...] + cw_ref[0:1, :] * u_ref[pl.ds(HALO - 2, tm), :]
        acc = acc + cw_ref[1:2, :] * u_ref[pl.ds(HALO - 1, tm), :]
        return acc + cw_ref[2:3, :] * u_ref[pl.ds(HALO, tm), :]

    val = conv(uv_ref, cwv_ref, cbv_ref)
    gt = conv(ug_ref, cwg_ref, cbg_ref)
    y = _mm(gt * jax.nn.sigmoid(gt) * val, wd_ref[...])

    @pl.when(f == 0)
    def _():
        o_ref[...] = x_ref[...] + y

    @pl.when(f > 0)
    def _():
        o_ref[...] = o_ref[...] + y


def _ffn(x1, g, w_up, conv_w, conv_b, w_down, seq, tm, tf):
    n, d = x1.shape
    dff = w_down.shape[0]
    nf = dff // tf
    return pl.pallas_call(
        functools.partial(_ffn_kernel, seq),
        grid=(n // tm, nf),
        in_specs=[pl.BlockSpec((tm, d), lambda i, f: (i, 0)),
                  pl.BlockSpec((HALO, d), lambda i, f: (jnp.maximum(i * (tm // HALO) - 1, 0), 0)),
                  pl.BlockSpec((1, d), lambda i, f: (0, 0)),
                  pl.BlockSpec((d, tf), lambda i, f: (0, f)),
                  pl.BlockSpec((d, tf), lambda i, f: (0, nf + f)),
                  pl.BlockSpec((CONV_WIDTH, tf), lambda i, f: (0, f)),
                  pl.BlockSpec((CONV_WIDTH, tf), lambda i, f: (0, nf + f)),
                  pl.BlockSpec((1, tf), lambda i, f: (0, f)),
                  pl.BlockSpec((1, tf), lambda i, f: (0, nf + f)),
                  pl.BlockSpec((tf, d), lambda i, f: (f, 0))],
        out_specs=pl.BlockSpec((tm, d), lambda i, f: (i, 0)),
        out_shape=jax.ShapeDtypeStruct((n, d), F32),
        scratch_shapes=[pltpu.VMEM((tm + HALO, d), BF16),
                        pltpu.VMEM((tm + HALO, tf), F32),
                        pltpu.VMEM((tm + HALO, tf), F32)],
        compiler_params=_params("parallel", "arbitrary"),
        name="convffn",
    )(x1, x1, g, w_up, w_up, conv_w, conv_w, conv_b, conv_b, w_down)


def _t5_bucket(dist):
    n = np.maximum(dist, 0)
    max_exact = REL_BUCKETS // 2
    ratio = np.log(np.maximum(n, 1).astype(np.float32) / max_exact) / math.log(REL_MAX_DIST / max_exact)
    large = np.minimum(max_exact + (ratio * (REL_BUCKETS - max_exact)).astype(np.int32), REL_BUCKETS - 1)
    return np.where(n < max_exact, n, large).astype(np.int32)


def _bias_tables(rel_bias, seq, ncmp):
    tq, tk = Q_TILE, K_TILE
    G, R = NSA_KV_GROUPS, NSA_GROUP
    nq = seq // tq
    step = tq // CMP_STRIDE
    tab = rel_bias.astype(F32)

    off = (nq - 1) * step
    d_c = np.arange(tq)[None, :] - (np.arange(ncmp + off)[:, None] - off) * CMP_STRIDE - (CMP_BLOCK - 1)
    n_b = -(-REL_MAX_DIST // tk)
    d0 = np.arange(tq)[None, :] - np.arange(tk)[:, None]
    d_n = np.stack([d0 - o * tk for o in range(-n_b, tq // tk)])

    buckets = np.concatenate([_t5_bucket(d_c).reshape(-1), _t5_bucket(d_n).reshape(-1)])
    onehot = (jnp.arange(REL_BUCKETS, dtype=jnp.int32)[:, None] == jnp.asarray(buckets)[None, :]).astype(F32)
    vals = jnp.dot(tab.T * LOG2E, onehot, precision=lax.Precision.HIGHEST, preferred_element_type=F32)
    base = jnp.where(jnp.asarray(d_c >= 0), vals[:, :d_c.size].reshape(G, R, ncmp + off, tq), NEG_INF)

    far = (tab[REL_BUCKETS - 1] * LOG2E).reshape(G, R, 1, 1, 1)
    near = vals[:, d_c.size:].reshape(G, R, d_n.shape[0], tk, tq) - far
    bias_near = jnp.where(jnp.asarray(d_n >= 0), near, NEG_INF)
    win = np.stack([np.where(d0 - o * tk < WINDOW, 0.0, NEG_INF) for o in _win_masked(tq, tk)])
    return base, bias_near, jnp.asarray(win.astype(np.float32))


def _overlap_matrix(ncmp_pad, n_slc, n_cmp):
    m = np.zeros((ncmp_pad, n_slc), np.float32)
    ratio = SLC_BLOCK // CMP_STRIDE
    for j in range(n_slc):
        for o, wgt in enumerate(OVERLAP_W):
            cidx = ratio * j + o - (CMP_BLOCK // CMP_STRIDE - 1)
            if 0 <= cidx < n_cmp:
                m[cidx, j] += wgt
    return jnp.asarray(m.T)


def _layer(x, attn_norm_g, w_in, rel_bias, q_norm_g, k_norm_g, cmp_pe_k, cmp_w1_k, cmp_w2_k,
           cmp_pe_v, cmp_w1_v, cmp_w2_v, rwkv_mu, rwkv_w0, rwkv_w2, rwkv_a0, rwkv_a2, rwkv_g2,
           rwkv_k_k, rwkv_k_a, rwkv_r_k, rwkv_ln_g, rwkv_ln_b, w_proj_a, w_proj_b, w_out,
           ffn_norm_g, w_up, conv_w, conv_b, w_down):
    batch, seq, d = x.shape
    n = batch * seq
    assert seq % Q_TILE == 0 and Q_TILE % K_TILE == 0 and WINDOW % K_TILE == 0
    assert seq % CMP_STRIDE == 0 and seq % FFN_ROWS == 0 and seq % PROJ_ROWS == 0 and PROJ_ROWS % K_TILE == 0 and PROJ_ROWS % Q_TILE == 0
    assert seq // SLC_BLOCK <= LANES - HEAD_DIM
    assert NSA_KV_WIDTH == LANES and CMP_PACK * HEAD_DIM == 2 * LANES
    x2 = x.reshape(n, d)
    row = lambda a: a.reshape(1, -1).astype(F32)

    o_kvc = NSA_WIDTH
    o_kv = o_kvc + 2 * NSA_KV_WIDTH
    o_gn = o_kv + 4 * NSA_KV_WIDTH
    o_rw = o_gn + 3 * NSA_HEADS
    o_gab = o_rw + RWKV_IN_WIDTH
    w16 = w_in.astype(BF16)
    gn_pad = jnp.zeros((d, LANES - 3 * NSA_HEADS), BF16)
    ws = (w16[:, o_rw:o_gab], w16[:, :o_kvc], w16[:, o_kvc:o_kv], w16[:, o_kv:o_gn], w16[:, o_gab:],
          jnp.concatenate([w16[:, o_gn:o_rw], gn_pad], axis=1))
    gk = row(k_norm_g)
    rw, q, kv_cmp, gab, gates, ksx, kwn, vst, vwt = _inproj(
        x2, row(attn_norm_g), ws, jnp.concatenate([gk, gk], axis=1), q_norm_g.reshape(-1, 1).astype(F32), seq,
        tm=PROJ_ROWS)

    nchunk = seq // CMP_STRIDE
    n_cmp = nchunk - CMP_BLOCK // CMP_STRIDE + 1
    pe = jnp.stack([cmp_pe_k, cmp_pe_v]).astype(F32)
    pe = jnp.concatenate([pe] * NSA_KV_GROUPS, axis=-1)
    w1 = jnp.stack([cmp_w1_k, cmp_w1_v]).astype(BF16)
    w1 = w1.reshape(2, CMP_BLOCK // CMP_PACK, CMP_PACK * HEAD_DIM, CMP_HIDDEN)
    w2 = jnp.stack([cmp_w2_k, cmp_w2_v]).astype(BF16)
    kc, vc = _compress(kv_cmp, pe, w1, w2, gk, batch, seq)

    bias_c, bias_near, win_mask = _bias_tables(rel_bias, seq, nchunk)
    wov = _overlap_matrix(nchunk, seq // SLC_BLOCK, n_cmp)
    o_a = _attention(q, gates, kc, jnp.swapaxes(vc, -1, -2), ksx, vst, kwn, vwt, bias_c, bias_near, win_mask,
                     wov, batch, seq)

    vecs = tuple(row(a) for a in (rwkv_mu, rwkv_w0, rwkv_a0, rwkv_k_k, rwkv_k_a, rwkv_r_k, rwkv_ln_g, rwkv_ln_b))
    o_b = _rwkv(rw, vecs, rwkv_w2.astype(BF16), rwkv_a2.astype(BF16), rwkv_g2.astype(BF16), batch, seq)

    x1 = _merge(x2, o_a, o_b, gab, w_proj_a.astype(BF16), w_proj_b.astype(BF16), w_out.astype(BF16), tm=FFN_ROWS)
    dff = w_down.shape[0]
    out = _ffn(x1, row(ffn_norm_g), w_up.astype(BF16), conv_w.astype(F32), row(conv_b), w_down.astype(BF16),
               seq, tm=FFN_ROWS, tf=dff // FFN_COL_SPLITS)
    return out.reshape(batch, seq, d)


def kernel(x, attn_norm_g, w_in, rel_bias, q_norm_g, k_norm_g, cmp_pe_k, cmp_w1_k, cmp_w2_k, cmp_pe_v, cmp_w1_v,
           cmp_w2_v, rwkv_mu, rwkv_w0, rwkv_w2, rwkv_a0, rwkv_a2, rwkv_g2, rwkv_k_k, rwkv_k_a, rwkv_r_k,
           rwkv_ln_g, rwkv_ln_b, w_proj_a, w_proj_b, w_out, ffn_norm_g, w_up, conv_w, conv_b, w_down):
    per_layer = (attn_norm_g, w_in, None, q_norm_g, k_norm_g, cmp_pe_k, cmp_w1_k, cmp_w2_k, cmp_pe_v, cmp_w1_v,
                 cmp_w2_v, rwkv_mu, rwkv_w0, rwkv_w2, rwkv_a0, rwkv_a2, rwkv_g2, rwkv_k_k, rwkv_k_a, rwkv_r_k,
                 rwkv_ln_g, rwkv_ln_b, w_proj_a, w_proj_b, w_out, ffn_norm_g, w_up, conv_w, conv_b, w_down)
    for l in range(attn_norm_g.shape[0]):
        args = [rel_bias if p is None else p[l] for p in per_layer]
        x = _layer(x, *args)
    return x
```

```python
import functools
import math

import numpy as np
import jax
import jax.numpy as jnp
from jax import lax
from jax.experimental import pallas as pl
from jax.experimental.pallas import tpu as pltpu

F32 = jnp.float32
BF16 = jnp.bfloat16

HEAD_DIM = 64
NSA_HEADS = 8
NSA_KV_GROUPS = 2
NSA_GROUP = NSA_HEADS // NSA_KV_GROUPS
NSA_WIDTH = NSA_HEADS * HEAD_DIM
NSA_KV_WIDTH = NSA_KV_GROUPS * HEAD_DIM
CMP_BLOCK = 32
CMP_STRIDE = 16
CMP_HIDDEN = 256
SLC_BLOCK = 64
SLC_TOPK = 16
OVERLAP_W = (1, 2, 2, 2, 1)
WINDOW = 512
REL_BUCKETS = 32
REL_MAX_DIST = 128
RWKV_HEADS = 8
RWKV_WIDTH = RWKV_HEADS * HEAD_DIM
LORA_W = 64
LORA_A = 64
LORA_G = 128
RWKV_IN_WIDTH = 3 * RWKV_WIDTH + LORA_W + LORA_A + LORA_G
GN_EPS = 64e-5
CONV_WIDTH = 3
RMS_EPS = 1e-6
NEG_INF = -1e30
FORCE = 1e9
LOG2E = math.log2(math.e)
M_FLOOR = -1e20

LANES = 128
VMEM_LIMIT = 56 * 1024 * 1024

Q_TILE = 256
K_TILE = 256
VT_ROWS = 80
CHUNK = 128
RWKV_SEQS = 8
CMP_PACK = 4
HALO = 16
PROJ_ROWS = 512
FFN_ROWS = 1024
FFN_COL_SPLITS = 2

_NT = (((1,), (1,)), ((), ()))


def _params(*sem):
    return pltpu.CompilerParams(dimension_semantics=sem, vmem_limit_bytes=VMEM_LIMIT)


def _mm(a, b):
    return jnp.dot(a.astype(BF16), b.astype(BF16), preferred_element_type=F32)


def _mm_f32(a, b):
    return jnp.dot(a, b, preferred_element_type=F32, precision=lax.Precision.HIGHEST)


def _rms(x, g):
    return x * lax.rsqrt(jnp.mean(x * x, axis=-1, keepdims=True) + RMS_EPS) * g


def _kv_prep(kv, gk, row0, ksx_ref, kwn_ref, vs_ref, vw_ref):
    tm = kv.shape[0]
    lane = lax.broadcasted_iota(jnp.int32, (tm, LANES), 1)
    lo = lane < HEAD_DIM

    def norm2(x):
        x2 = x * x
        s_lo = jnp.sum(jnp.where(lo, x2, 0.0), axis=-1, keepdims=True)
        s_hi = jnp.sum(jnp.where(lo, 0.0, x2), axis=-1, keepdims=True)
        ms = jnp.where(lo, s_lo, s_hi) * (1.0 / HEAD_DIM)
        return x * lax.rsqrt(ms + RMS_EPS) * gk

    ks = norm2(kv[:, 0:LANES])
    kw = norm2(kv[:, 2 * LANES:3 * LANES])
    row = row0 + lax.broadcasted_iota(jnp.int32, (tm, LANES), 0)
    onehot = jnp.where(lane - HEAD_DIM == row // SLC_BLOCK, 1.0, 0.0)
    ksx_ref[0] = jnp.where(lo, ks, onehot).astype(BF16)
    ksx_ref[1] = jnp.where(lo, pltpu.roll(ks, HEAD_DIM, axis=1), onehot).astype(BF16)
    flag_col = jnp.where(lane == HEAD_DIM, 1.0, 0.0)
    kwn_ref[0] = jnp.where(lo, kw, flag_col).astype(BF16)
    kwn_ref[1] = jnp.where(lo, pltpu.roll(kw, HEAD_DIM, axis=1), flag_col).astype(BF16)
    tail = jnp.where(lax.broadcasted_iota(jnp.int32, (VT_ROWS - HEAD_DIM, K_TILE), 0) == 0, 1.0, 0.0)
    for col, ref in ((1, vs_ref), (3, vw_ref)):
        for t in range(tm // K_TILE):
            vt = kv[t * K_TILE:(t + 1) * K_TILE, col * LANES:(col + 1) * LANES].T
            for g in range(NSA_KV_GROUPS):
                ref[g, t] = jnp.concatenate([vt[g * HEAD_DIM:(g + 1) * HEAD_DIM], tail], axis=0).astype(BF16)


def _inproj_kernel(seq, x_ref, g_ref, w_rw, w_q, w_kvc, w_kv, w_gab, w_gn, gk_ref, gq_ref, rw_ref, q_ref, kvc_ref,
                   gab_ref, gn_ref, ksx_ref, kwn_ref, vs_ref, vw_ref):
    tm = x_ref.shape[0]
    h = _rms(x_ref[...], g_ref[...]).astype(BF16)
    for w_ref, ref in ((w_rw, rw_ref), (w_q, q_ref), (w_kvc, kvc_ref), (w_kv, None), (w_gab, gab_ref),
                       (w_gn, gn_ref)):
        y = jnp.dot(h, w_ref[...], preferred_element_type=F32)
        if ref is None:
            _kv_prep(y, gk_ref[...], lax.rem(pl.program_id(0) * tm, seq), ksx_ref, kwn_ref, vs_ref, vw_ref)
        elif ref is q_ref:
            for t in range(tm // Q_TILE):
                yt = y[t * Q_TILE:(t + 1) * Q_TILE].T
                heads = []
                for hd in range(NSA_HEADS):
                    x = yt[hd * HEAD_DIM:(hd + 1) * HEAD_DIM]
                    x = x * lax.rsqrt(jnp.mean(x * x, axis=0, keepdims=True) + RMS_EPS)
                    heads.append(x * gq_ref[...] * (HEAD_DIM ** -0.5 * LOG2E))
                q_ref[t] = jnp.concatenate(heads, axis=0).astype(q_ref.dtype)
        elif ref is gn_ref:
            for t in range(tm // Q_TILE):
                gn_ref[t] = jax.nn.sigmoid(y[t * Q_TILE:(t + 1) * Q_TILE].T)
        else:
            ref[...] = y.astype(ref.dtype)


def _inproj(x2, g, ws, gk2, gq, seq, tm):
    n, d = x2.shape
    rows = lambda wd, dt: (pl.BlockSpec((tm, wd), lambda i: (i, 0)), jax.ShapeDtypeStruct((n, wd), dt))
    tiles = lambda r, c, dt: (pl.BlockSpec((tm // c, r, c), lambda i: (i, 0, 0)),
                              jax.ShapeDtypeStruct((n // c, r, c), dt))
    keys = (pl.BlockSpec((2, tm, LANES), lambda i: (0, i, 0)), jax.ShapeDtypeStruct((2, n, LANES), BF16))
    vts = (pl.BlockSpec((2, tm // K_TILE, VT_ROWS, K_TILE), lambda i: (0, i, 0, 0)),
           jax.ShapeDtypeStruct((2, n // K_TILE, VT_ROWS, K_TILE), BF16))
    outs = [rows(RWKV_IN_WIDTH, F32), tiles(NSA_WIDTH, Q_TILE, BF16), rows(2 * NSA_KV_WIDTH, F32),
            rows(2 * d, BF16), tiles(LANES, Q_TILE, F32), keys, keys, vts, vts]
    return pl.pallas_call(
        functools.partial(_inproj_kernel, seq),
        grid=(n // tm,),
        in_specs=[pl.BlockSpec((tm, d), lambda i: (i, 0)),
                  pl.BlockSpec((1, d), lambda i: (0, 0))]
        + [pl.BlockSpec(w.shape, lambda i: (0, 0), pipeline_mode=pl.Buffered(1)) for w in ws]
        + [pl.BlockSpec((1, LANES), lambda i: (0, 0)), pl.BlockSpec((HEAD_DIM, 1), lambda i: (0, 0))],
        out_specs=[o[0] for o in outs],
        out_shape=[o[1] for o in outs],
        compiler_params=_params("parallel"),
        name="inproj",
    )(x2, g, *ws, gk2, gq)


def _compress_kernel(k_ref, v_ref, pe_ref, w1_ref, w2_ref, gk_ref, kc_ref, vc_ref):
    nchunk = k_ref.shape[0] // CMP_STRIDE
    span = CMP_BLOCK // CMP_STRIDE
    G = NSA_KV_GROUPS
    lo = lax.broadcasted_iota(jnp.int32, (nchunk, LANES), 1) < HEAD_DIM
    acc = [[[jnp.zeros((nchunk, CMP_HIDDEN), F32) for _ in range(span)] for _ in range(G)] for _ in range(2)]
    for p0 in range(0, CMP_STRIDE, CMP_PACK):
        for s, ref in enumerate((k_ref, v_ref)):
            xs = [ref[pl.ds(p0 + i, nchunk, stride=CMP_STRIDE), :] for i in range(CMP_PACK)]
            for half in range(span):
                q0 = half * CMP_STRIDE + p0
                xq = [xs[i] + pe_ref[s, q0 + i:q0 + i + 1, :] for i in range(CMP_PACK)]
                for g in range(G):
                    pairs = [jnp.where(lo, xq[i], pltpu.roll(xq[i + 1], HEAD_DIM, axis=1)) if g == 0 else
                             jnp.where(lo, pltpu.roll(xq[i], HEAD_DIM, axis=1), xq[i + 1])
                             for i in range(0, CMP_PACK, 2)]
                    acc[s][g][half] = acc[s][g][half] + _mm(jnp.concatenate(pairs, axis=1),
                                                            w1_ref[s, q0 // CMP_PACK])
    for s, out_ref in ((0, kc_ref), (1, vc_ref)):
        for g in range(G):
            hid = acc[s][g][0]
            for half in range(1, span):
                hid = hid + pltpu.roll(acc[s][g][half], nchunk - half, axis=0)
            out = _mm(jax.nn.gelu(hid), w2_ref[s])
            out_ref[0, g] = _rms(out, gk_ref[...]) if s == 0 else out


def _compress(kv, pe, w1, w2, gk, batch, seq):
    nchunk = seq // CMP_STRIDE
    out_spec = lambda: pl.BlockSpec((1, NSA_KV_GROUPS, nchunk, HEAD_DIM), lambda b: (b, 0, 0, 0))
    out_shape = jax.ShapeDtypeStruct((batch, NSA_KV_GROUPS, nchunk, HEAD_DIM), F32)
    const = lambda a: pl.BlockSpec(a.shape, lambda b: (0,) * a.ndim)
    return pl.pallas_call(
        _compress_kernel,
        grid=(batch,),
        in_specs=[pl.BlockSpec((seq, NSA_KV_WIDTH), lambda b: (b, 0)), pl.BlockSpec((seq, NSA_KV_WIDTH), lambda b: (b, 1)),
                  const(pe), const(w1), const(w2), const(gk)],
        out_specs=[out_spec(), out_spec()],
        out_shape=[out_shape, out_shape],
        compiler_params=_params("parallel"),
        name="compress",
    )(kv, kv, pe, w1, w2, gk)


def _rank_select(scores, cur_t, n_live, rank_ref):
    nb, t = scores[0].shape
    sub = 8
    groups = range(nb // sub)
    rank_ref[...] = jnp.zeros(rank_ref.shape, F32)
    below = lax.broadcasted_iota(jnp.int32, (sub, t), 0)
    for gj in groups:
        @pl.when(gj * sub < n_live)
        def _():
            for n, x in enumerate(scores):
                xs = [x[g * sub:(g + 1) * sub] for g in groups]
                inc = [jnp.zeros((sub, t), F32) for _ in groups]
                for jl in range(sub):
                    j = gj * sub + jl
                    row = jnp.broadcast_to(x[j:j + 1, :], (sub, t))
                    for g in groups:
                        if g > gj:
                            one = jnp.where(row >= xs[g], 1.0, 0.0)
                        elif g < gj:
                            one = jnp.where(row > xs[g], 1.0, 0.0)
                        else:
                            one = jnp.where(below > jl, jnp.where(row >= xs[g], 1.0, 0.0),
                                            jnp.where(row > xs[g], 1.0, 0.0))
                        inc[g] = inc[g] + one
                rank_ref[n] = rank_ref[n] + jnp.concatenate(inc, axis=0)
    jj = lax.broadcasted_iota(jnp.int32, (nb, t), 0)
    live = jj <= cur_t
    return [jnp.where(live, jnp.where(rank_ref[n] < float(min(SLC_TOPK, nb)), 0.0, NEG_INF), NEG_INF)
            for n in range(len(scores))]


def _win_masked(tq, tk):
    return [o for o in range(-(WINDOW // tk), 0) if tq - 1 - o * tk >= WINDOW]


def _scores(ks, ws):
    return [jnp.dot(k, w, preferred_element_type=F32) for k, w in zip(ks, ws)]


def _softmax_update(m_ref, acc_ref, slots, s, vts):
    m_old = [m_ref[i] for i in slots]
    ncol = s[0].shape[1] // LANES
    m_cols = [[] for _ in slots]
    p_cols = [[] for _ in slots]
    for c in range(ncol):
        cols = slice(c * LANES, (c + 1) * LANES)
        for n, x in enumerate(s):
            xc = x[:, cols]
            mc = jnp.maximum(m_old[n][:, cols], jnp.max(xc, axis=0, keepdims=True))
            m_cols[n].append(mc)
            p_cols[n].append(jnp.exp2(xc - mc).astype(BF16))
    m_new = [jnp.concatenate(mc, axis=1) for mc in m_cols]
    pv = [jnp.dot(vt, jnp.concatenate(pc, axis=1), preferred_element_type=F32) for vt, pc in zip(vts, p_cols)]
    acc = [jnp.exp2(mo - mn) * acc_ref[i] + z for mo, mn, i, z in zip(m_old, m_new, slots, pv)]
    for i, mn, ac in zip(slots, m_new, acc):
        m_ref[i] = mn
        acc_ref[i] = ac


def _attn_kernel(q_ref, gn_ref, kc_ref, vct_ref, ksx_ref, vst_ref, kwn_ref, vwt_ref, bc_ref, bn_ref, wm_ref,
                 wov_ref, o_ref, m_ref, acc_ref, sa_ref, sb_ref, rank_ref):
    tq = q_ref.shape[2]
    tk = K_TILE
    nblk = wov_ref.shape[0]
    G, R = NSA_KV_GROUPS, NSA_GROUP
    qt = pl.program_id(1)
    n_q = tq // tk
    n_b = -(-REL_MAX_DIST // tk)
    j0 = qt * n_q
    gates = gn_ref[0]

    qn, qx, negs, o_cmp, scores = [], [], [], [], []
    ncmp = kc_ref.shape[2]
    step = tq // CMP_STRIDE
    bc_rows = pl.ds(pl.multiple_of(bc_ref.shape[2] - ncmp - qt * step, 8), ncmp)
    per_head = tq // LANES
    for g in range(G):
        qn.append(jnp.concatenate([q_ref[0, h * HEAD_DIM:(h + 1) * HEAD_DIM, :] for h in range(g * R, (g + 1) * R)],
                                  axis=1))

    for g in range(G):
        s = jnp.dot(kc_ref[0, g].astype(BF16), qn[g], preferred_element_type=F32)
        p_cols = []
        for c in range(R * tq // LANES):
            cols = slice(c * LANES, (c + 1) * LANES)
            t_cols = slice(c % per_head * LANES, (c % per_head + 1) * LANES)
            sc = s[:, cols] + bc_ref[g, c // per_head, bc_rows, t_cols]
            e = jnp.exp2(sc - jnp.maximum(jnp.max(sc, axis=0, keepdims=True), M_FLOOR))
            p_cols.append(e * (1.0 / jnp.maximum(jnp.sum(e, axis=0, keepdims=True), 1e-30)))
        o_cmp.append(jnp.dot(vct_ref[0, g].astype(BF16), jnp.concatenate(p_cols, axis=1).astype(BF16),
                             preferred_element_type=F32))
        psum = jnp.concatenate([sum(p_cols[r * per_head + t] for r in range(R)) for t in range(per_head)], axis=1)
        imp = _mm_f32(wov_ref[...], psum)

        blk = lax.broadcasted_iota(jnp.int32, imp.shape, 0)
        cur = (qt * tq + lax.broadcasted_iota(jnp.int32, imp.shape, 1)) // SLC_BLOCK
        forced = (blk == 0) | (blk == cur) | (blk == cur - 1)
        scores.append(jnp.where(forced, FORCE, jnp.where(blk <= cur, imp, -FORCE)))

    cur = (qt * tq + lax.broadcasted_iota(jnp.int32, (nblk, tq), 1)) // SLC_BLOCK
    for g, neg in enumerate(_rank_select(scores, cur, (qt + 1) * (tq // SLC_BLOCK), rank_ref)):
        if nblk < LANES - HEAD_DIM:
            neg = jnp.concatenate([neg, jnp.zeros((LANES - HEAD_DIM - nblk, tq), F32)], axis=0)
        neg = neg.astype(BF16)
        negs.append(neg)
        qx.append(jnp.concatenate([qn[g], jnp.concatenate([neg] * R, axis=1)], axis=0))

    m_ref[...] = jnp.full(m_ref.shape, M_FLOOR, F32)
    acc_ref[...] = jnp.zeros(acc_ref.shape, F32)

    def invalid_before(level):
        return jnp.where(qt >= level, 0.0, NEG_INF)

    def slc_queries(g, level):
        if level == 0:
            return qx[g]
        rows = jnp.minimum(negs[g], invalid_before(level).astype(BF16))
        return jnp.concatenate([qn[g], jnp.concatenate([rows] * R, axis=1)], axis=0)

    def win_queries(g, level):
        flag_row = lax.broadcasted_iota(jnp.int32, (LANES - HEAD_DIM, R * tq), 0) == 0
        extra = jnp.where(flag_row, invalid_before(level), 0.0).astype(BF16)
        return jnp.concatenate([qn[g], extra], axis=0)

    def slc_probs(o, bias):
        j = jnp.maximum(j0 + o, 0)
        sl = pl.ds(pl.multiple_of(j * tk, tk), tk)
        return [(g, ksx_ref[g, sl, :], vst_ref[g, j], slc_queries(g, max(-(o // n_q), 0)),
                 functools.partial(bias, g)) for g in range(G)]

    def win_probs(o, bias):
        j = jnp.maximum(j0 + o, 0)
        sl = pl.ds(pl.multiple_of(j * tk, tk), tk)
        return [(G + g, kwn_ref[g, sl, :], vwt_ref[g, j], win_queries(g, max(-(o // n_q), 0)),
                 None if bias is None else functools.partial(bias, g)) for g in range(G)]

    n_far = jnp.maximum(j0 - n_b, 0)
    groups = list(range(G))

    def far_scores(j, buf, gs=groups):
        sl = pl.ds(pl.multiple_of(j * tk, tk), tk)
        for g, x in zip(gs, _scores([ksx_ref[g, sl, :] for g in gs], [qx[g] for g in gs])):
            buf[g] = x

    def far_update(j, buf, gs=groups):
        _softmax_update(m_ref, acc_ref, gs, [buf[g] for g in gs], [vst_ref[g, j] for g in gs])

    far_scores(0, sa_ref)

    def far_body(i, carry):
        for g in groups:
            far_scores(2 * i + 1, sb_ref, [g])
            far_update(2 * i, sa_ref, [g])
        for g in groups:
            far_scores(2 * i + 2, sa_ref, [g])
            far_update(2 * i + 1, sb_ref, [g])
        return carry

    lax.fori_loop(0, n_far // 2, far_body, 0)

    @pl.when(n_far % 2 == 1)
    def _():
        far_update(n_far - 1, sa_ref)

    masked = _win_masked(tq, tk)
    rounds = []
    for o in range(-(WINDOW // tk), n_q):
        if o >= -n_b:
            near = lambda g, r, i=o + n_b: bn_ref[g, r, i]
            rounds.append(slc_probs(o, near) + win_probs(o, near))
        elif o in masked:
            rounds.append(win_probs(o, lambda g, r, i=masked.index(o): wm_ref[i]))
        else:
            rounds.append(win_probs(o, None))

    def round_scores(probs):
        sc = _scores([p[1] for p in probs], [p[3] for p in probs])
        return [x if p[4] is None else
                jnp.concatenate([x[:, r * tq:(r + 1) * tq] + p[4](r) for r in range(R)], axis=1)
                for x, p in zip(sc, probs)]

    sc = round_scores(rounds[0])
    for i, probs in enumerate(rounds):
        sc_next = round_scores(rounds[i + 1]) if i + 1 < len(rounds) else None
        _softmax_update(m_ref, acc_ref, [p[0] for p in probs], sc, [p[2] for p in probs])
        sc = sc_next

    outs = []
    for g in range(G):
        acc = acc_ref[g]
        o_slc = acc[:HEAD_DIM] / acc[HEAD_DIM:HEAD_DIM + 1]
        acc = acc_ref[G + g]
        o_win = acc[:HEAD_DIM] / acc[HEAD_DIM:HEAD_DIM + 1]
        for r in range(R):
            h = g * R + r
            cols = slice(r * tq, (r + 1) * tq)
            outs.append(gates[3 * h:3 * h + 1] * o_cmp[g][:, cols] + gates[3 * h + 1:3 * h + 2] * o_slc[:, cols]
                        + gates[3 * h + 2:3 * h + 3] * o_win[:, cols])
    o_ref[...] = jnp.concatenate(outs, axis=0).T.astype(o_ref.dtype)


def _attention(q, gn, kc, vct, ksx, vst, kwn, vwt, bias_c, bias_near, win_mask, wov, batch, seq):
    n = batch * seq
    tq = Q_TILE
    nq = seq // tq
    ncmp = kc.shape[2]
    row = lambda b, i: (b * nq + i, 0)
    qtile = lambda b, i: (b * nq + i, 0, 0)
    whole = lambda b, i: (0, b, 0)
    tiles = lambda b, i: (0, b, 0, 0)
    const = lambda a: pl.BlockSpec(a.shape, lambda b, i: (0,) * a.ndim, pipeline_mode=pl.Buffered(1))
    return pl.pallas_call(
        _attn_kernel,
        grid=(batch, nq),
        in_specs=[pl.BlockSpec((1, NSA_WIDTH, tq), qtile),
                  pl.BlockSpec((1, LANES, tq), qtile),
                  pl.BlockSpec((1, NSA_KV_GROUPS, ncmp, HEAD_DIM), lambda b, i: (b, 0, 0, 0)),
                  pl.BlockSpec((1, NSA_KV_GROUPS, HEAD_DIM, ncmp), lambda b, i: (b, 0, 0, 0)),
                  pl.BlockSpec((2, seq, LANES), whole),
                  pl.BlockSpec((2, seq // K_TILE, VT_ROWS, K_TILE), tiles),
                  pl.BlockSpec((2, seq, LANES), whole),
                  pl.BlockSpec((2, seq // K_TILE, VT_ROWS, K_TILE), tiles),
                  const(bias_c),
                  const(bias_near), const(win_mask), const(wov)],
        out_specs=pl.BlockSpec((tq, NSA_WIDTH), row),
        out_shape=jax.ShapeDtypeStruct((n, NSA_WIDTH), BF16),
        scratch_shapes=[pltpu.VMEM((2 * NSA_KV_GROUPS, 1, NSA_GROUP * tq), F32),
                        pltpu.VMEM((2 * NSA_KV_GROUPS, VT_ROWS, NSA_GROUP * tq), F32),
                        pltpu.VMEM((NSA_KV_GROUPS, K_TILE, NSA_GROUP * tq), F32),
                        pltpu.VMEM((NSA_KV_GROUPS, K_TILE, NSA_GROUP * tq), F32),
                        pltpu.VMEM((NSA_KV_GROUPS, seq // SLC_BLOCK, tq), F32)],
        compiler_params=_params("parallel", "parallel"),
        name="nsa_attention",
    )(q, gn, kc, vct, ksx, vst, kwn, vwt, bias_c, bias_near, win_mask, wov)


def _split_bf16(z, parts):
    out = []
    for _ in range(parts - 1):
        hi = z.astype(BF16)
        out.append(hi)
        z = z - hi.astype(F32)
    return out + [z.astype(BF16)]


def _head_sums(z, ones):
    width = ones.shape[0]
    nb = z.shape[1] // width
    rows = z.shape[0]
    zb = z.astype(BF16)
    stacked = jnp.concatenate([zb[:, m * width:(m + 1) * width] for m in range(nb)], axis=0)
    sums = jnp.dot(stacked, ones, preferred_element_type=F32)
    return jnp.concatenate([sums[m * rows:(m + 1) * rows] for m in range(nb)], axis=1)


def _softplus(z):
    return jnp.maximum(z, 0.0) + jnp.log(1.0 + jnp.exp(-jnp.abs(z)))


def _rwkv_chunk(x, prev, st, mu_ref, w0_ref, w2_ref, a0_ref, a2_ref, g2_ref, kk_ref, ka_ref, rk_ref,
                lng_ref, lnb_ref, ones_ref):
    L = x.shape[0]
    W = RWKV_WIDTH
    N = HEAD_DIM
    row_id = lax.broadcasted_iota(jnp.int32, x.shape, 0)
    shifted = jnp.where(row_id == 0, prev, pltpu.roll(x, 1, axis=0))
    xl = x + (shifted - x) * mu_ref[...]
    yield None
    r = xl[:, 0:W]
    k = xl[:, W:2 * W]
    v = xl[:, 2 * W:3 * W]
    xw = xl[:, 3 * W:3 * W + LORA_W]
    xa = xl[:, 3 * W + LORA_W:3 * W + LORA_W + LORA_A]
    xg = xl[:, 3 * W + LORA_W + LORA_A:]
    w = -_softplus(-(w0_ref[...] + _mm(jnp.tanh(xw), w2_ref[...]))) - 0.5
    ld = -jnp.exp(w)
    a = jax.nn.sigmoid(a0_ref[...] + _mm(xa, a2_ref[...]))
    gate = _mm(jax.nn.sigmoid(xg), g2_ref[...])
    kkv = k * kk_ref[...]
    k2 = k * (1.0 + (a - 1.0) * ka_ref[...])
    yield None

    ti = lax.broadcasted_iota(jnp.int32, (L, L), 0)
    si = lax.broadcasted_iota(jnp.int32, (L, L), 1)
    incl = si <= ti
    strict = si < ti
    tri = jnp.where(incl, 1.0, 0.0).astype(BF16)
    cl3 = jnp.dot(tri, jnp.concatenate(_split_bf16(ld, 3), axis=1), preferred_element_type=F32)
    cl = cl3[:, :W] + (cl3[:, W:2 * W] + cl3[:, 2 * W:])
    cl_end = cl[L - 1:L, :]
    yield None
    e_pos = jnp.exp(cl)
    e_neg = jnp.exp(-cl)
    e_prev = jnp.exp(cl - ld)
    e_end = jnp.exp(cl_end - cl)
    eye = jnp.where(ti == si, 1.0, 0.0)
    yield None

    hsum = lambda z: _head_sums(z, ones_ref[...])

    kk_n = kkv * lax.rsqrt(jnp.maximum(hsum(kkv * kkv), 1e-24))
    bv = kk_n * a
    yield None
    a_t = (-kk_n * e_prev).astype(BF16)
    b_t = (bv * e_neg).astype(BF16)
    k_t = (k2 * e_neg).astype(BF16)
    r_t = (r * e_pos).astype(BF16)
    yield None
    v_b = v.astype(BF16)
    k_e = k2 * e_end
    b_e = bv * e_end

    yield "elementwise done"

    P = range(W // LANES)
    ps = [slice(m * LANES, (m + 1) * LANES) for m in P]
    first = lax.broadcasted_iota(jnp.int32, (1, LANES), 1) < N
    same_head = (lax.broadcasted_iota(jnp.int32, (LANES, LANES), 0) // N
                 == lax.broadcasted_iota(jnp.int32, (LANES, LANES), 1) // N)

    def diag_rows(x):
        zero = jnp.zeros_like(x)
        return jnp.concatenate([jnp.where(first, x, zero), jnp.where(first, zero, x)], axis=0)

    def diag_blocks(x):
        zero = jnp.zeros_like(x[:, :LANES])
        return jnp.concatenate([jnp.concatenate([x[:, :LANES], zero], axis=1),
                                jnp.concatenate([zero, x[:, LANES:]], axis=1)], axis=0)

    ke_t = [k_e[:, s].T for s in ps]
    be_t = [b_e[:, s].T for s in ps]
    p_end = [e_pos[:, s].T[:, L - 1:L] for s in ps]

    lhs = [jnp.concatenate([a_t[:, s], r_t[:, s]], axis=0) for s in ps]
    rhs = [jnp.concatenate([k_t[:, s], b_t[:, s]], axis=0) for s in ps]
    zero_b = jnp.zeros_like(lhs[0])
    aa = [[lax.dot_general(jnp.where(first, x, zero_b) if j == 0 else jnp.where(first, zero_b, x), y, _NT,
                           preferred_element_type=F32) for j in range(2)] for x, y in zip(lhs, rhs)]
    cat2 = lambda f: [jnp.concatenate([f(pair[0]), f(pair[1])], axis=1) for pair in aa]
    a_ak = cat2(lambda x: jnp.where(strict, x[:L, :L], 0.0))
    a_ab = cat2(lambda x: jnp.where(strict, x[:L, L:], 0.0))
    a_rk = cat2(lambda x: jnp.where(incl, x[L:, :L], 0.0))
    a_rb = cat2(lambda x: jnp.where(incl, x[L:, L:], 0.0))
    yield None

    eye2 = jnp.concatenate([eye, eye], axis=1)
    tinv = [eye2 + x for x in a_ab]
    pw = [_mm(x, diag_blocks(x)) for x in a_ab]
    yield None
    span = 2
    while 2 * span < L:
        both = [_mm(jnp.concatenate([t, p], axis=0), diag_blocks(p)) for t, p in zip(tinv, pw)]
        tinv = [t + x[:L] for t, x in zip(tinv, both)]
        pw = [x[L:] for x in both]
        yield None
        span *= 2
    tinv = [t + _mm(t, diag_blocks(p)) for t, p in zip(tinv, pw)]
    yield None

    v_d = [diag_rows(v_b[:, s]) for s in ps]
    av = [_mm(jnp.concatenate([a_ak[m], a_rk[m]], axis=0), v_d[m]) for m in P]
    tw = [_mm(tinv[m], jnp.concatenate([diag_rows(a_t[:, ps[m]]), diag_rows(av[m][:L].astype(BF16))], axis=1))
          for m in P]
    kv_loc = [jnp.where(same_head, _mm(ke_t[m], v_b[:, ps[m]]), 0.0) for m in P]
    yield None

    ws = [_mm(jnp.concatenate([tw[m][:, :LANES].astype(BF16), r_t[:, ps[m]]], axis=0), st[m]) for m in P]
    u = [ws[m][:L] + tw[m][:, LANES:] for m in P]
    yield None
    y = [ws[m][L:] + av[m][L:] + _mm(a_rb[m], diag_rows(u[m])) for m in P]
    st_new = [p_end[m] * st[m] + kv_loc[m] + jnp.where(same_head, _mm(be_t[m], u[m]), 0.0) for m in P]
    yield None

    y = jnp.concatenate(y, axis=1)
    d = y - hsum(y) * (1.0 / N)
    yn = d * lax.rsqrt(hsum(d * d) * (1.0 / N) + GN_EPS) * lng_ref[...] + lnb_ref[...]
    bonus = hsum(r * k2 * rk_ref[...])
    yield (yn + bonus * v) * gate, st_new


def _rwkv_kernel(x_ref, xp_ref, *refs):
    *param_refs, o_ref, st_ref = refs
    c = pl.program_id(1)

    @pl.when(c == 0)
    def _():
        st_ref[...] = jnp.zeros(st_ref.shape, F32)

    chunks = []
    for i in range(x_ref.shape[0]):
        prev = jnp.where(c > 0, xp_ref[i, xp_ref.shape[1] - 1:, :], 0.0)
        st = [st_ref[i, m] for m in range(st_ref.shape[1])]
        chunks.append(_rwkv_chunk(x_ref[i], prev, st, *param_refs))
    def advance(gen, until_result):
        item = next(gen)
        return (isinstance(item, tuple), item) if until_result else (item == "elementwise done", item)

    while not advance(chunks[0], False)[0]:
        pass
    outs, states = [], []
    for i, chunk in enumerate(chunks):
        following = chunks[i + 1] if i + 1 < len(chunks) else None
        result = None
        while result is None or following is not None:
            if result is None:
                done, item = advance(chunk, True)
                result = item if done else None
            if following is not None and advance(following, False)[0]:
                following = None
        outs.append(result[0].astype(o_ref.dtype))
        states.append(jnp.stack(result[1]))
    o_ref[...] = jnp.stack(outs)
    st_ref[...] = jnp.stack(states)


def _rwkv(rw, vecs, w2, a2, g2, batch, seq):
    width = rw.shape[1]
    L = min(CHUNK, seq)
    nc = seq // L
    nb = RWKV_SEQS if batch % RWKV_SEQS == 0 else 1
    sub = 8
    rw3 = rw.reshape(batch, seq, width)
    vec = lambda wd: pl.BlockSpec((1, wd), lambda b, c: (0, 0))
    mat = lambda m: pl.BlockSpec(m.shape, lambda b, c: (0, 0))
    mu, w0, a0, kk, ka, rk, lng, lnb = vecs
    head = np.arange(2 * LANES) // HEAD_DIM
    ones = jnp.asarray(head[:, None] == head[None, :], BF16)
    out = pl.pallas_call(
        _rwkv_kernel,
        grid=(batch // nb, nc),
        in_specs=[pl.BlockSpec((nb, L, width), lambda b, c: (b, c, 0)),
                  pl.BlockSpec((nb, sub, width), lambda b, c: (b, jnp.maximum(c * (L // sub) - 1, 0), 0)),
                  vec(width), vec(RWKV_WIDTH), mat(w2), vec(RWKV_WIDTH), mat(a2), mat(g2),
                  vec(RWKV_WIDTH), vec(RWKV_WIDTH), vec(RWKV_WIDTH), vec(RWKV_WIDTH), vec(RWKV_WIDTH), mat(ones)],
        out_specs=pl.BlockSpec((nb, L, RWKV_WIDTH), lambda b, c: (b, c, 0)),
        out_shape=jax.ShapeDtypeStruct((batch, seq, RWKV_WIDTH), BF16),
        scratch_shapes=[pltpu.VMEM((nb, RWKV_WIDTH // LANES, LANES, LANES), F32)],
        compiler_params=_params("parallel", "arbitrary"),
        name="rwkv7",
    )(rw3, rw3, mu, w0, w2, a0, a2, g2, kk, ka, rk, lng, lnb, ones)
    return out


def _merge_kernel(x_ref, oa_ref, ob_ref, gab_ref, wpa_ref, wpb_ref, wo_ref, o_ref):
    d = x_ref.shape[1]
    pa = jnp.dot(oa_ref[...], wpa_ref[...], preferred_element_type=F32)
    pb = jnp.dot(ob_ref[0], wpb_ref[...], preferred_element_type=F32)
    gab = gab_ref[...].astype(F32)
    merged = jax.nn.sigmoid(gab[:, :d]) * pa + jax.nn.sigmoid(gab[:, d:]) * pb
    o_ref[...] = x_ref[...] + _mm(merged, wo_ref[...])


def _merge(x2, oa, ob, gab, wpa, wpb, wo, tm):
    n, d = x2.shape
    per_seq = ob.shape[1] // tm
    row = lambda wd: pl.BlockSpec((tm, wd), lambda i: (i, 0))
    mat = lambda m: pl.BlockSpec(m.shape, lambda i: (0, 0))
    ob_spec = pl.BlockSpec((1, tm, ob.shape[2]), lambda i: (i // per_seq, i % per_seq, 0))
    return pl.pallas_call(
        _merge_kernel,
        grid=(n // tm,),
        in_specs=[row(d), row(oa.shape[1]), ob_spec, row(gab.shape[1]), mat(wpa), mat(wpb), mat(wo)],
        out_specs=row(d),
        out_shape=jax.ShapeDtypeStruct((n, d), F32),
        compiler_params=_params("parallel"),
        name="merge",
    )(x2, oa, ob, gab, wpa, wpb, wo)


def _ffn_kernel(seq, x_ref, xh_ref, g_ref, wv_ref, wg_ref, cwv_ref, cwg_ref, cbv_ref, cbg_ref, wd_ref, o_ref,
                h_ref, uv_ref, ug_ref):
    tm = x_ref.shape[0]
    i = pl.program_id(0)
    f = pl.program_id(1)

    @pl.when(f == 0)
    def _():
        first = lax.rem(i * tm, seq) == 0
        halo = jnp.where(first, 0.0, _rms(xh_ref[...], g_ref[...]))
        h_ref[0:HALO, :] = halo.astype(BF16)
        h_ref[HALO:, :] = _rms(x_ref[...], g_ref[...]).astype(BF16)

    h = h_ref[...]
    uv_ref[...] = jnp.dot(h, wv_ref[...], preferred_element_type=F32)
    ug_ref[...] = jnp.dot(h, wg_ref[...], preferred_element_type=F32)

    def conv(u_ref, cw_ref, cb_ref):
        acc = cb_ref[...] + cw_ref[0:1, :] * u_ref[pl.ds(HALO - 2, tm), :]
        acc = acc + cw_ref[1:2, :] * u_ref[pl.ds(HALO - 1, tm), :]
        return acc + cw_ref[2:3, :] * u_ref[pl.ds(HALO, tm), :]

    val = conv(uv_ref, cwv_ref, cbv_ref)
    gt = conv(ug_ref, cwg_ref, cbg_ref)
    y = _mm(gt * jax.nn.sigmoid(gt) * val, wd_ref[...])

    @pl.when(f == 0)
    def _():
        o_ref[...] = x_ref[...] + y

    @pl.when(f > 0)
    def _():
        o_ref[...] = o_ref[...] + y


def _ffn(x1, g, w_up, conv_w, conv_b, w_down, seq, tm, tf):
    n, d = x1.shape
    dff = w_down.shape[0]
    nf = dff // tf
    return pl.pallas_call(
        functools.partial(_ffn_kernel, seq),
        grid=(n // tm, nf),
        in_specs=[pl.BlockSpec((tm, d), lambda i, f: (i, 0)),
                  pl.BlockSpec((HALO, d), lambda i, f: (jnp.maximum(i * (tm // HALO) - 1, 0), 0)),
                  pl.BlockSpec((1, d), lambda i, f: (0, 0)),
                  pl.BlockSpec((d, tf), lambda i, f: (0, f)),
                  pl.BlockSpec((d, tf), lambda i, f: (0, nf + f)),
                  pl.BlockSpec((CONV_WIDTH, tf), lambda i, f: (0, f)),
                  pl.BlockSpec((CONV_WIDTH, tf), lambda i, f: (0, nf + f)),
                  pl.BlockSpec((1, tf), lambda i, f: (0, f)),
                  pl.BlockSpec((1, tf), lambda i, f: (0, nf + f)),
                  pl.BlockSpec((tf, d), lambda i, f: (f, 0))],
        out_specs=pl.BlockSpec((tm, d), lambda i, f: (i, 0)),
        out_shape=jax.ShapeDtypeStruct((n, d), F32),
        scratch_shapes=[pltpu.VMEM((tm + HALO, d), BF16),
                        pltpu.VMEM((tm + HALO, tf), F32),
                        pltpu.VMEM((tm + HALO, tf), F32)],
        compiler_params=_params("parallel", "arbitrary"),
        name="convffn",
    )(x1, x1, g, w_up, w_up, conv_w, conv_w, conv_b, conv_b, w_down)


def _t5_bucket(dist):
    n = np.maximum(dist, 0)
    max_exact = REL_BUCKETS // 2
    ratio = np.log(np.maximum(n, 1).astype(np.float32) / max_exact) / math.log(REL_MAX_DIST / max_exact)
    large = np.minimum(max_exact + (ratio * (REL_BUCKETS - max_exact)).astype(np.int32), REL_BUCKETS - 1)
    return np.where(n < max_exact, n, large).astype(np.int32)


def _bias_tables(rel_bias, seq, ncmp):
    tq, tk = Q_TILE, K_TILE
    G, R = NSA_KV_GROUPS, NSA_GROUP
    nq = seq // tq
    step = tq // CMP_STRIDE
    tab = rel_bias.astype(F32)

    off = (nq - 1) * step
    d_c = np.arange(tq)[None, :] - (np.arange(ncmp + off)[:, None] - off) * CMP_STRIDE - (CMP_BLOCK - 1)
    n_b = -(-REL_MAX_DIST // tk)
    d0 = np.arange(tq)[None, :] - np.arange(tk)[:, None]
    d_n = np.stack([d0 - o * tk for o in range(-n_b, tq // tk)])

    buckets = np.concatenate([_t5_bucket(d_c).reshape(-1), _t5_bucket(d_n).reshape(-1)])
    onehot = (jnp.arange(REL_BUCKETS, dtype=jnp.int32)[:, None] == jnp.asarray(buckets)[None, :]).astype(F32)
    vals = jnp.dot(tab.T * LOG2E, onehot, precision=lax.Precision.HIGHEST, preferred_element_type=F32)
    base = jnp.where(jnp.asarray(d_c >= 0), vals[:, :d_c.size].reshape(G, R, ncmp + off, tq), NEG_INF)

    far = (tab[REL_BUCKETS - 1] * LOG2E).reshape(G, R, 1, 1, 1)
    near = vals[:, d_c.size:].reshape(G, R, d_n.shape[0], tk, tq) - far
    bias_near = jnp.where(jnp.asarray(d_n >= 0), near, NEG_INF)
    win = np.stack([np.where(d0 - o * tk < WINDOW, 0.0, NEG_INF) for o in _win_masked(tq, tk)])
    return base, bias_near, jnp.asarray(win.astype(np.float32))


def _overlap_matrix(ncmp_pad, n_slc, n_cmp):
    m = np.zeros((ncmp_pad, n_slc), np.float32)
    ratio = SLC_BLOCK // CMP_STRIDE
    for j in range(n_slc):
        for o, wgt in enumerate(OVERLAP_W):
            cidx = ratio * j + o - (CMP_BLOCK // CMP_STRIDE - 1)
            if 0 <= cidx < n_cmp:
                m[cidx, j] += wgt
    return jnp.asarray(m.T)


def _layer(x, attn_norm_g, w_in, rel_bias, q_norm_g, k_norm_g, cmp_pe_k, cmp_w1_k, cmp_w2_k,
           cmp_pe_v, cmp_w1_v, cmp_w2_v, rwkv_mu, rwkv_w0, rwkv_w2, rwkv_a0, rwkv_a2, rwkv_g2,
           rwkv_k_k, rwkv_k_a, rwkv_r_k, rwkv_ln_g, rwkv_ln_b, w_proj_a, w_proj_b, w_out,
           ffn_norm_g, w_up, conv_w, conv_b, w_down):
    batch, seq, d = x.shape
    n = batch * seq
    assert seq % Q_TILE == 0 and Q_TILE % K_TILE == 0 and WINDOW % K_TILE == 0
    assert seq % CMP_STRIDE == 0 and seq % FFN_ROWS == 0 and seq % PROJ_ROWS == 0 and PROJ_ROWS % K_TILE == 0 and PROJ_ROWS % Q_TILE == 0
    assert seq // SLC_BLOCK <= LANES - HEAD_DIM
    assert NSA_KV_WIDTH == LANES and CMP_PACK * HEAD_DIM == 2 * LANES
    x2 = x.reshape(n, d)
    row = lambda a: a.reshape(1, -1).astype(F32)

    o_kvc = NSA_WIDTH
    o_kv = o_kvc + 2 * NSA_KV_WIDTH
    o_gn = o_kv + 4 * NSA_KV_WIDTH
    o_rw = o_gn + 3 * NSA_HEADS
    o_gab = o_rw + RWKV_IN_WIDTH
    w16 = w_in.astype(BF16)
    gn_pad = jnp.zeros((d, LANES - 3 * NSA_HEADS), BF16)
    ws = (w16[:, o_rw:o_gab], w16[:, :o_kvc], w16[:, o_kvc:o_kv], w16[:, o_kv:o_gn], w16[:, o_gab:],
          jnp.concatenate([w16[:, o_gn:o_rw], gn_pad], axis=1))
    gk = row(k_norm_g)
    rw, q, kv_cmp, gab, gates, ksx, kwn, vst, vwt = _inproj(
        x2, row(attn_norm_g), ws, jnp.concatenate([gk, gk], axis=1), q_norm_g.reshape(-1, 1).astype(F32), seq,
        tm=PROJ_ROWS)

    nchunk = seq // CMP_STRIDE
    n_cmp = nchunk - CMP_BLOCK // CMP_STRIDE + 1
    pe = jnp.stack([cmp_pe_k, cmp_pe_v]).astype(F32)
    pe = jnp.concatenate([pe] * NSA_KV_GROUPS, axis=-1)
    w1 = jnp.stack([cmp_w1_k, cmp_w1_v]).astype(BF16)
    w1 = w1.reshape(2, CMP_BLOCK // CMP_PACK, CMP_PACK * HEAD_DIM, CMP_HIDDEN)
    w2 = jnp.stack([cmp_w2_k, cmp_w2_v]).astype(BF16)
    kc, vc = _compress(kv_cmp, pe, w1, w2, gk, batch, seq)

    bias_c, bias_near, win_mask = _bias_tables(rel_bias, seq, nchunk)
    wov = _overlap_matrix(nchunk, seq // SLC_BLOCK, n_cmp)
    o_a = _attention(q, gates, kc, jnp.swapaxes(vc, -1, -2), ksx, vst, kwn, vwt, bias_c, bias_near, win_mask,
                     wov, batch, seq)

    vecs = tuple(row(a) for a in (rwkv_mu, rwkv_w0, rwkv_a0, rwkv_k_k, rwkv_k_a, rwkv_r_k, rwkv_ln_g, rwkv_ln_b))
    o_b = _rwkv(rw, vecs, rwkv_w2.astype(BF16), rwkv_a2.astype(BF16), rwkv_g2.astype(BF16), batch, seq)

    x1 = _merge(x2, o_a, o_b, gab, w_proj_a.astype(BF16), w_proj_b.astype(BF16), w_out.astype(BF16), tm=FFN_ROWS)
    dff = w_down.shape[0]
    out = _ffn(x1, row(ffn_norm_g), w_up.astype(BF16), conv_w.astype(F32), row(conv_b), w_down.astype(BF16),
               seq, tm=FFN_ROWS, tf=dff // FFN_COL_SPLITS)
    return out.reshape(batch, seq, d)


def kernel(x, attn_norm_g, w_in, rel_bias, q_norm_g, k_norm_g, cmp_pe_k, cmp_w1_k, cmp_w2_k, cmp_pe_v, cmp_w1_v,
           cmp_w2_v, rwkv_mu, rwkv_w0, rwkv_w2, rwkv_a0, rwkv_a2, rwkv_g2, rwkv_k_k, rwkv_k_a, rwkv_r_k,
           rwkv_ln_g, rwkv_ln_b, w_proj_a, w_proj_b, w_out, ffn_norm_g, w_up, conv_w, conv_b, w_down):
    per_layer = (attn_norm_g, w_in, None, q_norm_g, k_norm_g, cmp_pe_k, cmp_w1_k, cmp_w2_k, cmp_pe_v, cmp_w1_v,
                 cmp_w2_v, rwkv_mu, rwkv_w0, rwkv_w2, rwkv_a0, rwkv_a2, rwkv_g2, rwkv_k_k, rwkv_k_a, rwkv_r_k,
                 rwkv_ln_g, rwkv_ln_b, w_proj_a, w_proj_b, w_out, ffn_norm_g, w_up, conv_w, conv_b, w_down)
    for l in range(attn_norm_g.shape[0]):
        args = [rel_bias if p is None else p[l] for p in per_layer]
        x = _layer(x, *args)
    return x
```

```python
import functools
import math

import numpy as np
import jax
import jax.numpy as jnp
from jax import lax
from jax.experimental import pallas as pl
from jax.experimental.pallas import tpu as pltpu

F32 = jnp.float32
BF16 = jnp.bfloat16

HEAD_DIM = 64
NSA_HEADS = 8
NSA_KV_GROUPS = 2
NSA_GROUP = NSA_HEADS // NSA_KV_GROUPS
NSA_WIDTH = NSA_HEADS * HEAD_DIM
NSA_KV_WIDTH = NSA_KV_GROUPS * HEAD_DIM
CMP_BLOCK = 32
CMP_STRIDE = 16
CMP_HIDDEN = 256
SLC_BLOCK = 64
SLC_TOPK = 16
OVERLAP_W = (1, 2, 2, 2, 1)
WINDOW = 512
REL_BUCKETS = 32
REL_MAX_DIST = 128
RWKV_HEADS = 8
RWKV_WIDTH = RWKV_HEADS * HEAD_DIM
LORA_W = 64
LORA_A = 64
LORA_G = 128
RWKV_IN_WIDTH = 3 * RWKV_WIDTH + LORA_W + LORA_A + LORA_G
GN_EPS = 64e-5
CONV_WIDTH = 3
RMS_EPS = 1e-6
NEG_INF = -1e30
FORCE = 1e9
LOG2E = math.log2(math.e)
M_FLOOR = -1e20

LANES = 128
VMEM_LIMIT = 56 * 1024 * 1024

Q_TILE = 256
K_TILE = 256
VT_ROWS = 80
CHUNK = 128
RWKV_SEQS = 8
CMP_PACK = 4
HALO = 16
PROJ_ROWS = 512
FFN_ROWS = 1024
FFN_COL_SPLITS = 2

_NT = (((1,), (1,)), ((), ()))


def _params(*sem):
    return pltpu.CompilerParams(dimension_semantics=sem, vmem_limit_bytes=VMEM_LIMIT)


def _mm(a, b):
    return jnp.dot(a.astype(BF16), b.astype(BF16), preferred_element_type=F32)


def _mm_f32(a, b):
    return jnp.dot(a, b, preferred_element_type=F32, precision=lax.Precision.HIGHEST)


def _rms(x, g):
    return x * lax.rsqrt(jnp.mean(x * x, axis=-1, keepdims=True) + RMS_EPS) * g


def _kv_prep(kv, gk, row0, ksx_ref, kwn_ref, vs_ref, vw_ref):
    tm = kv.shape[0]
    lane = lax.broadcasted_iota(jnp.int32, (tm, LANES), 1)
    lo = lane < HEAD_DIM

    def norm2(x):
        x2 = x * x
        s_lo = jnp.sum(jnp.where(lo, x2, 0.0), axis=-1, keepdims=True)
        s_hi = jnp.sum(jnp.where(lo, 0.0, x2), axis=-1, keepdims=True)
        ms = jnp.where(lo, s_lo, s_hi) * (1.0 / HEAD_DIM)
        return x * lax.rsqrt(ms + RMS_EPS) * gk

    ks = norm2(kv[:, 0:LANES])
    kw = norm2(kv[:, 2 * LANES:3 * LANES])
    row = row0 + lax.broadcasted_iota(jnp.int32, (tm, LANES), 0)
    onehot = jnp.where(lane - HEAD_DIM == row // SLC_BLOCK, 1.0, 0.0)
    ksx_ref[0] = jnp.where(lo, ks, onehot).astype(BF16)
    ksx_ref[1] = jnp.where(lo, pltpu.roll(ks, HEAD_DIM, axis=1), onehot).astype(BF16)
    flag_col = jnp.where(lane == HEAD_DIM, 1.0, 0.0)
    kwn_ref[0] = jnp.where(lo, kw, flag_col).astype(BF16)
    kwn_ref[1] = jnp.where(lo, pltpu.roll(kw, HEAD_DIM, axis=1), flag_col).astype(BF16)
    tail = jnp.where(lax.broadcasted_iota(jnp.int32, (VT_ROWS - HEAD_DIM, K_TILE), 0) == 0, 1.0, 0.0)
    for col, ref in ((1, vs_ref), (3, vw_ref)):
        for t in range(tm // K_TILE):
            vt = kv[t * K_TILE:(t + 1) * K_TILE, col * LANES:(col + 1) * LANES].T
            for g in range(NSA_KV_GROUPS):
                ref[g, t] = jnp.concatenate([vt[g * HEAD_DIM:(g + 1) * HEAD_DIM], tail], axis=0).astype(BF16)


def _inproj_kernel(seq, x_ref, g_ref, w_rw, w_q, w_kvc, w_kv, w_gab, w_gn, gk_ref, gq_ref, rw_ref, q_ref, kvc_ref,
                   gab_ref, gn_ref, ksx_ref, kwn_ref, vs_ref, vw_ref):
    tm = x_ref.shape[0]
    h = _rms(x_ref[...], g_ref[...]).astype(BF16)
    for w_ref, ref in ((w_rw, rw_ref), (w_q, q_ref), (w_kvc, kvc_ref), (w_kv, None), (w_gab, gab_ref),
                       (w_gn, gn_ref)):
        y = jnp.dot(h, w_ref[...], preferred_element_type=F32)
        if ref is None:
            _kv_prep(y, gk_ref[...], lax.rem(pl.program_id(0) * tm, seq), ksx_ref, kwn_ref, vs_ref, vw_ref)
        elif ref is q_ref:
            for t in range(tm // Q_TILE):
                yt = y[t * Q_TILE:(t + 1) * Q_TILE].T
                heads = []
                for hd in range(NSA_HEADS):
                    x = yt[hd * HEAD_DIM:(hd + 1) * HEAD_DIM]
                    x = x * lax.rsqrt(jnp.mean(x * x, axis=0, keepdims=True) + RMS_EPS)
                    heads.append(x * gq_ref[...] * (HEAD_DIM ** -0.5 * LOG2E))
                q_ref[t] = jnp.concatenate(heads, axis=0).astype(q_ref.dtype)
        elif ref is gn_ref:
            for t in range(tm // Q_TILE):
                gn_ref[t] = jax.nn.sigmoid(y[t * Q_TILE:(t + 1) * Q_TILE].T)
        else:
            ref[...] = y.astype(ref.dtype)


def _inproj(x2, g, ws, gk2, gq, seq, tm):
    n, d = x2.shape
    rows = lambda wd, dt: (pl.BlockSpec((tm, wd), lambda i: (i, 0)), jax.ShapeDtypeStruct((n, wd), dt))
    tiles = lambda r, c, dt: (pl.BlockSpec((tm // c, r, c), lambda i: (i, 0, 0)),
                              jax.ShapeDtypeStruct((n // c, r, c), dt))
    keys = (pl.BlockSpec((2, tm, LANES), lambda i: (0, i, 0)), jax.ShapeDtypeStruct((2, n, LANES), BF16))
    vts = (pl.BlockSpec((2, tm // K_TILE, VT_ROWS, K_TILE), lambda i: (0, i, 0, 0)),
           jax.ShapeDtypeStruct((2, n // K_TILE, VT_ROWS, K_TILE), BF16))
    outs = [rows(RWKV_IN_WIDTH, F32), tiles(NSA_WIDTH, Q_TILE, BF16), rows(2 * NSA_KV_WIDTH, F32),
            rows(2 * d, BF16), tiles(LANES, Q_TILE, F32), keys, keys, vts, vts]
    return pl.pallas_call(
        functools.partial(_inproj_kernel, seq),
        grid=(n // tm,),
        in_specs=[pl.BlockSpec((tm, d), lambda i: (i, 0)),
                  pl.BlockSpec((1, d), lambda i: (0, 0))]
        + [pl.BlockSpec(w.shape, lambda i: (0, 0), pipeline_mode=pl.Buffered(1)) for w in ws]
        + [pl.BlockSpec((1, LANES), lambda i: (0, 0)), pl.BlockSpec((HEAD_DIM, 1), lambda i: (0, 0))],
        out_specs=[o[0] for o in outs],
        out_shape=[o[1] for o in outs],
        compiler_params=_params("parallel"),
        name="inproj",
    )(x2, g, *ws, gk2, gq)


def _compress_kernel(k_ref, v_ref, pe_ref, w1_ref, w2_ref, gk_ref, kc_ref, vc_ref):
    nchunk = k_ref.shape[0] // CMP_STRIDE
    span = CMP_BLOCK // CMP_STRIDE
    G = NSA_KV_GROUPS
    lo = lax.broadcasted_iota(jnp.int32, (nchunk, LANES), 1) < HEAD_DIM
    acc = [[[jnp.zeros((nchunk, CMP_HIDDEN), F32) for _ in range(span)] for _ in range(G)] for _ in range(2)]
    for p0 in range(0, CMP_STRIDE, CMP_PACK):
        for s, ref in enumerate((k_ref, v_ref)):
            xs = [ref[pl.ds(p0 + i, nchunk, stride=CMP_STRIDE), :] for i in range(CMP_PACK)]
            for half in range(span):
                q0 = half * CMP_STRIDE + p0
                xq = [xs[i] + pe_ref[s, q0 + i:q0 + i + 1, :] for i in range(CMP_PACK)]
                for g in range(G):
                    pairs = [jnp.where(lo, xq[i], pltpu.roll(xq[i + 1], HEAD_DIM, axis=1)) if g == 0 else
                             jnp.where(lo, pltpu.roll(xq[i], HEAD_DIM, axis=1), xq[i + 1])
                             for i in range(0, CMP_PACK, 2)]
                    acc[s][g][half] = acc[s][g][half] + _mm(jnp.concatenate(pairs, axis=1),
                                                            w1_ref[s, q0 // CMP_PACK])
    for s, out_ref in ((0, kc_ref), (1, vc_ref)):
        for g in range(G):
            hid = acc[s][g][0]
            for half in range(1, span):
                hid = hid + pltpu.roll(acc[s][g][half], nchunk - half, axis=0)
            out = _mm(jax.nn.gelu(hid), w2_ref[s])
            out_ref[0, g] = _rms(out, gk_ref[...]) if s == 0 else out


def _compress(kv, pe, w1, w2, gk, batch, seq):
    nchunk = seq // CMP_STRIDE
    out_spec = lambda: pl.BlockSpec((1, NSA_KV_GROUPS, nchunk, HEAD_DIM), lambda b: (b, 0, 0, 0))
    out_shape = jax.ShapeDtypeStruct((batch, NSA_KV_GROUPS, nchunk, HEAD_DIM), F32)
    const = lambda a: pl.BlockSpec(a.shape, lambda b: (0,) * a.ndim)
    return pl.pallas_call(
        _compress_kernel,
        grid=(batch,),
        in_specs=[pl.BlockSpec((seq, NSA_KV_WIDTH), lambda b: (b, 0)), pl.BlockSpec((seq, NSA_KV_WIDTH), lambda b: (b, 1)),
                  const(pe), const(w1), const(w2), const(gk)],
        out_specs=[out_spec(), out_spec()],
        out_shape=[out_shape, out_shape],
        compiler_params=_params("parallel"),
        name="compress",
    )(kv, kv, pe, w1, w2, gk)


def _rank_select(scores, cur_t, n_live, rank_ref):
    nb, t = scores[0].shape
    sub = 8
    groups = range(nb // sub)
    rank_ref[...] = jnp.zeros(rank_ref.shape, F32)
    below = lax.broadcasted_iota(jnp.int32, (sub, t), 0)
    for gj in groups:
        @pl.when(gj * sub < n_live)
        def _():
            for n, x in enumerate(scores):
                xs = [x[g * sub:(g + 1) * sub] for g in groups]
                inc = [jnp.zeros((sub, t), F32) for _ in groups]
                for jl in range(sub):
                    j = gj * sub + jl
                    row = jnp.broadcast_to(x[j:j + 1, :], (sub, t))
                    for g in groups:
                        if g > gj:
                            one = jnp.where(row >= xs[g], 1.0, 0.0)
                        elif g < gj:
                            one = jnp.where(row > xs[g], 1.0, 0.0)
                        else:
                            one = jnp.where(below > jl, jnp.where(row >= xs[g], 1.0, 0.0),
                                            jnp.where(row > xs[g], 1.0, 0.0))
                        inc[g] = inc[g] + one
                rank_ref[n] = rank_ref[n] + jnp.concatenate(inc, axis=0)
    jj = lax.broadcasted_iota(jnp.int32, (nb, t), 0)
    live = jj <= cur_t
    return [jnp.where(live, jnp.where(rank_ref[n] < float(min(SLC_TOPK, nb)), 0.0, NEG_INF), NEG_INF)
            for n in range(len(scores))]


def _win_masked(tq, tk):
    return [o for o in range(-(WINDOW // tk), 0) if tq - 1 - o * tk >= WINDOW]


def _scores(ks, ws):
    return [jnp.dot(k, w, preferred_element_type=F32) for k, w in zip(ks, ws)]


def _softmax_update(m_ref, acc_ref, slots, s, vts):
    m_old = [m_ref[i] for i in slots]
    ncol = s[0].shape[1] // LANES
    m_cols = [[] for _ in slots]
    p_cols = [[] for _ in slots]
    for c in range(ncol):
        cols = slice(c * LANES, (c + 1) * LANES)
        for n, x in enumerate(s):
            xc = x[:, cols]
            mc = jnp.maximum(m_old[n][:, cols], jnp.max(xc, axis=0, keepdims=True))
            m_cols[n].append(mc)
            p_cols[n].append(jnp.exp2(xc - mc).astype(BF16))
    m_new = [jnp.concatenate(mc, axis=1) for mc in m_cols]
    pv = [jnp.dot(vt, jnp.concatenate(pc, axis=1), preferred_element_type=F32) for vt, pc in zip(vts, p_cols)]
    acc = [jnp.exp2(mo - mn) * acc_ref[i] + z for mo, mn, i, z in zip(m_old, m_new, slots, pv)]
    for i, mn, ac in zip(slots, m_new, acc):
        m_ref[i] = mn
        acc_ref[i] = ac


def _attn_kernel(q_ref, gn_ref, kc_ref, vct_ref, ksx_ref, vst_ref, kwn_ref, vwt_ref, bc_ref, bn_ref, wm_ref,
                 wov_ref, o_ref, m_ref, acc_ref, sa_ref, sb_ref, rank_ref):
    tq = q_ref.shape[2]
    tk = K_TILE
    nblk = wov_ref.shape[0]
    G, R = NSA_KV_GROUPS, NSA_GROUP
    qt = pl.program_id(1)
    n_q = tq // tk
    n_b = -(-REL_MAX_DIST // tk)
    j0 = qt * n_q
    gates = gn_ref[0]

    qn, qx, negs, o_cmp, scores = [], [], [], [], []
    ncmp = kc_ref.shape[2]
    step = tq // CMP_STRIDE
    bc_rows = pl.ds(pl.multiple_of(bc_ref.shape[2] - ncmp - qt * step, 8), ncmp)
    per_head = tq // LANES
    for g in range(G):
        qn.append(jnp.concatenate([q_ref[0, h * HEAD_DIM:(h + 1) * HEAD_DIM, :] for h in range(g * R, (g + 1) * R)],
                                  axis=1))

    for g in range(G):
        s = jnp.dot(kc_ref[0, g].astype(BF16), qn[g], preferred_element_type=F32)
        p_cols = []
        for c in range(R * tq // LANES):
            cols = slice(c * LANES, (c + 1) * LANES)
            t_cols = slice(c % per_head * LANES, (c % per_head + 1) * LANES)
            sc = s[:, cols] + bc_ref[g, c // per_head, bc_rows, t_cols]
            e = jnp.exp2(sc - jnp.maximum(jnp.max(sc, axis=0, keepdims=True), M_FLOOR))
            p_cols.append(e * (1.0 / jnp.maximum(jnp.sum(e, axis=0, keepdims=True), 1e-30)))
        o_cmp.append(jnp.dot(vct_ref[0, g].astype(BF16), jnp.concatenate(p_cols, axis=1).astype(BF16),
                             preferred_element_type=F32))
        psum = jnp.concatenate([sum(p_cols[r * per_head + t] for r in range(R)) for t in range(per_head)], axis=1)
        imp = _mm_f32(wov_ref[...], psum)

        blk = lax.broadcasted_iota(jnp.int32, imp.shape, 0)
        cur = (qt * tq + lax.broadcasted_iota(jnp.int32, imp.shape, 1)) // SLC_BLOCK
        forced = (blk == 0) | (blk == cur) | (blk == cur - 1)
        scores.append(jnp.where(forced, FORCE, jnp.where(blk <= cur, imp, -FORCE)))

    cur = (qt * tq + lax.broadcasted_iota(jnp.int32, (nblk, tq), 1)) // SLC_BLOCK
    for g, neg in enumerate(_rank_select(scores, cur, (qt + 1) * (tq // SLC_BLOCK), rank_ref)):
        if nblk < LANES - HEAD_DIM:
            neg = jnp.concatenate([neg, jnp.zeros((LANES - HEAD_DIM - nblk, tq), F32)], axis=0)
        neg = neg.astype(BF16)
        negs.append(neg)
        qx.append(jnp.concatenate([qn[g], jnp.concatenate([neg] * R, axis=1)], axis=0))

    m_ref[...] = jnp.full(m_ref.shape, M_FLOOR, F32)
    acc_ref[...] = jnp.zeros(acc_ref.shape, F32)

    def invalid_before(level):
        return jnp.where(qt >= level, 0.0, NEG_INF)

    def slc_queries(g, level):
        if level == 0:
            return qx[g]
        rows = jnp.minimum(negs[g], invalid_before(level).astype(BF16))
        return jnp.concatenate([qn[g], jnp.concatenate([rows] * R, axis=1)], axis=0)

    def win_queries(g, level):
        flag_row = lax.broadcasted_iota(jnp.int32, (LANES - HEAD_DIM, R * tq), 0) == 0
        extra = jnp.where(flag_row, invalid_before(level), 0.0).astype(BF16)
        return jnp.concatenate([qn[g], extra], axis=0)

    def slc_probs(o, bias):
        j = jnp.maximum(j0 + o, 0)
        sl = pl.ds(pl.multiple_of(j * tk, tk), tk)
        return [(g, ksx_ref[g, sl, :], vst_ref[g, j], slc_queries(g, max(-(o // n_q), 0)),
                 functools.partial(bias, g)) for g in range(G)]

    def win_probs(o, bias):
        j = jnp.maximum(j0 + o, 0)
        sl = pl.ds(pl.multiple_of(j * tk, tk), tk)
        return [(G + g, kwn_ref[g, sl, :], vwt_ref[g, j], win_queries(g, max(-(o // n_q), 0)),
                 None if bias is None else functools.partial(bias, g)) for g in range(G)]

    n_far = jnp.maximum(j0 - n_b, 0)
    groups = list(range(G))

    def far_scores(j, buf, gs=groups):
        sl = pl.ds(pl.multiple_of(j * tk, tk), tk)
        for g, x in zip(gs, _scores([ksx_ref[g, sl, :] for g in gs], [qx[g] for g in gs])):
            buf[g] = x

    def far_update(j, buf, gs=groups):
        _softmax_update(m_ref, acc_ref, gs, [buf[g] for g in gs], [vst_ref[g, j] for g in gs])

    far_scores(0, sa_ref)

    def far_body(i, carry):
        for g in groups:
            far_scores(2 * i + 1, sb_ref, [g])
            far_update(2 * i, sa_ref, [g])
        for g in groups:
            far_scores(2 * i + 2, sa_ref, [g])
            far_update(2 * i + 1, sb_ref, [g])
        return carry

    lax.fori_loop(0, n_far // 2, far_body, 0)

    @pl.when(n_far % 2 == 1)
    def _():
        far_update(n_far - 1, sa_ref)

    masked = _win_masked(tq, tk)
    rounds = []
    for o in range(-(WINDOW // tk), n_q):
        if o >= -n_b:
            near = lambda g, r, i=o + n_b: bn_ref[g, r, i]
            rounds.append(slc_probs(o, near) + win_probs(o, near))
        elif o in masked:
            rounds.append(win_probs(o, lambda g, r, i=masked.index(o): wm_ref[i]))
        else:
            rounds.append(win_probs(o, None))

    def round_scores(probs):
        sc = _scores([p[1] for p in probs], [p[3] for p in probs])
        return [x if p[4] is None else
                jnp.concatenate([x[:, r * tq:(r + 1) * tq] + p[4](r) for r in range(R)], axis=1)
                for x, p in zip(sc, probs)]

    sc = round_scores(rounds[0])
    for i, probs in enumerate(rounds):
        sc_next = round_scores(rounds[i + 1]) if i + 1 < len(rounds) else None
        _softmax_update(m_ref, acc_ref, [p[0] for p in probs], sc, [p[2] for p in probs])
        sc = sc_next

    outs = []
    for g in range(G):
        acc = acc_ref[g]
        o_slc = acc[:HEAD_DIM] / acc[HEAD_DIM:HEAD_DIM + 1]
        acc = acc_ref[G + g]
        o_win = acc[:HEAD_DIM] / acc[HEAD_DIM:HEAD_DIM + 1]
        for r in range(R):
            h = g * R + r
            cols = slice(r * tq, (r + 1) * tq)
            outs.append(gates[3 * h:3 * h + 1] * o_cmp[g][:, cols] + gates[3 * h + 1:3 * h + 2] * o_slc[:, cols]
                        + gates[3 * h + 2:3 * h + 3] * o_win[:, cols])
    o_ref[...] = jnp.concatenate(outs, axis=0).T.astype(o_ref.dtype)


def _attention(q, gn, kc, vct, ksx, vst, kwn, vwt, bias_c, bias_near, win_mask, wov, batch, seq):
    n = batch * seq
    tq = Q_TILE
    nq = seq // tq
    ncmp = kc.shape[2]
    row = lambda b, i: (b * nq + i, 0)
    qtile = lambda b, i: (b * nq + i, 0, 0)
    whole = lambda b, i: (0, b, 0)
    tiles = lambda b, i: (0, b, 0, 0)
    const = lambda a: pl.BlockSpec(a.shape, lambda b, i: (0,) * a.ndim, pipeline_mode=pl.Buffered(1))
    return pl.pallas_call(
        _attn_kernel,
        grid=(batch, nq),
        in_specs=[pl.BlockSpec((1, NSA_WIDTH, tq), qtile),
                  pl.BlockSpec((1, LANES, tq), qtile),
                  pl.BlockSpec((1, NSA_KV_GROUPS, ncmp, HEAD_DIM), lambda b, i: (b, 0, 0, 0)),
                  pl.BlockSpec((1, NSA_KV_GROUPS, HEAD_DIM, ncmp), lambda b, i: (b, 0, 0, 0)),
                  pl.BlockSpec((2, seq, LANES), whole),
                  pl.BlockSpec((2, seq // K_TILE, VT_ROWS, K_TILE), tiles),
                  pl.BlockSpec((2, seq, LANES), whole),
                  pl.BlockSpec((2, seq // K_TILE, VT_ROWS, K_TILE), tiles),
                  const(bias_c),
                  const(bias_near), const(win_mask), const(wov)],
        out_specs=pl.BlockSpec((tq, NSA_WIDTH), row),
        out_shape=jax.ShapeDtypeStruct((n, NSA_WIDTH), BF16),
        scratch_shapes=[pltpu.VMEM((2 * NSA_KV_GROUPS, 1, NSA_GROUP * tq), F32),
                        pltpu.VMEM((2 * NSA_KV_GROUPS, VT_ROWS, NSA_GROUP * tq), F32),
                        pltpu.VMEM((NSA_KV_GROUPS, K_TILE, NSA_GROUP * tq), F32),
                        pltpu.VMEM((NSA_KV_GROUPS, K_TILE, NSA_GROUP * tq), F32),
                        pltpu.VMEM((NSA_KV_GROUPS, seq // SLC_BLOCK, tq), F32)],
        compiler_params=_params("parallel", "parallel"),
        name="nsa_attention",
    )(q, gn, kc, vct, ksx, vst, kwn, vwt, bias_c, bias_near, win_mask, wov)


def _split_bf16(z, parts):
    out = []
    for _ in range(parts - 1):
        hi = z.astype(BF16)
        out.append(hi)
        z = z - hi.astype(F32)
    return out + [z.astype(BF16)]


def _head_sums(z, ones):
    width = ones.shape[0]
    nb = z.shape[1] // width
    rows = z.shape[0]
    zb = z.astype(BF16)
    stacked = jnp.concatenate([zb[:, m * width:(m + 1) * width] for m in range(nb)], axis=0)
    sums = jnp.dot(stacked, ones, preferred_element_type=F32)
    return jnp.concatenate([sums[m * rows:(m + 1) * rows] for m in range(nb)], axis=1)


def _softplus(z):
    return jnp.maximum(z, 0.0) + jnp.log(1.0 + jnp.exp(-jnp.abs(z)))


def _rwkv_chunk(x, prev, st, mu_ref, w0_ref, w2_ref, a0_ref, a2_ref, g2_ref, kk_ref, ka_ref, rk_ref,
                lng_ref, lnb_ref, ones_ref):
    L = x.shape[0]
    W = RWKV_WIDTH
    N = HEAD_DIM
    row_id = lax.broadcasted_iota(jnp.int32, x.shape, 0)
    shifted = jnp.where(row_id == 0, prev, pltpu.roll(x, 1, axis=0))
    xl = x + (shifted - x) * mu_ref[...]
    yield None
    r = xl[:, 0:W]
    k = xl[:, W:2 * W]
    v = xl[:, 2 * W:3 * W]
    xw = xl[:, 3 * W:3 * W + LORA_W]
    xa = xl[:, 3 * W + LORA_W:3 * W + LORA_W + LORA_A]
    xg = xl[:, 3 * W + LORA_W + LORA_A:]
    w = -_softplus(-(w0_ref[...] + _mm(jnp.tanh(xw), w2_ref[...]))) - 0.5
    ld = -jnp.exp(w)
    a = jax.nn.sigmoid(a0_ref[...] + _mm(xa, a2_ref[...]))
    gate = _mm(jax.nn.sigmoid(xg), g2_ref[...])
    kkv = k * kk_ref[...]
    k2 = k * (1.0 + (a - 1.0) * ka_ref[...])
    yield None

    ti = lax.broadcasted_iota(jnp.int32, (L, L), 0)
    si = lax.broadcasted_iota(jnp.int32, (L, L), 1)
    incl = si <= ti
    strict = si < ti
    tri = jnp.where(incl, 1.0, 0.0).astype(BF16)
    cl3 = jnp.dot(tri, jnp.concatenate(_split_bf16(ld, 3), axis=1), preferred_element_type=F32)
    cl = cl3[:, :W] + (cl3[:, W:2 * W] + cl3[:, 2 * W:])
    cl_end = cl[L - 1:L, :]
    yield None
    e_pos = jnp.exp(cl)
    e_neg = jnp.exp(-cl)
    e_prev = jnp.exp(cl - ld)
    e_end = jnp.exp(cl_end - cl)
    eye = jnp.where(ti == si, 1.0, 0.0)
    yield None

    hsum = lambda z: _head_sums(z, ones_ref[...])

    kk_n = kkv * lax.rsqrt(jnp.maximum(hsum(kkv * kkv), 1e-24))
    bv = kk_n * a
    yield None
    a_t = (-kk_n * e_prev).astype(BF16)
    b_t = (bv * e_neg).astype(BF16)
    k_t = (k2 * e_neg).astype(BF16)
    r_t = (r * e_pos).astype(BF16)
    yield None
    v_b = v.astype(BF16)
    k_e = k2 * e_end
    b_e = bv * e_end

    yield "elementwise done"

    P = range(W // LANES)
    ps = [slice(m * LANES, (m + 1) * LANES) for m in P]
    first = lax.broadcasted_iota(jnp.int32, (1, LANES), 1) < N
    same_head = (lax.broadcasted_iota(jnp.int32, (LANES, LANES), 0) // N
                 == lax.broadcasted_iota(jnp.int32, (LANES, LANES), 1) // N)

    def diag_rows(x):
        zero = jnp.zeros_like(x)
        return jnp.concatenate([jnp.where(first, x, zero), jnp.where(first, zero, x)], axis=0)

    def diag_blocks(x):
        zero = jnp.zeros_like(x[:, :LANES])
        return jnp.concatenate([jnp.concatenate([x[:, :LANES], zero], axis=1),
                                jnp.concatenate([zero, x[:, LANES:]], axis=1)], axis=0)

    ke_t = [k_e[:, s].T for s in ps]
    be_t = [b_e[:, s].T for s in ps]
    p_end = [e_pos[:, s].T[:, L - 1:L] for s in ps]

    lhs = [jnp.concatenate([a_t[:, s], r_t[:, s]], axis=0) for s in ps]
    rhs = [jnp.concatenate([k_t[:, s], b_t[:, s]], axis=0) for s in ps]
    zero_b = jnp.zeros_like(lhs[0])
    aa = [[lax.dot_general(jnp.where(first, x, zero_b) if j == 0 else jnp.where(first, zero_b, x), y, _NT,
                           preferred_element_type=F32) for j in range(2)] for x, y in zip(lhs, rhs)]
    cat2 = lambda f: [jnp.concatenate([f(pair[0]), f(pair[1])], axis=1) for pair in aa]
    a_ak = cat2(lambda x: jnp.where(strict, x[:L, :L], 0.0))
    a_ab = cat2(lambda x: jnp.where(strict, x[:L, L:], 0.0))
    a_rk = cat2(lambda x: jnp.where(incl, x[L:, :L], 0.0))
    a_rb = cat2(lambda x: jnp.where(incl, x[L:, L:], 0.0))
    yield None

    eye2 = jnp.concatenate([eye, eye], axis=1)
    tinv = [eye2 + x for x in a_ab]
    pw = [_mm(x, diag_blocks(x)) for x in a_ab]
    yield None
    span = 2
    while 2 * span < L:
        both = [_mm(jnp.concatenate([t, p], axis=0), diag_blocks(p)) for t, p in zip(tinv, pw)]
        tinv = [t + x[:L] for t, x in zip(tinv, both)]
        pw = [x[L:] for x in both]
        yield None
        span *= 2
    tinv = [t + _mm(t, diag_blocks(p)) for t, p in zip(tinv, pw)]
    yield None

    v_d = [diag_rows(v_b[:, s]) for s in ps]
    av = [_mm(jnp.concatenate([a_ak[m], a_rk[m]], axis=0), v_d[m]) for m in P]
    tw = [_mm(tinv[m], jnp.concatenate([diag_rows(a_t[:, ps[m]]), diag_rows(av[m][:L].astype(BF16))], axis=1))
          for m in P]
    kv_loc = [jnp.where(same_head, _mm(ke_t[m], v_b[:, ps[m]]), 0.0) for m in P]
    yield None

    ws = [_mm(jnp.concatenate([tw[m][:, :LANES].astype(BF16), r_t[:, ps[m]]], axis=0), st[m]) for m in P]
    u = [ws[m][:L] + tw[m][:, LANES:] for m in P]
    yield None
    y = [ws[m][L:] + av[m][L:] + _mm(a_rb[m], diag_rows(u[m])) for m in P]
    st_new = [p_end[m] * st[m] + kv_loc[m] + jnp.where(same_head, _mm(be_t[m], u[m]), 0.0) for m in P]
    yield None

    y = jnp.concatenate(y, axis=1)
    d = y - hsum(y) * (1.0 / N)
    yn = d * lax.rsqrt(hsum(d * d) * (1.0 / N) + GN_EPS) * lng_ref[...] + lnb_ref[...]
    bonus = hsum(r * k2 * rk_ref[...])
    yield (yn + bonus * v) * gate, st_new


def _rwkv_kernel(x_ref, xp_ref, *refs):
    *param_refs, o_ref, st_ref = refs
    c = pl.program_id(1)

    @pl.when(c == 0)
    def _():
        st_ref[...] = jnp.zeros(st_ref.shape, F32)

    chunks = []
    for i in range(x_ref.shape[0]):
        prev = jnp.where(c > 0, xp_ref[i, xp_ref.shape[1] - 1:, :], 0.0)
        st = [st_ref[i, m] for m in range(st_ref.shape[1])]
        chunks.append(_rwkv_chunk(x_ref[i], prev, st, *param_refs))
    def advance(gen, until_result):
        item = next(gen)
        return (isinstance(item, tuple), item) if until_result else (item == "elementwise done", item)

    while not advance(chunks[0], False)[0]:
        pass
    outs, states = [], []
    for i, chunk in enumerate(chunks):
        following = chunks[i + 1] if i + 1 < len(chunks) else None
        result = None
        while result is None or following is not None:
            if result is None:
                done, item = advance(chunk, True)
                result = item if done else None
            if following is not None and advance(following, False)[0]:
                following = None
        outs.append(result[0].astype(o_ref.dtype))
        states.append(jnp.stack(result[1]))
    o_ref[...] = jnp.stack(outs)
    st_ref[...] = jnp.stack(states)


def _rwkv(rw, vecs, w2, a2, g2, batch, seq):
    width = rw.shape[1]
    L = min(CHUNK, seq)
    nc = seq // L
    nb = RWKV_SEQS if batch % RWKV_SEQS == 0 else 1
    sub = 8
    rw3 = rw.reshape(batch, seq, width)
    vec = lambda wd: pl.BlockSpec((1, wd), lambda b, c: (0, 0))
    mat = lambda m: pl.BlockSpec(m.shape, lambda b, c: (0, 0))
    mu, w0, a0, kk, ka, rk, lng, lnb = vecs
    head = np.arange(2 * LANES) // HEAD_DIM
    ones = jnp.asarray(head[:, None] == head[None, :], BF16)
    out = pl.pallas_call(
        _rwkv_kernel,
        grid=(batch // nb, nc),
        in_specs=[pl.BlockSpec((nb, L, width), lambda b, c: (b, c, 0)),
                  pl.BlockSpec((nb, sub, width), lambda b, c: (b, jnp.maximum(c * (L // sub) - 1, 0), 0)),
                  vec(width), vec(RWKV_WIDTH), mat(w2), vec(RWKV_WIDTH), mat(a2), mat(g2),
                  vec(RWKV_WIDTH), vec(RWKV_WIDTH), vec(RWKV_WIDTH), vec(RWKV_WIDTH), vec(RWKV_WIDTH), mat(ones)],
        out_specs=pl.BlockSpec((nb, L, RWKV_WIDTH), lambda b, c: (b, c, 0)),
        out_shape=jax.ShapeDtypeStruct((batch, seq, RWKV_WIDTH), BF16),
        scratch_shapes=[pltpu.VMEM((nb, RWKV_WIDTH // LANES, LANES, LANES), F32)],
        compiler_params=_params("parallel", "arbitrary"),
        name="rwkv7",
    )(rw3, rw3, mu, w0, w2, a0, a2, g2, kk, ka, rk, lng, lnb, ones)
    return out


def _merge_kernel(x_ref, oa_ref, ob_ref, gab_ref, wpa_ref, wpb_ref, wo_ref, o_ref):
    d = x_ref.shape[1]
    pa = jnp.dot(oa_ref[...], wpa_ref[...], preferred_element_type=F32)
    pb = jnp.dot(ob_ref[0], wpb_ref[...], preferred_element_type=F32)
    gab = gab_ref[...].astype(F32)
    merged = jax.nn.sigmoid(gab[:, :d]) * pa + jax.nn.sigmoid(gab[:, d:]) * pb
    o_ref[...] = x_ref[...] + _mm(merged, wo_ref[...])


def _merge(x2, oa, ob, gab, wpa, wpb, wo, tm):
    n, d = x2.shape
    per_seq = ob.shape[1] // tm
    row = lambda wd: pl.BlockSpec((tm, wd), lambda i: (i, 0))
    mat = lambda m: pl.BlockSpec(m.shape, lambda i: (0, 0))
    ob_spec = pl.BlockSpec((1, tm, ob.shape[2]), lambda i: (i // per_seq, i % per_seq, 0))
    return pl.pallas_call(
        _merge_kernel,
        grid=(n // tm,),
        in_specs=[row(d), row(oa.shape[1]), ob_spec, row(gab.shape[1]), mat(wpa), mat(wpb), mat(wo)],
        out_specs=row(d),
        out_shape=jax.ShapeDtypeStruct((n, d), F32),
        compiler_params=_params("parallel"),
        name="merge",
    )(x2, oa, ob, gab, wpa, wpb, wo)


def _ffn_kernel(seq, x_ref, xh_ref, g_ref, wv_ref, wg_ref, cwv_ref, cwg_ref, cbv_ref, cbg_ref, wd_ref, o_ref,
                h_ref, uv_ref, ug_ref):
    tm = x_ref.shape[0]
    i = pl.program_id(0)
    f = pl.program_id(1)

    @pl.when(f == 0)
    def _():
        first = lax.rem(i * tm, seq) == 0
        halo = jnp.where(first, 0.0, _rms(xh_ref[...], g_ref[...]))
        h_ref[0:HALO, :] = halo.astype(BF16)
        h_ref[HALO:, :] = _rms(x_ref[...], g_ref[...]).astype(BF16)

    h = h_ref[...]
    uv_ref[...] = jnp.dot(h, wv_ref[...], preferred_element_type=F32)
    ug_ref[...] = jnp.dot(h, wg_ref[...], preferred_element_type=F32)

    def conv(u_ref, cw_ref, cb_ref):
        acc = cb_ref[...] + cw_ref[0:1, :] * u_ref[pl.ds(HALO - 2, tm), :]
        acc = acc + cw_ref[1:2, :] * u_ref[pl.ds(HALO - 1, tm), :]
        return acc + cw_ref[2:3, :] * u_ref[pl.ds(HALO, tm), :]

    val = conv(uv_ref, cwv_ref, cbv_ref)
    gt = conv(ug_ref, cwg_ref, cbg_ref)
    y = _mm(gt * jax.nn.sigmoid(gt) * val, wd_ref[...])

    @pl.when(f == 0)
    def _():
        o_ref[...] = x_ref[...] + y

    @pl.when(f > 0)
    def _():
        o_ref[...] = o_ref[...] + y


def _ffn(x1, g, w_up, conv_w, conv_b, w_down, seq, tm, tf):
    n, d = x1.shape
    dff = w_down.shape[0]
    nf = dff // tf
    return pl.pallas_call(
        functools.partial(_ffn_kernel, seq),
        grid=(n // tm, nf),
        in_specs=[pl.BlockSpec((tm, d), lambda i, f: (i, 0)),
                  pl.BlockSpec((HALO, d), lambda i, f: (jnp.maximum(i * (tm // HALO) - 1, 0), 0)),
                  pl.BlockSpec((1, d), lambda i, f: (0, 0)),
                  pl.BlockSpec((d, tf), lambda i, f: (0, f)),
                  pl.BlockSpec((d, tf), lambda i, f: (0, nf + f)),
                  pl.BlockSpec((CONV_WIDTH, tf), lambda i, f: (0, f)),
                  pl.BlockSpec((CONV_WIDTH, tf), lambda i, f: (0, nf + f)),
                  pl.BlockSpec((1, tf), lambda i, f: (0, f)),
                  pl.BlockSpec((1, tf), lambda i, f: (0, nf + f)),
                  pl.BlockSpec((tf, d), lambda i, f: (f, 0))],
        out_specs=pl.BlockSpec((tm, d), lambda i, f: (i, 0)),
        out_shape=jax.ShapeDtypeStruct((n, d), F32),
        scratch_shapes=[pltpu.VMEM((tm + HALO, d), BF16),
                        pltpu.VMEM((tm + HALO, tf), F32),
                        pltpu.VMEM((tm + HALO, tf), F32)],
        compiler_params=_params("parallel", "arbitrary"),
        name="convffn",
    )(x1, x1, g, w_up, w_up, conv_w, conv_w, conv_b, conv_b, w_down)


def _t5_bucket(dist):
    n = np.maximum(dist, 0)
    max_exact = REL_BUCKETS // 2
    ratio = np.log(np.maximum(n, 1).astype(np.float32) / max_exact) / math.log(REL_MAX_DIST / max_exact)
    large = np.minimum(max_exact + (ratio * (REL_BUCKETS - max_exact)).astype(np.int32), REL_BUCKETS - 1)
    return np.where(n < max_exact, n, large).astype(np.int32)


def _bias_tables(rel_bias, seq, ncmp):
    tq, tk = Q_TILE, K_TILE
    G, R = NSA_KV_GROUPS, NSA_GROUP
    nq = seq // tq
    step = tq // CMP_STRIDE
    tab = rel_bias.astype(F32)

    off = (nq - 1) * step
    d_c = np.arange(tq)[None, :] - (np.arange(ncmp + off)[:, None] - off) * CMP_STRIDE - (CMP_BLOCK - 1)
    n_b = -(-REL_MAX_DIST // tk)
    d0 = np.arange(tq)[None, :] - np.arange(tk)[:, None]
    d_n = np.stack([d0 - o * tk for o in range(-n_b, tq // tk)])

    buckets = np.concatenate([_t5_bucket(d_c).reshape(-1), _t5_bucket(d_n).reshape(-1)])
    onehot = (jnp.arange(REL_BUCKETS, dtype=jnp.int32)[:, None] == jnp.asarray(buckets)[None, :]).astype(F32)
    vals = jnp.dot(tab.T * LOG2E, onehot, precision=lax.Precision.HIGHEST, preferred_element_type=F32)
    base = jnp.where(jnp.asarray(d_c >= 0), vals[:, :d_c.size].reshape(G, R, ncmp + off, tq), NEG_INF)

    far = (tab[REL_BUCKETS - 1] * LOG2E).reshape(G, R, 1, 1, 1)
    near = vals[:, d_c.size:].reshape(G, R, d_n.shape[0], tk, tq) - far
    bias_near = jnp.where(jnp.asarray(d_n >= 0), near, NEG_INF)
    win = np.stack([np.where(d0 - o * tk < WINDOW, 0.0, NEG_INF) for o in _win_masked(tq, tk)])
    return base, bias_near, jnp.asarray(win.astype(np.float32))


def _overlap_matrix(ncmp_pad, n_slc, n_cmp):
    m = np.zeros((ncmp_pad, n_slc), np.float32)
    ratio = SLC_BLOCK // CMP_STRIDE
    for j in range(n_slc):
        for o, wgt in enumerate(OVERLAP_W):
            cidx = ratio * j + o - (CMP_BLOCK // CMP_STRIDE - 1)
            if 0 <= cidx < n_cmp:
                m[cidx, j] += wgt
    return jnp.asarray(m.T)


def _layer(x, attn_norm_g, w_in, rel_bias, q_norm_g, k_norm_g, cmp_pe_k, cmp_w1_k, cmp_w2_k,
           cmp_pe_v, cmp_w1_v, cmp_w2_v, rwkv_mu, rwkv_w0, rwkv_w2, rwkv_a0, rwkv_a2, rwkv_g2,
           rwkv_k_k, rwkv_k_a, rwkv_r_k, rwkv_ln_g, rwkv_ln_b, w_proj_a, w_proj_b, w_out,
           ffn_norm_g, w_up, conv_w, conv_b, w_down):
    batch, seq, d = x.shape
    n = batch * seq
    assert seq % Q_TILE == 0 and Q_TILE % K_TILE == 0 and WINDOW % K_TILE == 0
    assert seq % CMP_STRIDE == 0 and seq % FFN_ROWS == 0 and seq % PROJ_ROWS == 0 and PROJ_ROWS % K_TILE == 0 and PROJ_ROWS % Q_TILE == 0
    assert seq // SLC_BLOCK <= LANES - HEAD_DIM
    assert NSA_KV_WIDTH == LANES and CMP_PACK * HEAD_DIM == 2 * LANES
    x2 = x.reshape(n, d)
    row = lambda a: a.reshape(1, -1).astype(F32)

    o_kvc = NSA_WIDTH
    o_kv = o_kvc + 2 * NSA_KV_WIDTH
    o_gn = o_kv + 4 * NSA_KV_WIDTH
    o_rw = o_gn + 3 * NSA_HEADS
    o_gab = o_rw + RWKV_IN_WIDTH
    cols = lambda a, b: w_in[:, a:b].astype(BF16)
    gn_pad = jnp.zeros((d, LANES - 3 * NSA_HEADS), BF16)
    ws = (cols(o_rw, o_gab), cols(0, o_kvc), cols(o_kvc, o_kv), cols(o_kv, o_gn), cols(o_gab, w_in.shape[1]),
          jnp.concatenate([cols(o_gn, o_rw), gn_pad], axis=1))
    gk = row(k_norm_g)
    rw, q, kv_cmp, gab, gates, ksx, kwn, vst, vwt = _inproj(
        x2, row(attn_norm_g), ws, jnp.concatenate([gk, gk], axis=1), q_norm_g.reshape(-1, 1).astype(F32), seq,
        tm=PROJ_ROWS)

    nchunk = seq // CMP_STRIDE
    n_cmp = nchunk - CMP_BLOCK // CMP_STRIDE + 1
    pe = jnp.stack([cmp_pe_k, cmp_pe_v]).astype(F32)
    pe = jnp.concatenate([pe] * NSA_KV_GROUPS, axis=-1)
    w1 = jnp.stack([cmp_w1_k, cmp_w1_v]).astype(BF16)
    w1 = w1.reshape(2, CMP_BLOCK // CMP_PACK, CMP_PACK * HEAD_DIM, CMP_HIDDEN)
    w2 = jnp.stack([cmp_w2_k, cmp_w2_v]).astype(BF16)
    kc, vc = _compress(kv_cmp, pe, w1, w2, gk, batch, seq)

    bias_c, bias_near, win_mask = _bias_tables(rel_bias, seq, nchunk)
    wov = _overlap_matrix(nchunk, seq // SLC_BLOCK, n_cmp)
    o_a = _attention(q, gates, kc, jnp.swapaxes(vc, -1, -2), ksx, vst, kwn, vwt, bias_c, bias_near, win_mask,
                     wov, batch, seq)

    vecs = tuple(row(a) for a in (rwkv_mu, rwkv_w0, rwkv_a0, rwkv_k_k, rwkv_k_a, rwkv_r_k, rwkv_ln_g, rwkv_ln_b))
    o_b = _rwkv(rw, vecs, rwkv_w2.astype(BF16), rwkv_a2.astype(BF16), rwkv_g2.astype(BF16), batch, seq)

    x1 = _merge(x2, o_a, o_b, gab, w_proj_a.astype(BF16), w_proj_b.astype(BF16), w_out.astype(BF16), tm=FFN_ROWS)
    dff = w_down.shape[0]
    out = _ffn(x1, row(ffn_norm_g), w_up.astype(BF16), conv_w.astype(F32), row(conv_b), w_down.astype(BF16),
               seq, tm=FFN_ROWS, tf=dff // FFN_COL_SPLITS)
    return out.reshape(batch, seq, d)


def kernel(x, attn_norm_g, w_in, rel_bias, q_norm_g, k_norm_g, cmp_pe_k, cmp_w1_k, cmp_w2_k, cmp_pe_v, cmp_w1_v,
           cmp_w2_v, rwkv_mu, rwkv_w0, rwkv_w2, rwkv_a0, rwkv_a2, rwkv_g2, rwkv_k_k, rwkv_k_a, rwkv_r_k,
           rwkv_ln_g, rwkv_ln_b, w_proj_a, w_proj_b, w_out, ffn_norm_g, w_up, conv_w, conv_b, w_down):
    per_layer = (attn_norm_g, w_in, None, q_norm_g, k_norm_g, cmp_pe_k, cmp_w1_k, cmp_w2_k, cmp_pe_v, cmp_w1_v,
                 cmp_w2_v, rwkv_mu, rwkv_w0, rwkv_w2, rwkv_a0, rwkv_a2, rwkv_g2, rwkv_k_k, rwkv_k_a, rwkv_r_k,
                 rwkv_ln_g, rwkv_ln_b, w_proj_a, w_proj_b, w_out, ffn_norm_g, w_up, conv_w, conv_b, w_down)
    for l in range(attn_norm_g.shape[0]):
        args = [rel_bias if p is None else p[l] for p in per_layer]
        x = _layer(x, *args)
    return x
```
